```python
import jax, jax.numpy as jnp
from jax import lax
import numpy as np

D_MODEL = 4096
BATCH = 2
SEQ = 4096
DEPTH = 1

N_BRANCHES = 2
POOL_WIDTH = D_MODEL // 2
CONV_WIDTH = D_MODEL // 2
POOL_WINDOWS = (2, 4, 8, 16)
N_POOL_GROUPS = len(POOL_WINDOWS)
POOL_GROUP_DIM = POOL_WIDTH // N_POOL_GROUPS
CONV_K = 3
NORM_EPS = 1e-6
SPLIT_SIZES = (POOL_WIDTH, POOL_WIDTH, CONV_WIDTH, CONV_WIDTH, CONV_WIDTH, CONV_WIDTH,
               N_BRANCHES * D_MODEL)
PROJ_WIDTH = sum(SPLIT_SIZES)
SPLIT_POINTS = tuple(int(v) for v in np.cumsum(SPLIT_SIZES)[:-1])

kernel_name = "hybrid_pool_shortconv_gated_merge"


def rmsnorm(x, w):
    xf = x.astype(jnp.float32)
    y = xf * lax.rsqrt(jnp.mean(xf * xf, axis=-1, keepdims=True) + NORM_EPS)
    return (y * w.astype(jnp.float32)).astype(x.dtype)


def causal_multiscale_pool(u, pool_w, pool_scale):
    b, s, _ = u.shape
    ug = u.reshape(b, s, N_POOL_GROUPS, POOL_GROUP_DIM).astype(jnp.float32)
    cs = lax.cumsum(ug, axis=1)
    cs_pad = jnp.concatenate([jnp.zeros((b, 1, N_POOL_GROUPS, POOL_GROUP_DIM), jnp.float32), cs], axis=1)
    t1 = jnp.arange(1, s + 1, dtype=jnp.float32)
    pooled = []
    for g, w in enumerate(POOL_WINDOWS):
        upper = cs_pad[:, 1:, g]
        lower = jnp.concatenate([jnp.zeros((b, w - 1, POOL_GROUP_DIM), jnp.float32),
                                 cs_pad[:, : s + 1 - w, g]], axis=1)
        count = jnp.minimum(t1, jnp.float32(w))[None, :, None]
        pooled.append((upper - lower) / count - ug[:, :, g])
    pooled = jnp.stack(pooled, axis=2).astype(u.dtype)
    mixed = jnp.einsum('bsgc,gcd->bsgd', pooled, pool_w)
    return mixed.reshape(b, s, POOL_WIDTH) * pool_scale


def causal_gated_shortconv(u, b_gate, c_gate, conv_w, conv_b):
    s = u.shape[1]
    v = c_gate * u
    vpad = jnp.pad(v, ((0, 0), (CONV_K - 1, 0), (0, 0)))
    y = conv_b + sum(conv_w[k] * vpad[:, k:k + s] for k in range(CONV_K))
    return b_gate * y


def setup_inputs(seed: int = 0) -> dict:
    key = jax.random.key(seed)
    ks = jax.random.split(key, 11)
    d = D_MODEL
    x = jax.random.normal(ks[0], (BATCH, SEQ, d), jnp.float32)
    norm_w = 1.0 + 0.02 * jax.random.normal(ks[1], (DEPTH, d), jnp.float32)
    w_in = jax.random.normal(ks[2], (DEPTH, d, PROJ_WIDTH), jnp.float32) * d ** -0.5
    pool_w = jax.random.normal(ks[3], (DEPTH, N_POOL_GROUPS, POOL_GROUP_DIM, POOL_GROUP_DIM), jnp.float32) * POOL_GROUP_DIM ** -0.5
    pool_scale = 1.0 + 0.1 * jax.random.normal(ks[4], (DEPTH, POOL_WIDTH), jnp.float32)
    conv_w = jax.random.normal(ks[5], (DEPTH, CONV_K, CONV_WIDTH), jnp.float32) * CONV_K ** -0.5
    conv_b = 0.02 * jax.random.normal(ks[6], (DEPTH, CONV_WIDTH), jnp.float32)
    gate_b = 0.02 * jax.random.normal(ks[7], (DEPTH, N_BRANCHES, d), jnp.float32)
    w_branch = jax.random.normal(ks[8], (DEPTH, N_BRANCHES, POOL_WIDTH, d), jnp.float32) * POOL_WIDTH ** -0.5
    w_out = jax.random.normal(ks[9], (DEPTH, d, d), jnp.float32) * d ** -0.5
    final_norm_w = 1.0 + 0.02 * jax.random.normal(ks[10], (d,), jnp.float32)
    return {"x": x, "norm_w": norm_w, "w_in": w_in, "pool_w": pool_w,
            "pool_scale": pool_scale, "conv_w": conv_w, "conv_b": conv_b,
            "gate_b": gate_b, "w_branch": w_branch, "w_out": w_out,
            "final_norm_w": final_norm_w}


def reference(x, norm_w, w_in, pool_w, pool_scale, conv_w, conv_b, gate_b, w_branch, w_out, final_norm_w):
    b, s, d = x.shape
    for l in range(DEPTH):
        h = rmsnorm(x, norm_w[l])
        proj = jnp.einsum('bsd,de->bse', h, w_in[l])
        u_p, z_p, u_c, b_c, c_c, z_c, g_logit = jnp.split(proj, SPLIT_POINTS, axis=-1)
        y_pool = causal_multiscale_pool(u_p, pool_w[l], pool_scale[l]) * jax.nn.silu(z_p)
        y_conv = causal_gated_shortconv(u_c, b_c, c_c, conv_w[l], conv_b[l]) * jax.nn.silu(z_c)
        ys = jnp.stack([y_pool, y_conv], axis=2)
        br = jnp.einsum('bsnc,ncd->bsnd', ys, w_branch[l])
        gates = jax.nn.sigmoid(g_logit.reshape(b, s, N_BRANCHES, d) + gate_b[l])
        merged = jnp.sum(gates * br, axis=2)
        x = x + jnp.einsum('bsd,de->bse', merged, w_out[l])
    return rmsnorm(x, final_norm_w)
```

```python
import functools

import jax
import jax.numpy as jnp
from jax import lax
from jax.experimental import pallas as pl
from jax.experimental.pallas import tpu as pltpu

D_MODEL = 4096
SEQ = 4096
HALF = D_MODEL // 2
POOL_WINDOWS = (2, 4, 8, 16)
N_GROUPS = len(POOL_WINDOWS)
GROUP_DIM = HALF // N_GROUPS
CONV_K = 3
NORM_EPS = 1e-6

COL_BLOCK = GROUP_DIM
N_HALF_BLOCKS = HALF // COL_BLOCK
N_MODEL_BLOCKS = D_MODEL // COL_BLOCK
ROW_TILE = 512
POOL_HALO = 16
POOL_PAD = 16
POOL_TOP = POOL_PAD + POOL_HALO
CONV_HALO = 8

VMEM_LIMIT_BYTES = 56 * 1024 * 1024


def _dot(a, b):
    return jnp.dot(a, b, preferred_element_type=jnp.float32)


def _silu(z):
    return z / (1.0 + jnp.exp(-z))


def _sigmoid(z):
    return 1.0 / (1.0 + jnp.exp(-z))


def _rmsnorm_kernel(x_ref, w_ref, o_ref):
    x = x_ref[...]
    ms = jnp.mean(x * x, axis=-1, keepdims=True)
    o_ref[...] = (x * lax.rsqrt(ms + NORM_EPS) * w_ref[...]).astype(o_ref.dtype)


def _rmsnorm_bf16(x2d, w):
    m, d = x2d.shape
    tm = 512
    return pl.pallas_call(
        _rmsnorm_kernel,
        grid=(m // tm,),
        in_specs=[pl.BlockSpec((tm, d), lambda i: (i, 0)),
                  pl.BlockSpec((1, d), lambda i: (0, 0))],
        out_specs=pl.BlockSpec((tm, d), lambda i: (i, 0)),
        out_shape=jax.ShapeDtypeStruct((m, d), jnp.bfloat16),
        compiler_params=pltpu.CompilerParams(
            dimension_semantics=("arbitrary",), vmem_limit_bytes=VMEM_LIMIT_BYTES),
        name="rmsnorm_in",
    )(x2d, w.reshape(1, d))


def _mixer_kernel(h_ref, wa_ref, wb_ref, wc_ref, wd_ref, pw_ref, ps_ref, cw_ref, cb_ref,
                  ys_ref, ubuf, sbuf_a, sbuf_b, vbuf, ucarry, vcarry, *, tiles_per_seq):
    i = pl.program_id(0)
    j = pl.program_id(1)
    tm = h_ref.shape[0]
    seq_tile = i % tiles_per_seq
    h = h_ref[...]

    @pl.when(j < N_HALF_BLOCKS)
    def _pool():
        u = _dot(h, wa_ref[...])
        z = _dot(h, wb_ref[...])
        zeros_pad = jnp.zeros((POOL_PAD, COL_BLOCK), jnp.float32)
        ubuf[0:POOL_PAD, :] = zeros_pad

        @pl.when(seq_tile == 0)
        def _():
            ubuf[POOL_PAD:POOL_TOP, :] = jnp.zeros((POOL_HALO, COL_BLOCK), jnp.float32)

        @pl.when(seq_tile != 0)
        def _():
            ubuf[POOL_PAD:POOL_TOP, :] = ucarry[j]

        ubuf[POOL_TOP:POOL_TOP + tm, :] = u
        ucarry[j] = u[tm - POOL_HALO:, :]

        end = POOL_TOP + tm

        def shifted_sum(src, lo, shift):
            return src[lo:end, :] + src[lo - shift:end - shift, :]

        sbuf_a[8:end, :] = shifted_sum(ubuf, 8, 1)
        for g in range(N_GROUPS):
            @pl.when(j == g)
            def _(g=g):
                if g == 0:
                    win = sbuf_a[POOL_TOP:end, :]
                elif g == 1:
                    win = shifted_sum(sbuf_a, POOL_TOP, 2)
                elif g == 2:
                    sbuf_b[24:end, :] = shifted_sum(sbuf_a, 24, 2)
                    win = shifted_sum(sbuf_b, POOL_TOP, 4)
                else:
                    sbuf_b[16:end, :] = shifted_sum(sbuf_a, 16, 2)
                    sbuf_a[24:end, :] = shifted_sum(sbuf_b, 24, 4)
                    win = shifted_sum(sbuf_a, POOL_TOP, 8)
                w = POOL_WINDOWS[g]
                t1 = (seq_tile * tm + 1
                      + lax.broadcasted_iota(jnp.int32, (tm, 1), 0)).astype(jnp.float32)
                inv_count = 1.0 / jnp.minimum(t1, jnp.float32(w))
                pooled = win * inv_count - ubuf[POOL_TOP:end, :]
                mixed = _dot(pooled.astype(jnp.bfloat16), pw_ref[0])
                y = mixed * ps_ref[...] * _silu(z)
                ys_ref[...] = y.astype(ys_ref.dtype)

    @pl.when(j >= N_HALF_BLOCKS)
    def _conv():
        c = j - N_HALF_BLOCKS
        u = _dot(h, wa_ref[...])
        b_gate = _dot(h, wb_ref[...])
        c_gate = _dot(h, wc_ref[...])
        z = _dot(h, wd_ref[...])
        v = c_gate * u

        @pl.when(seq_tile == 0)
        def _():
            vbuf[0:CONV_HALO, :] = jnp.zeros((CONV_HALO, COL_BLOCK), jnp.float32)

        @pl.when(seq_tile != 0)
        def _():
            vbuf[0:CONV_HALO, :] = vcarry[c]

        vbuf[CONV_HALO:CONV_HALO + tm, :] = v
        vcarry[c] = v[tm - CONV_HALO:, :]
        cw = cw_ref[...]
        y = (cb_ref[...]
             + cw[0:1, :] * vbuf[CONV_HALO - 2:CONV_HALO - 2 + tm, :]
             + cw[1:2, :] * vbuf[CONV_HALO - 1:CONV_HALO - 1 + tm, :]
             + cw[2:3, :] * v)
        ys_ref[...] = (b_gate * y * _silu(z)).astype(ys_ref.dtype)


def _mixers(h, w_in_bf, pool_w_bf, pool_scale, conv_w, conv_b):
    m = h.shape[0]
    tm = ROW_TILE
    nb = N_HALF_BLOCKS

    def wa_map(i, j):
        return (0, jnp.where(j < nb, j, j + nb))

    def wb_map(i, j):
        return (0, jnp.where(j < nb, j + nb, j + 2 * nb))

    def wc_map(i, j):
        return (0, 4 * nb + jnp.maximum(j - nb, 0))

    def wd_map(i, j):
        return (0, 5 * nb + jnp.maximum(j - nb, 0))

    def pool_idx(j):
        return jnp.minimum(j, nb - 1)

    def conv_idx(j):
        return jnp.maximum(j - nb, 0)

    w_spec = lambda f: pl.BlockSpec((D_MODEL, COL_BLOCK), f)
    kern = functools.partial(_mixer_kernel, tiles_per_seq=SEQ // tm)
    return pl.pallas_call(
        kern,
        grid=(m // tm, 2 * nb),
        in_specs=[
            pl.BlockSpec((tm, D_MODEL), lambda i, j: (i, 0), pipeline_mode=pl.Buffered(1)),
            w_spec(wa_map), w_spec(wb_map), w_spec(wc_map), w_spec(wd_map),
            pl.BlockSpec((1, GROUP_DIM, GROUP_DIM), lambda i, j: (pool_idx(j), 0, 0)),
            pl.BlockSpec((1, COL_BLOCK), lambda i, j: (0, pool_idx(j))),
            pl.BlockSpec((CONV_K, COL_BLOCK), lambda i, j: (0, conv_idx(j))),
            pl.BlockSpec((1, COL_BLOCK), lambda i, j: (0, conv_idx(j))),
        ],
        out_specs=pl.BlockSpec((tm, COL_BLOCK), lambda i, j: (i, j)),
        out_shape=jax.ShapeDtypeStruct((m, D_MODEL), jnp.bfloat16),
        scratch_shapes=[
            pltpu.VMEM((POOL_TOP + tm, COL_BLOCK), jnp.float32),
            pltpu.VMEM((POOL_TOP + tm, COL_BLOCK), jnp.float32),
            pltpu.VMEM((POOL_TOP + tm, COL_BLOCK), jnp.float32),
            pltpu.VMEM((CONV_HALO + tm, COL_BLOCK), jnp.float32),
            pltpu.VMEM((nb, POOL_HALO, COL_BLOCK), jnp.float32),
            pltpu.VMEM((nb, CONV_HALO, COL_BLOCK), jnp.float32),
        ],
        compiler_params=pltpu.CompilerParams(
            dimension_semantics=("arbitrary", "arbitrary"),
            vmem_limit_bytes=VMEM_LIMIT_BYTES),
        name="proj_mixers",
    )(h, w_in_bf, w_in_bf, w_in_bf, w_in_bf, pool_w_bf,
      pool_scale.reshape(1, HALF), conv_w, conv_b.reshape(1, HALF))


def _merge_kernel(h_ref, ys_ref, wg0_ref, wg1_ref, gb0_ref, gb1_ref, wb0_ref, wb1_ref, o_ref):
    h = h_ref[...]
    g0 = _sigmoid(_dot(h, wg0_ref[...]) + gb0_ref[...])
    g1 = _sigmoid(_dot(h, wg1_ref[...]) + gb1_ref[...])
    br0 = _dot(ys_ref[:, :HALF], wb0_ref[0])
    br1 = _dot(ys_ref[:, HALF:], wb1_ref[0])
    o_ref[...] = (g0 * br0 + g1 * br1).astype(o_ref.dtype)


def _merge(h, ys, w_in_bf, gate_b, w_branch_bf):
    m = h.shape[0]
    tm = ROW_TILE
    g0_first = 6 * N_HALF_BLOCKS
    g1_first = g0_first + N_MODEL_BLOCKS
    return pl.pallas_call(
        _merge_kernel,
        grid=(m // tm, N_MODEL_BLOCKS),
        in_specs=[
            pl.BlockSpec((tm, D_MODEL), lambda i, n: (i, 0)),
            pl.BlockSpec((tm, D_MODEL), lambda i, n: (i, 0)),
            pl.BlockSpec((D_MODEL, COL_BLOCK), lambda i, n: (0, g0_first + n)),
            pl.BlockSpec((D_MODEL, COL_BLOCK), lambda i, n: (0, g1_first + n)),
            pl.BlockSpec((1, COL_BLOCK), lambda i, n: (0, n)),
            pl.BlockSpec((1, COL_BLOCK), lambda i, n: (0, n)),
            pl.BlockSpec((1, HALF, COL_BLOCK), lambda i, n: (0, 0, n)),
            pl.BlockSpec((1, HALF, COL_BLOCK), lambda i, n: (1, 0, n)),
        ],
        out_specs=pl.BlockSpec((tm, COL_BLOCK), lambda i, n: (i, n)),
        out_shape=jax.ShapeDtypeStruct((m, D_MODEL), jnp.bfloat16),
        compiler_params=pltpu.CompilerParams(
            dimension_semantics=("arbitrary", "arbitrary"),
            vmem_limit_bytes=VMEM_LIMIT_BYTES),
        name="branch_merge",
    )(h, ys, w_in_bf, w_in_bf, gate_b[0:1], gate_b[1:2], w_branch_bf, w_branch_bf)


def _out_kernel(mg_ref, w_ref, x_ref, fw_ref, o_ref, rows, sumsq):
    n = pl.program_id(1)
    last = pl.num_programs(1) - 1
    xn = x_ref[...] + _dot(mg_ref[...], w_ref[...])
    rows[n] = xn
    part = jnp.sum(xn * xn, axis=-1, keepdims=True)

    @pl.when(n == 0)
    def _():
        sumsq[...] = part

    @pl.when(n != 0)
    def _():
        sumsq[...] += part

    @pl.when(n == last)
    def _():
        scale = lax.rsqrt(sumsq[...] * (1.0 / D_MODEL) + NORM_EPS)
        for k in range(N_MODEL_BLOCKS):
            cols = slice(k * COL_BLOCK, (k + 1) * COL_BLOCK)
            o_ref[:, cols] = rows[k] * scale * fw_ref[:, cols]


def _out_proj(merged, w_out_bf, x2d, final_w):
    m = x2d.shape[0]
    tm = ROW_TILE
    return pl.pallas_call(
        _out_kernel,
        grid=(m // tm, N_MODEL_BLOCKS),
        in_specs=[
            pl.BlockSpec((tm, D_MODEL), lambda i, n: (i, 0)),
            pl.BlockSpec((D_MODEL, COL_BLOCK), lambda i, n: (0, n)),
            pl.BlockSpec((tm, COL_BLOCK), lambda i, n: (i, n)),
            pl.BlockSpec((1, D_MODEL), lambda i, n: (0, 0)),
        ],
        out_specs=pl.BlockSpec((tm, D_MODEL), lambda i, n: (i, 0)),
        out_shape=jax.ShapeDtypeStruct((m, D_MODEL), jnp.float32),
        scratch_shapes=[
            pltpu.VMEM((N_MODEL_BLOCKS, tm, COL_BLOCK), jnp.float32),
            pltpu.VMEM((tm, 1), jnp.float32),
        ],
        compiler_params=pltpu.CompilerParams(
            dimension_semantics=("arbitrary", "arbitrary"),
            vmem_limit_bytes=VMEM_LIMIT_BYTES),
        name="out_proj_norm",
    )(merged, w_out_bf, x2d, final_w.reshape(1, D_MODEL))


def kernel(x, norm_w, w_in, pool_w, pool_scale, conv_w, conv_b, gate_b, w_branch, w_out,
           final_norm_w):
    b, s, d = x.shape
    assert (s, d) == (SEQ, D_MODEL) and norm_w.shape[0] == 1
    x2d = x.reshape(b * s, d)
    w_in_bf = w_in[0].astype(jnp.bfloat16)
    pool_w_bf = pool_w[0].astype(jnp.bfloat16)
    w_branch_bf = w_branch[0].astype(jnp.bfloat16)
    w_out_bf = w_out[0].astype(jnp.bfloat16)

    h = _rmsnorm_bf16(x2d, norm_w[0])
    ys = _mixers(h, w_in_bf, pool_w_bf, pool_scale[0], conv_w[0], conv_b[0])
    merged = _merge(h, ys, w_in_bf, gate_b[0], w_branch_bf)
    out = _out_proj(merged, w_out_bf, x2d, final_norm_w)
    return out.reshape(b, s, d)
```

```python
import functools

import jax
import jax.numpy as jnp
from jax import lax
from jax.experimental import pallas as pl
from jax.experimental.pallas import tpu as pltpu

D_MODEL = 4096
SEQ = 4096
HALF = D_MODEL // 2
POOL_WINDOWS = (2, 4, 8, 16)
N_GROUPS = len(POOL_WINDOWS)
GROUP_DIM = HALF // N_GROUPS
CONV_K = 3
NORM_EPS = 1e-6

COL_BLOCK = GROUP_DIM
N_HALF_BLOCKS = HALF // COL_BLOCK
N_MODEL_BLOCKS = D_MODEL // COL_BLOCK
OUT_COL_BLOCK = 1024
ROW_TILE = 512
POOL_HALO = 16
POOL_PAD = 16
POOL_TOP = POOL_PAD + POOL_HALO
CONV_HALO = 8

VMEM_LIMIT_BYTES = 56 * 1024 * 1024


def _dot(a, b):
    return jnp.dot(a, b, preferred_element_type=jnp.float32)


def _silu(z):
    return z / (1.0 + jnp.exp(-z))


def _sigmoid(z):
    return 1.0 / (1.0 + jnp.exp(-z))


def _params(n_axes):
    return pltpu.CompilerParams(dimension_semantics=("arbitrary",) * n_axes,
                                vmem_limit_bytes=VMEM_LIMIT_BYTES)


def _rmsnorm_kernel(x_ref, w_ref, o_ref):
    x = x_ref[...]
    ms = jnp.mean(x * x, axis=-1, keepdims=True)
    o_ref[...] = (x * lax.rsqrt(ms + NORM_EPS) * w_ref[...]).astype(o_ref.dtype)


def _rmsnorm_bf16(x2d, w):
    m, d = x2d.shape
    tm = ROW_TILE
    return pl.pallas_call(
        _rmsnorm_kernel,
        grid=(m // tm,),
        in_specs=[pl.BlockSpec((tm, d), lambda i: (i, 0)),
                  pl.BlockSpec((1, d), lambda i: (0, 0))],
        out_specs=pl.BlockSpec((tm, d), lambda i: (i, 0)),
        out_shape=jax.ShapeDtypeStruct((m, d), jnp.bfloat16),
        compiler_params=_params(1),
        name="rmsnorm_in",
    )(x2d, w.reshape(1, d))


def _pool_kernel(h_ref, wu_ref, wz_ref, pw_ref, ps_ref, y_ref,
                 ubuf, s2buf, s4buf, s8buf, ucarry, *, tiles_per_seq):
    g = pl.program_id(0)
    i = pl.program_id(1)
    tm = h_ref.shape[0]
    seq_tile = i % tiles_per_seq
    end = POOL_TOP + tm

    @pl.when(i == 0)
    def _():
        ucarry[...] = jnp.zeros_like(ucarry)

    h = h_ref[...]
    u = _dot(h, wu_ref[0])
    z = _dot(h, wz_ref[0])

    ubuf[0:POOL_PAD, :] = jnp.zeros((POOL_PAD, COL_BLOCK), jnp.float32)
    ubuf[POOL_PAD:POOL_TOP, :] = jnp.where(seq_tile != 0, ucarry[...], 0.0)
    ubuf[POOL_TOP:end, :] = u
    ucarry[...] = u[tm - POOL_HALO:, :]

    def shifted_sum(src, lo, shift):
        return src[lo:end, :] + src[lo - shift:end - shift, :]

    s2buf[8:end, :] = shifted_sum(ubuf, 8, 1)
    s4buf[16:end, :] = shifted_sum(s2buf, 16, 2)
    s8buf[24:end, :] = shifted_sum(s4buf, 24, 4)
    win = jnp.where(g == 0, s2buf[POOL_TOP:end, :],
                    jnp.where(g == 1, s4buf[POOL_TOP:end, :],
                              jnp.where(g == 2, s8buf[POOL_TOP:end, :],
                                        shifted_sum(s8buf, POOL_TOP, 8))))
    window = lax.shift_left(jnp.int32(2), g)
    t1 = seq_tile * tm + 1 + lax.broadcasted_iota(jnp.int32, (tm, 1), 0)
    inv_count = 1.0 / jnp.minimum(t1, window).astype(jnp.float32)
    pooled = win * inv_count - u
    mixed = _dot(pooled.astype(jnp.bfloat16), pw_ref[0])
    y_ref[...] = (mixed * ps_ref[...] * _silu(z)).astype(y_ref.dtype)


def _pool_mixer(h, w_in_blk, pool_w_bf, pool_scale):
    m = h.shape[0]
    tm = ROW_TILE
    nb = N_HALF_BLOCKS
    buf = pltpu.VMEM((POOL_TOP + tm, COL_BLOCK), jnp.float32)
    return pl.pallas_call(
        functools.partial(_pool_kernel, tiles_per_seq=SEQ // tm),
        grid=(nb, m // tm),
        in_specs=[
            pl.BlockSpec((tm, D_MODEL), lambda g, i: (i, 0)),
            pl.BlockSpec((1, D_MODEL, COL_BLOCK), lambda g, i: (g, 0, 0)),
            pl.BlockSpec((1, D_MODEL, COL_BLOCK), lambda g, i: (nb + g, 0, 0)),
            pl.BlockSpec((1, GROUP_DIM, GROUP_DIM), lambda g, i: (g, 0, 0)),
            pl.BlockSpec((1, COL_BLOCK), lambda g, i: (0, g)),
        ],
        out_specs=pl.BlockSpec((tm, COL_BLOCK), lambda g, i: (i, g)),
        out_shape=jax.ShapeDtypeStruct((m, HALF), jnp.bfloat16),
        scratch_shapes=[buf, buf, buf, buf,
                        pltpu.VMEM((POOL_HALO, COL_BLOCK), jnp.float32)],
        compiler_params=_params(2),
        name="pool_mixer",
    )(h, w_in_blk, w_in_blk, pool_w_bf, pool_scale.reshape(1, HALF))


def _conv_kernel(h_ref, wu_ref, wb_ref, wc_ref, wz_ref, cw_ref, cb_ref, y_ref,
                 vbuf, vcarry, *, tiles_per_seq):
    i = pl.program_id(1)
    tm = h_ref.shape[0]
    seq_tile = i % tiles_per_seq

    @pl.when(i == 0)
    def _():
        vcarry[...] = jnp.zeros_like(vcarry)

    h = h_ref[...]
    u = _dot(h, wu_ref[0])
    c_gate = _dot(h, wc_ref[0])
    v = c_gate * u
    vbuf[0:CONV_HALO, :] = jnp.where(seq_tile != 0, vcarry[...], 0.0)
    vbuf[CONV_HALO:CONV_HALO + tm, :] = v
    vcarry[...] = v[tm - CONV_HALO:, :]
    cw = cw_ref[...]
    y = (cb_ref[...]
         + cw[0:1, :] * vbuf[CONV_HALO - 2:CONV_HALO - 2 + tm, :]
         + cw[1:2, :] * vbuf[CONV_HALO - 1:CONV_HALO - 1 + tm, :]
         + cw[2:3, :] * v)
    b_gate = _dot(h, wb_ref[0])
    z = _dot(h, wz_ref[0])
    y_ref[...] = (b_gate * y * _silu(z)).astype(y_ref.dtype)


def _conv_mixer(h, w_in_blk, conv_w, conv_b):
    m = h.shape[0]
    tm = ROW_TILE
    nb = N_HALF_BLOCKS
    w_spec = lambda first: pl.BlockSpec((1, D_MODEL, COL_BLOCK),
                                        lambda c, i: (first + c, 0, 0))
    return pl.pallas_call(
        functools.partial(_conv_kernel, tiles_per_seq=SEQ // tm),
        grid=(nb, m // tm),
        in_specs=[
            pl.BlockSpec((tm, D_MODEL), lambda c, i: (i, 0)),
            w_spec(2 * nb), w_spec(3 * nb), w_spec(4 * nb), w_spec(5 * nb),
            pl.BlockSpec((CONV_K, COL_BLOCK), lambda c, i: (0, c)),
            pl.BlockSpec((1, COL_BLOCK), lambda c, i: (0, c)),
        ],
        out_specs=pl.BlockSpec((tm, COL_BLOCK), lambda c, i: (i, c)),
        out_shape=jax.ShapeDtypeStruct((m, HALF), jnp.bfloat16),
        scratch_shapes=[pltpu.VMEM((CONV_HALO + tm, COL_BLOCK), jnp.float32),
                        pltpu.VMEM((CONV_HALO, COL_BLOCK), jnp.float32)],
        compiler_params=_params(2),
        name="conv_mixer",
    )(h, w_in_blk, w_in_blk, w_in_blk, w_in_blk, conv_w, conv_b.reshape(1, HALF))


def _merge_kernel(h_ref, yp_ref, yc_ref, wg0_ref, wg1_ref, gb0_ref, gb1_ref,
                  wb0_ref, wb1_ref, o_ref):
    h = h_ref[...]
    g0 = _sigmoid(_dot(h, wg0_ref[0]) + gb0_ref[...])
    br0 = _dot(yp_ref[...], wb0_ref[0])
    g1 = _sigmoid(_dot(h, wg1_ref[0]) + gb1_ref[...])
    br1 = _dot(yc_ref[...], wb1_ref[0])
    o_ref[...] = (g0 * br0 + g1 * br1).astype(o_ref.dtype)


def _merge(h, y_pool, y_conv, w_in_blk, gate_b, w_branch_bf):
    m = h.shape[0]
    tm = ROW_TILE
    g0_first = 6 * N_HALF_BLOCKS
    g1_first = g0_first + N_MODEL_BLOCKS
    return pl.pallas_call(
        _merge_kernel,
        grid=(N_MODEL_BLOCKS, m // tm),
        in_specs=[
            pl.BlockSpec((tm, D_MODEL), lambda n, i: (i, 0)),
            pl.BlockSpec((tm, HALF), lambda n, i: (i, 0)),
            pl.BlockSpec((tm, HALF), lambda n, i: (i, 0)),
            pl.BlockSpec((1, D_MODEL, COL_BLOCK), lambda n, i: (g0_first + n, 0, 0)),
            pl.BlockSpec((1, D_MODEL, COL_BLOCK), lambda n, i: (g1_first + n, 0, 0)),
            pl.BlockSpec((1, COL_BLOCK), lambda n, i: (0, n)),
            pl.BlockSpec((1, COL_BLOCK), lambda n, i: (0, n)),
            pl.BlockSpec((1, HALF, COL_BLOCK), lambda n, i: (0, 0, n)),
            pl.BlockSpec((1, HALF, COL_BLOCK), lambda n, i: (1, 0, n)),
        ],
        out_specs=pl.BlockSpec((tm, COL_BLOCK), lambda n, i: (i, n)),
        out_shape=jax.ShapeDtypeStruct((m, D_MODEL), jnp.bfloat16),
        compiler_params=_params(2),
        name="branch_merge",
    )(h, y_pool, y_conv, w_in_blk, w_in_blk, gate_b[0:1], gate_b[1:2],
      w_branch_bf, w_branch_bf)


def _out_kernel(mg_ref, w_ref, x_ref, fw_ref, o_ref, sumsq):
    n = pl.program_id(1)
    n_blocks = pl.num_programs(1)
    tn = x_ref.shape[1]
    xn = x_ref[...] + _dot(mg_ref[...], w_ref[...])
    o_ref[:, pl.ds(pl.multiple_of(n * tn, tn), tn)] = xn
    part = jnp.sum(xn * xn, axis=-1, keepdims=True)
    sumsq[...] = jnp.where(n == 0, part, sumsq[...] + part)

    @pl.when(n == n_blocks - 1)
    def _():
        scale = lax.rsqrt(sumsq[...] * (1.0 / D_MODEL) + NORM_EPS)
        for k in range(D_MODEL // tn):
            cols = slice(k * tn, (k + 1) * tn)
            o_ref[:, cols] = o_ref[:, cols] * scale * fw_ref[:, cols]


def _out_proj(merged, w_out_bf, x2d, final_w):
    m = x2d.shape[0]
    tm = ROW_TILE
    tn = OUT_COL_BLOCK
    return pl.pallas_call(
        _out_kernel,
        grid=(m // tm, D_MODEL // tn),
        in_specs=[
            pl.BlockSpec((tm, D_MODEL), lambda i, n: (i, 0)),
            pl.BlockSpec((D_MODEL, tn), lambda i, n: (0, n)),
            pl.BlockSpec((tm, tn), lambda i, n: (i, n)),
            pl.BlockSpec((1, D_MODEL), lambda i, n: (0, 0)),
        ],
        out_specs=pl.BlockSpec((tm, D_MODEL), lambda i, n: (i, 0)),
        out_shape=jax.ShapeDtypeStruct((m, D_MODEL), jnp.float32),
        scratch_shapes=[pltpu.VMEM((tm, 1), jnp.float32)],
        compiler_params=_params(2),
        name="out_proj_norm",
    )(merged, w_out_bf, x2d, final_w.reshape(1, D_MODEL))


def _column_blocks_bf16(w, block):
    k, n = w.shape
    return w.reshape(k, n // block, block).transpose(1, 0, 2).astype(jnp.bfloat16)


def kernel(x, norm_w, w_in, pool_w, pool_scale, conv_w, conv_b, gate_b, w_branch, w_out,
           final_norm_w):
    b, s, d = x.shape
    assert (s, d) == (SEQ, D_MODEL) and norm_w.shape[0] == 1
    x2d = x.reshape(b * s, d)
    w_in_blk = _column_blocks_bf16(w_in[0], COL_BLOCK)
    pool_w_bf = pool_w[0].astype(jnp.bfloat16)
    w_branch_bf = w_branch[0].astype(jnp.bfloat16)
    w_out_bf = w_out[0].astype(jnp.bfloat16)

    h = _rmsnorm_bf16(x2d, norm_w[0])
    y_pool = _pool_mixer(h, w_in_blk, pool_w_bf, pool_scale[0])
    y_conv = _conv_mixer(h, w_in_blk, conv_w[0], conv_b[0])
    merged = _merge(h, y_pool, y_conv, w_in_blk, gate_b[0], w_branch_bf)
    out = _out_proj(merged, w_out_bf, x2d, final_norm_w)
    return out.reshape(b, s, d)
```

```python
import functools

import jax
import jax.numpy as jnp
from jax import lax
from jax.experimental import pallas as pl
from jax.experimental.pallas import tpu as pltpu

D_MODEL = 4096
SEQ = 4096
HALF = D_MODEL // 2
POOL_WINDOWS = (2, 4, 8, 16)
N_GROUPS = len(POOL_WINDOWS)
GROUP_DIM = HALF // N_GROUPS
CONV_K = 3
NORM_EPS = 1e-6

COL_BLOCK = GROUP_DIM
N_HALF_BLOCKS = HALF // COL_BLOCK
N_MODEL_BLOCKS = D_MODEL // COL_BLOCK
OUT_COL_BLOCK = 1024
ROW_TILE = 512
POOL_HALO = 16
POOL_PAD = 16
POOL_TOP = POOL_PAD + POOL_HALO
CONV_HALO = 8

VMEM_LIMIT_BYTES = 56 * 1024 * 1024


def _dot(a, b):
    return jnp.dot(a, b, preferred_element_type=jnp.float32)


def _silu(z):
    return z / (1.0 + jnp.exp(-z))


def _sigmoid(z):
    return 1.0 / (1.0 + jnp.exp(-z))


def _params(n_axes):
    return pltpu.CompilerParams(dimension_semantics=("arbitrary",) * n_axes,
                                vmem_limit_bytes=VMEM_LIMIT_BYTES)


def _rmsnorm_kernel(x_ref, w_ref, o_ref):
    x = x_ref[...]
    ms = jnp.mean(x * x, axis=-1, keepdims=True)
    o_ref[...] = (x * lax.rsqrt(ms + NORM_EPS) * w_ref[...]).astype(o_ref.dtype)


def _rmsnorm_bf16(x2d, w):
    m, d = x2d.shape
    tm = ROW_TILE
    return pl.pallas_call(
        _rmsnorm_kernel,
        grid=(m // tm,),
        in_specs=[pl.BlockSpec((tm, d), lambda i: (i, 0)),
                  pl.BlockSpec((1, d), lambda i: (0, 0))],
        out_specs=pl.BlockSpec((tm, d), lambda i: (i, 0)),
        out_shape=jax.ShapeDtypeStruct((m, d), jnp.bfloat16),
        compiler_params=_params(1),
        name="rmsnorm_in",
    )(x2d, w.reshape(1, d))


def _stage_chunks(jj, i, n_blocks, pairs):
    @pl.when(jj < n_blocks)
    def _():
        slot = jj % 2
        for chunk_ref, buf in pairs:
            rows, cols = chunk_ref.shape[-2:]
            r0 = pl.multiple_of(i * rows, rows)
            chunk = chunk_ref[...].reshape(rows, cols)
            buf[slot, pl.ds(r0, rows), :] = chunk.astype(buf.dtype)


def _resident_maps(n_blocks, n_tiles):
    def row(jj, i):
        return jnp.where(jj > 0, i, 0)

    def col(jj):
        return jnp.maximum(jj - 1, 0)

    def chunk(jj, i):
        return jnp.where(jj < n_blocks, i, n_tiles - 1)

    def staged(jj):
        return jnp.minimum(jj, n_blocks - 1)

    return row, col, chunk, staged


def _pool_kernel(h_ref, wu_ref, wz_ref, pw_ref, ps_ref, y_ref,
                 wu_buf, wz_buf, ubuf, s2buf, s4buf, s8buf, ucarry, *, tiles_per_seq):
    jj = pl.program_id(0)
    i = pl.program_id(1)
    tm = h_ref.shape[0]

    @pl.when((jj == 0) & (i == 0))
    def _():
        ucarry[...] = jnp.zeros_like(ucarry)

    _stage_chunks(jj, i, N_GROUPS, [(wu_ref, wu_buf), (wz_ref, wz_buf)])

    @pl.when(jj > 0)
    def _():
        g = jj - 1
        slot = g % 2
        seq_tile = i % tiles_per_seq
        end = POOL_TOP + tm
        h = h_ref[...]
        u = _dot(h, wu_buf[slot])
        z = _dot(h, wz_buf[slot])

        ubuf[0:POOL_PAD, :] = jnp.zeros((POOL_PAD, COL_BLOCK), jnp.float32)
        ubuf[POOL_PAD:POOL_TOP, :] = jnp.where(seq_tile != 0, ucarry[...], 0.0)
        ubuf[POOL_TOP:end, :] = u
        ucarry[...] = u[tm - POOL_HALO:, :]

        def shifted_sum(src, lo, shift):
            return src[lo:end, :] + src[lo - shift:end - shift, :]

        s2buf[8:end, :] = shifted_sum(ubuf, 8, 1)
        s4buf[16:end, :] = shifted_sum(s2buf, 16, 2)
        s8buf[24:end, :] = shifted_sum(s4buf, 24, 4)
        win = jnp.where(g == 0, s2buf[POOL_TOP:end, :],
                        jnp.where(g == 1, s4buf[POOL_TOP:end, :],
                                  jnp.where(g == 2, s8buf[POOL_TOP:end, :],
                                            shifted_sum(s8buf, POOL_TOP, 8))))
        window = lax.shift_left(jnp.int32(2), g)
        t1 = seq_tile * tm + 1 + lax.broadcasted_iota(jnp.int32, (tm, 1), 0)
        inv_count = 1.0 / jnp.minimum(t1, window).astype(jnp.float32)
        pooled = win * inv_count - u
        mixed = _dot(pooled.astype(jnp.bfloat16), pw_ref[0].astype(jnp.bfloat16))
        y_ref[...] = (mixed * ps_ref[...] * _silu(z)).astype(y_ref.dtype)


def _pool_mixer(h, w_in2d, pool_w, pool_scale):
    m = h.shape[0]
    tm = ROW_TILE
    nb = N_GROUPS
    n_tiles = m // tm
    chunk_rows = D_MODEL // n_tiles
    row, col, chunk, staged = _resident_maps(nb, n_tiles)
    w_chunk = lambda first: pl.BlockSpec((chunk_rows, COL_BLOCK),
                                         lambda jj, i: (chunk(jj, i), first + staged(jj)))
    w_buf = pltpu.VMEM((2, D_MODEL, COL_BLOCK), jnp.bfloat16)
    buf = pltpu.VMEM((POOL_TOP + tm, COL_BLOCK), jnp.float32)
    return pl.pallas_call(
        functools.partial(_pool_kernel, tiles_per_seq=SEQ // tm),
        grid=(nb + 1, n_tiles),
        in_specs=[
            pl.BlockSpec((tm, D_MODEL), lambda jj, i: (row(jj, i), 0)),
            w_chunk(0), w_chunk(nb),
            pl.BlockSpec((1, GROUP_DIM, GROUP_DIM), lambda jj, i: (col(jj), 0, 0)),
            pl.BlockSpec((1, COL_BLOCK), lambda jj, i: (0, col(jj))),
        ],
        out_specs=pl.BlockSpec((tm, COL_BLOCK), lambda jj, i: (row(jj, i), col(jj))),
        out_shape=jax.ShapeDtypeStruct((m, HALF), jnp.bfloat16),
        scratch_shapes=[w_buf, w_buf, buf, buf, buf, buf,
                        pltpu.VMEM((POOL_HALO, COL_BLOCK), jnp.float32)],
        compiler_params=_params(2),
        name="pool_mixer",
    )(h, w_in2d, w_in2d, pool_w, pool_scale.reshape(1, HALF))


def _conv_kernel(h_ref, wu_ref, wb_ref, wc_ref, wz_ref, cw_ref, cb_ref, y_ref,
                 wu_buf, wb_buf, wc_buf, wz_buf, vbuf, vcarry, *, tiles_per_seq):
    jj = pl.program_id(0)
    i = pl.program_id(1)
    tm = h_ref.shape[0]

    @pl.when((jj == 0) & (i == 0))
    def _():
        vcarry[...] = jnp.zeros_like(vcarry)

    _stage_chunks(jj, i, N_HALF_BLOCKS, [(wu_ref, wu_buf), (wb_ref, wb_buf),
                                         (wc_ref, wc_buf), (wz_ref, wz_buf)])

    @pl.when(jj > 0)
    def _():
        slot = (jj - 1) % 2
        seq_tile = i % tiles_per_seq
        h = h_ref[...]
        u = _dot(h, wu_buf[slot])
        c_gate = _dot(h, wc_buf[slot])
        v = c_gate * u
        vbuf[0:CONV_HALO, :] = jnp.where(seq_tile != 0, vcarry[...], 0.0)
        vbuf[CONV_HALO:CONV_HALO + tm, :] = v
        vcarry[...] = v[tm - CONV_HALO:, :]
        cw = cw_ref[...]
        y = (cb_ref[...]
             + cw[0:1, :] * vbuf[CONV_HALO - 2:CONV_HALO - 2 + tm, :]
             + cw[1:2, :] * vbuf[CONV_HALO - 1:CONV_HALO - 1 + tm, :]
             + cw[2:3, :] * v)
        b_gate = _dot(h, wb_buf[slot])
        z = _dot(h, wz_buf[slot])
        y_ref[...] = (b_gate * y * _silu(z)).astype(y_ref.dtype)


def _conv_mixer(h, w_in2d, conv_w, conv_b):
    m = h.shape[0]
    tm = ROW_TILE
    nb = N_HALF_BLOCKS
    n_tiles = m // tm
    chunk_rows = D_MODEL // n_tiles
    row, col, chunk, staged = _resident_maps(nb, n_tiles)
    w_chunk = lambda first: pl.BlockSpec((chunk_rows, COL_BLOCK),
                                         lambda jj, i: (chunk(jj, i), first + staged(jj)))
    w_buf = pltpu.VMEM((2, D_MODEL, COL_BLOCK), jnp.bfloat16)
    return pl.pallas_call(
        functools.partial(_conv_kernel, tiles_per_seq=SEQ // tm),
        grid=(nb + 1, n_tiles),
        in_specs=[
            pl.BlockSpec((tm, D_MODEL), lambda jj, i: (row(jj, i), 0)),
            w_chunk(2 * nb), w_chunk(3 * nb), w_chunk(4 * nb), w_chunk(5 * nb),
            pl.BlockSpec((CONV_K, COL_BLOCK), lambda jj, i: (0, col(jj))),
            pl.BlockSpec((1, COL_BLOCK), lambda jj, i: (0, col(jj))),
        ],
        out_specs=pl.BlockSpec((tm, COL_BLOCK), lambda jj, i: (row(jj, i), col(jj))),
        out_shape=jax.ShapeDtypeStruct((m, HALF), jnp.bfloat16),
        scratch_shapes=[w_buf, w_buf, w_buf, w_buf,
                        pltpu.VMEM((CONV_HALO + tm, COL_BLOCK), jnp.float32),
                        pltpu.VMEM((CONV_HALO, COL_BLOCK), jnp.float32)],
        compiler_params=_params(2),
        name="conv_mixer",
    )(h, w_in2d, w_in2d, w_in2d, w_in2d, conv_w, conv_b.reshape(1, HALF))


def _merge_kernel(h_ref, yp_ref, yc_ref, wg0_ref, wg1_ref, gb0_ref, gb1_ref,
                  wb0_ref, wb1_ref, o_ref, wg0_buf, wg1_buf, wb0_buf, wb1_buf):
    jj = pl.program_id(0)
    i = pl.program_id(1)
    _stage_chunks(jj, i, N_MODEL_BLOCKS, [(wg0_ref, wg0_buf), (wg1_ref, wg1_buf),
                                          (wb0_ref, wb0_buf), (wb1_ref, wb1_buf)])

    @pl.when(jj > 0)
    def _():
        slot = (jj - 1) % 2
        h = h_ref[...]
        g0 = _sigmoid(_dot(h, wg0_buf[slot]) + gb0_ref[...])
        br0 = _dot(yp_ref[...], wb0_buf[slot])
        g1 = _sigmoid(_dot(h, wg1_buf[slot]) + gb1_ref[...])
        br1 = _dot(yc_ref[...], wb1_buf[slot])
        o_ref[...] = (g0 * br0 + g1 * br1).astype(o_ref.dtype)


def _merge(h, y_pool, y_conv, w_in2d, gate_b, w_branch):
    m = h.shape[0]
    tm = ROW_TILE
    nb = N_MODEL_BLOCKS
    n_tiles = m // tm
    row, col, chunk, staged = _resident_maps(nb, n_tiles)
    g0_first = 6 * N_HALF_BLOCKS
    g1_first = g0_first + nb
    gate_chunk = lambda first: pl.BlockSpec(
        (D_MODEL // n_tiles, COL_BLOCK), lambda jj, i: (chunk(jj, i), first + staged(jj)))
    branch_chunk = lambda b: pl.BlockSpec(
        (1, HALF // n_tiles, COL_BLOCK), lambda jj, i: (b, chunk(jj, i), staged(jj)))
    tile = lambda width: pl.BlockSpec((tm, width), lambda jj, i: (row(jj, i), 0))
    bias = pl.BlockSpec((1, COL_BLOCK), lambda jj, i: (0, col(jj)))
    return pl.pallas_call(
        _merge_kernel,
        grid=(nb + 1, n_tiles),
        in_specs=[tile(D_MODEL), tile(HALF), tile(HALF),
                  gate_chunk(g0_first), gate_chunk(g1_first), bias, bias,
                  branch_chunk(0), branch_chunk(1)],
        out_specs=pl.BlockSpec((tm, COL_BLOCK), lambda jj, i: (row(jj, i), col(jj))),
        out_shape=jax.ShapeDtypeStruct((m, D_MODEL), jnp.bfloat16),
        scratch_shapes=[pltpu.VMEM((2, D_MODEL, COL_BLOCK), jnp.bfloat16),
                        pltpu.VMEM((2, D_MODEL, COL_BLOCK), jnp.bfloat16),
                        pltpu.VMEM((2, HALF, COL_BLOCK), jnp.bfloat16),
                        pltpu.VMEM((2, HALF, COL_BLOCK), jnp.bfloat16)],
        compiler_params=_params(2),
        name="branch_merge",
    )(h, y_pool, y_conv, w_in2d, w_in2d, gate_b[0:1], gate_b[1:2], w_branch, w_branch)


def _out_kernel(mg_ref, w_ref, x_ref, fw_ref, o_ref, sumsq):
    n = pl.program_id(1)
    n_blocks = pl.num_programs(1)
    tn = x_ref.shape[1]
    xn = x_ref[...] + _dot(mg_ref[...], w_ref[...])
    o_ref[:, pl.ds(pl.multiple_of(n * tn, tn), tn)] = xn
    part = jnp.sum(xn * xn, axis=-1, keepdims=True)
    sumsq[...] = jnp.where(n == 0, part, sumsq[...] + part)

    @pl.when(n == n_blocks - 1)
    def _():
        scale = lax.rsqrt(sumsq[...] * (1.0 / D_MODEL) + NORM_EPS)
        for k in range(D_MODEL // tn):
            cols = slice(k * tn, (k + 1) * tn)
            o_ref[:, cols] = o_ref[:, cols] * scale * fw_ref[:, cols]


def _out_proj(merged, w_out_bf, x2d, final_w):
    m = x2d.shape[0]
    tm = ROW_TILE
    tn = OUT_COL_BLOCK
    return pl.pallas_call(
        _out_kernel,
        grid=(m // tm, D_MODEL // tn),
        in_specs=[
            pl.BlockSpec((tm, D_MODEL), lambda i, n: (i, 0)),
            pl.BlockSpec((D_MODEL, tn), lambda i, n: (0, n)),
            pl.BlockSpec((tm, tn), lambda i, n: (i, n)),
            pl.BlockSpec((1, D_MODEL), lambda i, n: (0, 0)),
        ],
        out_specs=pl.BlockSpec((tm, D_MODEL), lambda i, n: (i, 0)),
        out_shape=jax.ShapeDtypeStruct((m, D_MODEL), jnp.float32),
        scratch_shapes=[pltpu.VMEM((tm, 1), jnp.float32)],
        compiler_params=_params(2),
        name="out_proj_norm",
    )(merged, w_out_bf, x2d, final_w.reshape(1, D_MODEL))


def kernel(x, norm_w, w_in, pool_w, pool_scale, conv_w, conv_b, gate_b, w_branch, w_out,
           final_norm_w):
    b, s, d = x.shape
    assert (s, d) == (SEQ, D_MODEL) and norm_w.shape[0] == 1
    x2d = x.reshape(b * s, d)
    w_in2d = w_in[0]

    h = _rmsnorm_bf16(x2d, norm_w[0])
    y_pool = _pool_mixer(h, w_in2d, pool_w[0], pool_scale[0])
    y_conv = _conv_mixer(h, w_in2d, conv_w[0], conv_b[0])
    merged = _merge(h, y_pool, y_conv, w_in2d, gate_b[0], w_branch[0])
    out = _out_proj(merged, w_out[0].astype(jnp.bfloat16), x2d, final_norm_w)
    return out.reshape(b, s, d)
```

```python
import functools

import jax
import jax.numpy as jnp
from jax import lax
from jax.experimental import pallas as pl
from jax.experimental.pallas import tpu as pltpu

D_MODEL = 4096
SEQ = 4096
HALF = D_MODEL // 2
POOL_WINDOWS = (2, 4, 8, 16)
N_GROUPS = len(POOL_WINDOWS)
GROUP_DIM = HALF // N_GROUPS
CONV_K = 3
NORM_EPS = 1e-6

COL_BLOCK = GROUP_DIM
N_HALF_BLOCKS = HALF // COL_BLOCK
N_MODEL_BLOCKS = D_MODEL // COL_BLOCK
OUT_COL_BLOCK = 1024
ROW_TILE = 512
POOL_HALO = 16
POOL_PAD = 16
POOL_TOP = POOL_PAD + POOL_HALO
CONV_HALO = 8

VMEM_LIMIT_BYTES = 56 * 1024 * 1024


def _dot(a, b):
    return jnp.dot(a, b, preferred_element_type=jnp.float32)


def _silu(z):
    return z / (1.0 + jnp.exp(-z))


def _sigmoid(z):
    return 1.0 / (1.0 + jnp.exp(-z))


def _params(n_axes):
    return pltpu.CompilerParams(dimension_semantics=("arbitrary",) * n_axes,
                                vmem_limit_bytes=VMEM_LIMIT_BYTES)


def _rmsnorm_kernel(x_ref, w_ref, o_ref):
    x = x_ref[...]
    ms = jnp.mean(x * x, axis=-1, keepdims=True)
    o_ref[...] = (x * lax.rsqrt(ms + NORM_EPS) * w_ref[...]).astype(o_ref.dtype)


def _rmsnorm_bf16(x2d, w):
    m, d = x2d.shape
    tm = ROW_TILE
    return pl.pallas_call(
        _rmsnorm_kernel,
        grid=(m // tm,),
        in_specs=[pl.BlockSpec((tm, d), lambda i: (i, 0)),
                  pl.BlockSpec((1, d), lambda i: (0, 0))],
        out_specs=pl.BlockSpec((tm, d), lambda i: (i, 0)),
        out_shape=jax.ShapeDtypeStruct((m, d), jnp.bfloat16),
        compiler_params=_params(1),
        name="rmsnorm_in",
    )(x2d, w.reshape(1, d))


def _stage_chunks(jj, i, pairs, fold=None):
    slot = jj % 2
    for k, (chunk_ref, buf) in enumerate(pairs):
        rows, cols = chunk_ref.shape[-2:]
        r0 = pl.multiple_of(i * rows, rows)
        chunk = chunk_ref[...].reshape(rows, cols).astype(buf.dtype)
        if k == 0 and fold is not None:
            chunk = _dot(chunk, fold).astype(buf.dtype)
        buf[slot, pl.ds(r0, rows), :] = chunk


def _staged_steps(jj, stage, compute):
    @pl.when(jj == 0)
    def _():
        stage()

    @pl.when(jj > 0)
    def _():
        compute()
        stage()


def _resident_maps(n_blocks, n_tiles):
    def row(jj, i):
        return jnp.where(jj > 0, i, 0)

    def col(jj):
        return jnp.maximum(jj - 1, 0)

    def chunk(jj, i):
        return jnp.where(jj < n_blocks, i, n_tiles - 1)

    def staged(jj):
        return jnp.minimum(jj, n_blocks - 1)

    return row, col, chunk, staged


def _pool_kernel(h_ref, wu_ref, wz_ref, pw_ref, ps_ref, y_ref,
                 wu_buf, wz_buf, ubuf, s2buf, s4buf, s8buf, mbuf, ucarry, *, tiles_per_seq):
    jj = pl.program_id(0)
    i = pl.program_id(1)
    tm = h_ref.shape[0]

    @pl.when((jj == 0) & (i == 0))
    def _():
        ucarry[...] = jnp.zeros_like(ucarry)

    def stage():
        _stage_chunks(jj, i, [(wu_ref, wu_buf), (wz_ref, wz_buf)],
                      fold=pw_ref[0].astype(jnp.bfloat16))

    def compute():
        g = jj - 1
        slot = g % 2
        seq_tile = i % tiles_per_seq
        end = POOL_TOP + tm
        h = h_ref[...]
        u = _dot(h, wu_buf[slot])
        z = _dot(h, wz_buf[slot])

        ubuf[0:POOL_PAD, :] = jnp.zeros((POOL_PAD, COL_BLOCK), jnp.float32)
        ubuf[POOL_PAD:POOL_TOP, :] = jnp.where(seq_tile != 0, ucarry[...], 0.0)
        ubuf[POOL_TOP:end, :] = u
        ucarry[...] = ubuf[end - POOL_HALO:end, :]

        def shifted_sum(src, lo, shift):
            return src[lo:end, :] + src[lo - shift:end - shift, :]

        s2buf[8:end, :] = shifted_sum(ubuf, 8, 1)
        s4buf[16:end, :] = shifted_sum(s2buf, 16, 2)
        s8buf[24:end, :] = shifted_sum(s4buf, 24, 4)
        win = jnp.where(g == 0, s2buf[POOL_TOP:end, :],
                        jnp.where(g == 1, s4buf[POOL_TOP:end, :],
                                  jnp.where(g == 2, s8buf[POOL_TOP:end, :],
                                            shifted_sum(s8buf, POOL_TOP, 8))))
        window = lax.shift_left(jnp.int32(2), g)
        t1 = seq_tile * tm + 1 + lax.broadcasted_iota(jnp.int32, (tm, 1), 0)
        inv_count = 1.0 / jnp.minimum(t1, window).astype(jnp.float32)
        mixed = win * inv_count - ubuf[POOL_TOP:end, :]
        mbuf[...] = mixed * ps_ref[...]
        y_ref[...] = (mbuf[...] * _silu(z)).astype(y_ref.dtype)

    _staged_steps(jj, stage, compute)


def _pool_mixer(h, w_in2d, pool_w, pool_scale):
    m = h.shape[0]
    tm = ROW_TILE
    nb = N_GROUPS
    n_tiles = m // tm
    chunk_rows = D_MODEL // n_tiles
    row, col, chunk, staged = _resident_maps(nb, n_tiles)
    w_chunk = lambda first: pl.BlockSpec((chunk_rows, COL_BLOCK),
                                         lambda jj, i: (chunk(jj, i), first + staged(jj)))
    w_buf = pltpu.VMEM((2, D_MODEL, COL_BLOCK), jnp.bfloat16)
    buf = pltpu.VMEM((POOL_TOP + tm, COL_BLOCK), jnp.float32)
    return pl.pallas_call(
        functools.partial(_pool_kernel, tiles_per_seq=SEQ // tm),
        grid=(nb + 1, n_tiles),
        in_specs=[
            pl.BlockSpec((tm, D_MODEL), lambda jj, i: (row(jj, i), 0)),
            w_chunk(0), w_chunk(nb),
            pl.BlockSpec((1, GROUP_DIM, GROUP_DIM), lambda jj, i: (staged(jj), 0, 0)),
            pl.BlockSpec((1, COL_BLOCK), lambda jj, i: (0, col(jj))),
        ],
        out_specs=pl.BlockSpec((tm, COL_BLOCK), lambda jj, i: (row(jj, i), col(jj))),
        out_shape=jax.ShapeDtypeStruct((m, HALF), jnp.bfloat16),
        scratch_shapes=[w_buf, w_buf, buf, buf, buf, buf,
                        pltpu.VMEM((tm, COL_BLOCK), jnp.float32),
                        pltpu.VMEM((POOL_HALO, COL_BLOCK), jnp.float32)],
        compiler_params=_params(2),
        name="pool_mixer",
    )(h, w_in2d, w_in2d, pool_w, pool_scale.reshape(1, HALF))


def _conv_kernel(h_ref, wu_ref, wb_ref, wc_ref, wz_ref, cw_ref, cb_ref, y_ref,
                 wu_buf, wb_buf, wc_buf, wz_buf, vbuf, vcarry, *, tiles_per_seq):
    jj = pl.program_id(0)
    i = pl.program_id(1)
    tm = h_ref.shape[0]

    @pl.when((jj == 0) & (i == 0))
    def _():
        vcarry[...] = jnp.zeros_like(vcarry)

    def stage():
        _stage_chunks(jj, i, [(wu_ref, wu_buf), (wb_ref, wb_buf),
                              (wc_ref, wc_buf), (wz_ref, wz_buf)])

    def compute():
        slot = (jj - 1) % 2
        seq_tile = i % tiles_per_seq
        h = h_ref[...]
        u = _dot(h, wu_buf[slot])
        c_gate = _dot(h, wc_buf[slot])
        v = c_gate * u
        vbuf[0:CONV_HALO, :] = jnp.where(seq_tile != 0, vcarry[...], 0.0)
        vbuf[CONV_HALO:CONV_HALO + tm, :] = v
        vcarry[...] = v[tm - CONV_HALO:, :]
        cw = cw_ref[...]
        y = (cb_ref[...]
             + cw[0:1, :] * vbuf[CONV_HALO - 2:CONV_HALO - 2 + tm, :]
             + cw[1:2, :] * vbuf[CONV_HALO - 1:CONV_HALO - 1 + tm, :]
             + cw[2:3, :] * v)
        b_gate = _dot(h, wb_buf[slot])
        z = _dot(h, wz_buf[slot])
        y_ref[...] = (b_gate * y * _silu(z)).astype(y_ref.dtype)

    _staged_steps(jj, stage, compute)


def _conv_mixer(h, w_in2d, conv_w, conv_b):
    m = h.shape[0]
    tm = ROW_TILE
    nb = N_HALF_BLOCKS
    n_tiles = m // tm
    chunk_rows = D_MODEL // n_tiles
    row, col, chunk, staged = _resident_maps(nb, n_tiles)
    w_chunk = lambda first: pl.BlockSpec((chunk_rows, COL_BLOCK),
                                         lambda jj, i: (chunk(jj, i), first + staged(jj)))
    w_buf = pltpu.VMEM((2, D_MODEL, COL_BLOCK), jnp.bfloat16)
    return pl.pallas_call(
        functools.partial(_conv_kernel, tiles_per_seq=SEQ // tm),
        grid=(nb + 1, n_tiles),
        in_specs=[
            pl.BlockSpec((tm, D_MODEL), lambda jj, i: (row(jj, i), 0)),
            w_chunk(2 * nb), w_chunk(3 * nb), w_chunk(4 * nb), w_chunk(5 * nb),
            pl.BlockSpec((CONV_K, COL_BLOCK), lambda jj, i: (0, col(jj))),
            pl.BlockSpec((1, COL_BLOCK), lambda jj, i: (0, col(jj))),
        ],
        out_specs=pl.BlockSpec((tm, COL_BLOCK), lambda jj, i: (row(jj, i), col(jj))),
        out_shape=jax.ShapeDtypeStruct((m, HALF), jnp.bfloat16),
        scratch_shapes=[w_buf, w_buf, w_buf, w_buf,
                        pltpu.VMEM((CONV_HALO + tm, COL_BLOCK), jnp.float32),
                        pltpu.VMEM((CONV_HALO, COL_BLOCK), jnp.float32)],
        compiler_params=_params(2),
        name="conv_mixer",
    )(h, w_in2d, w_in2d, w_in2d, w_in2d, conv_w, conv_b.reshape(1, HALF))


def _merge_kernel(h_ref, yp_ref, yc_ref, wg0_ref, wg1_ref, gb0_ref, gb1_ref,
                  wb0_ref, wb1_ref, wo_ref, o_ref, wo_bf_ref,
                  wg0_buf, wg1_buf, wb0_buf, wb1_buf):
    jj = pl.program_id(0)
    i = pl.program_id(1)

    def stage():
        _stage_chunks(jj, i, [(wg0_ref, wg0_buf), (wg1_ref, wg1_buf),
                              (wb0_ref, wb0_buf), (wb1_ref, wb1_buf)])

    def compute():
        slot = (jj - 1) % 2
        h = h_ref[...]
        g0 = _sigmoid(_dot(h, wg0_buf[slot]) + gb0_ref[...])
        br0 = _dot(yp_ref[...], wb0_buf[slot])
        g1 = _sigmoid(_dot(h, wg1_buf[slot]) + gb1_ref[...])
        br1 = _dot(yc_ref[...], wb1_buf[slot])
        o_ref[...] = (g0 * br0 + g1 * br1).astype(o_ref.dtype)
        wo_bf_ref[...] = wo_ref[...].astype(wo_bf_ref.dtype)

    _staged_steps(jj, stage, compute)


def _merge(h, y_pool, y_conv, w_in2d, gate_b, w_branch, w_out2d):
    m = h.shape[0]
    tm = ROW_TILE
    nb = N_MODEL_BLOCKS
    n_tiles = m // tm
    row, col, chunk, staged = _resident_maps(nb, n_tiles)
    g0_first = 6 * N_HALF_BLOCKS
    g1_first = g0_first + nb
    gate_chunk = lambda first: pl.BlockSpec(
        (D_MODEL // n_tiles, COL_BLOCK), lambda jj, i: (chunk(jj, i), first + staged(jj)))
    branch_chunk = lambda b: pl.BlockSpec(
        (1, HALF // n_tiles, COL_BLOCK), lambda jj, i: (b, chunk(jj, i), staged(jj)))
    tile = lambda width: pl.BlockSpec((tm, width), lambda jj, i: (row(jj, i), 0))
    bias = pl.BlockSpec((1, COL_BLOCK), lambda jj, i: (0, col(jj)))
    wo_rows = D_MODEL // (nb * n_tiles)
    wo_slice = pl.BlockSpec(
        (wo_rows, D_MODEL), lambda jj, i: (jnp.where(jj > 0, (jj - 1) * n_tiles + i, 0), 0))
    return pl.pallas_call(
        _merge_kernel,
        grid=(nb + 1, n_tiles),
        in_specs=[tile(D_MODEL), tile(HALF), tile(HALF),
                  gate_chunk(g0_first), gate_chunk(g1_first), bias, bias,
                  branch_chunk(0), branch_chunk(1), wo_slice],
        out_specs=[pl.BlockSpec((tm, COL_BLOCK), lambda jj, i: (row(jj, i), col(jj))),
                   wo_slice],
        out_shape=[jax.ShapeDtypeStruct((m, D_MODEL), jnp.bfloat16),
                   jax.ShapeDtypeStruct((D_MODEL, D_MODEL), jnp.bfloat16)],
        scratch_shapes=[pltpu.VMEM((2, D_MODEL, COL_BLOCK), jnp.bfloat16),
                        pltpu.VMEM((2, D_MODEL, COL_BLOCK), jnp.bfloat16),
                        pltpu.VMEM((2, HALF, COL_BLOCK), jnp.bfloat16),
                        pltpu.VMEM((2, HALF, COL_BLOCK), jnp.bfloat16)],
        compiler_params=_params(2),
        name="branch_merge",
    )(h, y_pool, y_conv, w_in2d, w_in2d, gate_b[0:1], gate_b[1:2], w_branch, w_branch,
      w_out2d)


def _out_kernel(mg_ref, w_ref, x_ref, fw_ref, o_ref, sumsq):
    n = pl.program_id(1)
    n_blocks = pl.num_programs(1)
    tn = x_ref.shape[1]
    xn = x_ref[...] + _dot(mg_ref[...], w_ref[...])
    o_ref[:, pl.ds(pl.multiple_of(n * tn, tn), tn)] = xn
    part = jnp.sum(xn * xn, axis=-1, keepdims=True)
    sumsq[...] = jnp.where(n == 0, part, sumsq[...] + part)

    @pl.when(n == n_blocks - 1)
    def _():
        scale = lax.rsqrt(sumsq[...] * (1.0 / D_MODEL) + NORM_EPS)
        for k in range(D_MODEL // tn):
            cols = slice(k * tn, (k + 1) * tn)
            o_ref[:, cols] = o_ref[:, cols] * scale * fw_ref[:, cols]


def _out_proj(merged, w_out_bf, x2d, final_w):
    m = x2d.shape[0]
    tm = ROW_TILE
    tn = OUT_COL_BLOCK
    return pl.pallas_call(
        _out_kernel,
        grid=(m // tm, D_MODEL // tn),
        in_specs=[
            pl.BlockSpec((tm, D_MODEL), lambda i, n: (i, 0)),
            pl.BlockSpec((D_MODEL, tn), lambda i, n: (0, n)),
            pl.BlockSpec((tm, tn), lambda i, n: (i, n)),
            pl.BlockSpec((1, D_MODEL), lambda i, n: (0, 0)),
        ],
        out_specs=pl.BlockSpec((tm, D_MODEL), lambda i, n: (i, 0)),
        out_shape=jax.ShapeDtypeStruct((m, D_MODEL), jnp.float32),
        scratch_shapes=[pltpu.VMEM((tm, 1), jnp.float32)],
        compiler_params=_params(2),
        name="out_proj_norm",
    )(merged, w_out_bf, x2d, final_w.reshape(1, D_MODEL))


def kernel(x, norm_w, w_in, pool_w, pool_scale, conv_w, conv_b, gate_b, w_branch, w_out,
           final_norm_w):
    b, s, d = x.shape
    assert (s, d) == (SEQ, D_MODEL) and norm_w.shape[0] == 1
    x2d = x.reshape(b * s, d)
    w_in2d = w_in[0]

    h = _rmsnorm_bf16(x2d, norm_w[0])
    y_pool = _pool_mixer(h, w_in2d, pool_w[0], pool_scale[0])
    y_conv = _conv_mixer(h, w_in2d, conv_w[0], conv_b[0])
    merged, w_out_bf = _merge(h, y_pool, y_conv, w_in2d, gate_b[0], w_branch[0], w_out[0])
    out = _out_proj(merged, w_out_bf, x2d, final_norm_w)
    return out.reshape(b, s, d)
```

```python
import functools

import jax
import jax.numpy as jnp
from jax import lax
from jax.experimental import pallas as pl
from jax.experimental.pallas import tpu as pltpu

D_MODEL = 4096
SEQ = 4096
HALF = D_MODEL // 2
POOL_WINDOWS = (2, 4, 8, 16)
N_GROUPS = len(POOL_WINDOWS)
GROUP_DIM = HALF // N_GROUPS
CONV_K = 3
NORM_EPS = 1e-6

COL_BLOCK = GROUP_DIM
N_HALF_BLOCKS = HALF // COL_BLOCK
N_MODEL_BLOCKS = D_MODEL // COL_BLOCK
OUT_COL_BLOCK = 512
OUT_ROW_TILE = 1024
ROW_TILE = 512
POOL_HALO = 16
POOL_PAD = 16
POOL_TOP = POOL_PAD + POOL_HALO
CONV_HALO = 8

VMEM_LIMIT_BYTES = 56 * 1024 * 1024


def _dot(a, b):
    return jnp.dot(a, b, preferred_element_type=jnp.float32)


def _silu(z):
    return z / (1.0 + jnp.exp(-z))


def _sigmoid(z):
    return 1.0 / (1.0 + jnp.exp(-z))


def _params(n_axes):
    return pltpu.CompilerParams(dimension_semantics=("arbitrary",) * n_axes,
                                vmem_limit_bytes=VMEM_LIMIT_BYTES)


def _rmsnorm_kernel(x_ref, w_ref, o_ref):
    x = x_ref[...]
    ms = jnp.mean(x * x, axis=-1, keepdims=True)
    o_ref[...] = (x * lax.rsqrt(ms + NORM_EPS) * w_ref[...]).astype(o_ref.dtype)


def _rmsnorm_bf16(x2d, w):
    m, d = x2d.shape
    tm = ROW_TILE
    return pl.pallas_call(
        _rmsnorm_kernel,
        grid=(m // tm,),
        in_specs=[pl.BlockSpec((tm, d), lambda i: (i, 0)),
                  pl.BlockSpec((1, d), lambda i: (0, 0))],
        out_specs=pl.BlockSpec((tm, d), lambda i: (i, 0)),
        out_shape=jax.ShapeDtypeStruct((m, d), jnp.bfloat16),
        compiler_params=_params(1),
        name="rmsnorm_in",
    )(x2d, w.reshape(1, d))


def _stage_chunks(jj, i, pairs, fold=None):
    slot = jj % 2
    for k, (chunk_ref, buf) in enumerate(pairs):
        rows, cols = chunk_ref.shape[-2:]
        r0 = pl.multiple_of(i * rows, rows)
        chunk = chunk_ref[...].reshape(rows, cols).astype(buf.dtype)
        if k == 0 and fold is not None:
            chunk = _dot(chunk, fold).astype(buf.dtype)
        buf[slot, pl.ds(r0, rows), :] = chunk


def _staged_steps(jj, stage, compute):
    @pl.when(jj == 0)
    def _():
        stage()

    @pl.when(jj > 0)
    def _():
        compute()
        stage()


def _resident_maps(n_blocks, n_tiles):
    def row(jj, i):
        return jnp.where(jj > 0, i, 0)

    def col(jj):
        return jnp.maximum(jj - 1, 0)

    def chunk(jj, i):
        return jnp.where(jj < n_blocks, i, n_tiles - 1)

    def staged(jj):
        return jnp.minimum(jj, n_blocks - 1)

    return row, col, chunk, staged


def _pool_kernel(h_ref, wu_ref, wz_ref, pw_ref, ps_ref, y_ref,
                 wu_buf, wz_buf, ubuf, s2buf, s4buf, s8buf, mbuf, ucarry, *, tiles_per_seq):
    jj = pl.program_id(0)
    i = pl.program_id(1)
    tm = h_ref.shape[0]

    @pl.when((jj == 0) & (i == 0))
    def _():
        ucarry[...] = jnp.zeros_like(ucarry)

    def stage():
        _stage_chunks(jj, i, [(wu_ref, wu_buf), (wz_ref, wz_buf)],
                      fold=pw_ref[0].astype(jnp.bfloat16))

    def compute():
        g = jj - 1
        slot = g % 2
        seq_tile = i % tiles_per_seq
        end = POOL_TOP + tm
        h = h_ref[...]
        u = _dot(h, wu_buf[slot])
        z = _dot(h, wz_buf[slot])

        ubuf[0:POOL_PAD, :] = jnp.zeros((POOL_PAD, COL_BLOCK), jnp.float32)
        ubuf[POOL_PAD:POOL_TOP, :] = jnp.where(seq_tile != 0, ucarry[...], 0.0)
        ubuf[POOL_TOP:end, :] = u
        ucarry[...] = ubuf[end - POOL_HALO:end, :]

        def shifted_sum(src, lo, shift):
            return src[lo:end, :] + src[lo - shift:end - shift, :]

        s2buf[8:end, :] = shifted_sum(ubuf, 8, 1)
        s4buf[16:end, :] = shifted_sum(s2buf, 16, 2)
        s8buf[24:end, :] = shifted_sum(s4buf, 24, 4)
        win = jnp.where(g == 0, s2buf[POOL_TOP:end, :],
                        jnp.where(g == 1, s4buf[POOL_TOP:end, :],
                                  jnp.where(g == 2, s8buf[POOL_TOP:end, :],
                                            shifted_sum(s8buf, POOL_TOP, 8))))
        window = lax.shift_left(jnp.int32(2), g)
        t1 = seq_tile * tm + 1 + lax.broadcasted_iota(jnp.int32, (tm, 1), 0)
        inv_count = 1.0 / jnp.minimum(t1, window).astype(jnp.float32)
        mixed = win * inv_count - ubuf[POOL_TOP:end, :]
        mbuf[...] = mixed * ps_ref[...]
        y_ref[...] = (mbuf[...] * _silu(z)).astype(y_ref.dtype)

    _staged_steps(jj, stage, compute)


def _pool_mixer(h, w_in2d, pool_w, pool_scale):
    m = h.shape[0]
    tm = ROW_TILE
    nb = N_GROUPS
    n_tiles = m // tm
    chunk_rows = D_MODEL // n_tiles
    row, col, chunk, staged = _resident_maps(nb, n_tiles)
    w_chunk = lambda first: pl.BlockSpec((chunk_rows, COL_BLOCK),
                                         lambda jj, i: (chunk(jj, i), first + staged(jj)))
    w_buf = pltpu.VMEM((2, D_MODEL, COL_BLOCK), jnp.bfloat16)
    buf = pltpu.VMEM((POOL_TOP + tm, COL_BLOCK), jnp.float32)
    return pl.pallas_call(
        functools.partial(_pool_kernel, tiles_per_seq=SEQ // tm),
        grid=(nb + 1, n_tiles),
        in_specs=[
            pl.BlockSpec((tm, D_MODEL), lambda jj, i: (row(jj, i), 0)),
            w_chunk(0), w_chunk(nb),
            pl.BlockSpec((1, GROUP_DIM, GROUP_DIM), lambda jj, i: (staged(jj), 0, 0)),
            pl.BlockSpec((1, COL_BLOCK), lambda jj, i: (0, col(jj))),
        ],
        out_specs=pl.BlockSpec((tm, COL_BLOCK), lambda jj, i: (row(jj, i), col(jj))),
        out_shape=jax.ShapeDtypeStruct((m, HALF), jnp.bfloat16),
        scratch_shapes=[w_buf, w_buf, buf, buf, buf, buf,
                        pltpu.VMEM((tm, COL_BLOCK), jnp.float32),
                        pltpu.VMEM((POOL_HALO, COL_BLOCK), jnp.float32)],
        compiler_params=_params(2),
        name="pool_mixer",
    )(h, w_in2d, w_in2d, pool_w, pool_scale.reshape(1, HALF))


def _conv_kernel(h_ref, wu_ref, wb_ref, wc_ref, wz_ref, cw_ref, cb_ref, y_ref,
                 wu_buf, wb_buf, wc_buf, wz_buf, vbuf, vcarry, *, tiles_per_seq):
    jj = pl.program_id(0)
    i = pl.program_id(1)
    tm = h_ref.shape[0]

    @pl.when((jj == 0) & (i == 0))
    def _():
        vcarry[...] = jnp.zeros_like(vcarry)

    def stage():
        _stage_chunks(jj, i, [(wu_ref, wu_buf), (wb_ref, wb_buf),
                              (wc_ref, wc_buf), (wz_ref, wz_buf)])

    def compute():
        slot = (jj - 1) % 2
        seq_tile = i % tiles_per_seq
        h = h_ref[...]
        u = _dot(h, wu_buf[slot])
        c_gate = _dot(h, wc_buf[slot])
        vbuf[0:CONV_HALO, :] = jnp.where(seq_tile != 0, vcarry[...], 0.0)
        vbuf[CONV_HALO:CONV_HALO + tm, :] = c_gate * u
        vcarry[...] = vbuf[tm:CONV_HALO + tm, :]
        cw = cw_ref[...]
        y = (cb_ref[...]
             + cw[0:1, :] * vbuf[CONV_HALO - 2:CONV_HALO - 2 + tm, :]
             + cw[1:2, :] * vbuf[CONV_HALO - 1:CONV_HALO - 1 + tm, :]
             + cw[2:3, :] * vbuf[CONV_HALO:CONV_HALO + tm, :])
        gated = y * _silu(_dot(h, wz_buf[slot]))
        y_ref[...] = (_dot(h, wb_buf[slot]) * gated).astype(y_ref.dtype)

    _staged_steps(jj, stage, compute)


def _conv_mixer(h, w_in2d, conv_w, conv_b):
    m = h.shape[0]
    tm = ROW_TILE
    nb = N_HALF_BLOCKS
    n_tiles = m // tm
    chunk_rows = D_MODEL // n_tiles
    row, col, chunk, staged = _resident_maps(nb, n_tiles)
    w_chunk = lambda first: pl.BlockSpec((chunk_rows, COL_BLOCK),
                                         lambda jj, i: (chunk(jj, i), first + staged(jj)))
    w_buf = pltpu.VMEM((2, D_MODEL, COL_BLOCK), jnp.bfloat16)
    return pl.pallas_call(
        functools.partial(_conv_kernel, tiles_per_seq=SEQ // tm),
        grid=(nb + 1, n_tiles),
        in_specs=[
            pl.BlockSpec((tm, D_MODEL), lambda jj, i: (row(jj, i), 0)),
            w_chunk(2 * nb), w_chunk(3 * nb), w_chunk(4 * nb), w_chunk(5 * nb),
            pl.BlockSpec((CONV_K, COL_BLOCK), lambda jj, i: (0, col(jj))),
            pl.BlockSpec((1, COL_BLOCK), lambda jj, i: (0, col(jj))),
        ],
        out_specs=pl.BlockSpec((tm, COL_BLOCK), lambda jj, i: (row(jj, i), col(jj))),
        out_shape=jax.ShapeDtypeStruct((m, HALF), jnp.bfloat16),
        scratch_shapes=[w_buf, w_buf, w_buf, w_buf,
                        pltpu.VMEM((CONV_HALO + tm, COL_BLOCK), jnp.float32),
                        pltpu.VMEM((CONV_HALO, COL_BLOCK), jnp.float32)],
        compiler_params=_params(2),
        name="conv_mixer",
    )(h, w_in2d, w_in2d, w_in2d, w_in2d, conv_w, conv_b.reshape(1, HALF))


def _merge_kernel(h_ref, yp_ref, yc_ref, wg0_ref, wg1_ref, gb0_ref, gb1_ref,
                  wb0_ref, wb1_ref, wo_ref, o_ref, wo_bf_ref,
                  wg0_buf, wg1_buf, wb0_buf, wb1_buf):
    jj = pl.program_id(0)
    i = pl.program_id(1)

    def stage():
        _stage_chunks(jj, i, [(wg0_ref, wg0_buf), (wg1_ref, wg1_buf),
                              (wb0_ref, wb0_buf), (wb1_ref, wb1_buf)])

    def compute():
        slot = (jj - 1) % 2
        h = h_ref[...]
        g0 = _sigmoid(_dot(h, wg0_buf[slot]) + gb0_ref[...])
        br0 = _dot(yp_ref[...], wb0_buf[slot])
        g1 = _sigmoid(_dot(h, wg1_buf[slot]) + gb1_ref[...])
        br1 = _dot(yc_ref[...], wb1_buf[slot])
        o_ref[...] = (g0 * br0 + g1 * br1).astype(o_ref.dtype)
        wo_bf_ref[...] = wo_ref[...].astype(wo_bf_ref.dtype)

    _staged_steps(jj, stage, compute)


def _merge(h, y_pool, y_conv, w_in2d, gate_b, w_branch, w_out2d):
    m = h.shape[0]
    tm = ROW_TILE
    nb = N_MODEL_BLOCKS
    n_tiles = m // tm
    row, col, chunk, staged = _resident_maps(nb, n_tiles)
    g0_first = 6 * N_HALF_BLOCKS
    g1_first = g0_first + nb
    gate_chunk = lambda first: pl.BlockSpec(
        (D_MODEL // n_tiles, COL_BLOCK), lambda jj, i: (chunk(jj, i), first + staged(jj)))
    branch_chunk = lambda b: pl.BlockSpec(
        (1, HALF // n_tiles, COL_BLOCK), lambda jj, i: (b, chunk(jj, i), staged(jj)))
    tile = lambda width: pl.BlockSpec((tm, width), lambda jj, i: (row(jj, i), 0))
    bias = pl.BlockSpec((1, COL_BLOCK), lambda jj, i: (0, col(jj)))
    wo_rows = D_MODEL // (nb * n_tiles)
    wo_slice = pl.BlockSpec(
        (wo_rows, D_MODEL), lambda jj, i: (jnp.where(jj > 0, (jj - 1) * n_tiles + i, 0), 0))
    return pl.pallas_call(
        _merge_kernel,
        grid=(nb + 1, n_tiles),
        in_specs=[tile(D_MODEL), tile(HALF), tile(HALF),
                  gate_chunk(g0_first), gate_chunk(g1_first), bias, bias,
                  branch_chunk(0), branch_chunk(1), wo_slice],
        out_specs=[pl.BlockSpec((tm, COL_BLOCK), lambda jj, i: (row(jj, i), col(jj))),
                   wo_slice],
        out_shape=[jax.ShapeDtypeStruct((m, D_MODEL), jnp.bfloat16),
                   jax.ShapeDtypeStruct((D_MODEL, D_MODEL), jnp.bfloat16)],
        scratch_shapes=[pltpu.VMEM((2, D_MODEL, COL_BLOCK), jnp.bfloat16),
                        pltpu.VMEM((2, D_MODEL, COL_BLOCK), jnp.bfloat16),
                        pltpu.VMEM((2, HALF, COL_BLOCK), jnp.bfloat16),
                        pltpu.VMEM((2, HALF, COL_BLOCK), jnp.bfloat16)],
        compiler_params=_params(2),
        name="branch_merge",
    )(h, y_pool, y_conv, w_in2d, w_in2d, gate_b[0:1], gate_b[1:2], w_branch, w_branch,
      w_out2d)


def _out_kernel(mg_ref, w_ref, x_ref, fw_ref, o_ref, rows, sumsq):
    i = pl.program_id(0)
    n = pl.program_id(1)
    n_tiles = pl.num_programs(0) - 1

    def emit_previous():
        scale = lax.rsqrt(sumsq[(i + 1) % 2] * (1.0 / D_MODEL) + NORM_EPS)
        o_ref[...] = rows[n] * scale * fw_ref[...]

    def project():
        xn = x_ref[...] + _dot(mg_ref[...], w_ref[...])
        rows[n] = xn
        part = jnp.sum(xn * xn, axis=-1, keepdims=True)
        cur = i % 2
        sumsq[cur] = jnp.where(n == 0, part, sumsq[cur] + part)

    @pl.when(i == 0)
    def _():
        project()

    @pl.when((i > 0) & (i < n_tiles))
    def _():
        emit_previous()
        project()

    @pl.when(i == n_tiles)
    def _():
        emit_previous()


def _out_proj(merged, w_out_bf, x2d, final_w):
    m = x2d.shape[0]
    tm = OUT_ROW_TILE
    tn = OUT_COL_BLOCK
    n_tiles = m // tm
    n_blocks = D_MODEL // tn
    last = n_tiles - 1
    in_row = lambda i: jnp.minimum(i, last)
    in_col = lambda i, n: jnp.where(i < n_tiles, n, n_blocks - 1)
    return pl.pallas_call(
        _out_kernel,
        grid=(n_tiles + 1, n_blocks),
        in_specs=[
            pl.BlockSpec((tm, D_MODEL), lambda i, n: (in_row(i), 0)),
            pl.BlockSpec((D_MODEL, tn), lambda i, n: (0, in_col(i, n))),
            pl.BlockSpec((tm, tn), lambda i, n: (in_row(i), in_col(i, n))),
            pl.BlockSpec((1, tn), lambda i, n: (0, n)),
        ],
        out_specs=pl.BlockSpec(
            (tm, tn), lambda i, n: (jnp.maximum(i - 1, 0), jnp.where(i > 0, n, 0))),
        out_shape=jax.ShapeDtypeStruct((m, D_MODEL), jnp.float32),
        scratch_shapes=[pltpu.VMEM((n_blocks, tm, tn), jnp.float32),
                        pltpu.VMEM((2, tm, 1), jnp.float32)],
        compiler_params=_params(2),
        name="out_proj_norm",
    )(merged, w_out_bf, x2d, final_w.reshape(1, D_MODEL))


def kernel(x, norm_w, w_in, pool_w, pool_scale, conv_w, conv_b, gate_b, w_branch, w_out,
           final_norm_w):
    b, s, d = x.shape
    assert (s, d) == (SEQ, D_MODEL) and norm_w.shape[0] == 1
    x2d = x.reshape(b * s, d)
    w_in2d = w_in[0]

    h = _rmsnorm_bf16(x2d, norm_w[0])
    y_pool = _pool_mixer(h, w_in2d, pool_w[0], pool_scale[0])
    y_conv = _conv_mixer(h, w_in2d, conv_w[0], conv_b[0])
    merged, w_out_bf = _merge(h, y_pool, y_conv, w_in2d, gate_b[0], w_branch[0], w_out[0])
    out = _out_proj(merged, w_out_bf, x2d, final_norm_w)
    return out.reshape(b, s, d)
```

```python
import functools

import jax
import jax.numpy as jnp
from jax import lax
from jax.experimental import pallas as pl
from jax.experimental.pallas import tpu as pltpu

D_MODEL = 4096
SEQ = 4096
HALF = D_MODEL // 2
POOL_WINDOWS = (2, 4, 8, 16)
N_GROUPS = len(POOL_WINDOWS)
GROUP_DIM = HALF // N_GROUPS
CONV_K = 3
NORM_EPS = 1e-6

COL_BLOCK = GROUP_DIM
N_HALF_BLOCKS = HALF // COL_BLOCK
N_MODEL_BLOCKS = D_MODEL // COL_BLOCK
OUT_COL_BLOCK = 512
OUT_ROW_TILE = 1024
ROW_TILE = 512
POOL_ROW_TILE = 1024
POOL_HALO = 16
POOL_PAD = 16
POOL_TOP = POOL_PAD + POOL_HALO
CONV_HALO = 8

VMEM_LIMIT_BYTES = 60 * 1024 * 1024


def _dot(a, b):
    return jnp.dot(a, b, preferred_element_type=jnp.float32)


def _silu(z):
    return z / (1.0 + jnp.exp(-z))


def _sigmoid(z):
    return 1.0 / (1.0 + jnp.exp(-z))


def _params(n_axes):
    return pltpu.CompilerParams(dimension_semantics=("arbitrary",) * n_axes,
                                vmem_limit_bytes=VMEM_LIMIT_BYTES)


def _rmsnorm_kernel(x_ref, w_ref, o_ref):
    x = x_ref[...]
    ms = jnp.mean(x * x, axis=-1, keepdims=True)
    o_ref[...] = (x * lax.rsqrt(ms + NORM_EPS) * w_ref[...]).astype(o_ref.dtype)


def _rmsnorm_bf16(x2d, w):
    m, d = x2d.shape
    tm = ROW_TILE
    return pl.pallas_call(
        _rmsnorm_kernel,
        grid=(m // tm,),
        in_specs=[pl.BlockSpec((tm, d), lambda i: (i, 0)),
                  pl.BlockSpec((1, d), lambda i: (0, 0))],
        out_specs=pl.BlockSpec((tm, d), lambda i: (i, 0)),
        out_shape=jax.ShapeDtypeStruct((m, d), jnp.bfloat16),
        compiler_params=_params(1),
        name="rmsnorm_in",
    )(x2d, w.reshape(1, d))


def _stage_chunks(slot, i, pairs, fold=None):
    for k, (chunk_ref, buf) in enumerate(pairs):
        rows, cols = chunk_ref.shape[-2:]
        r0 = pl.multiple_of(i * rows, rows)
        chunk = chunk_ref[...].reshape(rows, cols).astype(buf.dtype)
        if k == 0 and fold is not None:
            chunk = _dot(chunk, fold).astype(buf.dtype)
        buf[slot, pl.ds(r0, rows), :] = chunk


def _staged_steps(jj, stage, compute):
    @pl.when(jj == 0)
    def _():
        stage()

    @pl.when(jj > 0)
    def _():
        compute()
        stage()


def _resident_maps(n_blocks, n_tiles):
    def row(jj, i):
        return jnp.where(jj > 0, i, 0)

    def col(jj):
        return jnp.maximum(jj - 1, 0)

    def chunk(jj, i):
        return jnp.where(jj < n_blocks, i, n_tiles - 1)

    def staged(jj):
        return jnp.minimum(jj, n_blocks - 1)

    return row, col, chunk, staged


def _prestaged_maps(n_blocks, n_tiles):
    def chunk(jj, i):
        return jnp.where(jj + 1 < n_blocks, i, n_tiles - 1)

    def staged(jj):
        return jnp.minimum(jj + 1, n_blocks - 1)

    return chunk, staged


def _load_prestaged(first_step, pairs, sems):
    @pl.when(first_step)
    def _():
        copies = [pltpu.make_async_copy(src, buf.at[0], sems.at[k])
                  for k, (src, buf) in enumerate(pairs)]
        for cp in copies:
            cp.start()
        for cp in copies:
            cp.wait()


def _cast_rows(pairs):
    for src_ref, dst_ref in pairs:
        dst_ref[...] = src_ref[...].astype(dst_ref.dtype)


def _pool_kernel(h_ref, wu_ref, wz_ref, pw_ref, ps_ref, n0_ref, n1_ref, n2_ref, n3_ref,
                 y_ref, p0_ref, p1_ref, p2_ref, p3_ref,
                 wu_buf, wz_buf, ubuf, s2buf, s4buf, s8buf, ucarry, *, tiles_per_seq):
    jj = pl.program_id(0)
    i = pl.program_id(1)
    tm = h_ref.shape[0]

    @pl.when((jj == 0) & (i == 0))
    def _():
        ucarry[...] = jnp.zeros_like(ucarry)

    def stage():
        _stage_chunks(jj % 2, i, [(wu_ref, wu_buf), (wz_ref, wz_buf)],
                      fold=pw_ref[0].astype(jnp.bfloat16))

    def compute():
        _cast_rows([(n0_ref, p0_ref), (n1_ref, p1_ref), (n2_ref, p2_ref), (n3_ref, p3_ref)])
        g = jj - 1
        slot = g % 2
        seq_tile = i % tiles_per_seq
        end = POOL_TOP + tm
        h = h_ref[...]
        u = _dot(h, wu_buf[slot])
        z = _dot(h, wz_buf[slot])

        ubuf[0:POOL_PAD, :] = jnp.zeros((POOL_PAD, COL_BLOCK), jnp.float32)
        ubuf[POOL_PAD:POOL_TOP, :] = jnp.where(seq_tile != 0, ucarry[...], 0.0)
        ubuf[POOL_TOP:end, :] = u
        ucarry[...] = ubuf[end - POOL_HALO:end, :]

        def shifted_sum(src, lo, shift):
            return src[lo:end, :] + src[lo - shift:end - shift, :]

        s2buf[8:end, :] = shifted_sum(ubuf, 8, 1)
        s4buf[16:end, :] = shifted_sum(s2buf, 16, 2)
        s8buf[24:end, :] = shifted_sum(s4buf, 24, 4)
        win = jnp.where(g == 0, s2buf[POOL_TOP:end, :],
                        jnp.where(g == 1, s4buf[POOL_TOP:end, :],
                                  jnp.where(g == 2, s8buf[POOL_TOP:end, :],
                                            shifted_sum(s8buf, POOL_TOP, 8))))
        window = lax.shift_left(jnp.int32(2), g)
        t1 = seq_tile * tm + 1 + lax.broadcasted_iota(jnp.int32, (tm, 1), 0)
        inv_count = 1.0 / jnp.minimum(t1, window).astype(jnp.float32)
        mixed = win * inv_count - ubuf[POOL_TOP:end, :]
        ubuf[POOL_TOP:end, :] = mixed * ps_ref[...]
        y_ref[...] = (ubuf[POOL_TOP:end, :] * _silu(z)).astype(y_ref.dtype)

    _staged_steps(jj, stage, compute)


def _pool_mixer(h, w_in2d, pool_w, pool_scale, next_first_blocks):
    m = h.shape[0]
    tm = POOL_ROW_TILE
    nb = N_GROUPS
    n_tiles = m // tm
    chunk_rows = D_MODEL // n_tiles
    row, col, chunk, staged = _resident_maps(nb, n_tiles)
    w_chunk = lambda first: pl.BlockSpec((chunk_rows, COL_BLOCK),
                                         lambda jj, i: (chunk(jj, i), first + staged(jj)))
    side_rows = D_MODEL // (nb * n_tiles)
    side_step = lambda jj, i: jnp.where(jj > 0, (jj - 1) * n_tiles + i, 0)
    side_in = lambda blk: pl.BlockSpec((side_rows, COL_BLOCK),
                                       lambda jj, i: (side_step(jj, i), blk))
    side_out = pl.BlockSpec((side_rows, COL_BLOCK), lambda jj, i: (side_step(jj, i), 0))
    w_buf = pltpu.VMEM((2, D_MODEL, COL_BLOCK), jnp.bfloat16)
    buf = pltpu.VMEM((POOL_TOP + tm, COL_BLOCK), jnp.float32)
    n_side = len(next_first_blocks)
    outs = pl.pallas_call(
        functools.partial(_pool_kernel, tiles_per_seq=SEQ // tm),
        grid=(nb + 1, n_tiles),
        in_specs=[
            pl.BlockSpec((tm, D_MODEL), lambda jj, i: (row(jj, i), 0)),
            w_chunk(0), w_chunk(nb),
            pl.BlockSpec((1, GROUP_DIM, GROUP_DIM), lambda jj, i: (staged(jj), 0, 0)),
            pl.BlockSpec((1, COL_BLOCK), lambda jj, i: (0, col(jj))),
        ] + [side_in(blk) for blk in next_first_blocks],
        out_specs=[pl.BlockSpec((tm, COL_BLOCK), lambda jj, i: (row(jj, i), col(jj)))]
        + [side_out] * n_side,
        out_shape=[jax.ShapeDtypeStruct((m, HALF), jnp.bfloat16)]
        + [jax.ShapeDtypeStruct((D_MODEL, COL_BLOCK), jnp.bfloat16)] * n_side,
        scratch_shapes=[w_buf, w_buf, buf, buf, buf, buf,
                        pltpu.VMEM((POOL_HALO, COL_BLOCK), jnp.float32)],
        compiler_params=_params(2),
        name="pool_mixer",
    )(h, w_in2d, w_in2d, pool_w, pool_scale.reshape(1, HALF), *([w_in2d] * n_side))
    return outs[0], outs[1:]


def _conv_kernel(h_ref, wu_ref, wb_ref, wc_ref, wz_ref, cw_ref, cb_ref,
                 wu0_ref, wb0_ref, wc0_ref, wz0_ref, n0_ref, n1_ref, n2_ref, n3_ref,
                 y_ref, p0_ref, p1_ref, p2_ref, p3_ref,
                 wu_buf, wb_buf, wc_buf, wz_buf, vbuf, vcarry, sems, *, tiles_per_seq):
    jj = pl.program_id(0)
    i = pl.program_id(1)
    tm = h_ref.shape[0]
    first_step = (jj == 0) & (i == 0)

    @pl.when(first_step)
    def _():
        vcarry[...] = jnp.zeros_like(vcarry)

    _load_prestaged(first_step, [(wu0_ref, wu_buf), (wb0_ref, wb_buf),
                                 (wc0_ref, wc_buf), (wz0_ref, wz_buf)], sems)

    def stage():
        _stage_chunks((jj + 1) % 2, i, [(wu_ref, wu_buf), (wb_ref, wb_buf),
                                        (wc_ref, wc_buf), (wz_ref, wz_buf)])

    def compute():
        _cast_rows([(n0_ref, p0_ref), (n1_ref, p1_ref), (n2_ref, p2_ref), (n3_ref, p3_ref)])
        slot = jj % 2
        seq_tile = i % tiles_per_seq
        h = h_ref[...]
        u = _dot(h, wu_buf[slot])
        c_gate = _dot(h, wc_buf[slot])
        vbuf[0:CONV_HALO, :] = jnp.where(seq_tile != 0, vcarry[...], 0.0)
        vbuf[CONV_HALO:CONV_HALO + tm, :] = c_gate * u
        vcarry[...] = vbuf[tm:CONV_HALO + tm, :]
        cw = cw_ref[...]
        y = (cb_ref[...]
             + cw[0:1, :] * vbuf[CONV_HALO - 2:CONV_HALO - 2 + tm, :]
             + cw[1:2, :] * vbuf[CONV_HALO - 1:CONV_HALO - 1 + tm, :]
             + cw[2:3, :] * vbuf[CONV_HALO:CONV_HALO + tm, :])
        gated = y * _silu(_dot(h, wz_buf[slot]))
        y_ref[...] = (_dot(h, wb_buf[slot]) * gated).astype(y_ref.dtype)

    compute()
    stage()


def _conv_mixer(h, w_in2d, conv_w, conv_b, first_blocks, w_branch):
    m = h.shape[0]
    tm = ROW_TILE
    nb = N_HALF_BLOCKS
    n_tiles = m // tm
    chunk_rows = D_MODEL // n_tiles
    chunk, staged = _prestaged_maps(nb, n_tiles)
    w_chunk = lambda first: pl.BlockSpec((chunk_rows, COL_BLOCK),
                                         lambda jj, i: (chunk(jj, i), first + staged(jj)))
    hbm = pl.BlockSpec(memory_space=pl.ANY)
    n_steps = nb * n_tiles
    step = lambda jj, i: jj * n_tiles + i
    gate_rows = D_MODEL // n_steps
    branch_rows = HALF // n_steps
    gate_in = lambda blk: pl.BlockSpec((gate_rows, COL_BLOCK), lambda jj, i: (step(jj, i), blk))
    branch_in = lambda b: pl.BlockSpec((1, branch_rows, COL_BLOCK),
                                       lambda jj, i: (b, step(jj, i), 0))
    gate_out = pl.BlockSpec((gate_rows, COL_BLOCK), lambda jj, i: (step(jj, i), 0))
    branch_out = pl.BlockSpec((1, branch_rows, COL_BLOCK), lambda jj, i: (0, step(jj, i), 0))
    g0_first = 6 * N_HALF_BLOCKS
    g1_first = g0_first + N_MODEL_BLOCKS
    w_buf = pltpu.VMEM((2, D_MODEL, COL_BLOCK), jnp.bfloat16)
    outs = pl.pallas_call(
        functools.partial(_conv_kernel, tiles_per_seq=SEQ // tm),
        grid=(nb, n_tiles),
        in_specs=[
            pl.BlockSpec((tm, D_MODEL), lambda jj, i: (i, 0)),
            w_chunk(2 * nb), w_chunk(3 * nb), w_chunk(4 * nb), w_chunk(5 * nb),
            pl.BlockSpec((CONV_K, COL_BLOCK), lambda jj, i: (0, jj)),
            pl.BlockSpec((1, COL_BLOCK), lambda jj, i: (0, jj)),
            hbm, hbm, hbm, hbm,
            gate_in(g0_first), gate_in(g1_first), branch_in(0), branch_in(1),
        ],
        out_specs=[pl.BlockSpec((tm, COL_BLOCK), lambda jj, i: (i, jj)),
                   gate_out, gate_out, branch_out, branch_out],
        out_shape=[jax.ShapeDtypeStruct((m, HALF), jnp.bfloat16),
                   jax.ShapeDtypeStruct((D_MODEL, COL_BLOCK), jnp.bfloat16),
                   jax.ShapeDtypeStruct((D_MODEL, COL_BLOCK), jnp.bfloat16),
                   jax.ShapeDtypeStruct((1, HALF, COL_BLOCK), jnp.bfloat16),
                   jax.ShapeDtypeStruct((1, HALF, COL_BLOCK), jnp.bfloat16)],
        scratch_shapes=[w_buf, w_buf, w_buf, w_buf,
                        pltpu.VMEM((CONV_HALO + tm, COL_BLOCK), jnp.float32),
                        pltpu.VMEM((CONV_HALO, COL_BLOCK), jnp.float32),
                        pltpu.SemaphoreType.DMA((4,))],
        compiler_params=_params(2),
        name="conv_mixer",
    )(h, w_in2d, w_in2d, w_in2d, w_in2d, conv_w, conv_b.reshape(1, HALF),
      *first_blocks, w_in2d, w_in2d, w_branch, w_branch)
    return outs[0], outs[1:]


def _merge_kernel(h_ref, yp_ref, yc_ref, wg0_ref, wg1_ref, gb0_ref, gb1_ref,
                  wb0_ref, wb1_ref, wo_ref, wg00_ref, wg10_ref, wb00_ref, wb10_ref,
                  o_ref, wo_bf_ref, wg0_buf, wg1_buf, wb0_buf, wb1_buf, sems):
    jj = pl.program_id(0)
    i = pl.program_id(1)

    _load_prestaged((jj == 0) & (i == 0),
                    [(wg00_ref, wg0_buf), (wg10_ref, wg1_buf),
                     (wb00_ref.at[0], wb0_buf), (wb10_ref.at[0], wb1_buf)], sems)

    def stage():
        _stage_chunks((jj + 1) % 2, i, [(wg0_ref, wg0_buf), (wg1_ref, wg1_buf),
                                        (wb0_ref, wb0_buf), (wb1_ref, wb1_buf)])

    def compute():
        slot = jj % 2
        h = h_ref[...]
        g0 = _sigmoid(_dot(h, wg0_buf[slot]) + gb0_ref[...])
        br0 = _dot(yp_ref[...], wb0_buf[slot])
        g1 = _sigmoid(_dot(h, wg1_buf[slot]) + gb1_ref[...])
        br1 = _dot(yc_ref[...], wb1_buf[slot])
        o_ref[...] = (g0 * br0 + g1 * br1).astype(o_ref.dtype)
        wo_bf_ref[...] = wo_ref[...].astype(wo_bf_ref.dtype)

    compute()
    stage()


def _merge(h, y_pool, y_conv, w_in2d, gate_b, w_branch, w_out2d, first_blocks):
    m = h.shape[0]
    tm = ROW_TILE
    nb = N_MODEL_BLOCKS
    n_tiles = m // tm
    chunk, staged = _prestaged_maps(nb, n_tiles)
    g0_first = 6 * N_HALF_BLOCKS
    g1_first = g0_first + nb
    gate_chunk = lambda first: pl.BlockSpec(
        (D_MODEL // n_tiles, COL_BLOCK), lambda jj, i: (chunk(jj, i), first + staged(jj)))
    branch_chunk = lambda b: pl.BlockSpec(
        (1, HALF // n_tiles, COL_BLOCK), lambda jj, i: (b, chunk(jj, i), staged(jj)))
    tile = lambda width: pl.BlockSpec((tm, width), lambda jj, i: (i, 0))
    bias = pl.BlockSpec((1, COL_BLOCK), lambda jj, i: (0, jj))
    wo_rows = D_MODEL // (nb * n_tiles)
    wo_slice = pl.BlockSpec((wo_rows, D_MODEL), lambda jj, i: (jj * n_tiles + i, 0))
    hbm = pl.BlockSpec(memory_space=pl.ANY)
    return pl.pallas_call(
        _merge_kernel,
        grid=(nb, n_tiles),
        in_specs=[tile(D_MODEL), tile(HALF), tile(HALF),
                  gate_chunk(g0_first), gate_chunk(g1_first), bias, bias,
                  branch_chunk(0), branch_chunk(1), wo_slice, hbm, hbm, hbm, hbm],
        out_specs=[pl.BlockSpec((tm, COL_BLOCK), lambda jj, i: (i, jj)), wo_slice],
        out_shape=[jax.ShapeDtypeStruct((m, D_MODEL), jnp.bfloat16),
                   jax.ShapeDtypeStruct((D_MODEL, D_MODEL), jnp.bfloat16)],
        scratch_shapes=[pltpu.VMEM((2, D_MODEL, COL_BLOCK), jnp.bfloat16),
                        pltpu.VMEM((2, D_MODEL, COL_BLOCK), jnp.bfloat16),
                        pltpu.VMEM((2, HALF, COL_BLOCK), jnp.bfloat16),
                        pltpu.VMEM((2, HALF, COL_BLOCK), jnp.bfloat16),
                        pltpu.SemaphoreType.DMA((4,))],
        compiler_params=_params(2),
        name="branch_merge",
    )(h, y_pool, y_conv, w_in2d, w_in2d, gate_b[0:1], gate_b[1:2], w_branch, w_branch,
      w_out2d, *first_blocks)


def _out_kernel(mg_ref, w_ref, x_ref, fw_ref, o_ref, rows, sumsq):
    i = pl.program_id(0)
    n = pl.program_id(1)
    n_tiles = pl.num_programs(0) - 1

    def emit_previous():
        scale = lax.rsqrt(sumsq[(i + 1) % 2] * (1.0 / D_MODEL) + NORM_EPS)
        o_ref[...] = rows[n] * scale * fw_ref[...]

    def project():
        xn = x_ref[...] + _dot(mg_ref[...], w_ref[...])
        rows[n] = xn
        part = jnp.sum(xn * xn, axis=-1, keepdims=True)
        cur = i % 2
        sumsq[cur] = jnp.where(n == 0, part, sumsq[cur] + part)

    @pl.when(i == 0)
    def _():
        project()

    @pl.when((i > 0) & (i < n_tiles))
    def _():
        emit_previous()
        project()

    @pl.when(i == n_tiles)
    def _():
        emit_previous()


def _out_proj(merged, w_out_bf, x2d, final_w):
    m = x2d.shape[0]
    tm = OUT_ROW_TILE
    tn = OUT_COL_BLOCK
    n_tiles = m // tm
    n_blocks = D_MODEL // tn
    last = n_tiles - 1
    in_row = lambda i: jnp.minimum(i, last)
    in_col = lambda i, n: jnp.where(i < n_tiles, n, n_blocks - 1)
    return pl.pallas_call(
        _out_kernel,
        grid=(n_tiles + 1, n_blocks),
        in_specs=[
            pl.BlockSpec((tm, D_MODEL), lambda i, n: (in_row(i), 0)),
            pl.BlockSpec((D_MODEL, tn), lambda i, n: (0, in_col(i, n))),
            pl.BlockSpec((tm, tn), lambda i, n: (in_row(i), in_col(i, n))),
            pl.BlockSpec((1, tn), lambda i, n: (0, n)),
        ],
        out_specs=pl.BlockSpec(
            (tm, tn), lambda i, n: (jnp.maximum(i - 1, 0), jnp.where(i > 0, n, 0))),
        out_shape=jax.ShapeDtypeStruct((m, D_MODEL), jnp.float32),
        scratch_shapes=[pltpu.VMEM((n_blocks, tm, tn), jnp.float32),
                        pltpu.VMEM((2, tm, 1), jnp.float32)],
        compiler_params=_params(2),
        name="out_proj_norm",
    )(merged, w_out_bf, x2d, final_w.reshape(1, D_MODEL))


def kernel(x, norm_w, w_in, pool_w, pool_scale, conv_w, conv_b, gate_b, w_branch, w_out,
           final_norm_w):
    b, s, d = x.shape
    assert (s, d) == (SEQ, D_MODEL) and norm_w.shape[0] == 1
    x2d = x.reshape(b * s, d)
    w_in2d = w_in[0]

    nb = N_HALF_BLOCKS
    conv_first_cols = (2 * nb, 3 * nb, 4 * nb, 5 * nb)
    h = _rmsnorm_bf16(x2d, norm_w[0])
    y_pool, conv_first = _pool_mixer(h, w_in2d, pool_w[0], pool_scale[0], conv_first_cols)
    y_conv, merge_first = _conv_mixer(h, w_in2d, conv_w[0], conv_b[0], conv_first, w_branch[0])
    merged, w_out_bf = _merge(h, y_pool, y_conv, w_in2d, gate_b[0], w_branch[0], w_out[0],
                              merge_first)
    out = _out_proj(merged, w_out_bf, x2d, final_norm_w)
    return out.reshape(b, s, d)
```

```python
import functools

import jax
import jax.numpy as jnp
from jax import lax
from jax.experimental import pallas as pl
from jax.experimental.pallas import tpu as pltpu

D_MODEL = 4096
SEQ = 4096
HALF = D_MODEL // 2
POOL_WINDOWS = (2, 4, 8, 16)
N_GROUPS = len(POOL_WINDOWS)
GROUP_DIM = HALF // N_GROUPS
CONV_K = 3
NORM_EPS = 1e-6

COL_BLOCK = GROUP_DIM
N_HALF_BLOCKS = HALF // COL_BLOCK
N_MODEL_BLOCKS = D_MODEL // COL_BLOCK
OUT_COL_BLOCK = 512
OUT_ROW_TILE = 1024
ROW_TILE = 512
POOL_ROW_TILE = 1024
POOL_HALO = 16
POOL_PAD = 16
POOL_TOP = POOL_PAD + POOL_HALO
CONV_HALO = 8

VMEM_LIMIT_BYTES = 62 * 1024 * 1024


def _dot(a, b):
    return jnp.dot(a, b, preferred_element_type=jnp.float32)


def _silu(z):
    return z / (1.0 + jnp.exp(-z))


def _sigmoid(z):
    return 1.0 / (1.0 + jnp.exp(-z))


def _params(n_axes):
    return pltpu.CompilerParams(dimension_semantics=("arbitrary",) * n_axes,
                                vmem_limit_bytes=VMEM_LIMIT_BYTES)


def _rmsnorm_kernel(x_ref, w_ref, o_ref):
    x = x_ref[...]
    ms = jnp.mean(x * x, axis=-1, keepdims=True)
    o_ref[...] = (x * lax.rsqrt(ms + NORM_EPS) * w_ref[...]).astype(o_ref.dtype)


def _rmsnorm_bf16(x2d, w):
    m, d = x2d.shape
    tm = ROW_TILE
    return pl.pallas_call(
        _rmsnorm_kernel,
        grid=(m // tm,),
        in_specs=[pl.BlockSpec((tm, d), lambda i: (i, 0)),
                  pl.BlockSpec((1, d), lambda i: (0, 0))],
        out_specs=pl.BlockSpec((tm, d), lambda i: (i, 0)),
        out_shape=jax.ShapeDtypeStruct((m, d), jnp.bfloat16),
        compiler_params=_params(1),
        name="rmsnorm_in",
    )(x2d, w.reshape(1, d))


def _stage_chunks(slot, i, pairs, fold=None):
    for k, (chunk_ref, buf) in enumerate(pairs):
        rows, cols = chunk_ref.shape[-2:]
        r0 = pl.multiple_of(i * rows, rows)
        chunk = chunk_ref[...].reshape(rows, cols).astype(buf.dtype)
        if k == 0 and fold is not None:
            chunk = _dot(chunk, fold).astype(buf.dtype)
        buf[slot, pl.ds(r0, rows), :] = chunk


def _staged_steps(jj, stage, compute):
    @pl.when(jj == 0)
    def _():
        stage()

    @pl.when(jj > 0)
    def _():
        compute()
        stage()


def _resident_maps(n_blocks, n_tiles):
    def row(jj, i):
        return jnp.where(jj > 0, i, 0)

    def col(jj):
        return jnp.maximum(jj - 1, 0)

    def chunk(jj, i):
        return jnp.where(jj < n_blocks, i, n_tiles - 1)

    def staged(jj):
        return jnp.minimum(jj, n_blocks - 1)

    return row, col, chunk, staged


def _prestaged_maps(n_blocks, n_tiles):
    def chunk(jj, i):
        return jnp.where(jj + 1 < n_blocks, i, n_tiles - 1)

    def staged(jj):
        return jnp.minimum(jj + 1, n_blocks - 1)

    return chunk, staged


def _load_prestaged(first_step, pairs, sems):
    @pl.when(first_step)
    def _():
        copies = [pltpu.make_async_copy(src, buf.at[0], sems.at[k])
                  for k, (src, buf) in enumerate(pairs)]
        for cp in copies:
            cp.start()
        for cp in copies:
            cp.wait()


def _cast_rows(pairs):
    for src_ref, dst_ref in pairs:
        dst_ref[...] = src_ref[...].astype(dst_ref.dtype)


def _pool_kernel(h_ref, wu_ref, wz_ref, pw_ref, ps_ref, n0_ref, n1_ref, n2_ref, n3_ref,
                 y_ref, p0_ref, p1_ref, p2_ref, p3_ref,
                 wu_buf, wz_buf, ubuf, s2buf, s4buf, s8buf, ucarry, *, tiles_per_seq):
    jj = pl.program_id(0)
    i = pl.program_id(1)
    tm = h_ref.shape[0]

    @pl.when((jj == 0) & (i == 0))
    def _():
        ucarry[...] = jnp.zeros_like(ucarry)

    def stage():
        _stage_chunks(jj % 2, i, [(wu_ref, wu_buf), (wz_ref, wz_buf)],
                      fold=pw_ref[0].astype(jnp.bfloat16))

    def compute():
        _cast_rows([(n0_ref, p0_ref), (n1_ref, p1_ref), (n2_ref, p2_ref), (n3_ref, p3_ref)])
        g = jj - 1
        slot = g % 2
        seq_tile = i % tiles_per_seq
        end = POOL_TOP + tm
        h = h_ref[...]
        u = _dot(h, wu_buf[slot])
        z = _dot(h, wz_buf[slot])

        ubuf[0:POOL_PAD, :] = jnp.zeros((POOL_PAD, COL_BLOCK), jnp.float32)
        ubuf[POOL_PAD:POOL_TOP, :] = jnp.where(seq_tile != 0, ucarry[...], 0.0)
        ubuf[POOL_TOP:end, :] = u
        ucarry[...] = ubuf[end - POOL_HALO:end, :]

        def shifted_sum(src, lo, shift):
            return src[lo:end, :] + src[lo - shift:end - shift, :]

        s2buf[8:end, :] = shifted_sum(ubuf, 8, 1)
        s4buf[16:end, :] = shifted_sum(s2buf, 16, 2)
        s8buf[24:end, :] = shifted_sum(s4buf, 24, 4)
        win = jnp.where(g == 0, s2buf[POOL_TOP:end, :],
                        jnp.where(g == 1, s4buf[POOL_TOP:end, :],
                                  jnp.where(g == 2, s8buf[POOL_TOP:end, :],
                                            shifted_sum(s8buf, POOL_TOP, 8))))
        window = lax.shift_left(jnp.int32(2), g)
        t1 = seq_tile * tm + 1 + lax.broadcasted_iota(jnp.int32, (tm, 1), 0)
        inv_count = 1.0 / jnp.minimum(t1, window).astype(jnp.float32)
        mixed = win * inv_count - ubuf[POOL_TOP:end, :]
        ubuf[POOL_TOP:end, :] = mixed * ps_ref[...]
        y_ref[...] = (ubuf[POOL_TOP:end, :] * _silu(z)).astype(y_ref.dtype)

    _staged_steps(jj, stage, compute)


def _pool_mixer(h, w_in2d, pool_w, pool_scale, next_first_blocks):
    m = h.shape[0]
    tm = POOL_ROW_TILE
    nb = N_GROUPS
    n_tiles = m // tm
    chunk_rows = D_MODEL // n_tiles
    row, col, chunk, staged = _resident_maps(nb, n_tiles)
    w_chunk = lambda first: pl.BlockSpec((chunk_rows, COL_BLOCK),
                                         lambda jj, i: (chunk(jj, i), first + staged(jj)))
    side_rows = D_MODEL // (nb * n_tiles)
    side_step = lambda jj, i: jnp.where(jj > 0, (jj - 1) * n_tiles + i, 0)
    side_in = lambda blk: pl.BlockSpec((side_rows, COL_BLOCK),
                                       lambda jj, i: (side_step(jj, i), blk))
    side_out = pl.BlockSpec((side_rows, COL_BLOCK), lambda jj, i: (side_step(jj, i), 0))
    w_buf = pltpu.VMEM((2, D_MODEL, COL_BLOCK), jnp.bfloat16)
    buf = pltpu.VMEM((POOL_TOP + tm, COL_BLOCK), jnp.float32)
    n_side = len(next_first_blocks)
    outs = pl.pallas_call(
        functools.partial(_pool_kernel, tiles_per_seq=SEQ // tm),
        grid=(nb + 1, n_tiles),
        in_specs=[
            pl.BlockSpec((tm, D_MODEL), lambda jj, i: (row(jj, i), 0)),
            w_chunk(0), w_chunk(nb),
            pl.BlockSpec((1, GROUP_DIM, GROUP_DIM), lambda jj, i: (staged(jj), 0, 0)),
            pl.BlockSpec((1, COL_BLOCK), lambda jj, i: (0, col(jj))),
        ] + [side_in(blk) for blk in next_first_blocks],
        out_specs=[pl.BlockSpec((tm, COL_BLOCK), lambda jj, i: (row(jj, i), col(jj)))]
        + [side_out] * n_side,
        out_shape=[jax.ShapeDtypeStruct((m, HALF), jnp.bfloat16)]
        + [jax.ShapeDtypeStruct((D_MODEL, COL_BLOCK), jnp.bfloat16)] * n_side,
        scratch_shapes=[w_buf, w_buf, buf, buf, buf, buf,
                        pltpu.VMEM((POOL_HALO, COL_BLOCK), jnp.float32)],
        compiler_params=_params(2),
        name="pool_mixer",
    )(h, w_in2d, w_in2d, pool_w, pool_scale.reshape(1, HALF), *([w_in2d] * n_side))
    return outs[0], outs[1:]


def _conv_kernel(h_ref, wu_ref, wb_ref, wc_ref, wz_ref, cw_ref, cb_ref,
                 wu0_ref, wb0_ref, wc0_ref, wz0_ref, n0_ref, n1_ref, n2_ref, n3_ref, n4_ref,
                 y_ref, p0_ref, p1_ref, p2_ref, p3_ref, p4_ref,
                 wu_buf, wb_buf, wc_buf, wz_buf, vbuf, vcarry, sems, *, tiles_per_seq):
    jj = pl.program_id(0)
    i = pl.program_id(1)
    tm = h_ref.shape[0]
    first_step = (jj == 0) & (i == 0)

    @pl.when(first_step)
    def _():
        vcarry[...] = jnp.zeros_like(vcarry)

    _load_prestaged(first_step, [(wu0_ref, wu_buf), (wb0_ref, wb_buf),
                                 (wc0_ref, wc_buf), (wz0_ref, wz_buf)], sems)

    def stage():
        _stage_chunks((jj + 1) % 2, i, [(wu_ref, wu_buf), (wb_ref, wb_buf),
                                        (wc_ref, wc_buf), (wz_ref, wz_buf)])

    def compute():
        _cast_rows([(n0_ref, p0_ref), (n1_ref, p1_ref), (n2_ref, p2_ref), (n3_ref, p3_ref),
                    (n4_ref, p4_ref)])
        slot = jj % 2
        seq_tile = i % tiles_per_seq
        h = h_ref[...]
        u = _dot(h, wu_buf[slot])
        c_gate = _dot(h, wc_buf[slot])
        vbuf[0:CONV_HALO, :] = jnp.where(seq_tile != 0, vcarry[...], 0.0)
        vbuf[CONV_HALO:CONV_HALO + tm, :] = c_gate * u
        vcarry[...] = vbuf[tm:CONV_HALO + tm, :]
        cw = cw_ref[...]
        y = (cb_ref[...]
             + cw[0:1, :] * vbuf[CONV_HALO - 2:CONV_HALO - 2 + tm, :]
             + cw[1:2, :] * vbuf[CONV_HALO - 1:CONV_HALO - 1 + tm, :]
             + cw[2:3, :] * vbuf[CONV_HALO:CONV_HALO + tm, :])
        gated = y * _silu(_dot(h, wz_buf[slot]))
        y_ref[...] = (_dot(h, wb_buf[slot]) * gated).astype(y_ref.dtype)

    compute()
    stage()


def _conv_mixer(h, w_in2d, conv_w, conv_b, first_blocks, w_branch, w_out2d):
    m = h.shape[0]
    tm = ROW_TILE
    nb = N_HALF_BLOCKS
    n_tiles = m // tm
    chunk_rows = D_MODEL // n_tiles
    chunk, staged = _prestaged_maps(nb, n_tiles)
    w_chunk = lambda first: pl.BlockSpec((chunk_rows, COL_BLOCK),
                                         lambda jj, i: (chunk(jj, i), first + staged(jj)))
    hbm = pl.BlockSpec(memory_space=pl.ANY)
    n_steps = nb * n_tiles
    step = lambda jj, i: jj * n_tiles + i
    gate_rows = D_MODEL // n_steps
    branch_rows = HALF // n_steps
    gate_in = lambda blk: pl.BlockSpec((gate_rows, COL_BLOCK), lambda jj, i: (step(jj, i), blk))
    branch_in = lambda b: pl.BlockSpec((1, branch_rows, COL_BLOCK),
                                       lambda jj, i: (b, step(jj, i), 0))
    gate_out = pl.BlockSpec((gate_rows, COL_BLOCK), lambda jj, i: (step(jj, i), 0))
    branch_out = pl.BlockSpec((1, branch_rows, COL_BLOCK), lambda jj, i: (0, step(jj, i), 0))
    g0_first = 6 * N_HALF_BLOCKS
    g1_first = g0_first + N_MODEL_BLOCKS
    wo_slice = pl.BlockSpec((D_MODEL // n_steps, D_MODEL), lambda jj, i: (step(jj, i), 0))
    w_buf = pltpu.VMEM((2, D_MODEL, COL_BLOCK), jnp.bfloat16)
    outs = pl.pallas_call(
        functools.partial(_conv_kernel, tiles_per_seq=SEQ // tm),
        grid=(nb, n_tiles),
        in_specs=[
            pl.BlockSpec((tm, D_MODEL), lambda jj, i: (i, 0)),
            w_chunk(2 * nb), w_chunk(3 * nb), w_chunk(4 * nb), w_chunk(5 * nb),
            pl.BlockSpec((CONV_K, COL_BLOCK), lambda jj, i: (0, jj)),
            pl.BlockSpec((1, COL_BLOCK), lambda jj, i: (0, jj)),
            hbm, hbm, hbm, hbm,
            gate_in(g0_first), gate_in(g1_first), branch_in(0), branch_in(1), wo_slice,
        ],
        out_specs=[pl.BlockSpec((tm, COL_BLOCK), lambda jj, i: (i, jj)),
                   gate_out, gate_out, branch_out, branch_out, wo_slice],
        out_shape=[jax.ShapeDtypeStruct((m, HALF), jnp.bfloat16),
                   jax.ShapeDtypeStruct((D_MODEL, COL_BLOCK), jnp.bfloat16),
                   jax.ShapeDtypeStruct((D_MODEL, COL_BLOCK), jnp.bfloat16),
                   jax.ShapeDtypeStruct((1, HALF, COL_BLOCK), jnp.bfloat16),
                   jax.ShapeDtypeStruct((1, HALF, COL_BLOCK), jnp.bfloat16),
                   jax.ShapeDtypeStruct((D_MODEL, D_MODEL), jnp.bfloat16)],
        scratch_shapes=[w_buf, w_buf, w_buf, w_buf,
                        pltpu.VMEM((CONV_HALO + tm, COL_BLOCK), jnp.float32),
                        pltpu.VMEM((CONV_HALO, COL_BLOCK), jnp.float32),
                        pltpu.SemaphoreType.DMA((4,))],
        compiler_params=_params(2),
        name="conv_mixer",
    )(h, w_in2d, w_in2d, w_in2d, w_in2d, conv_w, conv_b.reshape(1, HALF),
      *first_blocks, w_in2d, w_in2d, w_branch, w_branch, w_out2d)
    return outs[0], outs[1:5], outs[5]


def _merge_kernel(gb0_ref, gb1_ref, h_hbm, yp_hbm, yc_hbm, win_hbm, wbr_hbm,
                  wg00_ref, wg10_ref, wb00_ref, wb10_ref, o_hbm,
                  wg0_buf, wg1_buf, wb0_buf, wb1_buf, hbuf, ypbuf, ycbuf, obuf,
                  cg0, cg1, cb0, cb1, sem_first, sem_in, sem_out, sem_ch,
                  *, n_pairs, g0_first, g1_first):
    jj = pl.program_id(0)
    n_blocks = pl.num_programs(0)
    tm = hbuf.shape[1]
    n_tiles = 2 * n_pairs
    n_in_slots = hbuf.shape[0]
    g_rows = cg0.shape[1]
    b_rows = cb0.shape[1]

    def rows_of(tile):
        return pl.ds(pl.multiple_of((tile % n_tiles) * tm, tm), tm)

    def cols_of(block):
        return pl.ds(pl.multiple_of(block * COL_BLOCK, COL_BLOCK), COL_BLOCK)

    def in_copies(tile):
        s = tile % n_in_slots
        rows = rows_of(tile)
        return [pltpu.make_async_copy(h_hbm.at[rows, :], hbuf.at[s], sem_in.at[3 * s]),
                pltpu.make_async_copy(yp_hbm.at[rows, :], ypbuf.at[s], sem_in.at[3 * s + 1]),
                pltpu.make_async_copy(yc_hbm.at[rows, :], ycbuf.at[s], sem_in.at[3 * s + 2])]

    def out_copy(o, tile):
        return pltpu.make_async_copy(obuf.at[o], o_hbm.at[rows_of(tile), cols_of(jj)],
                                     sem_out.at[o])

    def chunk_copies(cs, block, k):
        g = pl.ds(pl.multiple_of(k * g_rows, g_rows), g_rows)
        b = pl.ds(pl.multiple_of(k * b_rows, b_rows), b_rows)
        return [
            pltpu.make_async_copy(win_hbm.at[g, cols_of(g0_first + block)], cg0.at[cs],
                                  sem_ch.at[4 * cs]),
            pltpu.make_async_copy(win_hbm.at[g, cols_of(g1_first + block)], cg1.at[cs],
                                  sem_ch.at[4 * cs + 1]),
            pltpu.make_async_copy(wbr_hbm.at[0, b, cols_of(block)], cb0.at[cs],
                                  sem_ch.at[4 * cs + 2]),
            pltpu.make_async_copy(wbr_hbm.at[1, b, cols_of(block)], cb1.at[cs],
                                  sem_ch.at[4 * cs + 3]),
        ]

    def staged_block(p):
        return jnp.minimum(p + 1, n_blocks - 1)

    @pl.when(jj == 0)
    def _():
        first = [pltpu.make_async_copy(src, buf.at[0], sem_first.at[k])
                 for k, (src, buf) in enumerate([(wg00_ref, wg0_buf), (wg10_ref, wg1_buf),
                                                 (wb00_ref.at[0], wb0_buf),
                                                 (wb10_ref.at[0], wb1_buf)])]
        for cp in first:
            cp.start()
        obuf[...] = jnp.zeros_like(obuf)
        for o in range(obuf.shape[0]):
            out_copy(o, o).start()
        for tile in (0, 1):
            for cp in in_copies(tile):
                cp.start()
        for cp in chunk_copies(0, staged_block(0), 0):
            cp.start()
        for cp in first:
            cp.wait()

    wslot = jj % 2
    sslot = (jj + 1) % 2

    def tile_result(tile):
        s = tile % n_in_slots
        h = hbuf[s]
        g0 = _sigmoid(_dot(h, wg0_buf[wslot]) + gb0_ref[...])
        br0 = _dot(ypbuf[s], wb0_buf[wslot])
        g1 = _sigmoid(_dot(h, wg1_buf[wslot]) + gb1_ref[...])
        br1 = _dot(ycbuf[s], wb1_buf[wslot])
        return (g0 * br0 + g1 * br1).astype(obuf.dtype)

    def pair(k, carry):
        tile_a = jj * n_tiles + 2 * k
        tile_b = tile_a + 1
        out_a = 2 * (k % 2)
        out_b = out_a + 1
        cs = k % 2
        for cp in in_copies(tile_a) + in_copies(tile_b) + chunk_copies(cs, staged_block(jj), k):
            cp.wait()
        out_copy(out_a, tile_a).wait()
        out_copy(out_b, tile_b).wait()
        k_next = (k + 1) % n_pairs
        pass_next = jj + (k + 1) // n_pairs
        for cp in in_copies(tile_a + 2) + chunk_copies(1 - cs, staged_block(pass_next), k_next):
            cp.start()

        res_a = tile_result(tile_a)
        for cp in in_copies(tile_a + 3):
            cp.start()
        obuf[out_a] = res_a
        obuf[out_b] = tile_result(tile_b)
        out_copy(out_a, tile_a).start()
        out_copy(out_b, tile_b).start()

        g = pl.ds(pl.multiple_of(k * g_rows, g_rows), g_rows)
        b = pl.ds(pl.multiple_of(k * b_rows, b_rows), b_rows)
        wg0_buf[sslot, g, :] = cg0[cs].astype(wg0_buf.dtype)
        wg1_buf[sslot, g, :] = cg1[cs].astype(wg1_buf.dtype)
        wb0_buf[sslot, b, :] = cb0[cs].astype(wb0_buf.dtype)
        wb1_buf[sslot, b, :] = cb1[cs].astype(wb1_buf.dtype)
        return carry

    lax.fori_loop(0, n_pairs, pair, 0)

    @pl.when(jj == n_blocks - 1)
    def _():
        end = n_blocks * n_tiles
        for cp in in_copies(end) + in_copies(end + 1) + chunk_copies(n_pairs % 2,
                                                                     staged_block(jj), 0):
            cp.wait()
        for o in range(obuf.shape[0]):
            out_copy(o, o).wait()


def _merge(h, y_pool, y_conv, w_in2d, gate_b, w_branch, first_blocks):
    m = h.shape[0]
    tm = ROW_TILE
    nb = N_MODEL_BLOCKS
    n_pairs = m // (2 * tm)
    g0_first = 6 * N_HALF_BLOCKS
    g1_first = g0_first + nb
    bias = pl.BlockSpec((1, COL_BLOCK), lambda jj: (0, jj))
    hbm = pl.BlockSpec(memory_space=pl.ANY)
    f32, bf16 = jnp.float32, jnp.bfloat16
    return pl.pallas_call(
        functools.partial(_merge_kernel, n_pairs=n_pairs, g0_first=g0_first,
                          g1_first=g1_first),
        grid=(nb,),
        in_specs=[bias, bias] + [hbm] * 9,
        out_specs=hbm,
        out_shape=jax.ShapeDtypeStruct((m, D_MODEL), bf16),
        scratch_shapes=[pltpu.VMEM((2, D_MODEL, COL_BLOCK), bf16),
                        pltpu.VMEM((2, D_MODEL, COL_BLOCK), bf16),
                        pltpu.VMEM((2, HALF, COL_BLOCK), bf16),
                        pltpu.VMEM((2, HALF, COL_BLOCK), bf16),
                        pltpu.VMEM((3, tm, D_MODEL), bf16),
                        pltpu.VMEM((3, tm, HALF), bf16),
                        pltpu.VMEM((3, tm, HALF), bf16),
                        pltpu.VMEM((4, tm, COL_BLOCK), bf16),
                        pltpu.VMEM((2, D_MODEL // n_pairs, COL_BLOCK), f32),
                        pltpu.VMEM((2, D_MODEL // n_pairs, COL_BLOCK), f32),
                        pltpu.VMEM((2, HALF // n_pairs, COL_BLOCK), f32),
                        pltpu.VMEM((2, HALF // n_pairs, COL_BLOCK), f32),
                        pltpu.SemaphoreType.DMA((4,)),
                        pltpu.SemaphoreType.DMA((9,)),
                        pltpu.SemaphoreType.DMA((4,)),
                        pltpu.SemaphoreType.DMA((8,))],
        compiler_params=_params(1),
        name="branch_merge",
    )(gate_b[0:1], gate_b[1:2], h, y_pool, y_conv, w_in2d, w_branch, *first_blocks)


def _out_kernel(mg_ref, w_ref, x_ref, fw_ref, o_ref, rows, sumsq):
    i = pl.program_id(0)
    n = pl.program_id(1)
    n_tiles = pl.num_programs(0) - 1

    def emit_previous():
        scale = lax.rsqrt(sumsq[(i + 1) % 2] * (1.0 / D_MODEL) + NORM_EPS)
        o_ref[...] = rows[n] * scale * fw_ref[...]

    def project():
        xn = x_ref[...] + _dot(mg_ref[...], w_ref[...])
        rows[n] = xn
        part = jnp.sum(xn * xn, axis=-1, keepdims=True)
        cur = i % 2
        sumsq[cur] = jnp.where(n == 0, part, sumsq[cur] + part)

    @pl.when(i == 0)
    def _():
        project()

    @pl.when((i > 0) & (i < n_tiles))
    def _():
        emit_previous()
        project()

    @pl.when(i == n_tiles)
    def _():
        emit_previous()


def _out_proj(merged, w_out_bf, x2d, final_w):
    m = x2d.shape[0]
    tm = OUT_ROW_TILE
    tn = OUT_COL_BLOCK
    n_tiles = m // tm
    n_blocks = D_MODEL // tn
    last = n_tiles - 1
    in_row = lambda i: jnp.minimum(i, last)
    in_col = lambda i, n: jnp.where(i < n_tiles, n, n_blocks - 1)
    return pl.pallas_call(
        _out_kernel,
        grid=(n_tiles + 1, n_blocks),
        in_specs=[
            pl.BlockSpec((tm, D_MODEL), lambda i, n: (in_row(i), 0)),
            pl.BlockSpec((D_MODEL, tn), lambda i, n: (0, in_col(i, n))),
            pl.BlockSpec((tm, tn), lambda i, n: (in_row(i), in_col(i, n))),
            pl.BlockSpec((1, tn), lambda i, n: (0, n)),
        ],
        out_specs=pl.BlockSpec(
            (tm, tn), lambda i, n: (jnp.maximum(i - 1, 0), jnp.where(i > 0, n, 0))),
        out_shape=jax.ShapeDtypeStruct((m, D_MODEL), jnp.float32),
        scratch_shapes=[pltpu.VMEM((n_blocks, tm, tn), jnp.float32),
                        pltpu.VMEM((2, tm, 1), jnp.float32)],
        compiler_params=_params(2),
        name="out_proj_norm",
    )(merged, w_out_bf, x2d, final_w.reshape(1, D_MODEL))


def kernel(x, norm_w, w_in, pool_w, pool_scale, conv_w, conv_b, gate_b, w_branch, w_out,
           final_norm_w):
    b, s, d = x.shape
    assert (s, d) == (SEQ, D_MODEL) and norm_w.shape[0] == 1
    x2d = x.reshape(b * s, d)
    w_in2d = w_in[0]

    nb = N_HALF_BLOCKS
    conv_first_cols = (2 * nb, 3 * nb, 4 * nb, 5 * nb)
    h = _rmsnorm_bf16(x2d, norm_w[0])
    y_pool, conv_first = _pool_mixer(h, w_in2d, pool_w[0], pool_scale[0], conv_first_cols)
    y_conv, merge_first, w_out_bf = _conv_mixer(h, w_in2d, conv_w[0], conv_b[0], conv_first,
                                                w_branch[0], w_out[0])
    merged = _merge(h, y_pool, y_conv, w_in2d, gate_b[0], w_branch[0], merge_first)
    out = _out_proj(merged, w_out_bf, x2d, final_norm_w)
    return out.reshape(b, s, d)
```

```python
import functools

import jax
import jax.numpy as jnp
from jax import lax
from jax.experimental import pallas as pl
from jax.experimental.pallas import tpu as pltpu

D_MODEL = 4096
SEQ = 4096
HALF = D_MODEL // 2
POOL_WINDOWS = (2, 4, 8, 16)
N_GROUPS = len(POOL_WINDOWS)
GROUP_DIM = HALF // N_GROUPS
CONV_K = 3
NORM_EPS = 1e-6

COL_BLOCK = GROUP_DIM
N_HALF_BLOCKS = HALF // COL_BLOCK
N_MODEL_BLOCKS = D_MODEL // COL_BLOCK
OUT_COL_BLOCK = 512
OUT_ROW_TILE = 1024
ROW_TILE = 512
POOL_ROW_TILE = 1024
POOL_HALO = 16
POOL_PAD = 16
POOL_TOP = POOL_PAD + POOL_HALO
CONV_HALO = 8

VMEM_LIMIT_BYTES = 62 * 1024 * 1024
CHUNK_DMA_PRIORITY = 1


def _dot(a, b):
    return jnp.dot(a, b, preferred_element_type=jnp.float32)


def _silu(z):
    return z / (1.0 + jnp.exp(-z))


def _sigmoid(z):
    return 1.0 / (1.0 + jnp.exp(-z))


def _params(n_axes):
    return pltpu.CompilerParams(dimension_semantics=("arbitrary",) * n_axes,
                                vmem_limit_bytes=VMEM_LIMIT_BYTES)


def _rmsnorm_kernel(x_ref, w_ref, o_ref):
    x = x_ref[...]
    ms = jnp.mean(x * x, axis=-1, keepdims=True)
    o_ref[...] = (x * lax.rsqrt(ms + NORM_EPS) * w_ref[...]).astype(o_ref.dtype)


def _rmsnorm_bf16(x2d, w):
    m, d = x2d.shape
    tm = ROW_TILE
    return pl.pallas_call(
        _rmsnorm_kernel,
        grid=(m // tm,),
        in_specs=[pl.BlockSpec((tm, d), lambda i: (i, 0)),
                  pl.BlockSpec((1, d), lambda i: (0, 0))],
        out_specs=pl.BlockSpec((tm, d), lambda i: (i, 0)),
        out_shape=jax.ShapeDtypeStruct((m, d), jnp.bfloat16),
        compiler_params=_params(1),
        name="rmsnorm_in",
    )(x2d, w.reshape(1, d))


def _stage_chunks(slot, i, pairs, fold=None):
    for k, (chunk_ref, buf) in enumerate(pairs):
        rows, cols = chunk_ref.shape[-2:]
        r0 = pl.multiple_of(i * rows, rows)
        chunk = chunk_ref[...].reshape(rows, cols).astype(buf.dtype)
        if k == 0 and fold is not None:
            chunk = _dot(chunk, fold).astype(buf.dtype)
        buf[slot, pl.ds(r0, rows), :] = chunk


def _staged_steps(jj, stage, compute):
    @pl.when(jj == 0)
    def _():
        stage()

    @pl.when(jj > 0)
    def _():
        compute()
        stage()


def _resident_maps(n_blocks, n_tiles):
    def row(jj, i):
        return jnp.where(jj > 0, i, 0)

    def col(jj):
        return jnp.maximum(jj - 1, 0)

    def chunk(jj, i):
        return jnp.where(jj < n_blocks, i, n_tiles - 1)

    def staged(jj):
        return jnp.minimum(jj, n_blocks - 1)

    return row, col, chunk, staged


def _prestaged_maps(n_blocks, n_tiles):
    def chunk(jj, i):
        return jnp.where(jj + 1 < n_blocks, i, n_tiles - 1)

    def staged(jj):
        return jnp.minimum(jj + 1, n_blocks - 1)

    return chunk, staged


def _load_prestaged(first_step, pairs, sems):
    @pl.when(first_step)
    def _():
        copies = [pltpu.make_async_copy(src, buf.at[0], sems.at[k])
                  for k, (src, buf) in enumerate(pairs)]
        for cp in copies:
            cp.start()
        for cp in copies:
            cp.wait()


def _cast_rows(pairs):
    for src_ref, dst_ref in pairs:
        dst_ref[...] = src_ref[...].astype(dst_ref.dtype)


def _pool_kernel(h_ref, wu_ref, wz_ref, pw_ref, ps_ref, n0_ref, n1_ref, n2_ref, n3_ref,
                 y_ref, p0_ref, p1_ref, p2_ref, p3_ref,
                 wu_buf, wz_buf, ubuf, s2buf, s4buf, s8buf, ucarry, *, tiles_per_seq):
    jj = pl.program_id(0)
    i = pl.program_id(1)
    tm = h_ref.shape[0]

    @pl.when((jj == 0) & (i == 0))
    def _():
        ucarry[...] = jnp.zeros_like(ucarry)

    def stage():
        _stage_chunks(jj % 2, i, [(wu_ref, wu_buf), (wz_ref, wz_buf)],
                      fold=pw_ref[0].astype(jnp.bfloat16))

    def compute():
        _cast_rows([(n0_ref, p0_ref), (n1_ref, p1_ref), (n2_ref, p2_ref), (n3_ref, p3_ref)])
        g = jj - 1
        slot = g % 2
        seq_tile = i % tiles_per_seq
        end = POOL_TOP + tm
        h = h_ref[...]
        u = _dot(h, wu_buf[slot])
        z = _dot(h, wz_buf[slot])

        ubuf[0:POOL_PAD, :] = jnp.zeros((POOL_PAD, COL_BLOCK), jnp.float32)
        ubuf[POOL_PAD:POOL_TOP, :] = jnp.where(seq_tile != 0, ucarry[...], 0.0)
        ubuf[POOL_TOP:end, :] = u
        ucarry[...] = ubuf[end - POOL_HALO:end, :]

        def shifted_sum(src, lo, shift):
            return src[lo:end, :] + src[lo - shift:end - shift, :]

        s2buf[8:end, :] = shifted_sum(ubuf, 8, 1)
        s4buf[16:end, :] = shifted_sum(s2buf, 16, 2)
        s8buf[24:end, :] = shifted_sum(s4buf, 24, 4)
        win = jnp.where(g == 0, s2buf[POOL_TOP:end, :],
                        jnp.where(g == 1, s4buf[POOL_TOP:end, :],
                                  jnp.where(g == 2, s8buf[POOL_TOP:end, :],
                                            shifted_sum(s8buf, POOL_TOP, 8))))
        window = lax.shift_left(jnp.int32(2), g)
        t1 = seq_tile * tm + 1 + lax.broadcasted_iota(jnp.int32, (tm, 1), 0)
        inv_count = 1.0 / jnp.minimum(t1, window).astype(jnp.float32)
        mixed = win * inv_count - ubuf[POOL_TOP:end, :]
        ubuf[POOL_TOP:end, :] = mixed * ps_ref[...]
        y_ref[...] = (ubuf[POOL_TOP:end, :] * _silu(z)).astype(y_ref.dtype)

    _staged_steps(jj, stage, compute)


def _pool_mixer(h, w_in2d, pool_w, pool_scale, next_first_blocks):
    m = h.shape[0]
    tm = POOL_ROW_TILE
    nb = N_GROUPS
    n_tiles = m // tm
    chunk_rows = D_MODEL // n_tiles
    row, col, chunk, staged = _resident_maps(nb, n_tiles)
    w_chunk = lambda first: pl.BlockSpec((chunk_rows, COL_BLOCK),
                                         lambda jj, i: (chunk(jj, i), first + staged(jj)))
    side_rows = D_MODEL // (nb * n_tiles)
    side_step = lambda jj, i: jnp.where(jj > 0, (jj - 1) * n_tiles + i, 0)
    side_in = lambda blk: pl.BlockSpec((side_rows, COL_BLOCK),
                                       lambda jj, i: (side_step(jj, i), blk))
    side_out = pl.BlockSpec((side_rows, COL_BLOCK), lambda jj, i: (side_step(jj, i), 0))
    w_buf = pltpu.VMEM((2, D_MODEL, COL_BLOCK), jnp.bfloat16)
    buf = pltpu.VMEM((POOL_TOP + tm, COL_BLOCK), jnp.float32)
    n_side = len(next_first_blocks)
    outs = pl.pallas_call(
        functools.partial(_pool_kernel, tiles_per_seq=SEQ // tm),
        grid=(nb + 1, n_tiles),
        in_specs=[
            pl.BlockSpec((tm, D_MODEL), lambda jj, i: (row(jj, i), 0)),
            w_chunk(0), w_chunk(nb),
            pl.BlockSpec((1, GROUP_DIM, GROUP_DIM), lambda jj, i: (staged(jj), 0, 0)),
            pl.BlockSpec((1, COL_BLOCK), lambda jj, i: (0, col(jj))),
        ] + [side_in(blk) for blk in next_first_blocks],
        out_specs=[pl.BlockSpec((tm, COL_BLOCK), lambda jj, i: (row(jj, i), col(jj)))]
        + [side_out] * n_side,
        out_shape=[jax.ShapeDtypeStruct((m, HALF), jnp.bfloat16)]
        + [jax.ShapeDtypeStruct((D_MODEL, COL_BLOCK), jnp.bfloat16)] * n_side,
        scratch_shapes=[w_buf, w_buf, buf, buf, buf, buf,
                        pltpu.VMEM((POOL_HALO, COL_BLOCK), jnp.float32)],
        compiler_params=_params(2),
        name="pool_mixer",
    )(h, w_in2d, w_in2d, pool_w, pool_scale.reshape(1, HALF), *([w_in2d] * n_side))
    return outs[0], outs[1:]


def _conv_kernel(h_ref, wu_ref, wb_ref, wc_ref, wz_ref, cw_ref, cb_ref,
                 wu0_ref, wb0_ref, wc0_ref, wz0_ref, n0_ref, n1_ref, n2_ref, n3_ref, n4_ref,
                 y_ref, p0_ref, p1_ref, p2_ref, p3_ref, p4_ref,
                 wu_buf, wb_buf, wc_buf, wz_buf, vbuf, vcarry, sems, *, tiles_per_seq):
    jj = pl.program_id(0)
    i = pl.program_id(1)
    tm = h_ref.shape[0]
    first_step = (jj == 0) & (i == 0)

    @pl.when(first_step)
    def _():
        vcarry[...] = jnp.zeros_like(vcarry)

    _load_prestaged(first_step, [(wu0_ref, wu_buf), (wb0_ref, wb_buf),
                                 (wc0_ref, wc_buf), (wz0_ref, wz_buf)], sems)

    def stage():
        _stage_chunks((jj + 1) % 2, i, [(wu_ref, wu_buf), (wb_ref, wb_buf),
                                        (wc_ref, wc_buf), (wz_ref, wz_buf)])

    def compute():
        _cast_rows([(n0_ref, p0_ref), (n1_ref, p1_ref), (n2_ref, p2_ref), (n3_ref, p3_ref),
                    (n4_ref, p4_ref)])
        slot = jj % 2
        seq_tile = i % tiles_per_seq
        h = h_ref[...]
        u = _dot(h, wu_buf[slot])
        c_gate = _dot(h, wc_buf[slot])
        vbuf[0:CONV_HALO, :] = jnp.where(seq_tile != 0, vcarry[...], 0.0)
        vbuf[CONV_HALO:CONV_HALO + tm, :] = c_gate * u
        vcarry[...] = vbuf[tm:CONV_HALO + tm, :]
        cw = cw_ref[...]
        y = (cb_ref[...]
             + cw[0:1, :] * vbuf[CONV_HALO - 2:CONV_HALO - 2 + tm, :]
             + cw[1:2, :] * vbuf[CONV_HALO - 1:CONV_HALO - 1 + tm, :]
             + cw[2:3, :] * vbuf[CONV_HALO:CONV_HALO + tm, :])
        gated = y * _silu(_dot(h, wz_buf[slot]))
        y_ref[...] = (_dot(h, wb_buf[slot]) * gated).astype(y_ref.dtype)

    compute()
    stage()


def _conv_mixer(h, w_in2d, conv_w, conv_b, first_blocks, w_branch, w_out2d):
    m = h.shape[0]
    tm = ROW_TILE
    nb = N_HALF_BLOCKS
    n_tiles = m // tm
    chunk_rows = D_MODEL // n_tiles
    chunk, staged = _prestaged_maps(nb, n_tiles)
    w_chunk = lambda first: pl.BlockSpec((chunk_rows, COL_BLOCK),
                                         lambda jj, i: (chunk(jj, i), first + staged(jj)))
    hbm = pl.BlockSpec(memory_space=pl.ANY)
    n_steps = nb * n_tiles
    step = lambda jj, i: jj * n_tiles + i
    gate_rows = D_MODEL // n_steps
    branch_rows = HALF // n_steps
    gate_in = lambda blk: pl.BlockSpec((gate_rows, COL_BLOCK), lambda jj, i: (step(jj, i), blk))
    branch_in = lambda b: pl.BlockSpec((1, branch_rows, COL_BLOCK),
                                       lambda jj, i: (b, step(jj, i), 0))
    gate_out = pl.BlockSpec((gate_rows, COL_BLOCK), lambda jj, i: (step(jj, i), 0))
    branch_out = pl.BlockSpec((1, branch_rows, COL_BLOCK), lambda jj, i: (0, step(jj, i), 0))
    g0_first = 6 * N_HALF_BLOCKS
    g1_first = g0_first + N_MODEL_BLOCKS
    wo_slice = pl.BlockSpec((D_MODEL // n_steps, D_MODEL), lambda jj, i: (step(jj, i), 0))
    w_buf = pltpu.VMEM((2, D_MODEL, COL_BLOCK), jnp.bfloat16)
    outs = pl.pallas_call(
        functools.partial(_conv_kernel, tiles_per_seq=SEQ // tm),
        grid=(nb, n_tiles),
        in_specs=[
            pl.BlockSpec((tm, D_MODEL), lambda jj, i: (i, 0)),
            w_chunk(2 * nb), w_chunk(3 * nb), w_chunk(4 * nb), w_chunk(5 * nb),
            pl.BlockSpec((CONV_K, COL_BLOCK), lambda jj, i: (0, jj)),
            pl.BlockSpec((1, COL_BLOCK), lambda jj, i: (0, jj)),
            hbm, hbm, hbm, hbm,
            gate_in(g0_first), gate_in(g1_first), branch_in(0), branch_in(1), wo_slice,
        ],
        out_specs=[pl.BlockSpec((tm, COL_BLOCK), lambda jj, i: (i, jj)),
                   gate_out, gate_out, branch_out, branch_out, wo_slice],
        out_shape=[jax.ShapeDtypeStruct((m, HALF), jnp.bfloat16),
                   jax.ShapeDtypeStruct((D_MODEL, COL_BLOCK), jnp.bfloat16),
                   jax.ShapeDtypeStruct((D_MODEL, COL_BLOCK), jnp.bfloat16),
                   jax.ShapeDtypeStruct((1, HALF, COL_BLOCK), jnp.bfloat16),
                   jax.ShapeDtypeStruct((1, HALF, COL_BLOCK), jnp.bfloat16),
                   jax.ShapeDtypeStruct((D_MODEL, D_MODEL), jnp.bfloat16)],
        scratch_shapes=[w_buf, w_buf, w_buf, w_buf,
                        pltpu.VMEM((CONV_HALO + tm, COL_BLOCK), jnp.float32),
                        pltpu.VMEM((CONV_HALO, COL_BLOCK), jnp.float32),
                        pltpu.SemaphoreType.DMA((4,))],
        compiler_params=_params(2),
        name="conv_mixer",
    )(h, w_in2d, w_in2d, w_in2d, w_in2d, conv_w, conv_b.reshape(1, HALF),
      *first_blocks, w_in2d, w_in2d, w_branch, w_branch, w_out2d)
    return outs[0], outs[1:5], outs[5]


def _merge_kernel(gb0_ref, gb1_ref, h_hbm, yp_hbm, yc_hbm, win_hbm, wbr_hbm,
                  wg00_ref, wg10_ref, wb00_ref, wb10_ref, o_hbm,
                  wg0_buf, wg1_buf, wb0_buf, wb1_buf, hbuf, ypbuf, ycbuf, obuf,
                  cg0, cg1, cb0, cb1, sem_first, sem_in, sem_out, sem_ch,
                  *, n_pairs, g0_first, g1_first):
    jj = pl.program_id(0)
    n_blocks = pl.num_programs(0)
    tm = hbuf.shape[1]
    n_tiles = 2 * n_pairs
    n_in_slots = hbuf.shape[0]
    g_rows = cg0.shape[1]
    b_rows = cb0.shape[1]

    def rows_of(tile):
        return pl.ds(pl.multiple_of((tile % n_tiles) * tm, tm), tm)

    def cols_of(block):
        return pl.ds(pl.multiple_of(block * COL_BLOCK, COL_BLOCK), COL_BLOCK)

    def in_copies(tile):
        s = tile % n_in_slots
        rows = rows_of(tile)
        return [pltpu.make_async_copy(h_hbm.at[rows, :], hbuf.at[s], sem_in.at[3 * s]),
                pltpu.make_async_copy(yp_hbm.at[rows, :], ypbuf.at[s], sem_in.at[3 * s + 1]),
                pltpu.make_async_copy(yc_hbm.at[rows, :], ycbuf.at[s], sem_in.at[3 * s + 2])]

    def out_copy(o, tile):
        return pltpu.make_async_copy(obuf.at[o], o_hbm.at[rows_of(tile), cols_of(jj)],
                                     sem_out.at[o])

    def chunk_copies(cs, block, k):
        g = pl.ds(pl.multiple_of(k * g_rows, g_rows), g_rows)
        b = pl.ds(pl.multiple_of(k * b_rows, b_rows), b_rows)
        return [
            pltpu.make_async_copy(win_hbm.at[g, cols_of(g0_first + block)], cg0.at[cs],
                                  sem_ch.at[4 * cs]),
            pltpu.make_async_copy(win_hbm.at[g, cols_of(g1_first + block)], cg1.at[cs],
                                  sem_ch.at[4 * cs + 1]),
            pltpu.make_async_copy(wbr_hbm.at[0, b, cols_of(block)], cb0.at[cs],
                                  sem_ch.at[4 * cs + 2]),
            pltpu.make_async_copy(wbr_hbm.at[1, b, cols_of(block)], cb1.at[cs],
                                  sem_ch.at[4 * cs + 3]),
        ]

    def staged_block(p):
        return jnp.minimum(p + 1, n_blocks - 1)

    @pl.when(jj == 0)
    def _():
        first = [pltpu.make_async_copy(src, buf.at[0], sem_first.at[k])
                 for k, (src, buf) in enumerate([(wg00_ref, wg0_buf), (wg10_ref, wg1_buf),
                                                 (wb00_ref.at[0], wb0_buf),
                                                 (wb10_ref.at[0], wb1_buf)])]
        for cp in first:
            cp.start()
        obuf[...] = jnp.zeros_like(obuf)
        for o in range(obuf.shape[0]):
            out_copy(o, o).start()
        for tile in (0, 1):
            for cp in in_copies(tile):
                cp.start()
        for cp in chunk_copies(0, staged_block(0), 0):
            cp.start(priority=CHUNK_DMA_PRIORITY)
        for cp in first:
            cp.wait()

    wslot = jj % 2
    sslot = (jj + 1) % 2

    def tile_result(tile):
        s = tile % n_in_slots
        h = hbuf[s]
        g0 = _sigmoid(_dot(h, wg0_buf[wslot]) + gb0_ref[...])
        br0 = _dot(ypbuf[s], wb0_buf[wslot])
        g1 = _sigmoid(_dot(h, wg1_buf[wslot]) + gb1_ref[...])
        br1 = _dot(ycbuf[s], wb1_buf[wslot])
        return (g0 * br0 + g1 * br1).astype(obuf.dtype)

    def pair(k, carry):
        tile_a = jj * n_tiles + 2 * k
        tile_b = tile_a + 1
        out_a = 2 * (k % 2)
        out_b = out_a + 1
        cs = k % 2
        for cp in in_copies(tile_a) + in_copies(tile_b) + chunk_copies(cs, staged_block(jj), k):
            cp.wait()
        out_copy(out_a, tile_a).wait()
        out_copy(out_b, tile_b).wait()
        k_next = (k + 1) % n_pairs
        pass_next = jj + (k + 1) // n_pairs
        for cp in in_copies(tile_a + 2):
            cp.start()
        for cp in chunk_copies(1 - cs, staged_block(pass_next), k_next):
            cp.start(priority=CHUNK_DMA_PRIORITY)

        res_a = tile_result(tile_a)
        for cp in in_copies(tile_a + 3):
            cp.start()
        obuf[out_a] = res_a
        obuf[out_b] = tile_result(tile_b)
        out_copy(out_a, tile_a).start()
        out_copy(out_b, tile_b).start()

        g = pl.ds(pl.multiple_of(k * g_rows, g_rows), g_rows)
        b = pl.ds(pl.multiple_of(k * b_rows, b_rows), b_rows)
        wg0_buf[sslot, g, :] = cg0[cs].astype(wg0_buf.dtype)
        wg1_buf[sslot, g, :] = cg1[cs].astype(wg1_buf.dtype)
        wb0_buf[sslot, b, :] = cb0[cs].astype(wb0_buf.dtype)
        wb1_buf[sslot, b, :] = cb1[cs].astype(wb1_buf.dtype)
        return carry

    lax.fori_loop(0, n_pairs, pair, 0)

    @pl.when(jj == n_blocks - 1)
    def _():
        end = n_blocks * n_tiles
        for cp in in_copies(end) + in_copies(end + 1) + chunk_copies(n_pairs % 2,
                                                                     staged_block(jj), 0):
            cp.wait()
        for o in range(obuf.shape[0]):
            out_copy(o, o).wait()


def _merge(h, y_pool, y_conv, w_in2d, gate_b, w_branch, first_blocks):
    m = h.shape[0]
    tm = ROW_TILE
    nb = N_MODEL_BLOCKS
    n_pairs = m // (2 * tm)
    g0_first = 6 * N_HALF_BLOCKS
    g1_first = g0_first + nb
    bias = pl.BlockSpec((1, COL_BLOCK), lambda jj: (0, jj))
    hbm = pl.BlockSpec(memory_space=pl.ANY)
    f32, bf16 = jnp.float32, jnp.bfloat16
    return pl.pallas_call(
        functools.partial(_merge_kernel, n_pairs=n_pairs, g0_first=g0_first,
                          g1_first=g1_first),
        grid=(nb,),
        in_specs=[bias, bias] + [hbm] * 9,
        out_specs=hbm,
        out_shape=jax.ShapeDtypeStruct((m, D_MODEL), bf16),
        scratch_shapes=[pltpu.VMEM((2, D_MODEL, COL_BLOCK), bf16),
                        pltpu.VMEM((2, D_MODEL, COL_BLOCK), bf16),
                        pltpu.VMEM((2, HALF, COL_BLOCK), bf16),
                        pltpu.VMEM((2, HALF, COL_BLOCK), bf16),
                        pltpu.VMEM((3, tm, D_MODEL), bf16),
                        pltpu.VMEM((3, tm, HALF), bf16),
                        pltpu.VMEM((3, tm, HALF), bf16),
                        pltpu.VMEM((4, tm, COL_BLOCK), bf16),
                        pltpu.VMEM((2, D_MODEL // n_pairs, COL_BLOCK), f32),
                        pltpu.VMEM((2, D_MODEL // n_pairs, COL_BLOCK), f32),
                        pltpu.VMEM((2, HALF // n_pairs, COL_BLOCK), f32),
                        pltpu.VMEM((2, HALF // n_pairs, COL_BLOCK), f32),
                        pltpu.SemaphoreType.DMA((4,)),
                        pltpu.SemaphoreType.DMA((9,)),
                        pltpu.SemaphoreType.DMA((4,)),
                        pltpu.SemaphoreType.DMA((8,))],
        compiler_params=_params(1),
        name="branch_merge",
    )(gate_b[0:1], gate_b[1:2], h, y_pool, y_conv, w_in2d, w_branch, *first_blocks)


def _out_kernel(mg_ref, w_ref, x_ref, fw_ref, o_ref, rows, sumsq):
    i = pl.program_id(0)
    n = pl.program_id(1)
    n_tiles = pl.num_programs(0) - 1

    def emit_previous():
        scale = lax.rsqrt(sumsq[(i + 1) % 2] * (1.0 / D_MODEL) + NORM_EPS)
        o_ref[...] = rows[n] * scale * fw_ref[...]

    def project():
        xn = x_ref[...] + _dot(mg_ref[...], w_ref[...])
        rows[n] = xn
        part = jnp.sum(xn * xn, axis=-1, keepdims=True)
        cur = i % 2
        sumsq[cur] = jnp.where(n == 0, part, sumsq[cur] + part)

    @pl.when(i == 0)
    def _():
        project()

    @pl.when((i > 0) & (i < n_tiles))
    def _():
        emit_previous()
        project()

    @pl.when(i == n_tiles)
    def _():
        emit_previous()


def _out_proj(merged, w_out_bf, x2d, final_w):
    m = x2d.shape[0]
    tm = OUT_ROW_TILE
    tn = OUT_COL_BLOCK
    n_tiles = m // tm
    n_blocks = D_MODEL // tn
    last = n_tiles - 1
    in_row = lambda i: jnp.minimum(i, last)
    in_col = lambda i, n: jnp.where(i < n_tiles, n, n_blocks - 1)
    return pl.pallas_call(
        _out_kernel,
        grid=(n_tiles + 1, n_blocks),
        in_specs=[
            pl.BlockSpec((tm, D_MODEL), lambda i, n: (in_row(i), 0)),
            pl.BlockSpec((D_MODEL, tn), lambda i, n: (0, in_col(i, n))),
            pl.BlockSpec((tm, tn), lambda i, n: (in_row(i), in_col(i, n))),
            pl.BlockSpec((1, tn), lambda i, n: (0, n)),
        ],
        out_specs=pl.BlockSpec(
            (tm, tn), lambda i, n: (jnp.maximum(i - 1, 0), jnp.where(i > 0, n, 0))),
        out_shape=jax.ShapeDtypeStruct((m, D_MODEL), jnp.float32),
        scratch_shapes=[pltpu.VMEM((n_blocks, tm, tn), jnp.float32),
                        pltpu.VMEM((2, tm, 1), jnp.float32)],
        compiler_params=_params(2),
        name="out_proj_norm",
    )(merged, w_out_bf, x2d, final_w.reshape(1, D_MODEL))


def kernel(x, norm_w, w_in, pool_w, pool_scale, conv_w, conv_b, gate_b, w_branch, w_out,
           final_norm_w):
    b, s, d = x.shape
    assert (s, d) == (SEQ, D_MODEL) and norm_w.shape[0] == 1
    x2d = x.reshape(b * s, d)
    w_in2d = w_in[0]

    nb = N_HALF_BLOCKS
    conv_first_cols = (2 * nb, 3 * nb, 4 * nb, 5 * nb)
    h = _rmsnorm_bf16(x2d, norm_w[0])
    y_pool, conv_first = _pool_mixer(h, w_in2d, pool_w[0], pool_scale[0], conv_first_cols)
    y_conv, merge_first, w_out_bf = _conv_mixer(h, w_in2d, conv_w[0], conv_b[0], conv_first,
                                                w_branch[0], w_out[0])
    merged = _merge(h, y_pool, y_conv, w_in2d, gate_b[0], w_branch[0], merge_first)
    out = _out_proj(merged, w_out_bf, x2d, final_norm_w)
    return out.reshape(b, s, d)
```

```python
import functools

import jax
import jax.numpy as jnp
from jax import lax
from jax.experimental import pallas as pl
from jax.experimental.pallas import tpu as pltpu

D_MODEL = 4096
SEQ = 4096
HALF = D_MODEL // 2
POOL_WINDOWS = (2, 4, 8, 16)
N_GROUPS = len(POOL_WINDOWS)
GROUP_DIM = HALF // N_GROUPS
CONV_K = 3
NORM_EPS = 1e-6

COL_BLOCK = GROUP_DIM
N_HALF_BLOCKS = HALF // COL_BLOCK
N_MODEL_BLOCKS = D_MODEL // COL_BLOCK
OUT_COL_BLOCK = 512
OUT_ROW_TILE = 1024
ROW_TILE = 512
POOL_ROW_TILE = 1024
POOL_HALO = 16
POOL_PAD = 16
POOL_TOP = POOL_PAD + POOL_HALO
CONV_HALO = 8

VMEM_LIMIT_BYTES = 62 * 1024 * 1024
CHUNK_DMA_PRIORITY = 1


def _dot(a, b):
    return jnp.dot(a, b, preferred_element_type=jnp.float32)


def _silu(z):
    return z / (1.0 + jnp.exp(-z))


def _sigmoid(z):
    return 1.0 / (1.0 + jnp.exp(-z))


def _params(n_axes):
    return pltpu.CompilerParams(dimension_semantics=("arbitrary",) * n_axes,
                                vmem_limit_bytes=VMEM_LIMIT_BYTES)


def _rmsnorm_kernel(x_ref, w_ref, o_ref):
    x = x_ref[...]
    ms = jnp.mean(x * x, axis=-1, keepdims=True)
    o_ref[...] = (x * lax.rsqrt(ms + NORM_EPS) * w_ref[...]).astype(o_ref.dtype)


def _rmsnorm_bf16(x2d, w):
    m, d = x2d.shape
    tm = ROW_TILE
    return pl.pallas_call(
        _rmsnorm_kernel,
        grid=(m // tm,),
        in_specs=[pl.BlockSpec((tm, d), lambda i: (i, 0)),
                  pl.BlockSpec((1, d), lambda i: (0, 0))],
        out_specs=pl.BlockSpec((tm, d), lambda i: (i, 0)),
        out_shape=jax.ShapeDtypeStruct((m, d), jnp.bfloat16),
        compiler_params=_params(1),
        name="rmsnorm_in",
    )(x2d, w.reshape(1, d))


def _stage_chunks(slot, i, pairs, fold=None):
    for k, (chunk_ref, buf) in enumerate(pairs):
        rows, cols = chunk_ref.shape[-2:]
        r0 = pl.multiple_of(i * rows, rows)
        chunk = chunk_ref[...].reshape(rows, cols).astype(buf.dtype)
        if k == 0 and fold is not None:
            chunk = _dot(chunk, fold).astype(buf.dtype)
        buf[slot, pl.ds(r0, rows), :] = chunk


def _staged_steps(jj, stage, compute):
    @pl.when(jj == 0)
    def _():
        stage()

    @pl.when(jj > 0)
    def _():
        compute()
        stage()


def _resident_maps(n_blocks, n_tiles):
    def row(jj, i):
        return jnp.where(jj > 0, i, 0)

    def col(jj):
        return jnp.maximum(jj - 1, 0)

    def chunk(jj, i):
        return jnp.where(jj < n_blocks, i, n_tiles - 1)

    def staged(jj):
        return jnp.minimum(jj, n_blocks - 1)

    return row, col, chunk, staged


def _prestaged_maps(n_blocks, n_tiles):
    def chunk(jj, i):
        return jnp.where(jj + 1 < n_blocks, i, n_tiles - 1)

    def staged(jj):
        return jnp.minimum(jj + 1, n_blocks - 1)

    return chunk, staged


def _load_prestaged(first_step, pairs, sems):
    @pl.when(first_step)
    def _():
        copies = [pltpu.make_async_copy(src, buf.at[0], sems.at[k])
                  for k, (src, buf) in enumerate(pairs)]
        for cp in copies:
            cp.start()
        for cp in copies:
            cp.wait()


def _cast_rows(pairs):
    for src_ref, dst_ref in pairs:
        dst_ref[...] = src_ref[...].astype(dst_ref.dtype)


def _pool_kernel(h_ref, wu_ref, wz_ref, pw_ref, ps_ref, n0_ref, n1_ref, n2_ref, n3_ref,
                 y_ref, p0_ref, p1_ref, p2_ref, p3_ref,
                 wu_buf, wz_buf, ubuf, s2buf, s4buf, s8buf, ucarry, *, tiles_per_seq):
    jj = pl.program_id(0)
    i = pl.program_id(1)
    tm = h_ref.shape[0]

    @pl.when((jj == 0) & (i == 0))
    def _():
        ucarry[...] = jnp.zeros_like(ucarry)

    def stage():
        _stage_chunks(jj % 2, i, [(wu_ref, wu_buf), (wz_ref, wz_buf)],
                      fold=pw_ref[0].astype(jnp.bfloat16))

    def compute():
        _cast_rows([(n0_ref, p0_ref), (n1_ref, p1_ref), (n2_ref, p2_ref), (n3_ref, p3_ref)])
        g = jj - 1
        slot = g % 2
        seq_tile = i % tiles_per_seq
        end = POOL_TOP + tm
        h = h_ref[...]
        u = _dot(h, wu_buf[slot])
        z = _dot(h, wz_buf[slot])

        ubuf[0:POOL_PAD, :] = jnp.zeros((POOL_PAD, COL_BLOCK), jnp.float32)
        ubuf[POOL_PAD:POOL_TOP, :] = jnp.where(seq_tile != 0, ucarry[...], 0.0)
        ubuf[POOL_TOP:end, :] = u
        ucarry[...] = ubuf[end - POOL_HALO:end, :]

        def shifted_sum(src, lo, shift):
            return src[lo:end, :] + src[lo - shift:end - shift, :]

        s2buf[8:end, :] = shifted_sum(ubuf, 8, 1)
        s4buf[16:end, :] = shifted_sum(s2buf, 16, 2)
        s8buf[24:end, :] = shifted_sum(s4buf, 24, 4)
        win = jnp.where(g == 0, s2buf[POOL_TOP:end, :],
                        jnp.where(g == 1, s4buf[POOL_TOP:end, :],
                                  jnp.where(g == 2, s8buf[POOL_TOP:end, :],
                                            shifted_sum(s8buf, POOL_TOP, 8))))
        window = lax.shift_left(jnp.int32(2), g)
        t1 = seq_tile * tm + 1 + lax.broadcasted_iota(jnp.int32, (tm, 1), 0)
        inv_count = 1.0 / jnp.minimum(t1, window).astype(jnp.float32)
        mixed = win * inv_count - ubuf[POOL_TOP:end, :]
        ubuf[POOL_TOP:end, :] = mixed * ps_ref[...]
        y_ref[...] = (ubuf[POOL_TOP:end, :] * _silu(z)).astype(y_ref.dtype)

    _staged_steps(jj, stage, compute)


def _pool_mixer(h, w_in2d, pool_w, pool_scale, next_first_blocks):
    m = h.shape[0]
    tm = POOL_ROW_TILE
    nb = N_GROUPS
    n_tiles = m // tm
    chunk_rows = D_MODEL // n_tiles
    row, col, chunk, staged = _resident_maps(nb, n_tiles)
    w_chunk = lambda first: pl.BlockSpec((chunk_rows, COL_BLOCK),
                                         lambda jj, i: (chunk(jj, i), first + staged(jj)))
    side_rows = D_MODEL // (nb * n_tiles)
    side_step = lambda jj, i: jnp.where(jj > 0, (jj - 1) * n_tiles + i, 0)
    side_in = lambda blk: pl.BlockSpec((side_rows, COL_BLOCK),
                                       lambda jj, i: (side_step(jj, i), blk))
    side_out = pl.BlockSpec((side_rows, COL_BLOCK), lambda jj, i: (side_step(jj, i), 0))
    w_buf = pltpu.VMEM((2, D_MODEL, COL_BLOCK), jnp.bfloat16)
    buf = pltpu.VMEM((POOL_TOP + tm, COL_BLOCK), jnp.float32)
    n_side = len(next_first_blocks)
    outs = pl.pallas_call(
        functools.partial(_pool_kernel, tiles_per_seq=SEQ // tm),
        grid=(nb + 1, n_tiles),
        in_specs=[
            pl.BlockSpec((tm, D_MODEL), lambda jj, i: (row(jj, i), 0)),
            w_chunk(0), w_chunk(nb),
            pl.BlockSpec((1, GROUP_DIM, GROUP_DIM), lambda jj, i: (staged(jj), 0, 0)),
            pl.BlockSpec((1, COL_BLOCK), lambda jj, i: (0, col(jj))),
        ] + [side_in(blk) for blk in next_first_blocks],
        out_specs=[pl.BlockSpec((tm, COL_BLOCK), lambda jj, i: (row(jj, i), col(jj)))]
        + [side_out] * n_side,
        out_shape=[jax.ShapeDtypeStruct((m, HALF), jnp.bfloat16)]
        + [jax.ShapeDtypeStruct((D_MODEL, COL_BLOCK), jnp.bfloat16)] * n_side,
        scratch_shapes=[w_buf, w_buf, buf, buf, buf, buf,
                        pltpu.VMEM((POOL_HALO, COL_BLOCK), jnp.float32)],
        compiler_params=_params(2),
        name="pool_mixer",
    )(h, w_in2d, w_in2d, pool_w, pool_scale.reshape(1, HALF), *([w_in2d] * n_side))
    return outs[0], outs[1:]


def _conv_kernel(h_ref, wu_ref, wb_ref, wc_ref, wz_ref, cw_ref, cb_ref,
                 wu0_ref, wb0_ref, wc0_ref, wz0_ref, n0_ref, n1_ref, n2_ref, n3_ref, n4_ref,
                 y_ref, p0_ref, p1_ref, p2_ref, p3_ref, p4_ref,
                 wu_buf, wb_buf, wc_buf, wz_buf, vbuf, vcarry, sems, *, tiles_per_seq):
    jj = pl.program_id(0)
    i = pl.program_id(1)
    tm = h_ref.shape[0]
    first_step = (jj == 0) & (i == 0)

    @pl.when(first_step)
    def _():
        vcarry[...] = jnp.zeros_like(vcarry)

    _load_prestaged(first_step, [(wu0_ref, wu_buf), (wb0_ref, wb_buf),
                                 (wc0_ref, wc_buf), (wz0_ref, wz_buf)], sems)

    def stage():
        _stage_chunks((jj + 1) % 2, i, [(wu_ref, wu_buf), (wb_ref, wb_buf),
                                        (wc_ref, wc_buf), (wz_ref, wz_buf)])

    def compute():
        _cast_rows([(n0_ref, p0_ref), (n1_ref, p1_ref), (n2_ref, p2_ref), (n3_ref, p3_ref),
                    (n4_ref, p4_ref)])
        slot = jj % 2
        seq_tile = i % tiles_per_seq
        h = h_ref[...]
        u = _dot(h, wu_buf[slot])
        c_gate = _dot(h, wc_buf[slot])
        vbuf[0:CONV_HALO, :] = jnp.where(seq_tile != 0, vcarry[...], 0.0)
        vbuf[CONV_HALO:CONV_HALO + tm, :] = c_gate * u
        vcarry[...] = vbuf[tm:CONV_HALO + tm, :]
        cw = cw_ref[...]
        y = (cb_ref[...]
             + cw[0:1, :] * vbuf[CONV_HALO - 2:CONV_HALO - 2 + tm, :]
             + cw[1:2, :] * vbuf[CONV_HALO - 1:CONV_HALO - 1 + tm, :]
             + cw[2:3, :] * vbuf[CONV_HALO:CONV_HALO + tm, :])
        gated = y * _silu(_dot(h, wz_buf[slot]))
        y_ref[...] = (_dot(h, wb_buf[slot]) * gated).astype(y_ref.dtype)

    compute()
    stage()


def _conv_mixer(h, w_in2d, conv_w, conv_b, first_blocks, w_branch, w_out2d):
    m = h.shape[0]
    tm = ROW_TILE
    nb = N_HALF_BLOCKS
    n_tiles = m // tm
    chunk_rows = D_MODEL // n_tiles
    chunk, staged = _prestaged_maps(nb, n_tiles)
    w_chunk = lambda first: pl.BlockSpec((chunk_rows, COL_BLOCK),
                                         lambda jj, i: (chunk(jj, i), first + staged(jj)))
    hbm = pl.BlockSpec(memory_space=pl.ANY)
    n_steps = nb * n_tiles
    step = lambda jj, i: jj * n_tiles + i
    gate_rows = D_MODEL // n_steps
    branch_rows = HALF // n_steps
    gate_in = lambda blk: pl.BlockSpec((gate_rows, COL_BLOCK), lambda jj, i: (step(jj, i), blk))
    branch_in = lambda b: pl.BlockSpec((1, branch_rows, COL_BLOCK),
                                       lambda jj, i: (b, step(jj, i), 0))
    gate_out = pl.BlockSpec((gate_rows, COL_BLOCK), lambda jj, i: (step(jj, i), 0))
    branch_out = pl.BlockSpec((1, branch_rows, COL_BLOCK), lambda jj, i: (0, step(jj, i), 0))
    g0_first = 6 * N_HALF_BLOCKS
    g1_first = g0_first + N_MODEL_BLOCKS
    wo_slice = pl.BlockSpec((D_MODEL // n_steps, D_MODEL), lambda jj, i: (step(jj, i), 0))
    w_buf = pltpu.VMEM((2, D_MODEL, COL_BLOCK), jnp.bfloat16)
    outs = pl.pallas_call(
        functools.partial(_conv_kernel, tiles_per_seq=SEQ // tm),
        grid=(nb, n_tiles),
        in_specs=[
            pl.BlockSpec((tm, D_MODEL), lambda jj, i: (i, 0)),
            w_chunk(2 * nb), w_chunk(3 * nb), w_chunk(4 * nb), w_chunk(5 * nb),
            pl.BlockSpec((CONV_K, COL_BLOCK), lambda jj, i: (0, jj)),
            pl.BlockSpec((1, COL_BLOCK), lambda jj, i: (0, jj)),
            hbm, hbm, hbm, hbm,
            gate_in(g0_first), gate_in(g1_first), branch_in(0), branch_in(1), wo_slice,
        ],
        out_specs=[pl.BlockSpec((tm, COL_BLOCK), lambda jj, i: (i, jj)),
                   gate_out, gate_out, branch_out, branch_out, wo_slice],
        out_shape=[jax.ShapeDtypeStruct((m, HALF), jnp.bfloat16),
                   jax.ShapeDtypeStruct((D_MODEL, COL_BLOCK), jnp.bfloat16),
                   jax.ShapeDtypeStruct((D_MODEL, COL_BLOCK), jnp.bfloat16),
                   jax.ShapeDtypeStruct((1, HALF, COL_BLOCK), jnp.bfloat16),
                   jax.ShapeDtypeStruct((1, HALF, COL_BLOCK), jnp.bfloat16),
                   jax.ShapeDtypeStruct((D_MODEL, D_MODEL), jnp.bfloat16)],
        scratch_shapes=[w_buf, w_buf, w_buf, w_buf,
                        pltpu.VMEM((CONV_HALO + tm, COL_BLOCK), jnp.float32),
                        pltpu.VMEM((CONV_HALO, COL_BLOCK), jnp.float32),
                        pltpu.SemaphoreType.DMA((4,))],
        compiler_params=_params(2),
        name="conv_mixer",
    )(h, w_in2d, w_in2d, w_in2d, w_in2d, conv_w, conv_b.reshape(1, HALF),
      *first_blocks, w_in2d, w_in2d, w_branch, w_branch, w_out2d)
    return outs[0], outs[1:5], outs[5]


def _merge_kernel(gb0_ref, gb1_ref, h_hbm, yp_hbm, yc_hbm, win_hbm, wbr_hbm,
                  wg00_ref, wg10_ref, wb00_ref, wb10_ref, o_hbm,
                  wg0_buf, wg1_buf, wb0_buf, wb1_buf, hbuf, ypbuf, ycbuf, obuf,
                  cg0, cg1, cb0, cb1, sem_first, sem_in, sem_out, sem_ch,
                  *, n_pairs, g0_first, g1_first):
    jj = pl.program_id(0)
    n_blocks = pl.num_programs(0)
    tm = hbuf.shape[1]
    n_tiles = 2 * n_pairs
    n_in_slots = hbuf.shape[0]
    g_rows = cg0.shape[1]
    b_rows = cb0.shape[1]

    def rows_of(tile):
        return pl.ds(pl.multiple_of((tile % n_tiles) * tm, tm), tm)

    def cols_of(block):
        return pl.ds(pl.multiple_of(block * COL_BLOCK, COL_BLOCK), COL_BLOCK)

    def in_copies(tile):
        s = tile % n_in_slots
        rows = rows_of(tile)
        return [pltpu.make_async_copy(h_hbm.at[rows, :], hbuf.at[s], sem_in.at[3 * s]),
                pltpu.make_async_copy(yp_hbm.at[rows, :], ypbuf.at[s], sem_in.at[3 * s + 1]),
                pltpu.make_async_copy(yc_hbm.at[rows, :], ycbuf.at[s], sem_in.at[3 * s + 2])]

    def out_copy(o, tile):
        return pltpu.make_async_copy(obuf.at[o], o_hbm.at[rows_of(tile), cols_of(jj)],
                                     sem_out.at[o])

    def chunk_copies(cs, block, k):
        g = pl.ds(pl.multiple_of(k * g_rows, g_rows), g_rows)
        b = pl.ds(pl.multiple_of(k * b_rows, b_rows), b_rows)
        return [
            pltpu.make_async_copy(win_hbm.at[g, cols_of(g0_first + block)], cg0.at[cs],
                                  sem_ch.at[4 * cs]),
            pltpu.make_async_copy(win_hbm.at[g, cols_of(g1_first + block)], cg1.at[cs],
                                  sem_ch.at[4 * cs + 1]),
            pltpu.make_async_copy(wbr_hbm.at[0, b, cols_of(block)], cb0.at[cs],
                                  sem_ch.at[4 * cs + 2]),
            pltpu.make_async_copy(wbr_hbm.at[1, b, cols_of(block)], cb1.at[cs],
                                  sem_ch.at[4 * cs + 3]),
        ]

    def staged_block(p):
        return jnp.minimum(p + 1, n_blocks - 1)

    @pl.when(jj == 0)
    def _():
        first = [pltpu.make_async_copy(src, buf.at[0], sem_first.at[k])
                 for k, (src, buf) in enumerate([(wg00_ref, wg0_buf), (wg10_ref, wg1_buf),
                                                 (wb00_ref.at[0], wb0_buf),
                                                 (wb10_ref.at[0], wb1_buf)])]
        for cp in first:
            cp.start()
        obuf[...] = jnp.zeros_like(obuf)
        for o in range(obuf.shape[0]):
            out_copy(o, o).start()
        for tile in (0, 1):
            for cp in in_copies(tile):
                cp.start()
        for cp in chunk_copies(0, staged_block(0), 0):
            cp.start(priority=CHUNK_DMA_PRIORITY)
        for cp in first:
            cp.wait()

    wslot = jj % 2
    sslot = (jj + 1) % 2

    def tile_result(tile):
        s = tile % n_in_slots
        h = hbuf[s]
        g0 = _sigmoid(_dot(h, wg0_buf[wslot]) + gb0_ref[...])
        br0 = _dot(ypbuf[s], wb0_buf[wslot])
        g1 = _sigmoid(_dot(h, wg1_buf[wslot]) + gb1_ref[...])
        br1 = _dot(ycbuf[s], wb1_buf[wslot])
        return (g0 * br0 + g1 * br1).astype(obuf.dtype)

    def pair(k, carry):
        tile_a = jj * n_tiles + 2 * k
        tile_b = tile_a + 1
        out_a = 2 * (k % 2)
        out_b = out_a + 1
        cs = k % 2
        for cp in in_copies(tile_a) + in_copies(tile_b) + chunk_copies(cs, staged_block(jj), k):
            cp.wait()
        out_copy(out_a, tile_a).wait()
        out_copy(out_b, tile_b).wait()
        k_next = (k + 1) % n_pairs
        pass_next = jj + (k + 1) // n_pairs
        for cp in chunk_copies(1 - cs, staged_block(pass_next), k_next):
            cp.start(priority=CHUNK_DMA_PRIORITY)
        for cp in in_copies(tile_a + 2):
            cp.start()

        res_a = tile_result(tile_a)
        for cp in in_copies(tile_a + 3):
            cp.start()
        obuf[out_a] = res_a
        obuf[out_b] = tile_result(tile_b)
        out_copy(out_a, tile_a).start()
        out_copy(out_b, tile_b).start()

        g = pl.ds(pl.multiple_of(k * g_rows, g_rows), g_rows)
        b = pl.ds(pl.multiple_of(k * b_rows, b_rows), b_rows)
        wg0_buf[sslot, g, :] = cg0[cs].astype(wg0_buf.dtype)
        wg1_buf[sslot, g, :] = cg1[cs].astype(wg1_buf.dtype)
        wb0_buf[sslot, b, :] = cb0[cs].astype(wb0_buf.dtype)
        wb1_buf[sslot, b, :] = cb1[cs].astype(wb1_buf.dtype)
        return carry

    lax.fori_loop(0, n_pairs, pair, 0)

    @pl.when(jj == n_blocks - 1)
    def _():
        end = n_blocks * n_tiles
        for cp in in_copies(end) + in_copies(end + 1) + chunk_copies(n_pairs % 2,
                                                                     staged_block(jj), 0):
            cp.wait()
        for o in range(obuf.shape[0]):
            out_copy(o, o).wait()


def _merge(h, y_pool, y_conv, w_in2d, gate_b, w_branch, first_blocks):
    m = h.shape[0]
    tm = ROW_TILE
    nb = N_MODEL_BLOCKS
    n_pairs = m // (2 * tm)
    g0_first = 6 * N_HALF_BLOCKS
    g1_first = g0_first + nb
    bias = pl.BlockSpec((1, COL_BLOCK), lambda jj: (0, jj))
    hbm = pl.BlockSpec(memory_space=pl.ANY)
    f32, bf16 = jnp.float32, jnp.bfloat16
    return pl.pallas_call(
        functools.partial(_merge_kernel, n_pairs=n_pairs, g0_first=g0_first,
                          g1_first=g1_first),
        grid=(nb,),
        in_specs=[bias, bias] + [hbm] * 9,
        out_specs=hbm,
        out_shape=jax.ShapeDtypeStruct((m, D_MODEL), bf16),
        scratch_shapes=[pltpu.VMEM((2, D_MODEL, COL_BLOCK), bf16),
                        pltpu.VMEM((2, D_MODEL, COL_BLOCK), bf16),
                        pltpu.VMEM((2, HALF, COL_BLOCK), bf16),
                        pltpu.VMEM((2, HALF, COL_BLOCK), bf16),
                        pltpu.VMEM((3, tm, D_MODEL), bf16),
                        pltpu.VMEM((3, tm, HALF), bf16),
                        pltpu.VMEM((3, tm, HALF), bf16),
                        pltpu.VMEM((4, tm, COL_BLOCK), bf16),
                        pltpu.VMEM((2, D_MODEL // n_pairs, COL_BLOCK), f32),
                        pltpu.VMEM((2, D_MODEL // n_pairs, COL_BLOCK), f32),
                        pltpu.VMEM((2, HALF // n_pairs, COL_BLOCK), f32),
                        pltpu.VMEM((2, HALF // n_pairs, COL_BLOCK), f32),
                        pltpu.SemaphoreType.DMA((4,)),
                        pltpu.SemaphoreType.DMA((9,)),
                        pltpu.SemaphoreType.DMA((4,)),
                        pltpu.SemaphoreType.DMA((8,))],
        compiler_params=_params(1),
        name="branch_merge",
    )(gate_b[0:1], gate_b[1:2], h, y_pool, y_conv, w_in2d, w_branch, *first_blocks)


def _out_kernel(mg_ref, w_ref, x_ref, fw_ref, o_ref, rows, sumsq):
    i = pl.program_id(0)
    n = pl.program_id(1)
    n_tiles = pl.num_programs(0) - 1

    def emit_previous():
        scale = lax.rsqrt(sumsq[(i + 1) % 2] * (1.0 / D_MODEL) + NORM_EPS)
        o_ref[...] = rows[n] * scale * fw_ref[...]

    def project():
        xn = x_ref[...] + _dot(mg_ref[...], w_ref[...])
        rows[n] = xn
        part = jnp.sum(xn * xn, axis=-1, keepdims=True)
        cur = i % 2
        sumsq[cur] = jnp.where(n == 0, part, sumsq[cur] + part)

    @pl.when(i == 0)
    def _():
        project()

    @pl.when((i > 0) & (i < n_tiles))
    def _():
        emit_previous()
        project()

    @pl.when(i == n_tiles)
    def _():
        emit_previous()


def _out_proj(merged, w_out_bf, x2d, final_w):
    m = x2d.shape[0]
    tm = OUT_ROW_TILE
    tn = OUT_COL_BLOCK
    n_tiles = m // tm
    n_blocks = D_MODEL // tn
    last = n_tiles - 1
    in_row = lambda i: jnp.minimum(i, last)
    in_col = lambda i, n: jnp.where(i < n_tiles, n, n_blocks - 1)
    return pl.pallas_call(
        _out_kernel,
        grid=(n_tiles + 1, n_blocks),
        in_specs=[
            pl.BlockSpec((tm, D_MODEL), lambda i, n: (in_row(i), 0)),
            pl.BlockSpec((D_MODEL, tn), lambda i, n: (0, in_col(i, n))),
            pl.BlockSpec((tm, tn), lambda i, n: (in_row(i), in_col(i, n))),
            pl.BlockSpec((1, tn), lambda i, n: (0, n)),
        ],
        out_specs=pl.BlockSpec(
            (tm, tn), lambda i, n: (jnp.maximum(i - 1, 0), jnp.where(i > 0, n, 0))),
        out_shape=jax.ShapeDtypeStruct((m, D_MODEL), jnp.float32),
        scratch_shapes=[pltpu.VMEM((n_blocks, tm, tn), jnp.float32),
                        pltpu.VMEM((2, tm, 1), jnp.float32)],
        compiler_params=_params(2),
        name="out_proj_norm",
    )(merged, w_out_bf, x2d, final_w.reshape(1, D_MODEL))


def kernel(x, norm_w, w_in, pool_w, pool_scale, conv_w, conv_b, gate_b, w_branch, w_out,
           final_norm_w):
    b, s, d = x.shape
    assert (s, d) == (SEQ, D_MODEL) and norm_w.shape[0] == 1
    x2d = x.reshape(b * s, d)
    w_in2d = w_in[0]

    nb = N_HALF_BLOCKS
    conv_first_cols = (2 * nb, 3 * nb, 4 * nb, 5 * nb)
    h = _rmsnorm_bf16(x2d, norm_w[0])
    y_pool, conv_first = _pool_mixer(h, w_in2d, pool_w[0], pool_scale[0], conv_first_cols)
    y_conv, merge_first, w_out_bf = _conv_mixer(h, w_in2d, conv_w[0], conv_b[0], conv_first,
                                                w_branch[0], w_out[0])
    merged = _merge(h, y_pool, y_conv, w_in2d, gate_b[0], w_branch[0], merge_first)
    out = _out_proj(merged, w_out_bf, x2d, final_norm_w)
    return out.reshape(b, s, d)
```

```python
import functools

import jax
import jax.numpy as jnp
from jax import lax
from jax.experimental import pallas as pl
from jax.experimental.pallas import tpu as pltpu

D_MODEL = 4096
SEQ = 4096
HALF = D_MODEL // 2
POOL_WINDOWS = (2, 4, 8, 16)
N_GROUPS = len(POOL_WINDOWS)
GROUP_DIM = HALF // N_GROUPS
CONV_K = 3
NORM_EPS = 1e-6

COL_BLOCK = GROUP_DIM
N_HALF_BLOCKS = HALF // COL_BLOCK
N_MODEL_BLOCKS = D_MODEL // COL_BLOCK
OUT_COL_BLOCK = 512
OUT_ROW_TILE = 1024
ROW_TILE = 512
POOL_ROW_TILE = 1024
POOL_HALO = 16
POOL_PAD = 16
POOL_TOP = POOL_PAD + POOL_HALO
CONV_HALO = 8

VMEM_LIMIT_BYTES = 62 * 1024 * 1024
CHUNK_DMA_PRIORITY = 1


def _dot(a, b):
    return jnp.dot(a, b, preferred_element_type=jnp.float32)


def _silu(z):
    return z / (1.0 + jnp.exp(-z))


def _sigmoid(z):
    return 1.0 / (1.0 + jnp.exp(-z))


def _params(n_axes):
    return pltpu.CompilerParams(dimension_semantics=("arbitrary",) * n_axes,
                                vmem_limit_bytes=VMEM_LIMIT_BYTES)


def _rmsnorm_kernel(x_ref, w_ref, o_ref):
    x = x_ref[...]
    ms = jnp.mean(x * x, axis=-1, keepdims=True)
    o_ref[...] = (x * lax.rsqrt(ms + NORM_EPS) * w_ref[...]).astype(o_ref.dtype)


def _rmsnorm_bf16(x2d, w):
    m, d = x2d.shape
    tm = ROW_TILE
    return pl.pallas_call(
        _rmsnorm_kernel,
        grid=(m // tm,),
        in_specs=[pl.BlockSpec((tm, d), lambda i: (i, 0)),
                  pl.BlockSpec((1, d), lambda i: (0, 0))],
        out_specs=pl.BlockSpec((tm, d), lambda i: (i, 0)),
        out_shape=jax.ShapeDtypeStruct((m, d), jnp.bfloat16),
        compiler_params=_params(1),
        name="rmsnorm_in",
    )(x2d, w.reshape(1, d))


def _stage_chunks(slot, i, pairs, fold=None):
    for k, (chunk_ref, buf) in enumerate(pairs):
        rows, cols = chunk_ref.shape[-2:]
        r0 = pl.multiple_of(i * rows, rows)
        chunk = chunk_ref[...].reshape(rows, cols).astype(buf.dtype)
        if k == 0 and fold is not None:
            chunk = _dot(chunk, fold).astype(buf.dtype)
        buf[slot, pl.ds(r0, rows), :] = chunk


def _staged_steps(jj, stage, compute):
    @pl.when(jj == 0)
    def _():
        stage()

    @pl.when(jj > 0)
    def _():
        compute()
        stage()


def _resident_maps(n_blocks, n_tiles):
    def row(jj, i):
        return jnp.where(jj > 0, i, 0)

    def col(jj):
        return jnp.maximum(jj - 1, 0)

    def chunk(jj, i):
        return jnp.where(jj < n_blocks, i, n_tiles - 1)

    def staged(jj):
        return jnp.minimum(jj, n_blocks - 1)

    return row, col, chunk, staged


def _prestaged_maps(n_blocks, n_tiles):
    def chunk(jj, i):
        return jnp.where(jj + 1 < n_blocks, i, n_tiles - 1)

    def staged(jj):
        return jnp.minimum(jj + 1, n_blocks - 1)

    return chunk, staged


def _load_prestaged(first_step, pairs, sems):
    @pl.when(first_step)
    def _():
        copies = [pltpu.make_async_copy(src, buf.at[0], sems.at[k])
                  for k, (src, buf) in enumerate(pairs)]
        for cp in copies:
            cp.start()
        for cp in copies:
            cp.wait()


def _cast_rows(pairs):
    for src_ref, dst_ref in pairs:
        dst_ref[...] = src_ref[...].astype(dst_ref.dtype)


def _pool_kernel(h_ref, wu_ref, wz_ref, pw_ref, ps_ref, n0_ref, n1_ref, n2_ref, n3_ref,
                 y_ref, p0_ref, p1_ref, p2_ref, p3_ref,
                 wu_buf, wz_buf, ubuf, s2buf, s4buf, s8buf, ucarry, *, tiles_per_seq):
    jj = pl.program_id(0)
    i = pl.program_id(1)
    tm = h_ref.shape[0]

    @pl.when((jj == 0) & (i == 0))
    def _():
        ucarry[...] = jnp.zeros_like(ucarry)

    def stage():
        _stage_chunks(jj % 2, i, [(wu_ref, wu_buf), (wz_ref, wz_buf)],
                      fold=pw_ref[0].astype(jnp.bfloat16))

    def compute():
        _cast_rows([(n0_ref, p0_ref), (n1_ref, p1_ref), (n2_ref, p2_ref), (n3_ref, p3_ref)])
        g = jj - 1
        slot = g % 2
        seq_tile = i % tiles_per_seq
        end = POOL_TOP + tm
        h = h_ref[...]
        u = _dot(h, wu_buf[slot])
        z = _dot(h, wz_buf[slot])

        ubuf[0:POOL_PAD, :] = jnp.zeros((POOL_PAD, COL_BLOCK), jnp.float32)
        ubuf[POOL_PAD:POOL_TOP, :] = jnp.where(seq_tile != 0, ucarry[...], 0.0)
        ubuf[POOL_TOP:end, :] = u
        ucarry[...] = ubuf[end - POOL_HALO:end, :]

        def shifted_sum(src, lo, shift):
            return src[lo:end, :] + src[lo - shift:end - shift, :]

        s2buf[8:end, :] = shifted_sum(ubuf, 8, 1)
        s4buf[16:end, :] = shifted_sum(s2buf, 16, 2)
        s8buf[24:end, :] = shifted_sum(s4buf, 24, 4)
        win = jnp.where(g == 0, s2buf[POOL_TOP:end, :],
                        jnp.where(g == 1, s4buf[POOL_TOP:end, :],
                                  jnp.where(g == 2, s8buf[POOL_TOP:end, :],
                                            shifted_sum(s8buf, POOL_TOP, 8))))
        window = lax.shift_left(jnp.int32(2), g)
        t1 = seq_tile * tm + 1 + lax.broadcasted_iota(jnp.int32, (tm, 1), 0)
        inv_count = 1.0 / jnp.minimum(t1, window).astype(jnp.float32)
        mixed = win * inv_count - ubuf[POOL_TOP:end, :]
        ubuf[POOL_TOP:end, :] = mixed * ps_ref[...]
        y_ref[...] = (ubuf[POOL_TOP:end, :] * _silu(z)).astype(y_ref.dtype)

    _staged_steps(jj, stage, compute)


def _pool_mixer(h, w_in2d, pool_w, pool_scale, next_first_blocks):
    m = h.shape[0]
    tm = POOL_ROW_TILE
    nb = N_GROUPS
    n_tiles = m // tm
    chunk_rows = D_MODEL // n_tiles
    row, col, chunk, staged = _resident_maps(nb, n_tiles)
    w_chunk = lambda first: pl.BlockSpec((chunk_rows, COL_BLOCK),
                                         lambda jj, i: (chunk(jj, i), first + staged(jj)))
    side_rows = D_MODEL // (nb * n_tiles)
    side_step = lambda jj, i: jnp.where(jj > 0, (jj - 1) * n_tiles + i, 0)
    side_in = lambda blk: pl.BlockSpec((side_rows, COL_BLOCK),
                                       lambda jj, i: (side_step(jj, i), blk))
    side_out = pl.BlockSpec((side_rows, COL_BLOCK), lambda jj, i: (side_step(jj, i), 0))
    w_buf = pltpu.VMEM((2, D_MODEL, COL_BLOCK), jnp.bfloat16)
    buf = pltpu.VMEM((POOL_TOP + tm, COL_BLOCK), jnp.float32)
    n_side = len(next_first_blocks)
    outs = pl.pallas_call(
        functools.partial(_pool_kernel, tiles_per_seq=SEQ // tm),
        grid=(nb + 1, n_tiles),
        in_specs=[
            pl.BlockSpec((tm, D_MODEL), lambda jj, i: (row(jj, i), 0)),
            w_chunk(0), w_chunk(nb),
            pl.BlockSpec((1, GROUP_DIM, GROUP_DIM), lambda jj, i: (staged(jj), 0, 0)),
            pl.BlockSpec((1, COL_BLOCK), lambda jj, i: (0, col(jj))),
        ] + [side_in(blk) for blk in next_first_blocks],
        out_specs=[pl.BlockSpec((tm, COL_BLOCK), lambda jj, i: (row(jj, i), col(jj)))]
        + [side_out] * n_side,
        out_shape=[jax.ShapeDtypeStruct((m, HALF), jnp.bfloat16)]
        + [jax.ShapeDtypeStruct((D_MODEL, COL_BLOCK), jnp.bfloat16)] * n_side,
        scratch_shapes=[w_buf, w_buf, buf, buf, buf, buf,
                        pltpu.VMEM((POOL_HALO, COL_BLOCK), jnp.float32)],
        compiler_params=_params(2),
        name="pool_mixer",
    )(h, w_in2d, w_in2d, pool_w, pool_scale.reshape(1, HALF), *([w_in2d] * n_side))
    return outs[0], outs[1:]


def _conv_kernel(h_ref, wu_ref, wb_ref, wc_ref, wz_ref, cw_ref, cb_ref,
                 wu0_ref, wb0_ref, wc0_ref, wz0_ref, n0_ref, n1_ref, n2_ref, n3_ref, n4_ref,
                 y_ref, p0_ref, p1_ref, p2_ref, p3_ref, p4_ref,
                 wu_buf, wb_buf, wc_buf, wz_buf, vbuf, vcarry, sems, *, tiles_per_seq):
    jj = pl.program_id(0)
    i = pl.program_id(1)
    tm = h_ref.shape[0]
    first_step = (jj == 0) & (i == 0)

    @pl.when(first_step)
    def _():
        vcarry[...] = jnp.zeros_like(vcarry)

    _load_prestaged(first_step, [(wu0_ref, wu_buf), (wb0_ref, wb_buf),
                                 (wc0_ref, wc_buf), (wz0_ref, wz_buf)], sems)

    def stage():
        _stage_chunks((jj + 1) % 2, i, [(wu_ref, wu_buf), (wb_ref, wb_buf),
                                        (wc_ref, wc_buf), (wz_ref, wz_buf)])

    def compute():
        _cast_rows([(n0_ref, p0_ref), (n1_ref, p1_ref), (n2_ref, p2_ref), (n3_ref, p3_ref),
                    (n4_ref, p4_ref)])
        slot = jj % 2
        seq_tile = i % tiles_per_seq
        h = h_ref[...]
        u = _dot(h, wu_buf[slot])
        c_gate = _dot(h, wc_buf[slot])
        vbuf[0:CONV_HALO, :] = jnp.where(seq_tile != 0, vcarry[...], 0.0)
        vbuf[CONV_HALO:CONV_HALO + tm, :] = c_gate * u
        vcarry[...] = vbuf[tm:CONV_HALO + tm, :]
        cw = cw_ref[...]
        y = (cb_ref[...]
             + cw[0:1, :] * vbuf[CONV_HALO - 2:CONV_HALO - 2 + tm, :]
             + cw[1:2, :] * vbuf[CONV_HALO - 1:CONV_HALO - 1 + tm, :]
             + cw[2:3, :] * vbuf[CONV_HALO:CONV_HALO + tm, :])
        gated = y * _silu(_dot(h, wz_buf[slot]))
        y_ref[...] = (_dot(h, wb_buf[slot]) * gated).astype(y_ref.dtype)

    compute()
    stage()


def _conv_mixer(h, w_in2d, conv_w, conv_b, first_blocks, w_branch, w_out2d):
    m = h.shape[0]
    tm = ROW_TILE
    nb = N_HALF_BLOCKS
    n_tiles = m // tm
    chunk_rows = D_MODEL // n_tiles
    chunk, staged = _prestaged_maps(nb, n_tiles)
    w_chunk = lambda first: pl.BlockSpec((chunk_rows, COL_BLOCK),
                                         lambda jj, i: (chunk(jj, i), first + staged(jj)))
    hbm = pl.BlockSpec(memory_space=pl.ANY)
    n_steps = nb * n_tiles
    step = lambda jj, i: jj * n_tiles + i
    gate_rows = D_MODEL // n_steps
    branch_rows = HALF // n_steps
    gate_in = lambda blk: pl.BlockSpec((gate_rows, COL_BLOCK), lambda jj, i: (step(jj, i), blk))
    branch_in = lambda b: pl.BlockSpec((1, branch_rows, COL_BLOCK),
                                       lambda jj, i: (b, step(jj, i), 0))
    gate_out = pl.BlockSpec((gate_rows, COL_BLOCK), lambda jj, i: (step(jj, i), 0))
    branch_out = pl.BlockSpec((1, branch_rows, COL_BLOCK), lambda jj, i: (0, step(jj, i), 0))
    g0_first = 6 * N_HALF_BLOCKS
    g1_first = g0_first + N_MODEL_BLOCKS
    wo_slice = pl.BlockSpec((D_MODEL // n_steps, D_MODEL), lambda jj, i: (step(jj, i), 0))
    w_buf = pltpu.VMEM((2, D_MODEL, COL_BLOCK), jnp.bfloat16)
    outs = pl.pallas_call(
        functools.partial(_conv_kernel, tiles_per_seq=SEQ // tm),
        grid=(nb, n_tiles),
        in_specs=[
            pl.BlockSpec((tm, D_MODEL), lambda jj, i: (i, 0)),
            w_chunk(2 * nb), w_chunk(3 * nb), w_chunk(4 * nb), w_chunk(5 * nb),
            pl.BlockSpec((CONV_K, COL_BLOCK), lambda jj, i: (0, jj)),
            pl.BlockSpec((1, COL_BLOCK), lambda jj, i: (0, jj)),
            hbm, hbm, hbm, hbm,
            gate_in(g0_first), gate_in(g1_first), branch_in(0), branch_in(1), wo_slice,
        ],
        out_specs=[pl.BlockSpec((tm, COL_BLOCK), lambda jj, i: (i, jj)),
                   gate_out, gate_out, branch_out, branch_out, wo_slice],
        out_shape=[jax.ShapeDtypeStruct((m, HALF), jnp.bfloat16),
                   jax.ShapeDtypeStruct((D_MODEL, COL_BLOCK), jnp.bfloat16),
                   jax.ShapeDtypeStruct((D_MODEL, COL_BLOCK), jnp.bfloat16),
                   jax.ShapeDtypeStruct((1, HALF, COL_BLOCK), jnp.bfloat16),
                   jax.ShapeDtypeStruct((1, HALF, COL_BLOCK), jnp.bfloat16),
                   jax.ShapeDtypeStruct((D_MODEL, D_MODEL), jnp.bfloat16)],
        scratch_shapes=[w_buf, w_buf, w_buf, w_buf,
                        pltpu.VMEM((CONV_HALO + tm, COL_BLOCK), jnp.float32),
                        pltpu.VMEM((CONV_HALO, COL_BLOCK), jnp.float32),
                        pltpu.SemaphoreType.DMA((4,))],
        compiler_params=_params(2),
        name="conv_mixer",
    )(h, w_in2d, w_in2d, w_in2d, w_in2d, conv_w, conv_b.reshape(1, HALF),
      *first_blocks, w_in2d, w_in2d, w_branch, w_branch, w_out2d)
    return outs[0], outs[1:5], outs[5]


def _merge_kernel(gb0_ref, gb1_ref, h_hbm, yp_hbm, yc_hbm, win_hbm, wbr_hbm,
                  wg00_ref, wg10_ref, wb00_ref, wb10_ref, o_hbm,
                  wg0_buf, wg1_buf, wb0_buf, wb1_buf, hbuf, ypbuf, ycbuf, obuf,
                  cg0, cg1, cb0, cb1, sem_first, sem_in, sem_out, sem_ch,
                  *, n_pairs, g0_first, g1_first):
    jj = pl.program_id(0)
    n_blocks = pl.num_programs(0)
    tm = hbuf.shape[1]
    n_tiles = 2 * n_pairs
    n_in_slots = hbuf.shape[0]
    g_rows = cg0.shape[1]
    b_rows = cb0.shape[1]

    def rows_of(tile):
        return pl.ds(pl.multiple_of((tile % n_tiles) * tm, tm), tm)

    def cols_of(block):
        return pl.ds(pl.multiple_of(block * COL_BLOCK, COL_BLOCK), COL_BLOCK)

    def in_copies(tile):
        s = tile % n_in_slots
        rows = rows_of(tile)
        return [pltpu.make_async_copy(h_hbm.at[rows, :], hbuf.at[s], sem_in.at[3 * s]),
                pltpu.make_async_copy(yp_hbm.at[rows, :], ypbuf.at[s], sem_in.at[3 * s + 1]),
                pltpu.make_async_copy(yc_hbm.at[rows, :], ycbuf.at[s], sem_in.at[3 * s + 2])]

    def start_in(tile):
        for cp in in_copies(tile):
            cp.start()

    def out_copy(o, tile):
        return pltpu.make_async_copy(obuf.at[o], o_hbm.at[jj, rows_of(tile), :],
                                     sem_out.at[o])

    def chunk_copies(cs, block, k):
        g = pl.ds(pl.multiple_of(k * g_rows, g_rows), g_rows)
        b = pl.ds(pl.multiple_of(k * b_rows, b_rows), b_rows)
        return [
            pltpu.make_async_copy(win_hbm.at[g, cols_of(g0_first + block)], cg0.at[cs],
                                  sem_ch.at[4 * cs]),
            pltpu.make_async_copy(win_hbm.at[g, cols_of(g1_first + block)], cg1.at[cs],
                                  sem_ch.at[4 * cs + 1]),
            pltpu.make_async_copy(wbr_hbm.at[0, b, cols_of(block)], cb0.at[cs],
                                  sem_ch.at[4 * cs + 2]),
            pltpu.make_async_copy(wbr_hbm.at[1, b, cols_of(block)], cb1.at[cs],
                                  sem_ch.at[4 * cs + 3]),
        ]

    def staged_block(p):
        return jnp.minimum(p + 1, n_blocks - 1)

    @pl.when(jj == 0)
    def _():
        first = [pltpu.make_async_copy(src, buf.at[0], sem_first.at[k])
                 for k, (src, buf) in enumerate([(wg00_ref, wg0_buf), (wg10_ref, wg1_buf),
                                                 (wb00_ref.at[0], wb0_buf),
                                                 (wb10_ref.at[0], wb1_buf)])]
        for cp in first:
            cp.start()
        obuf[...] = jnp.zeros_like(obuf)
        for o in range(obuf.shape[0]):
            out_copy(o, o).start()
        for tile in (0, 1):
            start_in(tile)
        for cp in chunk_copies(0, staged_block(0), 0):
            cp.start(priority=CHUNK_DMA_PRIORITY)
        for cp in first:
            cp.wait()

    wslot = jj % 2
    sslot = (jj + 1) % 2

    def tile_result(tile):
        s = tile % n_in_slots
        h = hbuf[s]
        g0 = _sigmoid(_dot(h, wg0_buf[wslot]) + gb0_ref[...])
        br0 = _dot(ypbuf[s], wb0_buf[wslot])
        g1 = _sigmoid(_dot(h, wg1_buf[wslot]) + gb1_ref[...])
        br1 = _dot(ycbuf[s], wb1_buf[wslot])
        return (g0 * br0 + g1 * br1).astype(obuf.dtype)

    def pair(k, carry):
        tile_a = jj * n_tiles + 2 * k
        tile_b = tile_a + 1
        out_a = 2 * (k % 2)
        out_b = out_a + 1
        cs = k % 2
        for cp in in_copies(tile_a) + in_copies(tile_b) + chunk_copies(cs, staged_block(jj), k):
            cp.wait()
        out_copy(out_a, tile_a).wait()
        out_copy(out_b, tile_b).wait()
        k_next = (k + 1) % n_pairs
        pass_next = jj + (k + 1) // n_pairs
        for cp in chunk_copies(1 - cs, staged_block(pass_next), k_next):
            cp.start(priority=CHUNK_DMA_PRIORITY)
        start_in(tile_a + 2)

        res_a = tile_result(tile_a)
        start_in(tile_a + 3)
        obuf[out_a] = res_a
        obuf[out_b] = tile_result(tile_b)
        out_copy(out_a, tile_a).start()
        out_copy(out_b, tile_b).start()

        g = pl.ds(pl.multiple_of(k * g_rows, g_rows), g_rows)
        b = pl.ds(pl.multiple_of(k * b_rows, b_rows), b_rows)
        wg0_buf[sslot, g, :] = cg0[cs].astype(wg0_buf.dtype)
        wg1_buf[sslot, g, :] = cg1[cs].astype(wg1_buf.dtype)
        wb0_buf[sslot, b, :] = cb0[cs].astype(wb0_buf.dtype)
        wb1_buf[sslot, b, :] = cb1[cs].astype(wb1_buf.dtype)
        return carry

    lax.fori_loop(0, n_pairs, pair, 0)

    @pl.when(jj == n_blocks - 1)
    def _():
        end = n_blocks * n_tiles
        for cp in in_copies(end) + in_copies(end + 1) + chunk_copies(n_pairs % 2,
                                                                     staged_block(jj), 0):
            cp.wait()
        for o in range(obuf.shape[0]):
            out_copy(o, o).wait()


def _merge(h, y_pool, y_conv, w_in2d, gate_b, w_branch, first_blocks):
    m = h.shape[0]
    tm = ROW_TILE
    nb = N_MODEL_BLOCKS
    n_pairs = m // (2 * tm)
    g0_first = 6 * N_HALF_BLOCKS
    g1_first = g0_first + nb
    bias = pl.BlockSpec((1, COL_BLOCK), lambda jj: (0, jj))
    hbm = pl.BlockSpec(memory_space=pl.ANY)
    f32, bf16 = jnp.float32, jnp.bfloat16
    return pl.pallas_call(
        functools.partial(_merge_kernel, n_pairs=n_pairs, g0_first=g0_first,
                          g1_first=g1_first),
        grid=(nb,),
        in_specs=[bias, bias] + [hbm] * 9,
        out_specs=hbm,
        out_shape=jax.ShapeDtypeStruct((nb, m, COL_BLOCK), bf16),
        scratch_shapes=[pltpu.VMEM((2, D_MODEL, COL_BLOCK), bf16),
                        pltpu.VMEM((2, D_MODEL, COL_BLOCK), bf16),
                        pltpu.VMEM((2, HALF, COL_BLOCK), bf16),
                        pltpu.VMEM((2, HALF, COL_BLOCK), bf16),
                        pltpu.VMEM((3, tm, D_MODEL), bf16),
                        pltpu.VMEM((3, tm, HALF), bf16),
                        pltpu.VMEM((3, tm, HALF), bf16),
                        pltpu.VMEM((4, tm, COL_BLOCK), bf16),
                        pltpu.VMEM((2, D_MODEL // n_pairs, COL_BLOCK), f32),
                        pltpu.VMEM((2, D_MODEL // n_pairs, COL_BLOCK), f32),
                        pltpu.VMEM((2, HALF // n_pairs, COL_BLOCK), f32),
                        pltpu.VMEM((2, HALF // n_pairs, COL_BLOCK), f32),
                        pltpu.SemaphoreType.DMA((4,)),
                        pltpu.SemaphoreType.DMA((9,)),
                        pltpu.SemaphoreType.DMA((4,)),
                        pltpu.SemaphoreType.DMA((8,))],
        compiler_params=_params(1),
        name="branch_merge",
    )(gate_b[0:1], gate_b[1:2], h, y_pool, y_conv, w_in2d, w_branch, *first_blocks)


def _out_kernel(mg_ref, w_ref, x_ref, fw_ref, o_ref, rows, sumsq):
    i = pl.program_id(0)
    n = pl.program_id(1)
    n_tiles = pl.num_programs(0) - 1

    def emit_previous():
        scale = lax.rsqrt(sumsq[(i + 1) % 2] * (1.0 / D_MODEL) + NORM_EPS)
        o_ref[...] = rows[n] * scale * fw_ref[...]

    def project():
        mg = jnp.concatenate([mg_ref[b] for b in range(mg_ref.shape[0])], axis=1)
        xn = x_ref[...] + _dot(mg, w_ref[...])
        rows[n] = xn
        part = jnp.sum(xn * xn, axis=-1, keepdims=True)
        cur = i % 2
        sumsq[cur] = jnp.where(n == 0, part, sumsq[cur] + part)

    @pl.when(i == 0)
    def _():
        project()

    @pl.when((i > 0) & (i < n_tiles))
    def _():
        emit_previous()
        project()

    @pl.when(i == n_tiles)
    def _():
        emit_previous()


def _out_proj(merged, w_out_bf, x2d, final_w):
    m = x2d.shape[0]
    tm = OUT_ROW_TILE
    tn = OUT_COL_BLOCK
    n_tiles = m // tm
    n_blocks = D_MODEL // tn
    last = n_tiles - 1
    in_row = lambda i: jnp.minimum(i, last)
    in_col = lambda i, n: jnp.where(i < n_tiles, n, n_blocks - 1)
    return pl.pallas_call(
        _out_kernel,
        grid=(n_tiles + 1, n_blocks),
        in_specs=[
            pl.BlockSpec((N_MODEL_BLOCKS, tm, COL_BLOCK), lambda i, n: (0, in_row(i), 0)),
            pl.BlockSpec((D_MODEL, tn), lambda i, n: (0, in_col(i, n))),
            pl.BlockSpec((tm, tn), lambda i, n: (in_row(i), in_col(i, n))),
            pl.BlockSpec((1, tn), lambda i, n: (0, n)),
        ],
        out_specs=pl.BlockSpec(
            (tm, tn), lambda i, n: (jnp.maximum(i - 1, 0), jnp.where(i > 0, n, 0))),
        out_shape=jax.ShapeDtypeStruct((m, D_MODEL), jnp.float32),
        scratch_shapes=[pltpu.VMEM((n_blocks, tm, tn), jnp.float32),
                        pltpu.VMEM((2, tm, 1), jnp.float32)],
        compiler_params=_params(2),
        name="out_proj_norm",
    )(merged, w_out_bf, x2d, final_w.reshape(1, D_MODEL))


def kernel(x, norm_w, w_in, pool_w, pool_scale, conv_w, conv_b, gate_b, w_branch, w_out,
           final_norm_w):
    b, s, d = x.shape
    assert (s, d) == (SEQ, D_MODEL) and norm_w.shape[0] == 1
    x2d = x.reshape(b * s, d)
    w_in2d = w_in[0]

    nb = N_HALF_BLOCKS
    conv_first_cols = (2 * nb, 3 * nb, 4 * nb, 5 * nb)
    h = _rmsnorm_bf16(x2d, norm_w[0])
    y_pool, conv_first = _pool_mixer(h, w_in2d, pool_w[0], pool_scale[0], conv_first_cols)
    y_conv, merge_first, w_out_bf = _conv_mixer(h, w_in2d, conv_w[0], conv_b[0], conv_first,
                                                w_branch[0], w_out[0])
    merged = _merge(h, y_pool, y_conv, w_in2d, gate_b[0], w_branch[0], merge_first)
    out = _out_proj(merged, w_out_bf, x2d, final_norm_w)
    return out.reshape(b, s, d)
```

```python
import functools

import jax
import jax.numpy as jnp
from jax import lax
from jax.experimental import pallas as pl
from jax.experimental.pallas import tpu as pltpu

D_MODEL = 4096
SEQ = 4096
HALF = D_MODEL // 2
POOL_WINDOWS = (2, 4, 8, 16)
N_GROUPS = len(POOL_WINDOWS)
GROUP_DIM = HALF // N_GROUPS
CONV_K = 3
NORM_EPS = 1e-6

COL_BLOCK = GROUP_DIM
N_HALF_BLOCKS = HALF // COL_BLOCK
N_MODEL_BLOCKS = D_MODEL // COL_BLOCK
OUT_COL_BLOCK = 512
OUT_ROW_TILE = 1024
ROW_TILE = 512
POOL_ROW_TILE = 1024
POOL_HALO = 16
POOL_PAD = 16
POOL_TOP = POOL_PAD + POOL_HALO
CONV_HALO = 8

VMEM_LIMIT_BYTES = 62 * 1024 * 1024
CHUNK_DMA_PRIORITY = 1


def _dot(a, b):
    return jnp.dot(a, b, preferred_element_type=jnp.float32)


def _silu(z):
    return z / (1.0 + jnp.exp(-z))


def _sigmoid(z):
    return 1.0 / (1.0 + jnp.exp(-z))


def _params(n_axes):
    return pltpu.CompilerParams(dimension_semantics=("arbitrary",) * n_axes,
                                vmem_limit_bytes=VMEM_LIMIT_BYTES)


def _rmsnorm_kernel(x_ref, w_ref, o_ref):
    x = x_ref[...]
    ms = jnp.mean(x * x, axis=-1, keepdims=True)
    o_ref[...] = (x * lax.rsqrt(ms + NORM_EPS) * w_ref[...]).astype(o_ref.dtype)


def _rmsnorm_bf16(x2d, w):
    m, d = x2d.shape
    tm = ROW_TILE
    return pl.pallas_call(
        _rmsnorm_kernel,
        grid=(m // tm,),
        in_specs=[pl.BlockSpec((tm, d), lambda i: (i, 0)),
                  pl.BlockSpec((1, d), lambda i: (0, 0))],
        out_specs=pl.BlockSpec((tm, d), lambda i: (i, 0)),
        out_shape=jax.ShapeDtypeStruct((m, d), jnp.bfloat16),
        compiler_params=_params(1),
        name="rmsnorm_in",
    )(x2d, w.reshape(1, d))


def _stage_chunks(slot, i, pairs, fold=None):
    for k, (chunk_ref, buf) in enumerate(pairs):
        rows, cols = chunk_ref.shape[-2:]
        r0 = pl.multiple_of(i * rows, rows)
        chunk = chunk_ref[...].reshape(rows, cols).astype(buf.dtype)
        if k == 0 and fold is not None:
            chunk = _dot(chunk, fold).astype(buf.dtype)
        buf[slot, pl.ds(r0, rows), :] = chunk


def _staged_steps(jj, stage, compute):
    @pl.when(jj == 0)
    def _():
        stage()

    @pl.when(jj > 0)
    def _():
        compute()
        stage()


def _resident_maps(n_blocks, n_tiles):
    def row(jj, i):
        return jnp.where(jj > 0, i, 0)

    def col(jj):
        return jnp.maximum(jj - 1, 0)

    def chunk(jj, i):
        return jnp.where(jj < n_blocks, i, n_tiles - 1)

    def staged(jj):
        return jnp.minimum(jj, n_blocks - 1)

    return row, col, chunk, staged


def _prestaged_maps(n_blocks, n_tiles):
    def chunk(jj, i):
        return jnp.where(jj + 1 < n_blocks, i, n_tiles - 1)

    def staged(jj):
        return jnp.minimum(jj + 1, n_blocks - 1)

    return chunk, staged


def _load_prestaged(first_step, pairs, sems):
    @pl.when(first_step)
    def _():
        copies = [pltpu.make_async_copy(src, buf.at[0], sems.at[k])
                  for k, (src, buf) in enumerate(pairs)]
        for cp in copies:
            cp.start()
        for cp in copies:
            cp.wait()


def _cast_rows(pairs):
    for src_ref, dst_ref in pairs:
        dst_ref[...] = src_ref[...].astype(dst_ref.dtype)


def _pool_kernel(h_ref, wu_ref, wz_ref, pw_ref, ps_ref, n0_ref, n1_ref, n2_ref, n3_ref,
                 y_ref, p0_ref, p1_ref, p2_ref, p3_ref,
                 wu_buf, wz_buf, ubuf, s2buf, s4buf, s8buf, ucarry, *, tiles_per_seq):
    jj = pl.program_id(0)
    i = pl.program_id(1)
    tm = h_ref.shape[0]

    @pl.when((jj == 0) & (i == 0))
    def _():
        ucarry[...] = jnp.zeros_like(ucarry)

    def stage():
        _stage_chunks(jj % 2, i, [(wu_ref, wu_buf), (wz_ref, wz_buf)],
                      fold=pw_ref[0].astype(jnp.bfloat16))

    def compute():
        _cast_rows([(n0_ref, p0_ref), (n1_ref, p1_ref), (n2_ref, p2_ref), (n3_ref, p3_ref)])
        g = jj - 1
        slot = g % 2
        seq_tile = i % tiles_per_seq
        end = POOL_TOP + tm
        h = h_ref[...]
        u = _dot(h, wu_buf[slot])
        z = _dot(h, wz_buf[slot])

        ubuf[0:POOL_PAD, :] = jnp.zeros((POOL_PAD, COL_BLOCK), jnp.float32)
        ubuf[POOL_PAD:POOL_TOP, :] = jnp.where(seq_tile != 0, ucarry[...], 0.0)
        ubuf[POOL_TOP:end, :] = u
        ucarry[...] = ubuf[end - POOL_HALO:end, :]

        def shifted_sum(src, lo, shift):
            return src[lo:end, :] + src[lo - shift:end - shift, :]

        s2buf[8:end, :] = shifted_sum(ubuf, 8, 1)
        s4buf[16:end, :] = shifted_sum(s2buf, 16, 2)
        s8buf[24:end, :] = shifted_sum(s4buf, 24, 4)
        win = jnp.where(g == 0, s2buf[POOL_TOP:end, :],
                        jnp.where(g == 1, s4buf[POOL_TOP:end, :],
                                  jnp.where(g == 2, s8buf[POOL_TOP:end, :],
                                            shifted_sum(s8buf, POOL_TOP, 8))))
        window = lax.shift_left(jnp.int32(2), g)
        t1 = seq_tile * tm + 1 + lax.broadcasted_iota(jnp.int32, (tm, 1), 0)
        inv_count = 1.0 / jnp.minimum(t1, window).astype(jnp.float32)
        mixed = win * inv_count - ubuf[POOL_TOP:end, :]
        ubuf[POOL_TOP:end, :] = mixed * ps_ref[...]
        y_ref[...] = (ubuf[POOL_TOP:end, :] * _silu(z)).astype(y_ref.dtype)

    _staged_steps(jj, stage, compute)


def _pool_mixer(h, w_in2d, pool_w, pool_scale, next_first_blocks):
    m = h.shape[0]
    tm = POOL_ROW_TILE
    nb = N_GROUPS
    n_tiles = m // tm
    chunk_rows = D_MODEL // n_tiles
    row, col, chunk, staged = _resident_maps(nb, n_tiles)
    w_chunk = lambda first: pl.BlockSpec((chunk_rows, COL_BLOCK),
                                         lambda jj, i: (chunk(jj, i), first + staged(jj)))
    side_rows = D_MODEL // (nb * n_tiles)
    side_step = lambda jj, i: jnp.where(jj > 0, (jj - 1) * n_tiles + i, 0)
    side_in = lambda blk: pl.BlockSpec((side_rows, COL_BLOCK),
                                       lambda jj, i: (side_step(jj, i), blk))
    side_out = pl.BlockSpec((side_rows, COL_BLOCK), lambda jj, i: (side_step(jj, i), 0))
    w_buf = pltpu.VMEM((2, D_MODEL, COL_BLOCK), jnp.bfloat16)
    buf = pltpu.VMEM((POOL_TOP + tm, COL_BLOCK), jnp.float32)
    n_side = len(next_first_blocks)
    outs = pl.pallas_call(
        functools.partial(_pool_kernel, tiles_per_seq=SEQ // tm),
        grid=(nb + 1, n_tiles),
        in_specs=[
            pl.BlockSpec((tm, D_MODEL), lambda jj, i: (row(jj, i), 0)),
            w_chunk(0), w_chunk(nb),
            pl.BlockSpec((1, GROUP_DIM, GROUP_DIM), lambda jj, i: (staged(jj), 0, 0)),
            pl.BlockSpec((1, COL_BLOCK), lambda jj, i: (0, col(jj))),
        ] + [side_in(blk) for blk in next_first_blocks],
        out_specs=[pl.BlockSpec((tm, COL_BLOCK), lambda jj, i: (row(jj, i), col(jj)))]
        + [side_out] * n_side,
        out_shape=[jax.ShapeDtypeStruct((m, HALF), jnp.bfloat16)]
        + [jax.ShapeDtypeStruct((D_MODEL, COL_BLOCK), jnp.bfloat16)] * n_side,
        scratch_shapes=[w_buf, w_buf, buf, buf, buf, buf,
                        pltpu.VMEM((POOL_HALO, COL_BLOCK), jnp.float32)],
        compiler_params=_params(2),
        name="pool_mixer",
    )(h, w_in2d, w_in2d, pool_w, pool_scale.reshape(1, HALF), *([w_in2d] * n_side))
    return outs[0], outs[1:]


def _conv_kernel(h_ref, wu_ref, wb_ref, wc_ref, wz_ref, cw_ref, cb_ref,
                 wu0_ref, wb0_ref, wc0_ref, wz0_ref, n0_ref, n1_ref, n2_ref, n3_ref, n4_ref,
                 y_ref, p0_ref, p1_ref, p2_ref, p3_ref, p4_ref,
                 wu_buf, wb_buf, wc_buf, wz_buf, vbuf, vcarry, sems, *, tiles_per_seq):
    jj = pl.program_id(0)
    i = pl.program_id(1)
    tm = h_ref.shape[0]
    first_step = (jj == 0) & (i == 0)

    @pl.when(first_step)
    def _():
        vcarry[...] = jnp.zeros_like(vcarry)

    _load_prestaged(first_step, [(wu0_ref, wu_buf), (wb0_ref, wb_buf),
                                 (wc0_ref, wc_buf), (wz0_ref, wz_buf)], sems)

    def stage():
        _stage_chunks((jj + 1) % 2, i, [(wu_ref, wu_buf), (wb_ref, wb_buf),
                                        (wc_ref, wc_buf), (wz_ref, wz_buf)])

    def compute():
        _cast_rows([(n0_ref, p0_ref), (n1_ref, p1_ref), (n2_ref, p2_ref), (n3_ref, p3_ref),
                    (n4_ref, p4_ref)])
        slot = jj % 2
        seq_tile = i % tiles_per_seq
        h = h_ref[...]
        u = _dot(h, wu_buf[slot])
        c_gate = _dot(h, wc_buf[slot])
        vbuf[0:CONV_HALO, :] = jnp.where(seq_tile != 0, vcarry[...], 0.0)
        vbuf[CONV_HALO:CONV_HALO + tm, :] = c_gate * u
        vcarry[...] = vbuf[tm:CONV_HALO + tm, :]
        cw = cw_ref[...]
        y = (cb_ref[...]
             + cw[0:1, :] * vbuf[CONV_HALO - 2:CONV_HALO - 2 + tm, :]
             + cw[1:2, :] * vbuf[CONV_HALO - 1:CONV_HALO - 1 + tm, :]
             + cw[2:3, :] * vbuf[CONV_HALO:CONV_HALO + tm, :])
        gated = y * _silu(_dot(h, wz_buf[slot]))
        y_ref[...] = (_dot(h, wb_buf[slot]) * gated).astype(y_ref.dtype)

    compute()
    stage()


def _conv_mixer(h, w_in2d, conv_w, conv_b, first_blocks, w_branch, w_out2d):
    m = h.shape[0]
    tm = ROW_TILE
    nb = N_HALF_BLOCKS
    n_tiles = m // tm
    chunk_rows = D_MODEL // n_tiles
    chunk, staged = _prestaged_maps(nb, n_tiles)
    w_chunk = lambda first: pl.BlockSpec((chunk_rows, COL_BLOCK),
                                         lambda jj, i: (chunk(jj, i), first + staged(jj)))
    hbm = pl.BlockSpec(memory_space=pl.ANY)
    n_steps = nb * n_tiles
    step = lambda jj, i: jj * n_tiles + i
    gate_rows = D_MODEL // n_steps
    branch_rows = HALF // n_steps
    gate_in = lambda blk: pl.BlockSpec((gate_rows, COL_BLOCK), lambda jj, i: (step(jj, i), blk))
    branch_in = lambda b: pl.BlockSpec((1, branch_rows, COL_BLOCK),
                                       lambda jj, i: (b, step(jj, i), 0))
    gate_out = pl.BlockSpec((gate_rows, COL_BLOCK), lambda jj, i: (step(jj, i), 0))
    branch_out = pl.BlockSpec((1, branch_rows, COL_BLOCK), lambda jj, i: (0, step(jj, i), 0))
    g0_first = 6 * N_HALF_BLOCKS
    g1_first = g0_first + N_MODEL_BLOCKS
    wo_slice = pl.BlockSpec((D_MODEL // n_steps, D_MODEL), lambda jj, i: (step(jj, i), 0))
    w_buf = pltpu.VMEM((2, D_MODEL, COL_BLOCK), jnp.bfloat16)
    outs = pl.pallas_call(
        functools.partial(_conv_kernel, tiles_per_seq=SEQ // tm),
        grid=(nb, n_tiles),
        in_specs=[
            pl.BlockSpec((tm, D_MODEL), lambda jj, i: (i, 0)),
            w_chunk(2 * nb), w_chunk(3 * nb), w_chunk(4 * nb), w_chunk(5 * nb),
            pl.BlockSpec((CONV_K, COL_BLOCK), lambda jj, i: (0, jj)),
            pl.BlockSpec((1, COL_BLOCK), lambda jj, i: (0, jj)),
            hbm, hbm, hbm, hbm,
            gate_in(g0_first), gate_in(g1_first), branch_in(0), branch_in(1), wo_slice,
        ],
        out_specs=[pl.BlockSpec((tm, COL_BLOCK), lambda jj, i: (i, jj)),
                   gate_out, gate_out, branch_out, branch_out, wo_slice],
        out_shape=[jax.ShapeDtypeStruct((m, HALF), jnp.bfloat16),
                   jax.ShapeDtypeStruct((D_MODEL, COL_BLOCK), jnp.bfloat16),
                   jax.ShapeDtypeStruct((D_MODEL, COL_BLOCK), jnp.bfloat16),
                   jax.ShapeDtypeStruct((1, HALF, COL_BLOCK), jnp.bfloat16),
                   jax.ShapeDtypeStruct((1, HALF, COL_BLOCK), jnp.bfloat16),
                   jax.ShapeDtypeStruct((D_MODEL, D_MODEL), jnp.bfloat16)],
        scratch_shapes=[w_buf, w_buf, w_buf, w_buf,
                        pltpu.VMEM((CONV_HALO + tm, COL_BLOCK), jnp.float32),
                        pltpu.VMEM((CONV_HALO, COL_BLOCK), jnp.float32),
                        pltpu.SemaphoreType.DMA((4,))],
        compiler_params=_params(2),
        name="conv_mixer",
    )(h, w_in2d, w_in2d, w_in2d, w_in2d, conv_w, conv_b.reshape(1, HALF),
      *first_blocks, w_in2d, w_in2d, w_branch, w_branch, w_out2d)
    return outs[0], outs[1:5], outs[5]


def _merge_kernel(gb0_ref, gb1_ref, h_hbm, yp_hbm, yc_hbm, win_hbm, wbr_hbm,
                  wg00_ref, wg10_ref, wb00_ref, wb10_ref, o_hbm,
                  wg0_buf, wg1_buf, wb0_buf, wb1_buf, hbuf, ypbuf, ycbuf, obuf,
                  cg0, cg1, cb0, cb1, sem_first, sem_in, sem_out, sem_ch,
                  *, n_tiles, g0_first, g1_first):
    jj = pl.program_id(0)
    n_blocks = pl.num_programs(0)
    tm = hbuf.shape[1]
    n_in_slots = hbuf.shape[0]
    g_rows = cg0.shape[1]
    b_rows = cb0.shape[1]

    def rows_of(tile):
        return pl.ds(pl.multiple_of((tile % n_tiles) * tm, tm), tm)

    def cols_of(block):
        return pl.ds(pl.multiple_of(block * COL_BLOCK, COL_BLOCK), COL_BLOCK)

    def in_copies(tile):
        s = tile % n_in_slots
        rows = rows_of(tile)
        return [pltpu.make_async_copy(h_hbm.at[rows, :], hbuf.at[s], sem_in.at[3 * s]),
                pltpu.make_async_copy(yp_hbm.at[rows, :], ypbuf.at[s], sem_in.at[3 * s + 1]),
                pltpu.make_async_copy(yc_hbm.at[rows, :], ycbuf.at[s], sem_in.at[3 * s + 2])]

    def start_in(tile):
        for cp in in_copies(tile):
            cp.start()

    def out_copy(o, tile):
        return pltpu.make_async_copy(obuf.at[o], o_hbm.at[jj, rows_of(tile), :],
                                     sem_out.at[o])

    def chunk_copies(cs, block, k):
        g = pl.ds(pl.multiple_of(k * g_rows, g_rows), g_rows)
        b = pl.ds(pl.multiple_of(k * b_rows, b_rows), b_rows)
        return [
            pltpu.make_async_copy(win_hbm.at[g, cols_of(g0_first + block)], cg0.at[cs],
                                  sem_ch.at[4 * cs]),
            pltpu.make_async_copy(win_hbm.at[g, cols_of(g1_first + block)], cg1.at[cs],
                                  sem_ch.at[4 * cs + 1]),
            pltpu.make_async_copy(wbr_hbm.at[0, b, cols_of(block)], cb0.at[cs],
                                  sem_ch.at[4 * cs + 2]),
            pltpu.make_async_copy(wbr_hbm.at[1, b, cols_of(block)], cb1.at[cs],
                                  sem_ch.at[4 * cs + 3]),
        ]

    def staged_block(p):
        return jnp.minimum(p + 1, n_blocks - 1)

    @pl.when(jj == 0)
    def _():
        first = [pltpu.make_async_copy(src, buf.at[0], sem_first.at[k])
                 for k, (src, buf) in enumerate([(wg00_ref, wg0_buf), (wg10_ref, wg1_buf),
                                                 (wb00_ref.at[0], wb0_buf),
                                                 (wb10_ref.at[0], wb1_buf)])]
        for cp in first:
            cp.start()
        obuf[...] = jnp.zeros_like(obuf)
        for o in range(obuf.shape[0]):
            out_copy(o, o).start()
        for tile in (0, 1):
            start_in(tile)
        for cp in chunk_copies(0, staged_block(0), 0):
            cp.start(priority=CHUNK_DMA_PRIORITY)
        for cp in first:
            cp.wait()

    wslot = jj % 2
    sslot = (jj + 1) % 2

    def tile_result(tile):
        s = tile % n_in_slots
        h = hbuf[s]
        g0 = _sigmoid(_dot(h, wg0_buf[wslot]) + gb0_ref[...])
        br0 = _dot(ypbuf[s], wb0_buf[wslot])
        g1 = _sigmoid(_dot(h, wg1_buf[wslot]) + gb1_ref[...])
        br1 = _dot(ycbuf[s], wb1_buf[wslot])
        return (g0 * br0 + g1 * br1).astype(obuf.dtype)

    def step(k, carry):
        tile = jj * n_tiles + k
        out = tile % obuf.shape[0]
        cs = k % 2
        for cp in in_copies(tile) + chunk_copies(cs, staged_block(jj), k):
            cp.wait()
        out_copy(out, tile).wait()
        k_next = (k + 1) % n_tiles
        pass_next = jj + (k + 1) // n_tiles
        for cp in chunk_copies(1 - cs, staged_block(pass_next), k_next):
            cp.start(priority=CHUNK_DMA_PRIORITY)
        start_in(tile + 2)

        obuf[out] = tile_result(tile)
        out_copy(out, tile).start()

        g = pl.ds(pl.multiple_of(k * g_rows, g_rows), g_rows)
        b = pl.ds(pl.multiple_of(k * b_rows, b_rows), b_rows)
        wg0_buf[sslot, g, :] = cg0[cs].astype(wg0_buf.dtype)
        wg1_buf[sslot, g, :] = cg1[cs].astype(wg1_buf.dtype)
        wb0_buf[sslot, b, :] = cb0[cs].astype(wb0_buf.dtype)
        wb1_buf[sslot, b, :] = cb1[cs].astype(wb1_buf.dtype)
        return carry

    lax.fori_loop(0, n_tiles, step, 0)

    @pl.when(jj == n_blocks - 1)
    def _():
        end = n_blocks * n_tiles
        for cp in in_copies(end) + in_copies(end + 1) + chunk_copies(n_tiles % 2,
                                                                     staged_block(jj), 0):
            cp.wait()
        for o in range(obuf.shape[0]):
            out_copy(o, o).wait()


def _merge(h, y_pool, y_conv, w_in2d, gate_b, w_branch, first_blocks):
    m = h.shape[0]
    tm = ROW_TILE
    nb = N_MODEL_BLOCKS
    n_tiles = m // tm
    g0_first = 6 * N_HALF_BLOCKS
    g1_first = g0_first + nb
    bias = pl.BlockSpec((1, COL_BLOCK), lambda jj: (0, jj))
    hbm = pl.BlockSpec(memory_space=pl.ANY)
    f32, bf16 = jnp.float32, jnp.bfloat16
    return pl.pallas_call(
        functools.partial(_merge_kernel, n_tiles=n_tiles, g0_first=g0_first,
                          g1_first=g1_first),
        grid=(nb,),
        in_specs=[bias, bias] + [hbm] * 9,
        out_specs=hbm,
        out_shape=jax.ShapeDtypeStruct((nb, m, COL_BLOCK), bf16),
        scratch_shapes=[pltpu.VMEM((2, D_MODEL, COL_BLOCK), bf16),
                        pltpu.VMEM((2, D_MODEL, COL_BLOCK), bf16),
                        pltpu.VMEM((2, HALF, COL_BLOCK), bf16),
                        pltpu.VMEM((2, HALF, COL_BLOCK), bf16),
                        pltpu.VMEM((3, tm, D_MODEL), bf16),
                        pltpu.VMEM((3, tm, HALF), bf16),
                        pltpu.VMEM((3, tm, HALF), bf16),
                        pltpu.VMEM((4, tm, COL_BLOCK), bf16),
                        pltpu.VMEM((2, D_MODEL // n_tiles, COL_BLOCK), f32),
                        pltpu.VMEM((2, D_MODEL // n_tiles, COL_BLOCK), f32),
                        pltpu.VMEM((2, HALF // n_tiles, COL_BLOCK), f32),
                        pltpu.VMEM((2, HALF // n_tiles, COL_BLOCK), f32),
                        pltpu.SemaphoreType.DMA((4,)),
                        pltpu.SemaphoreType.DMA((9,)),
                        pltpu.SemaphoreType.DMA((4,)),
                        pltpu.SemaphoreType.DMA((8,))],
        compiler_params=_params(1),
        name="branch_merge",
    )(gate_b[0:1], gate_b[1:2], h, y_pool, y_conv, w_in2d, w_branch, *first_blocks)


def _out_kernel(mg_ref, w_ref, x_ref, fw_ref, o_ref, rows, sumsq):
    i = pl.program_id(0)
    n = pl.program_id(1)
    n_tiles = pl.num_programs(0) - 1

    def emit_previous():
        scale = lax.rsqrt(sumsq[(i + 1) % 2] * (1.0 / D_MODEL) + NORM_EPS)
        o_ref[...] = rows[n] * scale * fw_ref[...]

    def project():
        mg = jnp.concatenate([mg_ref[b] for b in range(mg_ref.shape[0])], axis=1)
        xn = x_ref[...] + _dot(mg, w_ref[...])
        rows[n] = xn
        part = jnp.sum(xn * xn, axis=-1, keepdims=True)
        cur = i % 2
        sumsq[cur] = jnp.where(n == 0, part, sumsq[cur] + part)

    @pl.when(i == 0)
    def _():
        project()

    @pl.when((i > 0) & (i < n_tiles))
    def _():
        emit_previous()
        project()

    @pl.when(i == n_tiles)
    def _():
        emit_previous()


def _out_proj(merged, w_out_bf, x2d, final_w):
    m = x2d.shape[0]
    tm = OUT_ROW_TILE
    tn = OUT_COL_BLOCK
    n_tiles = m // tm
    n_blocks = D_MODEL // tn
    last = n_tiles - 1
    in_row = lambda i: jnp.minimum(i, last)
    in_col = lambda i, n: jnp.where(i < n_tiles, n, n_blocks - 1)
    return pl.pallas_call(
        _out_kernel,
        grid=(n_tiles + 1, n_blocks),
        in_specs=[
            pl.BlockSpec((N_MODEL_BLOCKS, tm, COL_BLOCK), lambda i, n: (0, in_row(i), 0)),
            pl.BlockSpec((D_MODEL, tn), lambda i, n: (0, in_col(i, n))),
            pl.BlockSpec((tm, tn), lambda i, n: (in_row(i), in_col(i, n))),
            pl.BlockSpec((1, tn), lambda i, n: (0, n)),
        ],
        out_specs=pl.BlockSpec(
            (tm, tn), lambda i, n: (jnp.maximum(i - 1, 0), jnp.where(i > 0, n, 0))),
        out_shape=jax.ShapeDtypeStruct((m, D_MODEL), jnp.float32),
        scratch_shapes=[pltpu.VMEM((n_blocks, tm, tn), jnp.float32),
                        pltpu.VMEM((2, tm, 1), jnp.float32)],
        compiler_params=_params(2),
        name="out_proj_norm",
    )(merged, w_out_bf, x2d, final_w.reshape(1, D_MODEL))


def kernel(x, norm_w, w_in, pool_w, pool_scale, conv_w, conv_b, gate_b, w_branch, w_out,
           final_norm_w):
    b, s, d = x.shape
    assert (s, d) == (SEQ, D_MODEL) and norm_w.shape[0] == 1
    x2d = x.reshape(b * s, d)
    w_in2d = w_in[0]

    nb = N_HALF_BLOCKS
    conv_first_cols = (2 * nb, 3 * nb, 4 * nb, 5 * nb)
    h = _rmsnorm_bf16(x2d, norm_w[0])
    y_pool, conv_first = _pool_mixer(h, w_in2d, pool_w[0], pool_scale[0], conv_first_cols)
    y_conv, merge_first, w_out_bf = _conv_mixer(h, w_in2d, conv_w[0], conv_b[0], conv_first,
                                                w_branch[0], w_out[0])
    merged = _merge(h, y_pool, y_conv, w_in2d, gate_b[0], w_branch[0], merge_first)
    out = _out_proj(merged, w_out_bf, x2d, final_norm_w)
    return out.reshape(b, s, d)
```

```python
import functools

import jax
import jax.numpy as jnp
from jax import lax
from jax.experimental import pallas as pl
from jax.experimental.pallas import tpu as pltpu

D_MODEL = 4096
SEQ = 4096
HALF = D_MODEL // 2
POOL_WINDOWS = (2, 4, 8, 16)
N_GROUPS = len(POOL_WINDOWS)
GROUP_DIM = HALF // N_GROUPS
CONV_K = 3
NORM_EPS = 1e-6

COL_BLOCK = GROUP_DIM
N_HALF_BLOCKS = HALF // COL_BLOCK
N_MODEL_BLOCKS = D_MODEL // COL_BLOCK
OUT_COL_BLOCK = 512
OUT_ROW_TILE = 1024
ROW_TILE = 512
POOL_ROW_TILE = 512
POOL_HALO = 16
POOL_PAD = 16
POOL_TOP = POOL_PAD + POOL_HALO
CONV_HALO = 8

VMEM_LIMIT_BYTES = 62 * 1024 * 1024


def _dot(a, b):
    return jnp.dot(a, b, preferred_element_type=jnp.float32)


def _silu(z):
    return z / (1.0 + jnp.exp(-z))


def _sigmoid(z):
    return 1.0 / (1.0 + jnp.exp(-z))


def _params(n_axes):
    return pltpu.CompilerParams(dimension_semantics=("arbitrary",) * n_axes,
                                vmem_limit_bytes=VMEM_LIMIT_BYTES)


def _stage_chunks(slot, i, pairs, fold=None, row_scale=None):
    for k, (chunk_ref, buf) in enumerate(pairs):
        rows, cols = chunk_ref.shape[-2:]
        r0 = pl.multiple_of(i * rows, rows)
        chunk = chunk_ref[...].reshape(rows, cols)
        if row_scale is not None:
            chunk = chunk * row_scale[...]
        chunk = chunk.astype(buf.dtype)
        if k == 0 and fold is not None:
            chunk = _dot(chunk, fold).astype(buf.dtype)
        buf[slot, pl.ds(r0, rows), :] = chunk


def _staged_steps(jj, stage, compute):
    @pl.when(jj == 0)
    def _():
        stage()

    @pl.when(jj == 1)
    def _():
        compute(True)
        stage()

    @pl.when(jj > 1)
    def _():
        compute(False)
        stage()


def _resident_maps(n_blocks, n_tiles):
    def row(jj, i):
        return jnp.where(jj > 0, i, 0)

    def col(jj):
        return jnp.maximum(jj - 1, 0)

    def chunk(jj, i):
        return jnp.where(jj < n_blocks, i, n_tiles - 1)

    def staged(jj):
        return jnp.minimum(jj, n_blocks - 1)

    return row, col, chunk, staged


def _prestaged_maps(n_blocks, n_tiles):
    def chunk(jj, i):
        return jnp.where(jj + 1 < n_blocks, i, n_tiles - 1)

    def staged(jj):
        return jnp.minimum(jj + 1, n_blocks - 1)

    return chunk, staged


def _load_prestaged(first_step, pairs, sems):
    @pl.when(first_step)
    def _():
        copies = [pltpu.make_async_copy(src, buf.at[0], sems.at[k])
                  for k, (src, buf) in enumerate(pairs)]
        for cp in copies:
            cp.start()
        for cp in copies:
            cp.wait()


def _cast_rows(pairs, row_scale=None):
    for src_ref, dst_ref in pairs:
        rows = src_ref[...]
        if row_scale is not None:
            rows = rows * row_scale[...]
        dst_ref[...] = rows.astype(dst_ref.dtype)


def _row_rms_scale(x):
    return lax.rsqrt(jnp.mean(x * x, axis=-1, keepdims=True) + NORM_EPS)


def _pool_kernel(x_ref, nws_ref, nwn_ref, wu_ref, wz_ref, pw_ref, ps_ref,
                 n0_ref, n1_ref, n2_ref, n3_ref,
                 y_ref, xb_ref, rs_ref, p0_ref, p1_ref, p2_ref, p3_ref,
                 wu_buf, wz_buf, ubuf, s2buf, s4buf, s8buf, ucarry, *, tiles_per_seq):
    jj = pl.program_id(0)
    i = pl.program_id(1)
    tm = x_ref.shape[0]

    @pl.when((jj == 0) & (i == 0))
    def _():
        ucarry[...] = jnp.zeros_like(ucarry)

    def stage():
        _stage_chunks(jj % 2, i, [(wu_ref, wu_buf), (wz_ref, wz_buf)],
                      fold=pw_ref[0].astype(jnp.bfloat16), row_scale=nws_ref)

    def compute(emit):
        _cast_rows([(n0_ref, p0_ref), (n1_ref, p1_ref), (n2_ref, p2_ref), (n3_ref, p3_ref)],
                   row_scale=nwn_ref)
        g = jj - 1
        slot = g % 2
        seq_tile = i % tiles_per_seq
        end = POOL_TOP + tm
        x = x_ref[...]
        xb = x.astype(jnp.bfloat16)
        rs = _row_rms_scale(x)
        if emit:
            xb_ref[...] = xb
            rs_ref[...] = rs
        u = rs * _dot(xb, wu_buf[slot])
        z = rs * _dot(xb, wz_buf[slot])

        ubuf[0:POOL_PAD, :] = jnp.zeros((POOL_PAD, COL_BLOCK), jnp.float32)
        ubuf[POOL_PAD:POOL_TOP, :] = jnp.where(seq_tile != 0, ucarry[...], 0.0)
        ubuf[POOL_TOP:end, :] = u
        ucarry[...] = ubuf[end - POOL_HALO:end, :]

        def shifted_sum(src, lo, shift):
            return src[lo:end, :] + src[lo - shift:end - shift, :]

        s2buf[8:end, :] = shifted_sum(ubuf, 8, 1)
        s4buf[16:end, :] = shifted_sum(s2buf, 16, 2)
        s8buf[24:end, :] = shifted_sum(s4buf, 24, 4)
        win = jnp.where(g == 0, s2buf[POOL_TOP:end, :],
                        jnp.where(g == 1, s4buf[POOL_TOP:end, :],
                                  jnp.where(g == 2, s8buf[POOL_TOP:end, :],
                                            shifted_sum(s8buf, POOL_TOP, 8))))
        window = lax.shift_left(jnp.int32(2), g)
        t1 = seq_tile * tm + 1 + lax.broadcasted_iota(jnp.int32, (tm, 1), 0)
        inv_count = 1.0 / jnp.minimum(t1, window).astype(jnp.float32)
        mixed = win * inv_count - ubuf[POOL_TOP:end, :]
        ubuf[POOL_TOP:end, :] = mixed * ps_ref[...]
        y_ref[...] = (ubuf[POOL_TOP:end, :] * _silu(z)).astype(y_ref.dtype)

    _staged_steps(jj, stage, compute)


def _pool_mixer(x2d, norm_col, w_in2d, pool_w, pool_scale, next_first_blocks):
    m = x2d.shape[0]
    tm = POOL_ROW_TILE
    nb = N_GROUPS
    n_tiles = m // tm
    chunk_rows = D_MODEL // n_tiles
    row, col, chunk, staged = _resident_maps(nb, n_tiles)
    w_chunk = lambda first: pl.BlockSpec((chunk_rows, COL_BLOCK),
                                         lambda jj, i: (chunk(jj, i), first + staged(jj)))
    side_rows = D_MODEL // (nb * n_tiles)
    side_step = lambda jj, i: jnp.where(jj > 0, (jj - 1) * n_tiles + i, 0)
    side_in = lambda blk: pl.BlockSpec((side_rows, COL_BLOCK),
                                       lambda jj, i: (side_step(jj, i), blk))
    side_out = pl.BlockSpec((side_rows, COL_BLOCK), lambda jj, i: (side_step(jj, i), 0))
    once = lambda jj, i: jnp.where(jj == 1, i, jnp.where(jj == 0, 0, n_tiles - 1))
    w_buf = pltpu.VMEM((2, D_MODEL, COL_BLOCK), jnp.bfloat16)
    buf = pltpu.VMEM((POOL_TOP + tm, COL_BLOCK), jnp.float32)
    n_side = len(next_first_blocks)
    outs = pl.pallas_call(
        functools.partial(_pool_kernel, tiles_per_seq=SEQ // tm),
        grid=(nb + 1, n_tiles),
        in_specs=[
            pl.BlockSpec((tm, D_MODEL), lambda jj, i: (row(jj, i), 0)),
            pl.BlockSpec((chunk_rows, 1), lambda jj, i: (chunk(jj, i), 0)),
            pl.BlockSpec((side_rows, 1), lambda jj, i: (side_step(jj, i), 0)),
            w_chunk(0), w_chunk(nb),
            pl.BlockSpec((1, GROUP_DIM, GROUP_DIM), lambda jj, i: (staged(jj), 0, 0)),
            pl.BlockSpec((1, COL_BLOCK), lambda jj, i: (0, col(jj))),
        ] + [side_in(blk) for blk in next_first_blocks],
        out_specs=[pl.BlockSpec((tm, COL_BLOCK), lambda jj, i: (row(jj, i), col(jj))),
                   pl.BlockSpec((tm, D_MODEL), lambda jj, i: (once(jj, i), 0)),
                   pl.BlockSpec((tm, 1), lambda jj, i: (once(jj, i), 0))]
        + [side_out] * n_side,
        out_shape=[jax.ShapeDtypeStruct((m, HALF), jnp.bfloat16),
                   jax.ShapeDtypeStruct((m, D_MODEL), jnp.bfloat16),
                   jax.ShapeDtypeStruct((m, 1), jnp.float32)]
        + [jax.ShapeDtypeStruct((D_MODEL, COL_BLOCK), jnp.bfloat16)] * n_side,
        scratch_shapes=[w_buf, w_buf, buf, buf, buf, buf,
                        pltpu.VMEM((POOL_HALO, COL_BLOCK), jnp.float32)],
        compiler_params=_params(2),
        name="pool_mixer",
    )(x2d, norm_col, norm_col, w_in2d, w_in2d, pool_w, pool_scale.reshape(1, HALF),
      *([w_in2d] * n_side))
    return outs[0], outs[1], outs[2], outs[3:]


def _conv_kernel(xb_ref, rs_ref, nws_ref, nwn_ref, wu_ref, wb_ref, wc_ref, wz_ref, cw_ref, cb_ref,
                 wu0_ref, wb0_ref, wc0_ref, wz0_ref, n0_ref, n1_ref, n2_ref, n3_ref,
                 y_ref, p0_ref, p1_ref, p2_ref, p3_ref,
                 wu_buf, wb_buf, wc_buf, wz_buf, vbuf, vcarry, sems, *, tiles_per_seq):
    jj = pl.program_id(0)
    i = pl.program_id(1)
    tm = xb_ref.shape[0]
    first_step = (jj == 0) & (i == 0)

    @pl.when(first_step)
    def _():
        vcarry[...] = jnp.zeros_like(vcarry)

    _load_prestaged(first_step, [(wu0_ref, wu_buf), (wb0_ref, wb_buf),
                                 (wc0_ref, wc_buf), (wz0_ref, wz_buf)], sems)

    def stage():
        _stage_chunks((jj + 1) % 2, i, [(wu_ref, wu_buf), (wb_ref, wb_buf),
                                        (wc_ref, wc_buf), (wz_ref, wz_buf)],
                      row_scale=nws_ref)

    def compute():
        _cast_rows([(n0_ref, p0_ref), (n1_ref, p1_ref)], row_scale=nwn_ref)
        _cast_rows([(n2_ref, p2_ref), (n3_ref, p3_ref)])
        slot = jj % 2
        seq_tile = i % tiles_per_seq
        h = xb_ref[...]
        rs = rs_ref[...]
        rs2 = rs * rs
        u = _dot(h, wu_buf[slot])
        c_gate = _dot(h, wc_buf[slot])
        vbuf[0:CONV_HALO, :] = jnp.where(seq_tile != 0, vcarry[...], 0.0)
        vbuf[CONV_HALO:CONV_HALO + tm, :] = rs2 * (c_gate * u)
        vcarry[...] = vbuf[tm:CONV_HALO + tm, :]
        cw = cw_ref[...]
        y = (cb_ref[...]
             + cw[0:1, :] * vbuf[CONV_HALO - 2:CONV_HALO - 2 + tm, :]
             + cw[1:2, :] * vbuf[CONV_HALO - 1:CONV_HALO - 1 + tm, :]
             + cw[2:3, :] * vbuf[CONV_HALO:CONV_HALO + tm, :])
        gated = (rs * y) * _silu(rs * _dot(h, wz_buf[slot]))
        y_ref[...] = (_dot(h, wb_buf[slot]) * gated).astype(y_ref.dtype)

    compute()
    stage()


def _conv_mixer(xb, rs, norm_col, w_in2d, conv_w, conv_b, first_blocks, w_branch):
    m = xb.shape[0]
    tm = ROW_TILE
    nb = N_HALF_BLOCKS
    n_tiles = m // tm
    chunk_rows = D_MODEL // n_tiles
    chunk, staged = _prestaged_maps(nb, n_tiles)
    w_chunk = lambda first: pl.BlockSpec((chunk_rows, COL_BLOCK),
                                         lambda jj, i: (chunk(jj, i), first + staged(jj)))
    hbm = pl.BlockSpec(memory_space=pl.ANY)
    n_steps = nb * n_tiles
    step = lambda jj, i: jj * n_tiles + i
    gate_rows = D_MODEL // n_steps
    branch_rows = HALF // n_steps
    gate_in = lambda blk: pl.BlockSpec((gate_rows, COL_BLOCK), lambda jj, i: (step(jj, i), blk))
    branch_in = lambda b: pl.BlockSpec((1, branch_rows, COL_BLOCK),
                                       lambda jj, i: (b, step(jj, i), 0))
    gate_out = pl.BlockSpec((gate_rows, COL_BLOCK), lambda jj, i: (step(jj, i), 0))
    branch_out = pl.BlockSpec((1, branch_rows, COL_BLOCK), lambda jj, i: (0, step(jj, i), 0))
    g0_first = 6 * N_HALF_BLOCKS
    g1_first = g0_first + N_MODEL_BLOCKS
    w_buf = pltpu.VMEM((2, D_MODEL, COL_BLOCK), jnp.bfloat16)
    outs = pl.pallas_call(
        functools.partial(_conv_kernel, tiles_per_seq=SEQ // tm),
        grid=(nb, n_tiles),
        in_specs=[
            pl.BlockSpec((tm, D_MODEL), lambda jj, i: (i, 0)),
            pl.BlockSpec((tm, 1), lambda jj, i: (i, 0)),
            pl.BlockSpec((chunk_rows, 1), lambda jj, i: (chunk(jj, i), 0)),
            pl.BlockSpec((gate_rows, 1), lambda jj, i: (step(jj, i), 0)),
            w_chunk(2 * nb), w_chunk(3 * nb), w_chunk(4 * nb), w_chunk(5 * nb),
            pl.BlockSpec((CONV_K, COL_BLOCK), lambda jj, i: (0, jj)),
            pl.BlockSpec((1, COL_BLOCK), lambda jj, i: (0, jj)),
            hbm, hbm, hbm, hbm,
            gate_in(g0_first), gate_in(g1_first), branch_in(0), branch_in(1),
        ],
        out_specs=[pl.BlockSpec((tm, COL_BLOCK), lambda jj, i: (i, jj)),
                   gate_out, gate_out, branch_out, branch_out],
        out_shape=[jax.ShapeDtypeStruct((m, HALF), jnp.bfloat16),
                   jax.ShapeDtypeStruct((D_MODEL, COL_BLOCK), jnp.bfloat16),
                   jax.ShapeDtypeStruct((D_MODEL, COL_BLOCK), jnp.bfloat16),
                   jax.ShapeDtypeStruct((1, HALF, COL_BLOCK), jnp.bfloat16),
                   jax.ShapeDtypeStruct((1, HALF, COL_BLOCK), jnp.bfloat16)],
        scratch_shapes=[w_buf, w_buf, w_buf, w_buf,
                        pltpu.VMEM((CONV_HALO + tm, COL_BLOCK), jnp.float32),
                        pltpu.VMEM((CONV_HALO, COL_BLOCK), jnp.float32),
                        pltpu.SemaphoreType.DMA((4,))],
        compiler_params=_params(2),
        name="conv_mixer",
    )(xb, rs, norm_col, norm_col, w_in2d, w_in2d, w_in2d, w_in2d, conv_w,
      conv_b.reshape(1, HALF), *first_blocks, w_in2d, w_in2d, w_branch, w_branch)
    return outs[0], outs[1:]


def _merge_kernel(xb_ref, rs_ref, nws_ref, yp_ref, yc_ref, wg0_ref, wg1_ref, gb0_ref, gb1_ref,
                  wb0_ref, wb1_ref, wo_ref, wg00_ref, wg10_ref, wb00_ref, wb10_ref,
                  o_ref, wo_bf_ref, wg0_buf, wg1_buf, wb0_buf, wb1_buf, sems):
    jj = pl.program_id(0)
    i = pl.program_id(1)

    _load_prestaged((jj == 0) & (i == 0),
                    [(wg00_ref, wg0_buf), (wg10_ref, wg1_buf),
                     (wb00_ref.at[0], wb0_buf), (wb10_ref.at[0], wb1_buf)], sems)

    def stage():
        _stage_chunks((jj + 1) % 2, i, [(wg0_ref, wg0_buf), (wg1_ref, wg1_buf)],
                      row_scale=nws_ref)
        _stage_chunks((jj + 1) % 2, i, [(wb0_ref, wb0_buf), (wb1_ref, wb1_buf)])

    def compute():
        slot = jj % 2
        h = xb_ref[...]
        rs = rs_ref[...]
        g0 = _sigmoid(rs * _dot(h, wg0_buf[slot]) + gb0_ref[...])
        br0 = _dot(yp_ref[...], wb0_buf[slot])
        g1 = _sigmoid(rs * _dot(h, wg1_buf[slot]) + gb1_ref[...])
        br1 = _dot(yc_ref[...], wb1_buf[slot])
        o_ref[...] = (g0 * br0 + g1 * br1).astype(o_ref.dtype)
        wo_bf_ref[...] = wo_ref[...].astype(wo_bf_ref.dtype)

    compute()
    stage()


def _merge(xb, rs, norm_col, y_pool, y_conv, w_in2d, gate_b, w_branch, w_out2d, first_blocks):
    m = xb.shape[0]
    tm = ROW_TILE
    nb = N_MODEL_BLOCKS
    n_tiles = m // tm
    chunk, staged = _prestaged_maps(nb, n_tiles)
    g0_first = 6 * N_HALF_BLOCKS
    g1_first = g0_first + nb
    gate_chunk = lambda first: pl.BlockSpec(
        (D_MODEL // n_tiles, COL_BLOCK), lambda jj, i: (chunk(jj, i), first + staged(jj)))
    branch_chunk = lambda b: pl.BlockSpec(
        (1, HALF // n_tiles, COL_BLOCK), lambda jj, i: (b, chunk(jj, i), staged(jj)))
    tile = lambda width: pl.BlockSpec((tm, width), lambda jj, i: (i, 0))
    bias = pl.BlockSpec((1, COL_BLOCK), lambda jj, i: (0, jj))
    wo_rows = D_MODEL // (nb * n_tiles)
    wo_slice = pl.BlockSpec((wo_rows, D_MODEL), lambda jj, i: (jj * n_tiles + i, 0))
    hbm = pl.BlockSpec(memory_space=pl.ANY)
    return pl.pallas_call(
        _merge_kernel,
        grid=(nb, n_tiles),
        in_specs=[tile(D_MODEL), tile(1),
                  pl.BlockSpec((D_MODEL // n_tiles, 1), lambda jj, i: (chunk(jj, i), 0)),
                  tile(HALF), tile(HALF),
                  gate_chunk(g0_first), gate_chunk(g1_first), bias, bias,
                  branch_chunk(0), branch_chunk(1), wo_slice, hbm, hbm, hbm, hbm],
        out_specs=[pl.BlockSpec((tm, COL_BLOCK), lambda jj, i: (i, jj)), wo_slice],
        out_shape=[jax.ShapeDtypeStruct((m, D_MODEL), jnp.bfloat16),
                   jax.ShapeDtypeStruct((D_MODEL, D_MODEL), jnp.bfloat16)],
        scratch_shapes=[pltpu.VMEM((2, D_MODEL, COL_BLOCK), jnp.bfloat16),
                        pltpu.VMEM((2, D_MODEL, COL_BLOCK), jnp.bfloat16),
                        pltpu.VMEM((2, HALF, COL_BLOCK), jnp.bfloat16),
                        pltpu.VMEM((2, HALF, COL_BLOCK), jnp.bfloat16),
                        pltpu.SemaphoreType.DMA((4,))],
        compiler_params=_params(2),
        name="branch_merge",
    )(xb, rs, norm_col, y_pool, y_conv, w_in2d, w_in2d, gate_b[0:1], gate_b[1:2],
      w_branch, w_branch, w_out2d, *first_blocks)


def _out_kernel(mg_ref, w_ref, x_ref, fw_ref, o_ref, rows, sumsq):
    i = pl.program_id(0)
    n = pl.program_id(1)
    n_tiles = pl.num_programs(0) - 1

    def emit_previous():
        scale = lax.rsqrt(sumsq[(i + 1) % 2] * (1.0 / D_MODEL) + NORM_EPS)
        o_ref[...] = rows[n] * scale * fw_ref[...]

    def project():
        xn = x_ref[...] + _dot(mg_ref[...], w_ref[...])
        rows[n] = xn
        part = jnp.sum(xn * xn, axis=-1, keepdims=True)
        cur = i % 2
        sumsq[cur] = jnp.where(n == 0, part, sumsq[cur] + part)

    @pl.when(i == 0)
    def _():
        project()

    @pl.when((i > 0) & (i < n_tiles))
    def _():
        emit_previous()
        project()

    @pl.when(i == n_tiles)
    def _():
        emit_previous()


def _out_proj(merged, w_out_bf, x2d, final_w):
    m = x2d.shape[0]
    tm = OUT_ROW_TILE
    tn = OUT_COL_BLOCK
    n_tiles = m // tm
    n_blocks = D_MODEL // tn
    last = n_tiles - 1
    in_row = lambda i: jnp.minimum(i, last)
    in_col = lambda i, n: jnp.where(i < n_tiles, n, n_blocks - 1)
    return pl.pallas_call(
        _out_kernel,
        grid=(n_tiles + 1, n_blocks),
        in_specs=[
            pl.BlockSpec((tm, D_MODEL), lambda i, n: (in_row(i), 0)),
            pl.BlockSpec((D_MODEL, tn), lambda i, n: (0, in_col(i, n))),
            pl.BlockSpec((tm, tn), lambda i, n: (in_row(i), in_col(i, n))),
            pl.BlockSpec((1, tn), lambda i, n: (0, n)),
        ],
        out_specs=pl.BlockSpec(
            (tm, tn), lambda i, n: (jnp.maximum(i - 1, 0), jnp.where(i > 0, n, 0))),
        out_shape=jax.ShapeDtypeStruct((m, D_MODEL), jnp.float32),
        scratch_shapes=[pltpu.VMEM((n_blocks, tm, tn), jnp.float32),
                        pltpu.VMEM((2, tm, 1), jnp.float32)],
        compiler_params=_params(2),
        name="out_proj_norm",
    )(merged, w_out_bf, x2d, final_w.reshape(1, D_MODEL))


def kernel(x, norm_w, w_in, pool_w, pool_scale, conv_w, conv_b, gate_b, w_branch, w_out,
           final_norm_w):
    b, s, d = x.shape
    assert (s, d) == (SEQ, D_MODEL) and norm_w.shape[0] == 1
    x2d = x.reshape(b * s, d)
    w_in2d = w_in[0]

    nb = N_HALF_BLOCKS
    conv_first_cols = (2 * nb, 3 * nb, 4 * nb, 5 * nb)
    norm_col = norm_w[0].reshape(d, 1)
    y_pool, xb, rs, conv_first = _pool_mixer(x2d, norm_col, w_in2d, pool_w[0], pool_scale[0],
                                             conv_first_cols)
    y_conv, merge_first = _conv_mixer(xb, rs, norm_col, w_in2d, conv_w[0], conv_b[0],
                                      conv_first, w_branch[0])
    merged, w_out_bf = _merge(xb, rs, norm_col, y_pool, y_conv, w_in2d, gate_b[0],
                              w_branch[0], w_out[0], merge_first)
    out = _out_proj(merged, w_out_bf, x2d, final_norm_w)
    return out.reshape(b, s, d)
```

```python
import functools

import jax
import jax.numpy as jnp
from jax import lax
from jax.experimental import pallas as pl
from jax.experimental.pallas import tpu as pltpu

D_MODEL = 4096
SEQ = 4096
HALF = D_MODEL // 2
POOL_WINDOWS = (2, 4, 8, 16)
N_GROUPS = len(POOL_WINDOWS)
GROUP_DIM = HALF // N_GROUPS
CONV_K = 3
NORM_EPS = 1e-6

COL_BLOCK = GROUP_DIM
N_HALF_BLOCKS = HALF // COL_BLOCK
N_MODEL_BLOCKS = D_MODEL // COL_BLOCK
OUT_COL_BLOCK = 512
OUT_ROW_TILE = 1024
ROW_TILE = 512
POOL_ROW_TILE = 512
REST_ROW_TILE = 1024
POOL_HALO = 16
POOL_PAD = 16
POOL_TOP = POOL_PAD + POOL_HALO
CONV_HALO = 8

VMEM_LIMIT_BYTES = 62 * 1024 * 1024


def _dot(a, b):
    return jnp.dot(a, b, preferred_element_type=jnp.float32)


def _silu(z):
    return z / (1.0 + jnp.exp(-z))


def _sigmoid(z):
    return 1.0 / (1.0 + jnp.exp(-z))


def _params(n_axes):
    return pltpu.CompilerParams(dimension_semantics=("arbitrary",) * n_axes,
                                vmem_limit_bytes=VMEM_LIMIT_BYTES)


def _stage_chunks(slot, i, pairs, fold=None, row_scale=None):
    for k, (chunk_ref, buf) in enumerate(pairs):
        rows, cols = chunk_ref.shape[-2:]
        r0 = pl.multiple_of(i * rows, rows)
        chunk = chunk_ref[...].reshape(rows, cols)
        if row_scale is not None:
            chunk = chunk * row_scale[...]
        chunk = chunk.astype(buf.dtype)
        if k == 0 and fold is not None:
            chunk = _dot(chunk, fold).astype(buf.dtype)
        buf[slot, pl.ds(r0, rows), :] = chunk


def _prestaged_maps(n_blocks, n_tiles):
    def chunk(jj, i):
        return jnp.where(jj + 1 < n_blocks, i, n_tiles - 1)

    def staged(jj):
        return jnp.minimum(jj + 1, n_blocks - 1)

    return chunk, staged


def _load_prestaged(first_step, pairs, sems):
    @pl.when(first_step)
    def _():
        copies = [pltpu.make_async_copy(src, buf.at[0], sems.at[k])
                  for k, (src, buf) in enumerate(pairs)]
        for cp in copies:
            cp.start()
        for cp in copies:
            cp.wait()


def _cast_rows(pairs, row_scale=None, fold=None):
    for k, (src_ref, dst_ref) in enumerate(pairs):
        rows = src_ref[...]
        if row_scale is not None:
            rows = rows * row_scale[...]
        rows = rows.astype(dst_ref.dtype)
        if k == 0 and fold is not None:
            rows = _dot(rows, fold).astype(dst_ref.dtype)
        dst_ref[...] = rows


def _row_rms_scale(x):
    return lax.rsqrt(jnp.mean(x * x, axis=-1, keepdims=True) + NORM_EPS)


def _pool_tile(xb, rs, wu, wz, g, seq_tile, ps_ref, y_ref, ubuf, sbufs, ucarry):
    tm = xb.shape[0]
    end = POOL_TOP + tm
    u = rs * _dot(xb, wu)
    z = rs * _dot(xb, wz)

    ubuf[0:POOL_PAD, :] = jnp.zeros((POOL_PAD, COL_BLOCK), jnp.float32)
    ubuf[POOL_PAD:POOL_TOP, :] = jnp.where(seq_tile != 0, ucarry[...], 0.0)
    ubuf[POOL_TOP:end, :] = u
    ucarry[...] = ubuf[end - POOL_HALO:end, :]

    def shifted_sum(src, lo, shift):
        return src[lo:end, :] + src[lo - shift:end - shift, :]

    s2buf = sbufs[0]
    s2buf[8:end, :] = shifted_sum(ubuf, 8, 1)
    if isinstance(g, int):
        assert g == 0
        win = s2buf[POOL_TOP:end, :]
        window = POOL_WINDOWS[0]
    else:
        s4buf = sbufs[1]
        s4buf[16:end, :] = shifted_sum(s2buf, 16, 2)
        s8buf = s2buf
        s8buf[24:end, :] = shifted_sum(s4buf, 24, 4)
        win = jnp.where(g == 1, s4buf[POOL_TOP:end, :],
                        jnp.where(g == 2, s8buf[POOL_TOP:end, :],
                                  shifted_sum(s8buf, POOL_TOP, 8)))
        window = lax.shift_left(jnp.int32(2), g)
    t1 = seq_tile * tm + 1 + lax.broadcasted_iota(jnp.int32, (tm, 1), 0)
    inv_count = 1.0 / jnp.minimum(t1, window).astype(jnp.float32)
    mixed = win * inv_count - ubuf[POOL_TOP:end, :]
    ubuf[POOL_TOP:end, :] = mixed * ps_ref[...]
    y_ref[...] = (ubuf[POOL_TOP:end, :] * _silu(z)).astype(y_ref.dtype)


def _pool_first_kernel(x_ref, nw_ref, wu_ref, wz_ref, pw_ref, ps_ref,
                       n0_ref, n1_ref, n2_ref, n3_ref,
                       y_ref, xb_ref, rs_ref, pu_ref, pz_ref, p0_ref, p1_ref, p2_ref, p3_ref,
                       wu_buf, wz_buf, ubuf, s2buf, ucarry, *, tiles_per_seq):
    jj = pl.program_id(0)
    i = pl.program_id(1)

    @pl.when((jj == 0) & (i == 0))
    def _():
        ucarry[...] = jnp.zeros_like(ucarry)

    @pl.when(jj == 0)
    def _():
        _stage_chunks(0, i, [(wu_ref, wu_buf), (wz_ref, wz_buf)],
                      fold=pw_ref[0].astype(jnp.bfloat16), row_scale=nw_ref)
        _cast_rows([(n0_ref, p0_ref), (n1_ref, p1_ref), (n2_ref, p2_ref), (n3_ref, p3_ref)],
                   row_scale=nw_ref)

    @pl.when(jj == 1)
    def _():
        _cast_rows([(wu_ref, pu_ref), (wz_ref, pz_ref)], row_scale=nw_ref,
                   fold=pw_ref[0].astype(jnp.bfloat16))
        x = x_ref[...]
        xb = x.astype(jnp.bfloat16)
        rs = _row_rms_scale(x)
        xb_ref[...] = xb
        rs_ref[...] = rs
        _pool_tile(xb, rs, wu_buf[0], wz_buf[0], 0, i % tiles_per_seq, ps_ref, y_ref,
                   ubuf, (s2buf,), ucarry)


def _pool_first(x2d, norm_col, w_in2d, pool_w, pool_scale, conv_blocks):
    m = x2d.shape[0]
    tm = POOL_ROW_TILE
    n_tiles = m // tm
    chunk_rows = D_MODEL // n_tiles
    tile = lambda jj, i: jnp.where(jj > 0, i, 0)
    early = lambda jj, i: jnp.where(jj > 0, n_tiles - 1, i)
    w_chunk = lambda first: pl.BlockSpec((chunk_rows, COL_BLOCK), lambda jj, i: (i, first + jj))
    side_out = pl.BlockSpec((chunk_rows, COL_BLOCK), lambda jj, i: (tile(jj, i), 0))
    conv_in = lambda blk: pl.BlockSpec((chunk_rows, COL_BLOCK), lambda jj, i: (early(jj, i), blk))
    conv_out = pl.BlockSpec((chunk_rows, COL_BLOCK), lambda jj, i: (early(jj, i), 0))
    n_conv = len(conv_blocks)
    w_buf = pltpu.VMEM((1, D_MODEL, COL_BLOCK), jnp.bfloat16)
    buf = pltpu.VMEM((POOL_TOP + tm, COL_BLOCK), jnp.float32)
    outs = pl.pallas_call(
        functools.partial(_pool_first_kernel, tiles_per_seq=SEQ // tm),
        grid=(2, n_tiles),
        in_specs=[
            pl.BlockSpec((tm, D_MODEL), lambda jj, i: (tile(jj, i), 0)),
            pl.BlockSpec((chunk_rows, 1), lambda jj, i: (i, 0)),
            w_chunk(0), w_chunk(N_GROUPS),
            pl.BlockSpec((1, GROUP_DIM, GROUP_DIM), lambda jj, i: (jj, 0, 0)),
            pl.BlockSpec((1, COL_BLOCK), lambda jj, i: (0, 0)),
        ] + [conv_in(blk) for blk in conv_blocks],
        out_specs=[pl.BlockSpec((tm, COL_BLOCK), lambda jj, i: (tile(jj, i), 0)),
                   pl.BlockSpec((tm, D_MODEL), lambda jj, i: (tile(jj, i), 0)),
                   pl.BlockSpec((tm, 1), lambda jj, i: (tile(jj, i), 0)),
                   side_out, side_out] + [conv_out] * n_conv,
        out_shape=[jax.ShapeDtypeStruct((m, COL_BLOCK), jnp.bfloat16),
                   jax.ShapeDtypeStruct((m, D_MODEL), jnp.bfloat16),
                   jax.ShapeDtypeStruct((m, 1), jnp.float32)]
        + [jax.ShapeDtypeStruct((D_MODEL, COL_BLOCK), jnp.bfloat16)] * (2 + n_conv),
        scratch_shapes=[w_buf, w_buf, buf, buf,
                        pltpu.VMEM((POOL_HALO, COL_BLOCK), jnp.float32)],
        compiler_params=_params(2),
        name="pool_first",
    )(x2d, norm_col, w_in2d, w_in2d, pool_w, pool_scale.reshape(1, HALF),
      *([w_in2d] * n_conv))
    return outs[0], outs[1], outs[2], outs[3:5], outs[5:]


def _pool_rest_kernel(xb_ref, rs_ref, nws_ref, wu_ref, wz_ref, pw_ref, ps_ref,
                      wu0_ref, wz0_ref, y_ref,
                      wu_buf, wz_buf, ubuf, s2buf, s4buf, ucarry, sems, *, tiles_per_seq):
    jj = pl.program_id(0)
    i = pl.program_id(1)
    first_step = (jj == 0) & (i == 0)

    @pl.when(first_step)
    def _():
        ucarry[...] = jnp.zeros_like(ucarry)

    _load_prestaged(first_step, [(wu0_ref, wu_buf), (wz0_ref, wz_buf)], sems)

    slot = jj % 2
    _pool_tile(xb_ref[...], rs_ref[...], wu_buf[slot], wz_buf[slot], jj + 1,
               i % tiles_per_seq, ps_ref, y_ref, ubuf, (s2buf, s4buf), ucarry)
    _stage_chunks((jj + 1) % 2, i, [(wu_ref, wu_buf), (wz_ref, wz_buf)],
                  fold=pw_ref[0].astype(jnp.bfloat16), row_scale=nws_ref)


def _pool_rest(xb, rs, norm_col, w_in2d, pool_w, pool_scale, first_blocks):
    m = xb.shape[0]
    tm = REST_ROW_TILE
    nb = N_GROUPS - 1
    n_tiles = m // tm
    chunk_rows = D_MODEL // n_tiles
    chunk, staged = _prestaged_maps(nb, n_tiles)
    w_chunk = lambda first: pl.BlockSpec((chunk_rows, COL_BLOCK),
                                         lambda jj, i: (chunk(jj, i), first + 1 + staged(jj)))
    hbm = pl.BlockSpec(memory_space=pl.ANY)
    w_buf = pltpu.VMEM((2, D_MODEL, COL_BLOCK), jnp.bfloat16)
    buf = pltpu.VMEM((POOL_TOP + tm, COL_BLOCK), jnp.float32)
    return pl.pallas_call(
        functools.partial(_pool_rest_kernel, tiles_per_seq=SEQ // tm),
        grid=(nb, n_tiles),
        in_specs=[
            pl.BlockSpec((tm, D_MODEL), lambda jj, i: (i, 0)),
            pl.BlockSpec((tm, 1), lambda jj, i: (i, 0)),
            pl.BlockSpec((chunk_rows, 1), lambda jj, i: (chunk(jj, i), 0)),
            w_chunk(0), w_chunk(N_GROUPS),
            pl.BlockSpec((1, GROUP_DIM, GROUP_DIM), lambda jj, i: (1 + staged(jj), 0, 0),
                         pipeline_mode=pl.Buffered(1)),
            pl.BlockSpec((1, COL_BLOCK), lambda jj, i: (0, jj + 1)),
            hbm, hbm,
        ],
        out_specs=pl.BlockSpec((tm, COL_BLOCK), lambda jj, i: (i, jj)),
        out_shape=jax.ShapeDtypeStruct((m, nb * COL_BLOCK), jnp.bfloat16),
        scratch_shapes=[w_buf, w_buf, buf, buf, buf,
                        pltpu.VMEM((POOL_HALO, COL_BLOCK), jnp.float32),
                        pltpu.SemaphoreType.DMA((2,))],
        compiler_params=_params(2),
        name="pool_rest",
    )(xb, rs, norm_col, w_in2d, w_in2d, pool_w, pool_scale.reshape(1, HALF), *first_blocks)


def _conv_kernel(xb_ref, rs_ref, nws_ref, nwn_ref, wu_ref, wb_ref, wc_ref, wz_ref, cw_ref, cb_ref,
                 wu0_ref, wb0_ref, wc0_ref, wz0_ref, n0_ref, n1_ref, n2_ref, n3_ref,
                 y_ref, p0_ref, p1_ref, p2_ref, p3_ref,
                 wu_buf, wb_buf, wc_buf, wz_buf, vbuf, vcarry, sems, *, tiles_per_seq):
    jj = pl.program_id(0)
    i = pl.program_id(1)
    tm = xb_ref.shape[0]
    first_step = (jj == 0) & (i == 0)

    @pl.when(first_step)
    def _():
        vcarry[...] = jnp.zeros_like(vcarry)

    _load_prestaged(first_step, [(wu0_ref, wu_buf), (wb0_ref, wb_buf),
                                 (wc0_ref, wc_buf), (wz0_ref, wz_buf)], sems)

    def stage():
        _stage_chunks((jj + 1) % 2, i, [(wu_ref, wu_buf), (wb_ref, wb_buf),
                                        (wc_ref, wc_buf), (wz_ref, wz_buf)],
                      row_scale=nws_ref)

    def compute():
        _cast_rows([(n0_ref, p0_ref), (n1_ref, p1_ref)], row_scale=nwn_ref)
        _cast_rows([(n2_ref, p2_ref), (n3_ref, p3_ref)])
        slot = jj % 2
        seq_tile = i % tiles_per_seq
        h = xb_ref[...]
        rs = rs_ref[...]
        rs2 = rs * rs
        u = _dot(h, wu_buf[slot])
        c_gate = _dot(h, wc_buf[slot])
        vbuf[0:CONV_HALO, :] = jnp.where(seq_tile != 0, vcarry[...], 0.0)
        vbuf[CONV_HALO:CONV_HALO + tm, :] = rs2 * (c_gate * u)
        vcarry[...] = vbuf[tm:CONV_HALO + tm, :]
        cw = cw_ref[...]
        y = (cb_ref[...]
             + cw[0:1, :] * vbuf[CONV_HALO - 2:CONV_HALO - 2 + tm, :]
             + cw[1:2, :] * vbuf[CONV_HALO - 1:CONV_HALO - 1 + tm, :]
             + cw[2:3, :] * vbuf[CONV_HALO:CONV_HALO + tm, :])
        gated = (rs * y) * _silu(rs * _dot(h, wz_buf[slot]))
        y_ref[...] = (_dot(h, wb_buf[slot]) * gated).astype(y_ref.dtype)

    compute()
    stage()


def _conv_mixer(xb, rs, norm_col, w_in2d, conv_w, conv_b, first_blocks, w_branch):
    m = xb.shape[0]
    tm = ROW_TILE
    nb = N_HALF_BLOCKS
    n_tiles = m // tm
    chunk_rows = D_MODEL // n_tiles
    chunk, staged = _prestaged_maps(nb, n_tiles)
    w_chunk = lambda first: pl.BlockSpec((chunk_rows, COL_BLOCK),
                                         lambda jj, i: (chunk(jj, i), first + staged(jj)))
    hbm = pl.BlockSpec(memory_space=pl.ANY)
    n_steps = nb * n_tiles
    step = lambda jj, i: jj * n_tiles + i
    gate_rows = D_MODEL // n_steps
    branch_rows = HALF // n_steps
    gate_in = lambda blk: pl.BlockSpec((gate_rows, COL_BLOCK), lambda jj, i: (step(jj, i), blk))
    branch_in = lambda b: pl.BlockSpec((1, branch_rows, COL_BLOCK),
                                       lambda jj, i: (b, step(jj, i), 0))
    gate_out = pl.BlockSpec((gate_rows, COL_BLOCK), lambda jj, i: (step(jj, i), 0))
    branch_out = pl.BlockSpec((1, branch_rows, COL_BLOCK), lambda jj, i: (0, step(jj, i), 0))
    g0_first = 6 * N_HALF_BLOCKS
    g1_first = g0_first + N_MODEL_BLOCKS
    w_buf = pltpu.VMEM((2, D_MODEL, COL_BLOCK), jnp.bfloat16)
    outs = pl.pallas_call(
        functools.partial(_conv_kernel, tiles_per_seq=SEQ // tm),
        grid=(nb, n_tiles),
        in_specs=[
            pl.BlockSpec((tm, D_MODEL), lambda jj, i: (i, 0)),
            pl.BlockSpec((tm, 1), lambda jj, i: (i, 0)),
            pl.BlockSpec((chunk_rows, 1), lambda jj, i: (chunk(jj, i), 0)),
            pl.BlockSpec((gate_rows, 1), lambda jj, i: (step(jj, i), 0)),
            w_chunk(2 * nb), w_chunk(3 * nb), w_chunk(4 * nb), w_chunk(5 * nb),
            pl.BlockSpec((CONV_K, COL_BLOCK), lambda jj, i: (0, jj)),
            pl.BlockSpec((1, COL_BLOCK), lambda jj, i: (0, jj)),
            hbm, hbm, hbm, hbm,
            gate_in(g0_first), gate_in(g1_first), branch_in(0), branch_in(1),
        ],
        out_specs=[pl.BlockSpec((tm, COL_BLOCK), lambda jj, i: (i, jj)),
                   gate_out, gate_out, branch_out, branch_out],
        out_shape=[jax.ShapeDtypeStruct((m, HALF), jnp.bfloat16),
                   jax.ShapeDtypeStruct((D_MODEL, COL_BLOCK), jnp.bfloat16),
                   jax.ShapeDtypeStruct((D_MODEL, COL_BLOCK), jnp.bfloat16),
                   jax.ShapeDtypeStruct((1, HALF, COL_BLOCK), jnp.bfloat16),
                   jax.ShapeDtypeStruct((1, HALF, COL_BLOCK), jnp.bfloat16)],
        scratch_shapes=[w_buf, w_buf, w_buf, w_buf,
                        pltpu.VMEM((CONV_HALO + tm, COL_BLOCK), jnp.float32),
                        pltpu.VMEM((CONV_HALO, COL_BLOCK), jnp.float32),
                        pltpu.SemaphoreType.DMA((4,))],
        compiler_params=_params(2),
        name="conv_mixer",
    )(xb, rs, norm_col, norm_col, w_in2d, w_in2d, w_in2d, w_in2d, conv_w,
      conv_b.reshape(1, HALF), *first_blocks, w_in2d, w_in2d, w_branch, w_branch)
    return outs[0], outs[1:]


def _merge_kernel(xb_ref, rs_ref, nws_ref, yp0_ref, ypr_ref, yc_ref, wg0_ref, wg1_ref, gb0_ref, gb1_ref,
                  wb0_ref, wb1_ref, wo_ref, wg00_ref, wg10_ref, wb00_ref, wb10_ref,
                  o_ref, wo_bf_ref, wg0_buf, wg1_buf, wb0_buf, wb1_buf, sems):
    jj = pl.program_id(0)
    i = pl.program_id(1)

    _load_prestaged((jj == 0) & (i == 0),
                    [(wg00_ref, wg0_buf), (wg10_ref, wg1_buf),
                     (wb00_ref.at[0], wb0_buf), (wb10_ref.at[0], wb1_buf)], sems)

    def stage():
        _stage_chunks((jj + 1) % 2, i, [(wg0_ref, wg0_buf), (wg1_ref, wg1_buf)],
                      row_scale=nws_ref)
        _stage_chunks((jj + 1) % 2, i, [(wb0_ref, wb0_buf), (wb1_ref, wb1_buf)])

    def compute():
        slot = jj % 2
        h = xb_ref[...]
        rs = rs_ref[...]
        g0 = _sigmoid(rs * _dot(h, wg0_buf[slot]) + gb0_ref[...])
        y_pool = jnp.concatenate([yp0_ref[...], ypr_ref[...]], axis=1)
        br0 = _dot(y_pool, wb0_buf[slot])
        g1 = _sigmoid(rs * _dot(h, wg1_buf[slot]) + gb1_ref[...])
        br1 = _dot(yc_ref[...], wb1_buf[slot])
        o_ref[...] = (g0 * br0 + g1 * br1).astype(o_ref.dtype)
        wo_bf_ref[...] = wo_ref[...].astype(wo_bf_ref.dtype)

    compute()
    stage()


def _merge(xb, rs, norm_col, y_pool0, y_pool_rest, y_conv, w_in2d, gate_b, w_branch, w_out2d,
           first_blocks):
    m = xb.shape[0]
    tm = ROW_TILE
    nb = N_MODEL_BLOCKS
    n_tiles = m // tm
    chunk, staged = _prestaged_maps(nb, n_tiles)
    g0_first = 6 * N_HALF_BLOCKS
    g1_first = g0_first + nb
    gate_chunk = lambda first: pl.BlockSpec(
        (D_MODEL // n_tiles, COL_BLOCK), lambda jj, i: (chunk(jj, i), first + staged(jj)))
    branch_chunk = lambda b: pl.BlockSpec(
        (1, HALF // n_tiles, COL_BLOCK), lambda jj, i: (b, chunk(jj, i), staged(jj)))
    tile = lambda width: pl.BlockSpec((tm, width), lambda jj, i: (i, 0))
    bias = pl.BlockSpec((1, COL_BLOCK), lambda jj, i: (0, jj))
    wo_rows = D_MODEL // (nb * n_tiles)
    wo_slice = pl.BlockSpec((wo_rows, D_MODEL), lambda jj, i: (jj * n_tiles + i, 0))
    hbm = pl.BlockSpec(memory_space=pl.ANY)
    return pl.pallas_call(
        _merge_kernel,
        grid=(nb, n_tiles),
        in_specs=[tile(D_MODEL), tile(1),
                  pl.BlockSpec((D_MODEL // n_tiles, 1), lambda jj, i: (chunk(jj, i), 0)),
                  tile(COL_BLOCK), tile(HALF - COL_BLOCK), tile(HALF),
                  gate_chunk(g0_first), gate_chunk(g1_first), bias, bias,
                  branch_chunk(0), branch_chunk(1), wo_slice, hbm, hbm, hbm, hbm],
        out_specs=[pl.BlockSpec((tm, COL_BLOCK), lambda jj, i: (i, jj)), wo_slice],
        out_shape=[jax.ShapeDtypeStruct((m, D_MODEL), jnp.bfloat16),
                   jax.ShapeDtypeStruct((D_MODEL, D_MODEL), jnp.bfloat16)],
        scratch_shapes=[pltpu.VMEM((2, D_MODEL, COL_BLOCK), jnp.bfloat16),
                        pltpu.VMEM((2, D_MODEL, COL_BLOCK), jnp.bfloat16),
                        pltpu.VMEM((2, HALF, COL_BLOCK), jnp.bfloat16),
                        pltpu.VMEM((2, HALF, COL_BLOCK), jnp.bfloat16),
                        pltpu.SemaphoreType.DMA((4,))],
        compiler_params=_params(2),
        name="branch_merge",
    )(xb, rs, norm_col, y_pool0, y_pool_rest, y_conv, w_in2d, w_in2d, gate_b[0:1], gate_b[1:2],
      w_branch, w_branch, w_out2d, *first_blocks)


def _out_kernel(mg_ref, w_ref, x_ref, fw_ref, o_ref, rows, sumsq):
    i = pl.program_id(0)
    n = pl.program_id(1)
    n_tiles = pl.num_programs(0) - 1

    def emit_previous():
        scale = lax.rsqrt(sumsq[(i + 1) % 2] * (1.0 / D_MODEL) + NORM_EPS)
        o_ref[...] = rows[n] * scale * fw_ref[...]

    def project():
        xn = x_ref[...] + _dot(mg_ref[...], w_ref[...])
        rows[n] = xn
        part = jnp.sum(xn * xn, axis=-1, keepdims=True)
        cur = i % 2
        sumsq[cur] = jnp.where(n == 0, part, sumsq[cur] + part)

    @pl.when(i == 0)
    def _():
        project()

    @pl.when((i > 0) & (i < n_tiles))
    def _():
        emit_previous()
        project()

    @pl.when(i == n_tiles)
    def _():
        emit_previous()


def _out_proj(merged, w_out_bf, x2d, final_w):
    m = x2d.shape[0]
    tm = OUT_ROW_TILE
    tn = OUT_COL_BLOCK
    n_tiles = m // tm
    n_blocks = D_MODEL // tn
    last = n_tiles - 1
    in_row = lambda i: jnp.minimum(i, last)
    in_col = lambda i, n: jnp.where(i < n_tiles, n, n_blocks - 1)
    return pl.pallas_call(
        _out_kernel,
        grid=(n_tiles + 1, n_blocks),
        in_specs=[
            pl.BlockSpec((tm, D_MODEL), lambda i, n: (in_row(i), 0)),
            pl.BlockSpec((D_MODEL, tn), lambda i, n: (0, in_col(i, n))),
            pl.BlockSpec((tm, tn), lambda i, n: (in_row(i), in_col(i, n))),
            pl.BlockSpec((1, tn), lambda i, n: (0, n)),
        ],
        out_specs=pl.BlockSpec(
            (tm, tn), lambda i, n: (jnp.maximum(i - 1, 0), jnp.where(i > 0, n, 0))),
        out_shape=jax.ShapeDtypeStruct((m, D_MODEL), jnp.float32),
        scratch_shapes=[pltpu.VMEM((n_blocks, tm, tn), jnp.float32),
                        pltpu.VMEM((2, tm, 1), jnp.float32)],
        compiler_params=_params(2),
        name="out_proj_norm",
    )(merged, w_out_bf, x2d, final_w.reshape(1, D_MODEL))


def kernel(x, norm_w, w_in, pool_w, pool_scale, conv_w, conv_b, gate_b, w_branch, w_out,
           final_norm_w):
    b, s, d = x.shape
    assert (s, d) == (SEQ, D_MODEL) and norm_w.shape[0] == 1
    x2d = x.reshape(b * s, d)
    w_in2d = w_in[0]

    nb = N_HALF_BLOCKS
    norm_col = norm_w[0].reshape(d, 1)
    conv_first_cols = (2 * nb, 3 * nb, 4 * nb, 5 * nb)
    y_pool0, xb, rs, pool_first, conv_first = _pool_first(
        x2d, norm_col, w_in2d, pool_w[0], pool_scale[0], conv_first_cols)
    y_pool_rest = _pool_rest(xb, rs, norm_col, w_in2d, pool_w[0], pool_scale[0], pool_first)
    y_conv, merge_first = _conv_mixer(xb, rs, norm_col, w_in2d, conv_w[0], conv_b[0],
                                      conv_first, w_branch[0])
    merged, w_out_bf = _merge(xb, rs, norm_col, y_pool0, y_pool_rest, y_conv, w_in2d, gate_b[0],
                              w_branch[0], w_out[0], merge_first)
    out = _out_proj(merged, w_out_bf, x2d, final_norm_w)
    return out.reshape(b, s, d)
```

```python
import functools

import jax
import jax.numpy as jnp
from jax import lax
from jax.experimental import pallas as pl
from jax.experimental.pallas import tpu as pltpu

D_MODEL = 4096
SEQ = 4096
HALF = D_MODEL // 2
POOL_WINDOWS = (2, 4, 8, 16)
N_GROUPS = len(POOL_WINDOWS)
GROUP_DIM = HALF // N_GROUPS
CONV_K = 3
NORM_EPS = 1e-6

COL_BLOCK = GROUP_DIM
N_HALF_BLOCKS = HALF // COL_BLOCK
N_MODEL_BLOCKS = D_MODEL // COL_BLOCK
OUT_COL_BLOCK = 512
OUT_ROW_TILE = 1024
ROW_TILE = 512
POOL_ROW_TILE = 512
REST_ROW_TILE = 1024
POOL_HALO = 16
POOL_PAD = 16
POOL_TOP = POOL_PAD + POOL_HALO
CONV_HALO = 8

VMEM_LIMIT_BYTES = 62 * 1024 * 1024


def _dot(a, b):
    return jnp.dot(a, b, preferred_element_type=jnp.float32)


def _silu(z):
    return z / (1.0 + jnp.exp(-z))


def _sigmoid(z):
    return 1.0 / (1.0 + jnp.exp(-z))


def _params(n_axes):
    return pltpu.CompilerParams(dimension_semantics=("arbitrary",) * n_axes,
                                vmem_limit_bytes=VMEM_LIMIT_BYTES)


def _stage_chunks(slot, i, pairs, fold=None, row_scale=None):
    for k, (chunk_ref, buf) in enumerate(pairs):
        rows, cols = chunk_ref.shape[-2:]
        r0 = pl.multiple_of(i * rows, rows)
        chunk = chunk_ref[...].reshape(rows, cols)
        if row_scale is not None:
            chunk = chunk * row_scale[...]
        chunk = chunk.astype(buf.dtype)
        if k == 0 and fold is not None:
            chunk = _dot(chunk, fold).astype(buf.dtype)
        buf[slot, pl.ds(r0, rows), :] = chunk


def _prestaged_maps(n_blocks, n_tiles):
    def chunk(jj, i):
        return jnp.where(jj + 1 < n_blocks, i, n_tiles - 1)

    def staged(jj):
        return jnp.minimum(jj + 1, n_blocks - 1)

    return chunk, staged


def _load_prestaged(first_step, pairs, sems):
    @pl.when(first_step)
    def _():
        copies = [pltpu.make_async_copy(src, buf.at[0], sems.at[k])
                  for k, (src, buf) in enumerate(pairs)]
        for cp in copies:
            cp.start()
        for cp in copies:
            cp.wait()


def _cast_rows(pairs, row_scale=None, fold=None):
    for k, (src_ref, dst_ref) in enumerate(pairs):
        rows = src_ref[...]
        if row_scale is not None:
            rows = rows * row_scale[...]
        rows = rows.astype(dst_ref.dtype)
        if k == 0 and fold is not None:
            rows = _dot(rows, fold).astype(dst_ref.dtype)
        dst_ref[...] = rows


def _row_rms_scale(x):
    return lax.rsqrt(jnp.mean(x * x, axis=-1, keepdims=True) + NORM_EPS)


def _pool_tile(load_xb, rs, wu, wz, g, seq_tile, ps_ref, y_ref, ubuf, sbufs, ucarry):
    tm = rs.shape[0]
    end = POOL_TOP + tm
    u = rs * _dot(load_xb(), wu)

    ubuf[0:POOL_PAD, :] = jnp.zeros((POOL_PAD, COL_BLOCK), jnp.float32)
    ubuf[POOL_PAD:POOL_TOP, :] = jnp.where(seq_tile != 0, ucarry[...], 0.0)
    ubuf[POOL_TOP:end, :] = u
    ucarry[...] = ubuf[end - POOL_HALO:end, :]
    z = rs * _dot(load_xb(), wz)

    def shifted_sum(src, lo, shift):
        return src[lo:end, :] + src[lo - shift:end - shift, :]

    s2buf = sbufs[0]
    s2buf[8:end, :] = shifted_sum(ubuf, 8, 1)
    if isinstance(g, int):
        assert g == 0
        win = s2buf[POOL_TOP:end, :]
        window = POOL_WINDOWS[0]
    else:
        s4buf = sbufs[1]
        s4buf[16:end, :] = shifted_sum(s2buf, 16, 2)
        s8buf = s2buf
        s8buf[24:end, :] = shifted_sum(s4buf, 24, 4)
        win = jnp.where(g == 1, s4buf[POOL_TOP:end, :],
                        jnp.where(g == 2, s8buf[POOL_TOP:end, :],
                                  shifted_sum(s8buf, POOL_TOP, 8)))
        window = lax.shift_left(jnp.int32(2), g)
    t1 = seq_tile * tm + 1 + lax.broadcasted_iota(jnp.int32, (tm, 1), 0)
    inv_count = 1.0 / jnp.minimum(t1, window).astype(jnp.float32)
    mixed = win * inv_count - ubuf[POOL_TOP:end, :]
    ubuf[POOL_TOP:end, :] = mixed * ps_ref[...]
    y_ref[...] = (ubuf[POOL_TOP:end, :] * _silu(z)).astype(y_ref.dtype)


def _pool_first_kernel(x_ref, nw_ref, wu_ref, wz_ref, pw_ref, ps_ref,
                       y_ref, xb_ref, rs_ref, pu_ref, pz_ref,
                       wu_buf, wz_buf, ubuf, s2buf, ucarry, *, tiles_per_seq):
    jj = pl.program_id(0)
    i = pl.program_id(1)

    @pl.when((jj == 0) & (i == 0))
    def _():
        ucarry[...] = jnp.zeros_like(ucarry)

    @pl.when(jj == 0)
    def _():
        _stage_chunks(0, i, [(wu_ref, wu_buf), (wz_ref, wz_buf)],
                      fold=pw_ref[0].astype(jnp.bfloat16), row_scale=nw_ref)

    @pl.when(jj == 1)
    def _():
        _cast_rows([(wu_ref, pu_ref), (wz_ref, pz_ref)], row_scale=nw_ref,
                   fold=pw_ref[0].astype(jnp.bfloat16))
        x = x_ref[...]
        xb = x.astype(jnp.bfloat16)
        rs = _row_rms_scale(x)
        xb_ref[...] = xb
        rs_ref[...] = rs
        _pool_tile(lambda: xb, rs, wu_buf[0], wz_buf[0], 0, i % tiles_per_seq, ps_ref, y_ref,
                   ubuf, (s2buf,), ucarry)


def _pool_first(x2d, norm_col, w_in2d, pool_w, pool_scale):
    m = x2d.shape[0]
    tm = POOL_ROW_TILE
    n_tiles = m // tm
    chunk_rows = D_MODEL // n_tiles
    tile = lambda jj, i: jnp.where(jj > 0, i, 0)
    w_chunk = lambda first: pl.BlockSpec((chunk_rows, COL_BLOCK), lambda jj, i: (i, first + jj))
    side_out = pl.BlockSpec((chunk_rows, COL_BLOCK), lambda jj, i: (tile(jj, i), 0))
    w_buf = pltpu.VMEM((1, D_MODEL, COL_BLOCK), jnp.bfloat16)
    buf = pltpu.VMEM((POOL_TOP + tm, COL_BLOCK), jnp.float32)
    outs = pl.pallas_call(
        functools.partial(_pool_first_kernel, tiles_per_seq=SEQ // tm),
        grid=(2, n_tiles),
        in_specs=[
            pl.BlockSpec((tm, D_MODEL), lambda jj, i: (tile(jj, i), 0)),
            pl.BlockSpec((chunk_rows, 1), lambda jj, i: (i, 0)),
            w_chunk(0), w_chunk(N_GROUPS),
            pl.BlockSpec((1, GROUP_DIM, GROUP_DIM), lambda jj, i: (jj, 0, 0)),
            pl.BlockSpec((1, COL_BLOCK), lambda jj, i: (0, 0)),
        ],
        out_specs=[pl.BlockSpec((tm, COL_BLOCK), lambda jj, i: (tile(jj, i), 0)),
                   pl.BlockSpec((tm, D_MODEL), lambda jj, i: (tile(jj, i), 0)),
                   pl.BlockSpec((tm, 1), lambda jj, i: (tile(jj, i), 0)),
                   side_out, side_out],
        out_shape=[jax.ShapeDtypeStruct((m, COL_BLOCK), jnp.bfloat16),
                   jax.ShapeDtypeStruct((m, D_MODEL), jnp.bfloat16),
                   jax.ShapeDtypeStruct((m, 1), jnp.float32)]
        + [jax.ShapeDtypeStruct((D_MODEL, COL_BLOCK), jnp.bfloat16)] * 2,
        scratch_shapes=[w_buf, w_buf, buf, buf,
                        pltpu.VMEM((POOL_HALO, COL_BLOCK), jnp.float32)],
        compiler_params=_params(2),
        name="pool_first",
    )(x2d, norm_col, w_in2d, w_in2d, pool_w, pool_scale.reshape(1, HALF))
    return outs[0], outs[1], outs[2], outs[3:]


def _pool_rest_kernel(xb_ref, rs_ref, nws_ref, nwn_ref, wu_ref, wz_ref, pw_ref, ps_ref,
                      wu0_ref, wz0_ref, n0_ref, n1_ref, n2_ref, n3_ref,
                      y_ref, p0_ref, p1_ref, p2_ref, p3_ref,
                      wu_buf, wz_buf, ubuf, s2buf, s4buf, ucarry, sems, *, tiles_per_seq):
    jj = pl.program_id(0)
    i = pl.program_id(1)
    first_step = (jj == 0) & (i == 0)

    @pl.when(first_step)
    def _():
        ucarry[...] = jnp.zeros_like(ucarry)

    _load_prestaged(first_step, [(wu0_ref, wu_buf), (wz0_ref, wz_buf)], sems)

    _cast_rows([(n0_ref, p0_ref), (n1_ref, p1_ref), (n2_ref, p2_ref), (n3_ref, p3_ref)],
               row_scale=nwn_ref)
    slot = jj % 2
    _pool_tile(lambda: xb_ref[...], rs_ref[...], wu_buf[slot], wz_buf[slot], jj + 1,
               i % tiles_per_seq, ps_ref, y_ref, ubuf, (s2buf, s4buf), ucarry)
    _stage_chunks((jj + 1) % 2, i, [(wu_ref, wu_buf), (wz_ref, wz_buf)],
                  fold=pw_ref[0].astype(jnp.bfloat16), row_scale=nws_ref)


def _pool_rest(xb, rs, norm_col, w_in2d, pool_w, pool_scale, first_blocks, conv_blocks):
    m = xb.shape[0]
    tm = REST_ROW_TILE
    nb = N_GROUPS - 1
    n_tiles = m // tm
    chunk_rows = D_MODEL // n_tiles
    chunk, staged = _prestaged_maps(nb, n_tiles)
    w_chunk = lambda first: pl.BlockSpec((chunk_rows, COL_BLOCK),
                                         lambda jj, i: (chunk(jj, i), first + 1 + staged(jj)))
    side_steps = 2 * n_tiles
    side_rows = D_MODEL // side_steps
    side_step = lambda jj, i: jnp.minimum(jj * n_tiles + i, side_steps - 1)
    side_in = lambda blk: pl.BlockSpec((side_rows, COL_BLOCK),
                                       lambda jj, i: (side_step(jj, i), blk))
    side_out = pl.BlockSpec((side_rows, COL_BLOCK), lambda jj, i: (side_step(jj, i), 0))
    n_side = len(conv_blocks)
    hbm = pl.BlockSpec(memory_space=pl.ANY)
    w_buf = pltpu.VMEM((2, D_MODEL, COL_BLOCK), jnp.bfloat16)
    buf = pltpu.VMEM((POOL_TOP + tm, COL_BLOCK), jnp.float32)
    outs = pl.pallas_call(
        functools.partial(_pool_rest_kernel, tiles_per_seq=SEQ // tm),
        grid=(nb, n_tiles),
        in_specs=[
            pl.BlockSpec((tm, D_MODEL), lambda jj, i: (i, 0)),
            pl.BlockSpec((tm, 1), lambda jj, i: (i, 0)),
            pl.BlockSpec((chunk_rows, 1), lambda jj, i: (chunk(jj, i), 0)),
            pl.BlockSpec((side_rows, 1), lambda jj, i: (side_step(jj, i), 0)),
            w_chunk(0), w_chunk(N_GROUPS),
            pl.BlockSpec((1, GROUP_DIM, GROUP_DIM), lambda jj, i: (1 + staged(jj), 0, 0),
                         pipeline_mode=pl.Buffered(1)),
            pl.BlockSpec((1, COL_BLOCK), lambda jj, i: (0, jj + 1)),
            hbm, hbm,
        ] + [side_in(blk) for blk in conv_blocks],
        out_specs=[pl.BlockSpec((tm, COL_BLOCK), lambda jj, i: (i, jj))] + [side_out] * n_side,
        out_shape=[jax.ShapeDtypeStruct((m, nb * COL_BLOCK), jnp.bfloat16)]
        + [jax.ShapeDtypeStruct((D_MODEL, COL_BLOCK), jnp.bfloat16)] * n_side,
        scratch_shapes=[w_buf, w_buf, buf, buf, buf,
                        pltpu.VMEM((POOL_HALO, COL_BLOCK), jnp.float32),
                        pltpu.SemaphoreType.DMA((2,))],
        compiler_params=_params(2),
        name="pool_rest",
    )(xb, rs, norm_col, norm_col, w_in2d, w_in2d, pool_w, pool_scale.reshape(1, HALF),
      *first_blocks, *([w_in2d] * n_side))
    return outs[0], outs[1:]


def _conv_kernel(xb_ref, rs_ref, nws_ref, nwn_ref, wu_ref, wb_ref, wc_ref, wz_ref, cw_ref, cb_ref,
                 wu0_ref, wb0_ref, wc0_ref, wz0_ref, n0_ref, n1_ref, n2_ref, n3_ref,
                 y_ref, p0_ref, p1_ref, p2_ref, p3_ref,
                 wu_buf, wb_buf, wc_buf, wz_buf, vbuf, vcarry, sems, *, tiles_per_seq):
    jj = pl.program_id(0)
    i = pl.program_id(1)
    tm = xb_ref.shape[0]
    first_step = (jj == 0) & (i == 0)

    @pl.when(first_step)
    def _():
        vcarry[...] = jnp.zeros_like(vcarry)

    _load_prestaged(first_step, [(wu0_ref, wu_buf), (wb0_ref, wb_buf),
                                 (wc0_ref, wc_buf), (wz0_ref, wz_buf)], sems)

    def stage():
        _stage_chunks((jj + 1) % 2, i, [(wu_ref, wu_buf), (wb_ref, wb_buf),
                                        (wc_ref, wc_buf), (wz_ref, wz_buf)],
                      row_scale=nws_ref)

    def compute():
        slot = jj % 2
        seq_tile = i % tiles_per_seq
        rs = rs_ref[...]
        rs2 = rs * rs
        u = _dot(xb_ref[...], wu_buf[slot])
        _cast_rows([(n0_ref, p0_ref), (n1_ref, p1_ref)], row_scale=nwn_ref)
        c_gate = _dot(xb_ref[...], wc_buf[slot])
        vbuf[0:CONV_HALO, :] = jnp.where(seq_tile != 0, vcarry[...], 0.0)
        vbuf[CONV_HALO:CONV_HALO + tm, :] = rs2 * (c_gate * u)
        vcarry[...] = vbuf[tm:CONV_HALO + tm, :]
        cw = cw_ref[...]
        y = (cb_ref[...]
             + cw[0:1, :] * vbuf[CONV_HALO - 2:CONV_HALO - 2 + tm, :]
             + cw[1:2, :] * vbuf[CONV_HALO - 1:CONV_HALO - 1 + tm, :]
             + cw[2:3, :] * vbuf[CONV_HALO:CONV_HALO + tm, :])
        gated = (rs * y) * _silu(rs * _dot(xb_ref[...], wz_buf[slot]))
        _cast_rows([(n2_ref, p2_ref), (n3_ref, p3_ref)])
        y_ref[...] = (_dot(xb_ref[...], wb_buf[slot]) * gated).astype(y_ref.dtype)

    compute()
    stage()


def _conv_mixer(xb, rs, norm_col, w_in2d, conv_w, conv_b, first_blocks, w_branch):
    m = xb.shape[0]
    tm = ROW_TILE
    nb = N_HALF_BLOCKS
    n_tiles = m // tm
    chunk_rows = D_MODEL // n_tiles
    chunk, staged = _prestaged_maps(nb, n_tiles)
    w_chunk = lambda first: pl.BlockSpec((chunk_rows, COL_BLOCK),
                                         lambda jj, i: (chunk(jj, i), first + staged(jj)))
    hbm = pl.BlockSpec(memory_space=pl.ANY)
    n_steps = nb * n_tiles
    step = lambda jj, i: jj * n_tiles + i
    gate_rows = D_MODEL // n_steps
    branch_rows = HALF // n_steps
    gate_in = lambda blk: pl.BlockSpec((gate_rows, COL_BLOCK), lambda jj, i: (step(jj, i), blk))
    branch_in = lambda b: pl.BlockSpec((1, branch_rows, COL_BLOCK),
                                       lambda jj, i: (b, step(jj, i), 0))
    gate_out = pl.BlockSpec((gate_rows, COL_BLOCK), lambda jj, i: (step(jj, i), 0))
    branch_out = pl.BlockSpec((1, branch_rows, COL_BLOCK), lambda jj, i: (0, step(jj, i), 0))
    g0_first = 6 * N_HALF_BLOCKS
    g1_first = g0_first + N_MODEL_BLOCKS
    w_buf = pltpu.VMEM((2, D_MODEL, COL_BLOCK), jnp.bfloat16)
    outs = pl.pallas_call(
        functools.partial(_conv_kernel, tiles_per_seq=SEQ // tm),
        grid=(nb, n_tiles),
        in_specs=[
            pl.BlockSpec((tm, D_MODEL), lambda jj, i: (i, 0)),
            pl.BlockSpec((tm, 1), lambda jj, i: (i, 0)),
            pl.BlockSpec((chunk_rows, 1), lambda jj, i: (chunk(jj, i), 0)),
            pl.BlockSpec((gate_rows, 1), lambda jj, i: (step(jj, i), 0)),
            w_chunk(2 * nb), w_chunk(3 * nb), w_chunk(4 * nb), w_chunk(5 * nb),
            pl.BlockSpec((CONV_K, COL_BLOCK), lambda jj, i: (0, jj)),
            pl.BlockSpec((1, COL_BLOCK), lambda jj, i: (0, jj)),
            hbm, hbm, hbm, hbm,
            gate_in(g0_first), gate_in(g1_first), branch_in(0), branch_in(1),
        ],
        out_specs=[pl.BlockSpec((tm, COL_BLOCK), lambda jj, i: (i, jj)),
                   gate_out, gate_out, branch_out, branch_out],
        out_shape=[jax.ShapeDtypeStruct((m, HALF), jnp.bfloat16),
                   jax.ShapeDtypeStruct((D_MODEL, COL_BLOCK), jnp.bfloat16),
                   jax.ShapeDtypeStruct((D_MODEL, COL_BLOCK), jnp.bfloat16),
                   jax.ShapeDtypeStruct((1, HALF, COL_BLOCK), jnp.bfloat16),
                   jax.ShapeDtypeStruct((1, HALF, COL_BLOCK), jnp.bfloat16)],
        scratch_shapes=[w_buf, w_buf, w_buf, w_buf,
                        pltpu.VMEM((CONV_HALO + tm, COL_BLOCK), jnp.float32),
                        pltpu.VMEM((CONV_HALO, COL_BLOCK), jnp.float32),
                        pltpu.SemaphoreType.DMA((4,))],
        compiler_params=_params(2),
        name="conv_mixer",
    )(xb, rs, norm_col, norm_col, w_in2d, w_in2d, w_in2d, w_in2d, conv_w,
      conv_b.reshape(1, HALF), *first_blocks, w_in2d, w_in2d, w_branch, w_branch)
    return outs[0], outs[1:]


def _merge_kernel(xb_ref, rs_ref, nws_ref, yp0_ref, ypr_ref, yc_ref, wg0_ref, wg1_ref, gb0_ref, gb1_ref,
                  wb0_ref, wb1_ref, wo_ref, wg00_ref, wg10_ref, wb00_ref, wb10_ref,
                  o_ref, wo_bf_ref, wg0_buf, wg1_buf, wb0_buf, wb1_buf, sems):
    jj = pl.program_id(0)
    i = pl.program_id(1)

    _load_prestaged((jj == 0) & (i == 0),
                    [(wg00_ref, wg0_buf), (wg10_ref, wg1_buf),
                     (wb00_ref.at[0], wb0_buf), (wb10_ref.at[0], wb1_buf)], sems)

    def stage():
        _stage_chunks((jj + 1) % 2, i, [(wg0_ref, wg0_buf), (wg1_ref, wg1_buf)],
                      row_scale=nws_ref)
        _stage_chunks((jj + 1) % 2, i, [(wb0_ref, wb0_buf), (wb1_ref, wb1_buf)])

    def compute():
        slot = jj % 2
        rs = rs_ref[...]
        g0 = _sigmoid(rs * _dot(xb_ref[...], wg0_buf[slot]) + gb0_ref[...])
        y_pool = jnp.concatenate([yp0_ref[...], ypr_ref[...]], axis=1)
        br0 = _dot(y_pool, wb0_buf[slot])
        wo_bf_ref[...] = wo_ref[...].astype(wo_bf_ref.dtype)
        g1 = _sigmoid(rs * _dot(xb_ref[...], wg1_buf[slot]) + gb1_ref[...])
        br1 = _dot(yc_ref[...], wb1_buf[slot])
        o_ref[...] = (g0 * br0 + g1 * br1).astype(o_ref.dtype)

    compute()
    stage()


def _merge(xb, rs, norm_col, y_pool0, y_pool_rest, y_conv, w_in2d, gate_b, w_branch, w_out2d,
           first_blocks):
    m = xb.shape[0]
    tm = ROW_TILE
    nb = N_MODEL_BLOCKS
    n_tiles = m // tm
    chunk, staged = _prestaged_maps(nb, n_tiles)
    g0_first = 6 * N_HALF_BLOCKS
    g1_first = g0_first + nb
    gate_chunk = lambda first: pl.BlockSpec(
        (D_MODEL // n_tiles, COL_BLOCK), lambda jj, i: (chunk(jj, i), first + staged(jj)))
    branch_chunk = lambda b: pl.BlockSpec(
        (1, HALF // n_tiles, COL_BLOCK), lambda jj, i: (b, chunk(jj, i), staged(jj)))
    tile = lambda width: pl.BlockSpec((tm, width), lambda jj, i: (i, 0))
    bias = pl.BlockSpec((1, COL_BLOCK), lambda jj, i: (0, jj))
    wo_rows = D_MODEL // (nb * n_tiles)
    wo_slice = pl.BlockSpec((wo_rows, D_MODEL), lambda jj, i: (jj * n_tiles + i, 0))
    hbm = pl.BlockSpec(memory_space=pl.ANY)
    return pl.pallas_call(
        _merge_kernel,
        grid=(nb, n_tiles),
        in_specs=[tile(D_MODEL), tile(1),
                  pl.BlockSpec((D_MODEL // n_tiles, 1), lambda jj, i: (chunk(jj, i), 0)),
                  tile(COL_BLOCK), tile(HALF - COL_BLOCK), tile(HALF),
                  gate_chunk(g0_first), gate_chunk(g1_first), bias, bias,
                  branch_chunk(0), branch_chunk(1), wo_slice, hbm, hbm, hbm, hbm],
        out_specs=[pl.BlockSpec((tm, COL_BLOCK), lambda jj, i: (i, jj)), wo_slice],
        out_shape=[jax.ShapeDtypeStruct((m, D_MODEL), jnp.bfloat16),
                   jax.ShapeDtypeStruct((D_MODEL, D_MODEL), jnp.bfloat16)],
        scratch_shapes=[pltpu.VMEM((2, D_MODEL, COL_BLOCK), jnp.bfloat16),
                        pltpu.VMEM((2, D_MODEL, COL_BLOCK), jnp.bfloat16),
                        pltpu.VMEM((2, HALF, COL_BLOCK), jnp.bfloat16),
                        pltpu.VMEM((2, HALF, COL_BLOCK), jnp.bfloat16),
                        pltpu.SemaphoreType.DMA((4,))],
        compiler_params=_params(2),
        name="branch_merge",
    )(xb, rs, norm_col, y_pool0, y_pool_rest, y_conv, w_in2d, w_in2d, gate_b[0:1], gate_b[1:2],
      w_branch, w_branch, w_out2d, *first_blocks)


def _out_kernel(mg_ref, w_ref, x_ref, fw_ref, o_ref, rows, sumsq):
    i = pl.program_id(0)
    n = pl.program_id(1)
    n_tiles = pl.num_programs(0) - 1

    def emit_previous():
        scale = lax.rsqrt(sumsq[(i + 1) % 2] * (1.0 / D_MODEL) + NORM_EPS)
        o_ref[...] = rows[n] * scale * fw_ref[...]

    def project():
        xn = x_ref[...] + _dot(mg_ref[...], w_ref[...])
        rows[n] = xn
        part = jnp.sum(xn * xn, axis=-1, keepdims=True)
        cur = i % 2
        sumsq[cur] = jnp.where(n == 0, part, sumsq[cur] + part)

    @pl.when(i == 0)
    def _():
        project()

    @pl.when((i > 0) & (i < n_tiles))
    def _():
        emit_previous()
        project()

    @pl.when(i == n_tiles)
    def _():
        emit_previous()


def _out_proj(merged, w_out_bf, x2d, final_w):
    m = x2d.shape[0]
    tm = OUT_ROW_TILE
    tn = OUT_COL_BLOCK
    n_tiles = m // tm
    n_blocks = D_MODEL // tn
    last = n_tiles - 1
    in_row = lambda i: jnp.minimum(i, last)
    in_col = lambda i, n: jnp.where(i < n_tiles, n, n_blocks - 1)
    return pl.pallas_call(
        _out_kernel,
        grid=(n_tiles + 1, n_blocks),
        in_specs=[
            pl.BlockSpec((tm, D_MODEL), lambda i, n: (in_row(i), 0)),
            pl.BlockSpec((D_MODEL, tn), lambda i, n: (0, in_col(i, n))),
            pl.BlockSpec((tm, tn), lambda i, n: (in_row(i), in_col(i, n))),
            pl.BlockSpec((1, tn), lambda i, n: (0, n)),
        ],
        out_specs=pl.BlockSpec(
            (tm, tn), lambda i, n: (jnp.maximum(i - 1, 0), jnp.where(i > 0, n, 0))),
        out_shape=jax.ShapeDtypeStruct((m, D_MODEL), jnp.float32),
        scratch_shapes=[pltpu.VMEM((n_blocks, tm, tn), jnp.float32),
                        pltpu.VMEM((2, tm, 1), jnp.float32)],
        compiler_params=_params(2),
        name="out_proj_norm",
    )(merged, w_out_bf, x2d, final_w.reshape(1, D_MODEL))


def kernel(x, norm_w, w_in, pool_w, pool_scale, conv_w, conv_b, gate_b, w_branch, w_out,
           final_norm_w):
    b, s, d = x.shape
    assert (s, d) == (SEQ, D_MODEL) and norm_w.shape[0] == 1
    x2d = x.reshape(b * s, d)
    w_in2d = w_in[0]

    nb = N_HALF_BLOCKS
    norm_col = norm_w[0].reshape(d, 1)
    conv_first_cols = (2 * nb, 3 * nb, 4 * nb, 5 * nb)
    y_pool0, xb, rs, pool_first = _pool_first(x2d, norm_col, w_in2d, pool_w[0], pool_scale[0])
    y_pool_rest, conv_first = _pool_rest(xb, rs, norm_col, w_in2d, pool_w[0], pool_scale[0],
                                         pool_first, conv_first_cols)
    y_conv, merge_first = _conv_mixer(xb, rs, norm_col, w_in2d, conv_w[0], conv_b[0],
                                      conv_first, w_branch[0])
    merged, w_out_bf = _merge(xb, rs, norm_col, y_pool0, y_pool_rest, y_conv, w_in2d, gate_b[0],
                              w_branch[0], w_out[0], merge_first)
    out = _out_proj(merged, w_out_bf, x2d, final_norm_w)
    return out.reshape(b, s, d)
```

```python
import functools

import jax
import jax.numpy as jnp
from jax import lax
from jax.experimental import pallas as pl
from jax.experimental.pallas import tpu as pltpu

D_MODEL = 4096
SEQ = 4096
HALF = D_MODEL // 2
POOL_WINDOWS = (2, 4, 8, 16)
N_GROUPS = len(POOL_WINDOWS)
GROUP_DIM = HALF // N_GROUPS
CONV_K = 3
NORM_EPS = 1e-6

COL_BLOCK = GROUP_DIM
N_HALF_BLOCKS = HALF // COL_BLOCK
N_MODEL_BLOCKS = D_MODEL // COL_BLOCK
OUT_COL_BLOCK = 512
OUT_ROW_TILE = 1024
ROW_TILE = 512
POOL_ROW_TILE = 512
REST_ROW_TILE = 1024
POOL_HALO = 16
POOL_PAD = 16
POOL_TOP = POOL_PAD + POOL_HALO
CONV_HALO = 8

VMEM_LIMIT_BYTES = 62 * 1024 * 1024


def _dot(a, b):
    return jnp.dot(a, b, preferred_element_type=jnp.float32)


def _silu(z):
    return z / (1.0 + jnp.exp(-z))


def _sigmoid(z):
    return 1.0 / (1.0 + jnp.exp(-z))


def _params(n_axes):
    return pltpu.CompilerParams(dimension_semantics=("arbitrary",) * n_axes,
                                vmem_limit_bytes=VMEM_LIMIT_BYTES)


def _stage_chunks(slot, i, pairs, fold=None, row_scale=None):
    for k, (chunk_ref, buf) in enumerate(pairs):
        rows, cols = chunk_ref.shape[-2:]
        r0 = pl.multiple_of(i * rows, rows)
        chunk = chunk_ref[...].reshape(rows, cols)
        if row_scale is not None:
            chunk = chunk * row_scale[...]
        chunk = chunk.astype(buf.dtype)
        if k == 0 and fold is not None:
            chunk = _dot(chunk, fold).astype(buf.dtype)
        buf[slot, pl.ds(r0, rows), :] = chunk


def _prestaged_maps(n_blocks, n_tiles):
    def chunk(jj, i):
        return jnp.where(jj + 1 < n_blocks, i, n_tiles - 1)

    def staged(jj):
        return jnp.minimum(jj + 1, n_blocks - 1)

    return chunk, staged


def _load_prestaged(first_step, pairs, sems):
    @pl.when(first_step)
    def _():
        copies = [pltpu.make_async_copy(src, buf.at[0], sems.at[k])
                  for k, (src, buf) in enumerate(pairs)]
        for cp in copies:
            cp.start()
        for cp in copies:
            cp.wait()


def _cast_rows(pairs, row_scale=None, fold=None):
    for k, (src_ref, dst_ref) in enumerate(pairs):
        rows = src_ref[...]
        if row_scale is not None:
            rows = rows * row_scale[...]
        rows = rows.astype(dst_ref.dtype)
        if k == 0 and fold is not None:
            rows = _dot(rows, fold).astype(dst_ref.dtype)
        dst_ref[...] = rows


def _row_rms_scale(x):
    return lax.rsqrt(jnp.mean(x * x, axis=-1, keepdims=True) + NORM_EPS)


def _pool_tile(load_xb, rs, wu, wz, g, seq_tile, ps_ref, y_ref, ubuf, sbufs, ucarry):
    tm = rs.shape[0]
    end = POOL_TOP + tm
    u = rs * _dot(load_xb(), wu)

    ubuf[0:POOL_PAD, :] = jnp.zeros((POOL_PAD, COL_BLOCK), jnp.float32)
    ubuf[POOL_PAD:POOL_TOP, :] = jnp.where(seq_tile != 0, ucarry[...], 0.0)
    ubuf[POOL_TOP:end, :] = u
    ucarry[...] = ubuf[end - POOL_HALO:end, :]
    z = rs * _dot(load_xb(), wz)

    def shifted_sum(src, lo, shift):
        return src[lo:end, :] + src[lo - shift:end - shift, :]

    s2buf = sbufs[0]
    s2buf[8:end, :] = shifted_sum(ubuf, 8, 1)
    if isinstance(g, int):
        assert g == 0
        win = s2buf[POOL_TOP:end, :]
        window = POOL_WINDOWS[0]
    else:
        s4buf = sbufs[1]
        s4buf[16:end, :] = shifted_sum(s2buf, 16, 2)
        s8buf = s2buf
        s8buf[24:end, :] = shifted_sum(s4buf, 24, 4)
        win = jnp.where(g == 1, s4buf[POOL_TOP:end, :],
                        jnp.where(g == 2, s8buf[POOL_TOP:end, :],
                                  shifted_sum(s8buf, POOL_TOP, 8)))
        window = lax.shift_left(jnp.int32(2), g)
    t1 = seq_tile * tm + 1 + lax.broadcasted_iota(jnp.int32, (tm, 1), 0)
    inv_count = 1.0 / jnp.minimum(t1, window).astype(jnp.float32)
    mixed = win * inv_count - ubuf[POOL_TOP:end, :]
    ubuf[POOL_TOP:end, :] = mixed * ps_ref[...]
    y_ref[...] = (ubuf[POOL_TOP:end, :] * _silu(z)).astype(y_ref.dtype)


def _pool_first_kernel(x_ref, nw_ref, wu_ref, wz_ref, pw_ref, ps_ref,
                       y_ref, xb_ref, rs_ref, pu_ref, pz_ref,
                       wu_buf, wz_buf, ubuf, s2buf, ucarry, *, tiles_per_seq):
    jj = pl.program_id(0)
    i = pl.program_id(1)

    @pl.when((jj == 0) & (i == 0))
    def _():
        ucarry[...] = jnp.zeros_like(ucarry)

    @pl.when(jj == 0)
    def _():
        _stage_chunks(0, i, [(wu_ref, wu_buf), (wz_ref, wz_buf)],
                      fold=pw_ref[0].astype(jnp.bfloat16), row_scale=nw_ref)

    @pl.when(jj == 1)
    def _():
        _cast_rows([(wu_ref, pu_ref), (wz_ref, pz_ref)], row_scale=nw_ref,
                   fold=pw_ref[0].astype(jnp.bfloat16))
        x = x_ref[...]
        xb = x.astype(jnp.bfloat16)
        rs = _row_rms_scale(x)
        xb_ref[...] = xb
        rs_ref[...] = rs
        _pool_tile(lambda: xb, rs, wu_buf[0], wz_buf[0], 0, i % tiles_per_seq, ps_ref, y_ref,
                   ubuf, (s2buf,), ucarry)


def _pool_first(x2d, norm_col, w_in2d, pool_w, pool_scale):
    m = x2d.shape[0]
    tm = POOL_ROW_TILE
    n_tiles = m // tm
    chunk_rows = D_MODEL // n_tiles
    tile = lambda jj, i: jnp.where(jj > 0, i, 0)
    w_chunk = lambda first: pl.BlockSpec((chunk_rows, COL_BLOCK), lambda jj, i: (i, first + jj))
    side_out = pl.BlockSpec((chunk_rows, COL_BLOCK), lambda jj, i: (tile(jj, i), 0))
    w_buf = pltpu.VMEM((1, D_MODEL, COL_BLOCK), jnp.bfloat16)
    buf = pltpu.VMEM((POOL_TOP + tm, COL_BLOCK), jnp.float32)
    outs = pl.pallas_call(
        functools.partial(_pool_first_kernel, tiles_per_seq=SEQ // tm),
        grid=(2, n_tiles),
        in_specs=[
            pl.BlockSpec((tm, D_MODEL), lambda jj, i: (tile(jj, i), 0)),
            pl.BlockSpec((chunk_rows, 1), lambda jj, i: (i, 0)),
            w_chunk(0), w_chunk(N_GROUPS),
            pl.BlockSpec((1, GROUP_DIM, GROUP_DIM), lambda jj, i: (jj, 0, 0)),
            pl.BlockSpec((1, COL_BLOCK), lambda jj, i: (0, 0)),
        ],
        out_specs=[pl.BlockSpec((tm, COL_BLOCK), lambda jj, i: (tile(jj, i), 0)),
                   pl.BlockSpec((tm, D_MODEL), lambda jj, i: (tile(jj, i), 0)),
                   pl.BlockSpec((tm, 1), lambda jj, i: (tile(jj, i), 0)),
                   side_out, side_out],
        out_shape=[jax.ShapeDtypeStruct((m, COL_BLOCK), jnp.bfloat16),
                   jax.ShapeDtypeStruct((m, D_MODEL), jnp.bfloat16),
                   jax.ShapeDtypeStruct((m, 1), jnp.float32)]
        + [jax.ShapeDtypeStruct((D_MODEL, COL_BLOCK), jnp.bfloat16)] * 2,
        scratch_shapes=[w_buf, w_buf, buf, buf,
                        pltpu.VMEM((POOL_HALO, COL_BLOCK), jnp.float32)],
        compiler_params=_params(2),
        name="pool_first",
    )(x2d, norm_col, w_in2d, w_in2d, pool_w, pool_scale.reshape(1, HALF))
    return outs[0], outs[1], outs[2], outs[3:]


def _pool_rest_kernel(xb_ref, rs_ref, nws_ref, nwn_ref, wu_ref, wz_ref, pw_ref, ps_ref,
                      wu0_ref, wz0_ref, n0_ref, n1_ref, n2_ref, n3_ref,
                      y_ref, p0_ref, p1_ref, p2_ref, p3_ref,
                      wu_buf, wz_buf, ubuf, s2buf, s4buf, ucarry, sems, *, tiles_per_seq):
    jj = pl.program_id(0)
    i = pl.program_id(1)
    first_step = (jj == 0) & (i == 0)

    @pl.when(first_step)
    def _():
        ucarry[...] = jnp.zeros_like(ucarry)

    _load_prestaged(first_step, [(wu0_ref, wu_buf), (wz0_ref, wz_buf)], sems)

    _cast_rows([(n0_ref, p0_ref), (n1_ref, p1_ref), (n2_ref, p2_ref), (n3_ref, p3_ref)],
               row_scale=nwn_ref)
    slot = jj % 2
    _pool_tile(lambda: xb_ref[...], rs_ref[...], wu_buf[slot], wz_buf[slot], jj + 1,
               i % tiles_per_seq, ps_ref, y_ref, ubuf, (s2buf, s4buf), ucarry)
    _stage_chunks((jj + 1) % 2, i, [(wu_ref, wu_buf), (wz_ref, wz_buf)],
                  fold=pw_ref[0].astype(jnp.bfloat16), row_scale=nws_ref)


def _pool_rest(xb, rs, norm_col, w_in2d, pool_w, pool_scale, first_blocks, conv_blocks):
    m = xb.shape[0]
    tm = REST_ROW_TILE
    nb = N_GROUPS - 1
    n_tiles = m // tm
    chunk_rows = D_MODEL // n_tiles
    chunk, staged = _prestaged_maps(nb, n_tiles)
    w_chunk = lambda first: pl.BlockSpec((chunk_rows, COL_BLOCK),
                                         lambda jj, i: (chunk(jj, i), first + 1 + staged(jj)))
    side_steps = 2 * n_tiles
    side_rows = D_MODEL // side_steps
    side_step = lambda jj, i: jnp.minimum(jj * n_tiles + i, side_steps - 1)
    side_in = lambda blk: pl.BlockSpec((side_rows, COL_BLOCK),
                                       lambda jj, i: (side_step(jj, i), blk))
    side_out = pl.BlockSpec((side_rows, COL_BLOCK), lambda jj, i: (side_step(jj, i), 0))
    n_side = len(conv_blocks)
    hbm = pl.BlockSpec(memory_space=pl.ANY)
    w_buf = pltpu.VMEM((2, D_MODEL, COL_BLOCK), jnp.bfloat16)
    buf = pltpu.VMEM((POOL_TOP + tm, COL_BLOCK), jnp.float32)
    outs = pl.pallas_call(
        functools.partial(_pool_rest_kernel, tiles_per_seq=SEQ // tm),
        grid=(nb, n_tiles),
        in_specs=[
            pl.BlockSpec((tm, D_MODEL), lambda jj, i: (i, 0)),
            pl.BlockSpec((tm, 1), lambda jj, i: (i, 0)),
            pl.BlockSpec((chunk_rows, 1), lambda jj, i: (chunk(jj, i), 0)),
            pl.BlockSpec((side_rows, 1), lambda jj, i: (side_step(jj, i), 0)),
            w_chunk(0), w_chunk(N_GROUPS),
            pl.BlockSpec((1, GROUP_DIM, GROUP_DIM), lambda jj, i: (1 + staged(jj), 0, 0),
                         pipeline_mode=pl.Buffered(1)),
            pl.BlockSpec((1, COL_BLOCK), lambda jj, i: (0, jj + 1)),
            hbm, hbm,
        ] + [side_in(blk) for blk in conv_blocks],
        out_specs=[pl.BlockSpec((tm, COL_BLOCK), lambda jj, i: (i, jj))] + [side_out] * n_side,
        out_shape=[jax.ShapeDtypeStruct((m, nb * COL_BLOCK), jnp.bfloat16)]
        + [jax.ShapeDtypeStruct((D_MODEL, COL_BLOCK), jnp.bfloat16)] * n_side,
        scratch_shapes=[w_buf, w_buf, buf, buf, buf,
                        pltpu.VMEM((POOL_HALO, COL_BLOCK), jnp.float32),
                        pltpu.SemaphoreType.DMA((2,))],
        compiler_params=_params(2),
        name="pool_rest",
    )(xb, rs, norm_col, norm_col, w_in2d, w_in2d, pool_w, pool_scale.reshape(1, HALF),
      *first_blocks, *([w_in2d] * n_side))
    return outs[0], outs[1:]


def _conv_kernel(xb_ref, rs_ref, nws_ref, nwn_ref, wu_ref, wb_ref, wc_ref, wz_ref, cw_ref, cb_ref,
                 wu0_ref, wb0_ref, wc0_ref, wz0_ref, n0_ref, n1_ref, n2_ref, n3_ref,
                 y_ref, p0_ref, p1_ref, p2_ref, p3_ref,
                 wu_buf, wb_buf, wc_buf, wz_buf, vbuf, vcarry, sems, *, tiles_per_seq):
    jj = pl.program_id(0)
    i = pl.program_id(1)
    tm = xb_ref.shape[0]
    first_step = (jj == 0) & (i == 0)

    @pl.when(first_step)
    def _():
        vcarry[...] = jnp.zeros_like(vcarry)

    _load_prestaged(first_step, [(wu0_ref, wu_buf), (wb0_ref, wb_buf),
                                 (wc0_ref, wc_buf), (wz0_ref, wz_buf)], sems)

    def stage():
        _stage_chunks((jj + 1) % 2, i, [(wu_ref, wu_buf), (wb_ref, wb_buf),
                                        (wc_ref, wc_buf), (wz_ref, wz_buf)],
                      row_scale=nws_ref)

    def compute():
        slot = jj % 2
        seq_tile = i % tiles_per_seq
        rs = rs_ref[...]
        rs2 = rs * rs
        u = _dot(xb_ref[...], wu_buf[slot])
        _cast_rows([(n0_ref, p0_ref), (n1_ref, p1_ref)], row_scale=nwn_ref)
        c_gate = _dot(xb_ref[...], wc_buf[slot])
        vbuf[0:CONV_HALO, :] = jnp.where(seq_tile != 0, vcarry[...], 0.0)
        vbuf[CONV_HALO:CONV_HALO + tm, :] = rs2 * (c_gate * u)
        vcarry[...] = vbuf[tm:CONV_HALO + tm, :]
        cw = cw_ref[...]
        y = (cb_ref[...]
             + cw[0:1, :] * vbuf[CONV_HALO - 2:CONV_HALO - 2 + tm, :]
             + cw[1:2, :] * vbuf[CONV_HALO - 1:CONV_HALO - 1 + tm, :]
             + cw[2:3, :] * vbuf[CONV_HALO:CONV_HALO + tm, :])
        gated = (rs * y) * _silu(rs * _dot(xb_ref[...], wz_buf[slot]))
        _cast_rows([(n2_ref, p2_ref), (n3_ref, p3_ref)])
        y_ref[...] = (_dot(xb_ref[...], wb_buf[slot]) * gated).astype(y_ref.dtype)

    compute()
    stage()


def _conv_mixer(xb, rs, norm_col, w_in2d, conv_w, conv_b, first_blocks, w_branch):
    m = xb.shape[0]
    tm = ROW_TILE
    nb = N_HALF_BLOCKS
    n_tiles = m // tm
    chunk_rows = D_MODEL // n_tiles
    chunk, staged = _prestaged_maps(nb, n_tiles)
    w_chunk = lambda first: pl.BlockSpec((chunk_rows, COL_BLOCK),
                                         lambda jj, i: (chunk(jj, i), first + staged(jj)))
    hbm = pl.BlockSpec(memory_space=pl.ANY)
    n_steps = nb * n_tiles
    step = lambda jj, i: jj * n_tiles + i
    gate_rows = D_MODEL // n_steps
    branch_rows = HALF // n_steps
    gate_in = lambda blk: pl.BlockSpec((gate_rows, COL_BLOCK), lambda jj, i: (step(jj, i), blk))
    branch_in = lambda b: pl.BlockSpec((1, branch_rows, COL_BLOCK),
                                       lambda jj, i: (b, step(jj, i), 0))
    gate_out = pl.BlockSpec((gate_rows, COL_BLOCK), lambda jj, i: (step(jj, i), 0))
    branch_out = pl.BlockSpec((1, branch_rows, COL_BLOCK), lambda jj, i: (0, step(jj, i), 0))
    g0_first = 6 * N_HALF_BLOCKS
    g1_first = g0_first + N_MODEL_BLOCKS
    w_buf = pltpu.VMEM((2, D_MODEL, COL_BLOCK), jnp.bfloat16)
    outs = pl.pallas_call(
        functools.partial(_conv_kernel, tiles_per_seq=SEQ // tm),
        grid=(nb, n_tiles),
        in_specs=[
            pl.BlockSpec((tm, D_MODEL), lambda jj, i: (i, 0)),
            pl.BlockSpec((tm, 1), lambda jj, i: (i, 0)),
            pl.BlockSpec((chunk_rows, 1), lambda jj, i: (chunk(jj, i), 0)),
            pl.BlockSpec((gate_rows, 1), lambda jj, i: (step(jj, i), 0)),
            w_chunk(2 * nb), w_chunk(3 * nb), w_chunk(4 * nb), w_chunk(5 * nb),
            pl.BlockSpec((CONV_K, COL_BLOCK), lambda jj, i: (0, jj)),
            pl.BlockSpec((1, COL_BLOCK), lambda jj, i: (0, jj)),
            hbm, hbm, hbm, hbm,
            gate_in(g0_first), gate_in(g1_first), branch_in(0), branch_in(1),
        ],
        out_specs=[pl.BlockSpec((tm, COL_BLOCK), lambda jj, i: (i, jj)),
                   gate_out, gate_out, branch_out, branch_out],
        out_shape=[jax.ShapeDtypeStruct((m, HALF), jnp.bfloat16),
                   jax.ShapeDtypeStruct((D_MODEL, COL_BLOCK), jnp.bfloat16),
                   jax.ShapeDtypeStruct((D_MODEL, COL_BLOCK), jnp.bfloat16),
                   jax.ShapeDtypeStruct((1, HALF, COL_BLOCK), jnp.bfloat16),
                   jax.ShapeDtypeStruct((1, HALF, COL_BLOCK), jnp.bfloat16)],
        scratch_shapes=[w_buf, w_buf, w_buf, w_buf,
                        pltpu.VMEM((CONV_HALO + tm, COL_BLOCK), jnp.float32),
                        pltpu.VMEM((CONV_HALO, COL_BLOCK), jnp.float32),
                        pltpu.SemaphoreType.DMA((4,))],
        compiler_params=_params(2),
        name="conv_mixer",
    )(xb, rs, norm_col, norm_col, w_in2d, w_in2d, w_in2d, w_in2d, conv_w,
      conv_b.reshape(1, HALF), *first_blocks, w_in2d, w_in2d, w_branch, w_branch)
    return outs[0], outs[1:]


def _merge_kernel(xb_ref, rs_ref, nws_ref, yp0_ref, ypr_ref, yc_ref, wg0_ref, wg1_ref, gb_ref,
                  wb0_ref, wb1_ref, wo_ref, wg00_ref, wg10_ref, wb00_ref, wb10_ref,
                  o_ref, wo_bf_ref, wg0_buf, wg1_buf, wb0_buf, wb1_buf, sems):
    jj = pl.program_id(0)
    i = pl.program_id(1)

    _load_prestaged((jj == 0) & (i == 0),
                    [(wg00_ref, wg0_buf), (wg10_ref, wg1_buf),
                     (wb00_ref.at[0], wb0_buf), (wb10_ref.at[0], wb1_buf)], sems)

    def stage():
        _stage_chunks((jj + 1) % 2, i, [(wg0_ref, wg0_buf), (wg1_ref, wg1_buf)],
                      row_scale=nws_ref)
        _stage_chunks((jj + 1) % 2, i, [(wb0_ref, wb0_buf), (wb1_ref, wb1_buf)])

    def compute():
        slot = jj % 2
        h = xb_ref[...]
        rs = rs_ref[...]
        g0 = _sigmoid(rs * _dot(h, wg0_buf[slot]) + gb_ref[0:1, :])
        y_pool = jnp.concatenate([yp0_ref[...], ypr_ref[...]], axis=1)
        br0 = _dot(y_pool, wb0_buf[slot])
        g1 = _sigmoid(rs * _dot(h, wg1_buf[slot]) + gb_ref[1:2, :])
        br1 = _dot(yc_ref[...], wb1_buf[slot])
        o_ref[...] = (g0 * br0 + g1 * br1).astype(o_ref.dtype)
        wo_bf_ref[...] = wo_ref[...].astype(wo_bf_ref.dtype)

    compute()
    stage()


def _merge(xb, rs, norm_col, y_pool0, y_pool_rest, y_conv, w_in2d, gate_b, w_branch, w_out2d,
           first_blocks):
    m = xb.shape[0]
    tm = ROW_TILE
    nb = N_MODEL_BLOCKS
    n_tiles = m // tm
    chunk, staged = _prestaged_maps(nb, n_tiles)
    g0_first = 6 * N_HALF_BLOCKS
    g1_first = g0_first + nb
    gate_chunk = lambda first: pl.BlockSpec(
        (D_MODEL // n_tiles, COL_BLOCK), lambda jj, i: (chunk(jj, i), first + staged(jj)))
    branch_chunk = lambda b: pl.BlockSpec(
        (1, HALF // n_tiles, COL_BLOCK), lambda jj, i: (b, chunk(jj, i), staged(jj)))
    tile = lambda width: pl.BlockSpec((tm, width), lambda jj, i: (i, 0))
    bias = pl.BlockSpec((2, COL_BLOCK), lambda jj, i: (0, jj))
    wo_rows = D_MODEL // (nb * n_tiles)
    wo_slice = pl.BlockSpec((wo_rows, D_MODEL), lambda jj, i: (jj * n_tiles + i, 0))
    hbm = pl.BlockSpec(memory_space=pl.ANY)
    return pl.pallas_call(
        _merge_kernel,
        grid=(nb, n_tiles),
        in_specs=[tile(D_MODEL), tile(1),
                  pl.BlockSpec((D_MODEL // n_tiles, 1), lambda jj, i: (chunk(jj, i), 0)),
                  tile(COL_BLOCK), tile(HALF - COL_BLOCK), tile(HALF),
                  gate_chunk(g0_first), gate_chunk(g1_first), bias,
                  branch_chunk(0), branch_chunk(1), wo_slice, hbm, hbm, hbm, hbm],
        out_specs=[pl.BlockSpec((tm, COL_BLOCK), lambda jj, i: (i, jj)), wo_slice],
        out_shape=[jax.ShapeDtypeStruct((m, D_MODEL), jnp.bfloat16),
                   jax.ShapeDtypeStruct((D_MODEL, D_MODEL), jnp.bfloat16)],
        scratch_shapes=[pltpu.VMEM((2, D_MODEL, COL_BLOCK), jnp.bfloat16),
                        pltpu.VMEM((2, D_MODEL, COL_BLOCK), jnp.bfloat16),
                        pltpu.VMEM((2, HALF, COL_BLOCK), jnp.bfloat16),
                        pltpu.VMEM((2, HALF, COL_BLOCK), jnp.bfloat16),
                        pltpu.SemaphoreType.DMA((4,))],
        compiler_params=_params(2),
        name="branch_merge",
    )(xb, rs, norm_col, y_pool0, y_pool_rest, y_conv, w_in2d, w_in2d, gate_b,
      w_branch, w_branch, w_out2d, *first_blocks)


def _out_kernel(mg_ref, w_ref, x_ref, fw_ref, o_ref, rows, sumsq):
    i = pl.program_id(0)
    n = pl.program_id(1)
    n_tiles = pl.num_programs(0) - 1

    def emit_previous():
        scale = lax.rsqrt(sumsq[(i + 1) % 2] * (1.0 / D_MODEL) + NORM_EPS)
        o_ref[...] = rows[n] * scale * fw_ref[...]

    def project():
        xn = x_ref[...] + _dot(mg_ref[...], w_ref[...])
        rows[n] = xn
        part = jnp.sum(xn * xn, axis=-1, keepdims=True)
        cur = i % 2
        sumsq[cur] = jnp.where(n == 0, part, sumsq[cur] + part)

    @pl.when(i == 0)
    def _():
        project()

    @pl.when((i > 0) & (i < n_tiles))
    def _():
        emit_previous()
        project()

    @pl.when(i == n_tiles)
    def _():
        emit_previous()


def _out_proj(merged, w_out_bf, x2d, final_w):
    m = x2d.shape[0]
    tm = OUT_ROW_TILE
    tn = OUT_COL_BLOCK
    n_tiles = m // tm
    n_blocks = D_MODEL // tn
    last = n_tiles - 1
    in_row = lambda i: jnp.minimum(i, last)
    in_col = lambda i, n: jnp.where(i < n_tiles, n, n_blocks - 1)
    return pl.pallas_call(
        _out_kernel,
        grid=(n_tiles + 1, n_blocks),
        in_specs=[
            pl.BlockSpec((tm, D_MODEL), lambda i, n: (in_row(i), 0)),
            pl.BlockSpec((D_MODEL, tn), lambda i, n: (0, in_col(i, n))),
            pl.BlockSpec((tm, tn), lambda i, n: (in_row(i), in_col(i, n))),
            pl.BlockSpec((1, tn), lambda i, n: (0, n)),
        ],
        out_specs=pl.BlockSpec(
            (tm, tn), lambda i, n: (jnp.maximum(i - 1, 0), jnp.where(i > 0, n, 0))),
        out_shape=jax.ShapeDtypeStruct((m, D_MODEL), jnp.float32),
        scratch_shapes=[pltpu.VMEM((n_blocks, tm, tn), jnp.float32),
                        pltpu.VMEM((2, tm, 1), jnp.float32)],
        compiler_params=_params(2),
        name="out_proj_norm",
    )(merged, w_out_bf, x2d, final_w.reshape(1, D_MODEL))


def kernel(x, norm_w, w_in, pool_w, pool_scale, conv_w, conv_b, gate_b, w_branch, w_out,
           final_norm_w):
    b, s, d = x.shape
    assert (s, d) == (SEQ, D_MODEL) and norm_w.shape[0] == 1
    x2d = x.reshape(b * s, d)
    w_in2d = w_in[0]

    nb = N_HALF_BLOCKS
    norm_col = norm_w[0].reshape(d, 1)
    conv_first_cols = (2 * nb, 3 * nb, 4 * nb, 5 * nb)
    y_pool0, xb, rs, pool_first = _pool_first(x2d, norm_col, w_in2d, pool_w[0], pool_scale[0])
    y_pool_rest, conv_first = _pool_rest(xb, rs, norm_col, w_in2d, pool_w[0], pool_scale[0],
                                         pool_first, conv_first_cols)
    y_conv, merge_first = _conv_mixer(xb, rs, norm_col, w_in2d, conv_w[0], conv_b[0],
                                      conv_first, w_branch[0])
    merged, w_out_bf = _merge(xb, rs, norm_col, y_pool0, y_pool_rest, y_conv, w_in2d, gate_b[0],
                              w_branch[0], w_out[0], merge_first)
    out = _out_proj(merged, w_out_bf, x2d, final_norm_w)
    return out.reshape(b, s, d)
```

```python
import functools

import jax
import jax.numpy as jnp
from jax import lax
from jax.experimental import pallas as pl
from jax.experimental.pallas import tpu as pltpu

D_MODEL = 4096
SEQ = 4096
HALF = D_MODEL // 2
POOL_WINDOWS = (2, 4, 8, 16)
N_GROUPS = len(POOL_WINDOWS)
GROUP_DIM = HALF // N_GROUPS
CONV_K = 3
NORM_EPS = 1e-6

COL_BLOCK = GROUP_DIM
N_HALF_BLOCKS = HALF // COL_BLOCK
N_MODEL_BLOCKS = D_MODEL // COL_BLOCK
OUT_COL_BLOCK = 512
OUT_ROW_TILE = 1024
ROW_TILE = 512
POOL_ROW_TILE = 512
FIRST_STAGE_STEPS = 8
REST_ROW_TILE = 1024
POOL_HALO = 16
POOL_PAD = 16
POOL_TOP = POOL_PAD + POOL_HALO
CONV_HALO = 8

VMEM_LIMIT_BYTES = 62 * 1024 * 1024


def _dot(a, b):
    return jnp.dot(a, b, preferred_element_type=jnp.float32)


def _sigmoid(z):
    return 0.5 * jnp.tanh(0.5 * z) + 0.5


def _silu(z):
    return z * _sigmoid(z)


def _silu_exp(z):
    return z / (1.0 + jnp.exp(-z))


def _params(n_axes):
    return pltpu.CompilerParams(dimension_semantics=("arbitrary",) * n_axes,
                                vmem_limit_bytes=VMEM_LIMIT_BYTES)


def _stage_chunks(slot, i, pairs, fold=None, row_scale=None):
    for k, (chunk_ref, buf) in enumerate(pairs):
        rows, cols = chunk_ref.shape[-2:]
        r0 = pl.multiple_of(i * rows, rows)
        chunk = chunk_ref[...].reshape(rows, cols)
        if row_scale is not None:
            chunk = chunk * row_scale[...]
        chunk = chunk.astype(buf.dtype)
        if k == 0 and fold is not None:
            chunk = _dot(chunk, fold).astype(buf.dtype)
        buf[slot, pl.ds(r0, rows), :] = chunk


def _prestaged_maps(n_blocks, n_tiles):
    def chunk(jj, i):
        return jnp.where(jj + 1 < n_blocks, i, n_tiles - 1)

    def staged(jj):
        return jnp.minimum(jj + 1, n_blocks - 1)

    return chunk, staged


def _load_prestaged(first_step, pairs, sems):
    @pl.when(first_step)
    def _():
        copies = [pltpu.make_async_copy(src, buf.at[0], sems.at[k])
                  for k, (src, buf) in enumerate(pairs)]
        for cp in copies:
            cp.start()
        for cp in copies:
            cp.wait()


def _cast_rows(pairs, row_scale=None, fold=None):
    for k, (src_ref, dst_ref) in enumerate(pairs):
        rows = src_ref[...]
        if row_scale is not None:
            rows = rows * row_scale[...]
        rows = rows.astype(dst_ref.dtype)
        if k == 0 and fold is not None:
            rows = _dot(rows, fold).astype(dst_ref.dtype)
        dst_ref[...] = rows


def _row_rms_scale(x):
    return lax.rsqrt(jnp.mean(x * x, axis=-1, keepdims=True) + NORM_EPS)


def _pool_tile(load_xb, rs, wu, wz, g, seq_tile, ps_ref, y_ref, ubuf, sbufs, ucarry):
    tm = rs.shape[0]
    end = POOL_TOP + tm
    u = rs * _dot(load_xb(), wu)

    ubuf[0:POOL_PAD, :] = jnp.zeros((POOL_PAD, COL_BLOCK), jnp.float32)
    ubuf[POOL_PAD:POOL_TOP, :] = jnp.where(seq_tile != 0, ucarry[...], 0.0)
    ubuf[POOL_TOP:end, :] = u
    ucarry[...] = ubuf[end - POOL_HALO:end, :]
    z = rs * _dot(load_xb(), wz)

    def shifted_sum(src, lo, shift):
        return src[lo:end, :] + src[lo - shift:end - shift, :]

    s2buf = sbufs[0]
    s2buf[8:end, :] = shifted_sum(ubuf, 8, 1)
    if isinstance(g, int):
        assert g == 0
        win = s2buf[POOL_TOP:end, :]
        window = POOL_WINDOWS[0]
    else:
        s4buf = sbufs[1]
        s4buf[16:end, :] = shifted_sum(s2buf, 16, 2)
        s8buf = s2buf
        s8buf[24:end, :] = shifted_sum(s4buf, 24, 4)
        win = jnp.where(g == 1, s4buf[POOL_TOP:end, :],
                        jnp.where(g == 2, s8buf[POOL_TOP:end, :],
                                  shifted_sum(s8buf, POOL_TOP, 8)))
        window = lax.shift_left(jnp.int32(2), g)
    t1 = seq_tile * tm + 1 + lax.broadcasted_iota(jnp.int32, (tm, 1), 0)
    inv_count = 1.0 / jnp.minimum(t1, window).astype(jnp.float32)
    mixed = win * inv_count - ubuf[POOL_TOP:end, :]
    ubuf[POOL_TOP:end, :] = mixed * ps_ref[...]
    y_ref[...] = (ubuf[POOL_TOP:end, :] * _silu(z)).astype(y_ref.dtype)


def _pool_first_kernel(x_ref, nw_ref, wu_ref, wz_ref, nu_ref, nz_ref, pw_ref, npw_ref, ps_ref,
                       y_ref, xb_ref, rs_ref, pu_ref, pz_ref,
                       wu_buf, wz_buf, ubuf, s2buf, ucarry, *, n_stage, tiles_per_seq):
    s = pl.program_id(0)

    @pl.when(s == 0)
    def _():
        ucarry[...] = jnp.zeros_like(ucarry)

    @pl.when(s < n_stage)
    def _():
        _stage_chunks(0, s, [(wu_ref, wu_buf), (wz_ref, wz_buf)],
                      fold=pw_ref[0].astype(jnp.bfloat16), row_scale=nw_ref)
        _cast_rows([(nu_ref, pu_ref), (nz_ref, pz_ref)], row_scale=nw_ref,
                   fold=npw_ref[0].astype(jnp.bfloat16))

    @pl.when(s >= n_stage)
    def _():
        x = x_ref[...]
        xb = x.astype(jnp.bfloat16)
        rs = _row_rms_scale(x)
        xb_ref[...] = xb
        rs_ref[...] = rs
        _pool_tile(lambda: xb, rs, wu_buf[0], wz_buf[0], 0, (s - n_stage) % tiles_per_seq,
                   ps_ref, y_ref, ubuf, (s2buf,), ucarry)


def _pool_first(x2d, norm_col, w_in2d, pool_w, pool_scale):
    m = x2d.shape[0]
    tm = POOL_ROW_TILE
    n_tiles = m // tm
    n_stage = FIRST_STAGE_STEPS
    chunk_rows = D_MODEL // n_stage
    chunk = lambda s: jnp.minimum(s, n_stage - 1)
    tile = lambda s: jnp.maximum(s - n_stage, 0)
    w_chunk = lambda blk: pl.BlockSpec((chunk_rows, COL_BLOCK), lambda s: (chunk(s), blk))
    group_map = lambda g: pl.BlockSpec((1, GROUP_DIM, GROUP_DIM), lambda s: (g, 0, 0))
    side_out = pl.BlockSpec((chunk_rows, COL_BLOCK), lambda s: (chunk(s), 0))
    w_buf = pltpu.VMEM((1, D_MODEL, COL_BLOCK), jnp.bfloat16)
    buf = pltpu.VMEM((POOL_TOP + tm, COL_BLOCK), jnp.float32)
    outs = pl.pallas_call(
        functools.partial(_pool_first_kernel, n_stage=n_stage, tiles_per_seq=SEQ // tm),
        grid=(n_stage + n_tiles,),
        in_specs=[
            pl.BlockSpec((tm, D_MODEL), lambda s: (tile(s), 0)),
            pl.BlockSpec((chunk_rows, 1), lambda s: (chunk(s), 0)),
            w_chunk(0), w_chunk(N_GROUPS),
            w_chunk(1), w_chunk(N_GROUPS + 1),
            group_map(0), group_map(1),
            pl.BlockSpec((1, COL_BLOCK), lambda s: (0, 0)),
        ],
        out_specs=[pl.BlockSpec((tm, COL_BLOCK), lambda s: (tile(s), 0)),
                   pl.BlockSpec((tm, D_MODEL), lambda s: (tile(s), 0)),
                   pl.BlockSpec((tm, 1), lambda s: (tile(s), 0)),
                   side_out, side_out],
        out_shape=[jax.ShapeDtypeStruct((m, COL_BLOCK), jnp.bfloat16),
                   jax.ShapeDtypeStruct((m, D_MODEL), jnp.bfloat16),
                   jax.ShapeDtypeStruct((m, 1), jnp.float32)]
        + [jax.ShapeDtypeStruct((D_MODEL, COL_BLOCK), jnp.bfloat16)] * 2,
        scratch_shapes=[w_buf, w_buf, buf, buf,
                        pltpu.VMEM((POOL_HALO, COL_BLOCK), jnp.float32)],
        compiler_params=_params(1),
        name="pool_first",
    )(x2d, norm_col, w_in2d, w_in2d, w_in2d, w_in2d, pool_w, pool_w,
      pool_scale.reshape(1, HALF))
    return outs[0], outs[1], outs[2], outs[3:]


def _pool_rest_kernel(xb_ref, rs_ref, nws_ref, nwn_ref, wu_ref, wz_ref, pw_ref, ps_ref,
                      wu0_ref, wz0_ref, n0_ref, n1_ref, n2_ref, n3_ref,
                      y_ref, p0_ref, p1_ref, p2_ref, p3_ref,
                      wu_buf, wz_buf, ubuf, s2buf, s4buf, ucarry, sems, *, tiles_per_seq):
    jj = pl.program_id(0)
    i = pl.program_id(1)
    first_step = (jj == 0) & (i == 0)

    @pl.when(first_step)
    def _():
        ucarry[...] = jnp.zeros_like(ucarry)

    _load_prestaged(first_step, [(wu0_ref, wu_buf), (wz0_ref, wz_buf)], sems)

    _cast_rows([(n0_ref, p0_ref), (n1_ref, p1_ref), (n2_ref, p2_ref), (n3_ref, p3_ref)],
               row_scale=nwn_ref)
    slot = jj % 2
    _pool_tile(lambda: xb_ref[...], rs_ref[...], wu_buf[slot], wz_buf[slot], jj + 1,
               i % tiles_per_seq, ps_ref, y_ref, ubuf, (s2buf, s4buf), ucarry)
    _stage_chunks((jj + 1) % 2, i, [(wu_ref, wu_buf), (wz_ref, wz_buf)],
                  fold=pw_ref[0].astype(jnp.bfloat16), row_scale=nws_ref)


def _pool_rest(xb, rs, norm_col, w_in2d, pool_w, pool_scale, first_blocks, conv_blocks):
    m = xb.shape[0]
    tm = REST_ROW_TILE
    nb = N_GROUPS - 1
    n_tiles = m // tm
    chunk_rows = D_MODEL // n_tiles
    chunk, staged = _prestaged_maps(nb, n_tiles)
    w_chunk = lambda first: pl.BlockSpec((chunk_rows, COL_BLOCK),
                                         lambda jj, i: (chunk(jj, i), first + 1 + staged(jj)))
    side_steps = 2 * n_tiles
    side_rows = D_MODEL // side_steps
    side_step = lambda jj, i: jnp.minimum(jj * n_tiles + i, side_steps - 1)
    side_in = lambda blk: pl.BlockSpec((side_rows, COL_BLOCK),
                                       lambda jj, i: (side_step(jj, i), blk))
    side_out = pl.BlockSpec((side_rows, COL_BLOCK), lambda jj, i: (side_step(jj, i), 0))
    n_side = len(conv_blocks)
    hbm = pl.BlockSpec(memory_space=pl.ANY)
    w_buf = pltpu.VMEM((2, D_MODEL, COL_BLOCK), jnp.bfloat16)
    buf = pltpu.VMEM((POOL_TOP + tm, COL_BLOCK), jnp.float32)
    outs = pl.pallas_call(
        functools.partial(_pool_rest_kernel, tiles_per_seq=SEQ // tm),
        grid=(nb, n_tiles),
        in_specs=[
            pl.BlockSpec((tm, D_MODEL), lambda jj, i: (i, 0)),
            pl.BlockSpec((tm, 1), lambda jj, i: (i, 0)),
            pl.BlockSpec((chunk_rows, 1), lambda jj, i: (chunk(jj, i), 0)),
            pl.BlockSpec((side_rows, 1), lambda jj, i: (side_step(jj, i), 0)),
            w_chunk(0), w_chunk(N_GROUPS),
            pl.BlockSpec((1, GROUP_DIM, GROUP_DIM), lambda jj, i: (1 + staged(jj), 0, 0),
                         pipeline_mode=pl.Buffered(1)),
            pl.BlockSpec((1, COL_BLOCK), lambda jj, i: (0, jj + 1)),
            hbm, hbm,
        ] + [side_in(blk) for blk in conv_blocks],
        out_specs=[pl.BlockSpec((tm, COL_BLOCK), lambda jj, i: (i, jj))] + [side_out] * n_side,
        out_shape=[jax.ShapeDtypeStruct((m, nb * COL_BLOCK), jnp.bfloat16)]
        + [jax.ShapeDtypeStruct((D_MODEL, COL_BLOCK), jnp.bfloat16)] * n_side,
        scratch_shapes=[w_buf, w_buf, buf, buf, buf,
                        pltpu.VMEM((POOL_HALO, COL_BLOCK), jnp.float32),
                        pltpu.SemaphoreType.DMA((2,))],
        compiler_params=_params(2),
        name="pool_rest",
    )(xb, rs, norm_col, norm_col, w_in2d, w_in2d, pool_w, pool_scale.reshape(1, HALF),
      *first_blocks, *([w_in2d] * n_side))
    return outs[0], outs[1:]


def _conv_kernel(xb_ref, rs_ref, nws_ref, nwn_ref, wu_ref, wb_ref, wc_ref, wz_ref, cw_ref, cb_ref,
                 wu0_ref, wb0_ref, wc0_ref, wz0_ref, n0_ref, n1_ref, n2_ref, n3_ref,
                 y_ref, p0_ref, p1_ref, p2_ref, p3_ref,
                 wu_buf, wb_buf, wc_buf, wz_buf, vbuf, vcarry, sems, *, tiles_per_seq):
    jj = pl.program_id(0)
    i = pl.program_id(1)
    tm = xb_ref.shape[0]
    first_step = (jj == 0) & (i == 0)

    @pl.when(first_step)
    def _():
        vcarry[...] = jnp.zeros_like(vcarry)

    _load_prestaged(first_step, [(wu0_ref, wu_buf), (wb0_ref, wb_buf),
                                 (wc0_ref, wc_buf), (wz0_ref, wz_buf)], sems)

    def stage():
        _stage_chunks((jj + 1) % 2, i, [(wu_ref, wu_buf), (wb_ref, wb_buf),
                                        (wc_ref, wc_buf), (wz_ref, wz_buf)],
                      row_scale=nws_ref)

    def compute():
        slot = jj % 2
        seq_tile = i % tiles_per_seq
        rs = rs_ref[...]
        rs2 = rs * rs
        u = _dot(xb_ref[...], wu_buf[slot])
        _cast_rows([(n0_ref, p0_ref), (n1_ref, p1_ref)], row_scale=nwn_ref)
        c_gate = _dot(xb_ref[...], wc_buf[slot])
        vbuf[0:CONV_HALO, :] = jnp.where(seq_tile != 0, vcarry[...], 0.0)
        vbuf[CONV_HALO:CONV_HALO + tm, :] = rs2 * (c_gate * u)
        vcarry[...] = vbuf[tm:CONV_HALO + tm, :]
        cw = cw_ref[...]
        y = (cb_ref[...]
             + cw[0:1, :] * vbuf[CONV_HALO - 2:CONV_HALO - 2 + tm, :]
             + cw[1:2, :] * vbuf[CONV_HALO - 1:CONV_HALO - 1 + tm, :]
             + cw[2:3, :] * vbuf[CONV_HALO:CONV_HALO + tm, :])
        gated = (rs * y) * _silu_exp(rs * _dot(xb_ref[...], wz_buf[slot]))
        _cast_rows([(n2_ref, p2_ref), (n3_ref, p3_ref)])
        y_ref[...] = (_dot(xb_ref[...], wb_buf[slot]) * gated).astype(y_ref.dtype)

    compute()
    stage()


def _conv_mixer(xb, rs, norm_col, w_in2d, conv_w, conv_b, first_blocks, w_branch):
    m = xb.shape[0]
    tm = ROW_TILE
    nb = N_HALF_BLOCKS
    n_tiles = m // tm
    chunk_rows = D_MODEL // n_tiles
    chunk, staged = _prestaged_maps(nb, n_tiles)
    w_chunk = lambda first: pl.BlockSpec((chunk_rows, COL_BLOCK),
                                         lambda jj, i: (chunk(jj, i), first + staged(jj)))
    hbm = pl.BlockSpec(memory_space=pl.ANY)
    n_steps = nb * n_tiles
    step = lambda jj, i: jj * n_tiles + i
    gate_rows = D_MODEL // n_steps
    branch_rows = HALF // n_steps
    gate_in = lambda blk: pl.BlockSpec((gate_rows, COL_BLOCK), lambda jj, i: (step(jj, i), blk))
    branch_in = lambda b: pl.BlockSpec((1, branch_rows, COL_BLOCK),
                                       lambda jj, i: (b, step(jj, i), 0))
    gate_out = pl.BlockSpec((gate_rows, COL_BLOCK), lambda jj, i: (step(jj, i), 0))
    branch_out = pl.BlockSpec((1, branch_rows, COL_BLOCK), lambda jj, i: (0, step(jj, i), 0))
    g0_first = 6 * N_HALF_BLOCKS
    g1_first = g0_first + N_MODEL_BLOCKS
    w_buf = pltpu.VMEM((2, D_MODEL, COL_BLOCK), jnp.bfloat16)
    outs = pl.pallas_call(
        functools.partial(_conv_kernel, tiles_per_seq=SEQ // tm),
        grid=(nb, n_tiles),
        in_specs=[
            pl.BlockSpec((tm, D_MODEL), lambda jj, i: (i, 0)),
            pl.BlockSpec((tm, 1), lambda jj, i: (i, 0)),
            pl.BlockSpec((chunk_rows, 1), lambda jj, i: (chunk(jj, i), 0)),
            pl.BlockSpec((gate_rows, 1), lambda jj, i: (step(jj, i), 0)),
            w_chunk(2 * nb), w_chunk(3 * nb), w_chunk(4 * nb), w_chunk(5 * nb),
            pl.BlockSpec((CONV_K, COL_BLOCK), lambda jj, i: (0, jj)),
            pl.BlockSpec((1, COL_BLOCK), lambda jj, i: (0, jj)),
            hbm, hbm, hbm, hbm,
            gate_in(g0_first), gate_in(g1_first), branch_in(0), branch_in(1),
        ],
        out_specs=[pl.BlockSpec((tm, COL_BLOCK), lambda jj, i: (i, jj)),
                   gate_out, gate_out, branch_out, branch_out],
        out_shape=[jax.ShapeDtypeStruct((m, HALF), jnp.bfloat16),
                   jax.ShapeDtypeStruct((D_MODEL, COL_BLOCK), jnp.bfloat16),
                   jax.ShapeDtypeStruct((D_MODEL, COL_BLOCK), jnp.bfloat16),
                   jax.ShapeDtypeStruct((1, HALF, COL_BLOCK), jnp.bfloat16),
                   jax.ShapeDtypeStruct((1, HALF, COL_BLOCK), jnp.bfloat16)],
        scratch_shapes=[w_buf, w_buf, w_buf, w_buf,
                        pltpu.VMEM((CONV_HALO + tm, COL_BLOCK), jnp.float32),
                        pltpu.VMEM((CONV_HALO, COL_BLOCK), jnp.float32),
                        pltpu.SemaphoreType.DMA((4,))],
        compiler_params=_params(2),
        name="conv_mixer",
    )(xb, rs, norm_col, norm_col, w_in2d, w_in2d, w_in2d, w_in2d, conv_w,
      conv_b.reshape(1, HALF), *first_blocks, w_in2d, w_in2d, w_branch, w_branch)
    return outs[0], outs[1:]


def _merge_kernel(xb_ref, rs_ref, nws_ref, yp0_ref, ypr_ref, yc_ref, wg0_ref, wg1_ref, gb_ref,
                  wb0_ref, wb1_ref, wo_ref, wg00_ref, wg10_ref, wb00_ref, wb10_ref,
                  o_ref, wo_bf_ref, wg0_buf, wg1_buf, wb0_buf, wb1_buf, sems):
    jj = pl.program_id(0)
    i = pl.program_id(1)

    _load_prestaged((jj == 0) & (i == 0),
                    [(wg00_ref, wg0_buf), (wg10_ref, wg1_buf),
                     (wb00_ref.at[0], wb0_buf), (wb10_ref.at[0], wb1_buf)], sems)

    def stage():
        _stage_chunks((jj + 1) % 2, i, [(wg0_ref, wg0_buf), (wg1_ref, wg1_buf)],
                      row_scale=nws_ref)
        _stage_chunks((jj + 1) % 2, i, [(wb0_ref, wb0_buf), (wb1_ref, wb1_buf)])

    def compute():
        slot = jj % 2
        h = xb_ref[...]
        rs = rs_ref[...]
        g0 = _sigmoid(rs * _dot(h, wg0_buf[slot]) + gb_ref[0:1, :])
        y_pool = jnp.concatenate([yp0_ref[...], ypr_ref[...]], axis=1)
        br0 = _dot(y_pool, wb0_buf[slot])
        g1 = _sigmoid(rs * _dot(h, wg1_buf[slot]) + gb_ref[1:2, :])
        br1 = _dot(yc_ref[...], wb1_buf[slot])
        o_ref[...] = (g0 * br0 + g1 * br1).astype(o_ref.dtype)
        wo_bf_ref[...] = wo_ref[...].astype(wo_bf_ref.dtype)

    compute()
    stage()


def _merge(xb, rs, norm_col, y_pool0, y_pool_rest, y_conv, w_in2d, gate_b, w_branch, w_out2d,
           first_blocks):
    m = xb.shape[0]
    tm = ROW_TILE
    nb = N_MODEL_BLOCKS
    n_tiles = m // tm
    chunk, staged = _prestaged_maps(nb, n_tiles)
    g0_first = 6 * N_HALF_BLOCKS
    g1_first = g0_first + nb
    gate_chunk = lambda first: pl.BlockSpec(
        (D_MODEL // n_tiles, COL_BLOCK), lambda jj, i: (chunk(jj, i), first + staged(jj)))
    branch_chunk = lambda b: pl.BlockSpec(
        (1, HALF // n_tiles, COL_BLOCK), lambda jj, i: (b, chunk(jj, i), staged(jj)))
    tile = lambda width: pl.BlockSpec((tm, width), lambda jj, i: (i, 0))
    bias = pl.BlockSpec((2, COL_BLOCK), lambda jj, i: (0, jj))
    wo_rows = D_MODEL // (nb * n_tiles)
    wo_slice = pl.BlockSpec((wo_rows, D_MODEL), lambda jj, i: (jj * n_tiles + i, 0))
    hbm = pl.BlockSpec(memory_space=pl.ANY)
    return pl.pallas_call(
        _merge_kernel,
        grid=(nb, n_tiles),
        in_specs=[tile(D_MODEL), tile(1),
                  pl.BlockSpec((D_MODEL // n_tiles, 1), lambda jj, i: (chunk(jj, i), 0)),
                  tile(COL_BLOCK), tile(HALF - COL_BLOCK), tile(HALF),
                  gate_chunk(g0_first), gate_chunk(g1_first), bias,
                  branch_chunk(0), branch_chunk(1), wo_slice, hbm, hbm, hbm, hbm],
        out_specs=[pl.BlockSpec((tm, COL_BLOCK), lambda jj, i: (i, jj)), wo_slice],
        out_shape=[jax.ShapeDtypeStruct((m, D_MODEL), jnp.bfloat16),
                   jax.ShapeDtypeStruct((D_MODEL, D_MODEL), jnp.bfloat16)],
        scratch_shapes=[pltpu.VMEM((2, D_MODEL, COL_BLOCK), jnp.bfloat16),
                        pltpu.VMEM((2, D_MODEL, COL_BLOCK), jnp.bfloat16),
                        pltpu.VMEM((2, HALF, COL_BLOCK), jnp.bfloat16),
                        pltpu.VMEM((2, HALF, COL_BLOCK), jnp.bfloat16),
                        pltpu.SemaphoreType.DMA((4,))],
        compiler_params=_params(2),
        name="branch_merge",
    )(xb, rs, norm_col, y_pool0, y_pool_rest, y_conv, w_in2d, w_in2d, gate_b,
      w_branch, w_branch, w_out2d, *first_blocks)


def _out_kernel(mg_ref, w_ref, x_ref, fw_ref, o_ref, rows, sumsq):
    i = pl.program_id(0)
    n = pl.program_id(1)
    n_tiles = pl.num_programs(0) - 1

    def emit_previous():
        scale = lax.rsqrt(sumsq[(i + 1) % 2] * (1.0 / D_MODEL) + NORM_EPS)
        o_ref[...] = rows[n] * scale * fw_ref[...]

    def project():
        xn = x_ref[...] + _dot(mg_ref[...], w_ref[...])
        rows[n] = xn
        part = jnp.sum(xn * xn, axis=-1, keepdims=True)
        cur = i % 2
        sumsq[cur] = jnp.where(n == 0, part, sumsq[cur] + part)

    @pl.when(i == 0)
    def _():
        project()

    @pl.when((i > 0) & (i < n_tiles))
    def _():
        emit_previous()
        project()

    @pl.when(i == n_tiles)
    def _():
        emit_previous()


def _out_proj(merged, w_out_bf, x2d, final_w):
    m = x2d.shape[0]
    tm = OUT_ROW_TILE
    tn = OUT_COL_BLOCK
    n_tiles = m // tm
    n_blocks = D_MODEL // tn
    last = n_tiles - 1
    in_row = lambda i: jnp.minimum(i, last)
    in_col = lambda i, n: jnp.where(i < n_tiles, n, n_blocks - 1)
    return pl.pallas_call(
        _out_kernel,
        grid=(n_tiles + 1, n_blocks),
        in_specs=[
            pl.BlockSpec((tm, D_MODEL), lambda i, n: (in_row(i), 0)),
            pl.BlockSpec((D_MODEL, tn), lambda i, n: (0, in_col(i, n))),
            pl.BlockSpec((tm, tn), lambda i, n: (in_row(i), in_col(i, n))),
            pl.BlockSpec((1, tn), lambda i, n: (0, n)),
        ],
        out_specs=pl.BlockSpec(
            (tm, tn), lambda i, n: (jnp.maximum(i - 1, 0), jnp.where(i > 0, n, 0))),
        out_shape=jax.ShapeDtypeStruct((m, D_MODEL), jnp.float32),
        scratch_shapes=[pltpu.VMEM((n_blocks, tm, tn), jnp.float32),
                        pltpu.VMEM((2, tm, 1), jnp.float32)],
        compiler_params=_params(2),
        name="out_proj_norm",
    )(merged, w_out_bf, x2d, final_w.reshape(1, D_MODEL))


def kernel(x, norm_w, w_in, pool_w, pool_scale, conv_w, conv_b, gate_b, w_branch, w_out,
           final_norm_w):
    b, s, d = x.shape
    assert (s, d) == (SEQ, D_MODEL) and norm_w.shape[0] == 1
    x2d = x.reshape(b * s, d)
    w_in2d = w_in[0]

    nb = N_HALF_BLOCKS
    norm_col = norm_w[0].reshape(d, 1)
    conv_first_cols = (2 * nb, 3 * nb, 4 * nb, 5 * nb)
    y_pool0, xb, rs, pool_first = _pool_first(x2d, norm_col, w_in2d, pool_w[0], pool_scale[0])
    y_pool_rest, conv_first = _pool_rest(xb, rs, norm_col, w_in2d, pool_w[0], pool_scale[0],
                                         pool_first, conv_first_cols)
    y_conv, merge_first = _conv_mixer(xb, rs, norm_col, w_in2d, conv_w[0], conv_b[0],
                                      conv_first, w_branch[0])
    merged, w_out_bf = _merge(xb, rs, norm_col, y_pool0, y_pool_rest, y_conv, w_in2d, gate_b[0],
                              w_branch[0], w_out[0], merge_first)
    out = _out_proj(merged, w_out_bf, x2d, final_norm_w)
    return out.reshape(b, s, d)
```

```python
import functools

import jax
import jax.numpy as jnp
from jax import lax
from jax.experimental import pallas as pl
from jax.experimental.pallas import tpu as pltpu

D_MODEL = 4096
SEQ = 4096
HALF = D_MODEL // 2
POOL_WINDOWS = (2, 4, 8, 16)
N_GROUPS = len(POOL_WINDOWS)
GROUP_DIM = HALF // N_GROUPS
CONV_K = 3
NORM_EPS = 1e-6

COL_BLOCK = GROUP_DIM
N_HALF_BLOCKS = HALF // COL_BLOCK
N_MODEL_BLOCKS = D_MODEL // COL_BLOCK
OUT_COL_BLOCK = 512
OUT_ROW_TILE = 1024
ROW_TILE = 512
POOL_ROW_TILE = 512
FIRST_STAGE_STEPS = 8
REST_ROW_TILE = 1024
POOL_HALO = 16
POOL_PAD = 16
POOL_TOP = POOL_PAD + POOL_HALO
CONV_HALO = 8

VMEM_LIMIT_BYTES = 62 * 1024 * 1024


def _dot(a, b):
    return jnp.dot(a, b, preferred_element_type=jnp.float32)


def _sigmoid(z):
    return 0.5 * jnp.tanh(0.5 * z) + 0.5


def _silu(z):
    return z * _sigmoid(z)


def _silu_exp(z):
    return z / (1.0 + jnp.exp(-z))


def _params(n_axes):
    return pltpu.CompilerParams(dimension_semantics=("arbitrary",) * n_axes,
                                vmem_limit_bytes=VMEM_LIMIT_BYTES)


def _stage_chunks(slot, i, pairs, fold=None):
    for k, (chunk_ref, buf) in enumerate(pairs):
        if isinstance(chunk_ref, tuple):
            chunk_ref, which = chunk_ref
            chunk = chunk_ref[which]
        else:
            chunk = chunk_ref[...]
        rows, cols = chunk.shape
        r0 = pl.multiple_of(i * rows, rows)
        chunk = chunk.astype(buf.dtype)
        if k == 0 and fold is not None:
            chunk = _dot(chunk, fold).astype(buf.dtype)
        buf[slot, pl.ds(r0, rows), :] = chunk


def _prestaged_maps(n_blocks, n_tiles):
    def chunk(jj, i):
        return jnp.where(jj + 1 < n_blocks, i, n_tiles - 1)

    def staged(jj):
        return jnp.minimum(jj + 1, n_blocks - 1)

    return chunk, staged


def _load_prestaged(first_step, pairs, sems):
    @pl.when(first_step)
    def _():
        copies = [pltpu.make_async_copy(src, buf.at[0], sems.at[k])
                  for k, (src, buf) in enumerate(pairs)]
        for cp in copies:
            cp.start()
        for cp in copies:
            cp.wait()


def _cast_rows(pairs, fold=None):
    for k, (src_ref, dst_ref) in enumerate(pairs):
        rows = src_ref[...].astype(dst_ref.dtype)
        if k == 0 and fold is not None:
            rows = _dot(rows, fold).astype(dst_ref.dtype)
        dst_ref[...] = rows


def _row_rms_scale(x):
    return lax.rsqrt(jnp.mean(x * x, axis=-1, keepdims=True) + NORM_EPS)


def _pool_tile(load_xb, rs, wu, wz, g, seq_tile, ps_ref, y_ref, ubuf, sbufs, ucarry):
    tm = rs.shape[0]
    end = POOL_TOP + tm
    u = rs * _dot(load_xb(), wu)

    ubuf[0:POOL_PAD, :] = jnp.zeros((POOL_PAD, COL_BLOCK), jnp.float32)
    ubuf[POOL_PAD:POOL_TOP, :] = jnp.where(seq_tile != 0, ucarry[...], 0.0)
    ubuf[POOL_TOP:end, :] = u
    ucarry[...] = ubuf[end - POOL_HALO:end, :]
    z = rs * _dot(load_xb(), wz)

    def shifted_sum(src, lo, shift):
        return src[lo:end, :] + src[lo - shift:end - shift, :]

    s2buf = sbufs[0]
    s2buf[8:end, :] = shifted_sum(ubuf, 8, 1)
    if isinstance(g, int):
        assert g == 0
        win = s2buf[POOL_TOP:end, :]
        window = POOL_WINDOWS[0]
    else:
        s4buf = sbufs[1]
        s4buf[16:end, :] = shifted_sum(s2buf, 16, 2)
        s8buf = s2buf
        s8buf[24:end, :] = shifted_sum(s4buf, 24, 4)
        win = jnp.where(g == 1, s4buf[POOL_TOP:end, :],
                        jnp.where(g == 2, s8buf[POOL_TOP:end, :],
                                  shifted_sum(s8buf, POOL_TOP, 8)))
        window = lax.shift_left(jnp.int32(2), g)
    t1 = seq_tile * tm + 1 + lax.broadcasted_iota(jnp.int32, (tm, 1), 0)
    inv_count = 1.0 / jnp.minimum(t1, window).astype(jnp.float32)
    mixed = win * inv_count - ubuf[POOL_TOP:end, :]
    ubuf[POOL_TOP:end, :] = mixed * ps_ref[...]
    y_ref[...] = (ubuf[POOL_TOP:end, :] * _silu(z)).astype(y_ref.dtype)


def _pool_first_kernel(x_ref, nw_ref, wu_ref, wz_ref, nu_ref, nz_ref, pw_ref, npw_ref, ps_ref,
                       y_ref, xb_ref, rs_ref, pu_ref, pz_ref,
                       wu_buf, wz_buf, ubuf, s2buf, ucarry, *, n_stage, tiles_per_seq):
    s = pl.program_id(0)

    @pl.when(s == 0)
    def _():
        ucarry[...] = jnp.zeros_like(ucarry)

    @pl.when(s < n_stage)
    def _():
        _stage_chunks(0, s, [(wu_ref, wu_buf), (wz_ref, wz_buf)],
                      fold=pw_ref[0].astype(jnp.bfloat16))
        _cast_rows([(nu_ref, pu_ref), (nz_ref, pz_ref)], fold=npw_ref[0].astype(jnp.bfloat16))

    @pl.when(s >= n_stage)
    def _():
        x = x_ref[...]
        xb = (x * nw_ref[...]).astype(jnp.bfloat16)
        rs = _row_rms_scale(x)
        xb_ref[...] = xb
        rs_ref[...] = rs
        _pool_tile(lambda: xb, rs, wu_buf[0], wz_buf[0], 0, (s - n_stage) % tiles_per_seq,
                   ps_ref, y_ref, ubuf, (s2buf,), ucarry)


def _pool_first(x2d, norm_row, w_in2d, pool_w, pool_scale):
    m = x2d.shape[0]
    tm = POOL_ROW_TILE
    n_tiles = m // tm
    n_stage = FIRST_STAGE_STEPS
    chunk_rows = D_MODEL // n_stage
    chunk = lambda s: jnp.minimum(s, n_stage - 1)
    tile = lambda s: jnp.maximum(s - n_stage, 0)
    w_chunk = lambda blk: pl.BlockSpec((chunk_rows, COL_BLOCK), lambda s: (chunk(s), blk))
    group_map = lambda g: pl.BlockSpec((1, GROUP_DIM, GROUP_DIM), lambda s: (g, 0, 0))
    side_out = pl.BlockSpec((chunk_rows, COL_BLOCK), lambda s: (chunk(s), 0))
    w_buf = pltpu.VMEM((1, D_MODEL, COL_BLOCK), jnp.bfloat16)
    buf = pltpu.VMEM((POOL_TOP + tm, COL_BLOCK), jnp.float32)
    outs = pl.pallas_call(
        functools.partial(_pool_first_kernel, n_stage=n_stage, tiles_per_seq=SEQ // tm),
        grid=(n_stage + n_tiles,),
        in_specs=[
            pl.BlockSpec((tm, D_MODEL), lambda s: (tile(s), 0)),
            pl.BlockSpec((1, D_MODEL), lambda s: (0, 0)),
            w_chunk(0), w_chunk(N_GROUPS),
            w_chunk(1), w_chunk(N_GROUPS + 1),
            group_map(0), group_map(1),
            pl.BlockSpec((1, COL_BLOCK), lambda s: (0, 0)),
        ],
        out_specs=[pl.BlockSpec((tm, COL_BLOCK), lambda s: (tile(s), 0)),
                   pl.BlockSpec((tm, D_MODEL), lambda s: (tile(s), 0)),
                   pl.BlockSpec((tm, 1), lambda s: (tile(s), 0)),
                   side_out, side_out],
        out_shape=[jax.ShapeDtypeStruct((m, COL_BLOCK), jnp.bfloat16),
                   jax.ShapeDtypeStruct((m, D_MODEL), jnp.bfloat16),
                   jax.ShapeDtypeStruct((m, 1), jnp.float32)]
        + [jax.ShapeDtypeStruct((D_MODEL, COL_BLOCK), jnp.bfloat16)] * 2,
        scratch_shapes=[w_buf, w_buf, buf, buf,
                        pltpu.VMEM((POOL_HALO, COL_BLOCK), jnp.float32)],
        compiler_params=_params(1),
        name="pool_first",
    )(x2d, norm_row, w_in2d, w_in2d, w_in2d, w_in2d, pool_w, pool_w,
      pool_scale.reshape(1, HALF))
    return outs[0], outs[1], outs[2], outs[3:]


def _pool_rest_kernel(xb_ref, rs_ref, wu_ref, wz_ref, pw_ref, ps_ref,
                      wu0_ref, wz0_ref, n0_ref, n1_ref, n2_ref, n3_ref,
                      y_ref, p0_ref, p1_ref, p2_ref, p3_ref,
                      wu_buf, wz_buf, ubuf, s2buf, s4buf, ucarry, sems, *, tiles_per_seq):
    jj = pl.program_id(0)
    i = pl.program_id(1)
    first_step = (jj == 0) & (i == 0)

    @pl.when(first_step)
    def _():
        ucarry[...] = jnp.zeros_like(ucarry)

    _load_prestaged(first_step, [(wu0_ref, wu_buf), (wz0_ref, wz_buf)], sems)

    _cast_rows([(n0_ref, p0_ref), (n1_ref, p1_ref), (n2_ref, p2_ref), (n3_ref, p3_ref)])
    slot = jj % 2
    _pool_tile(lambda: xb_ref[...], rs_ref[...], wu_buf[slot], wz_buf[slot], jj + 1,
               i % tiles_per_seq, ps_ref, y_ref, ubuf, (s2buf, s4buf), ucarry)
    _stage_chunks((jj + 1) % 2, i, [(wu_ref, wu_buf), (wz_ref, wz_buf)],
                  fold=pw_ref[0].astype(jnp.bfloat16))


def _pool_rest(xb, rs, w_in2d, pool_w, pool_scale, first_blocks, conv_blocks):
    m = xb.shape[0]
    tm = REST_ROW_TILE
    nb = N_GROUPS - 1
    n_tiles = m // tm
    chunk_rows = D_MODEL // n_tiles
    chunk, staged = _prestaged_maps(nb, n_tiles)
    w_chunk = lambda first: pl.BlockSpec((chunk_rows, COL_BLOCK),
                                         lambda jj, i: (chunk(jj, i), first + 1 + staged(jj)))
    side_steps = 2 * n_tiles
    side_rows = D_MODEL // side_steps
    side_step = lambda jj, i: jnp.minimum(jj * n_tiles + i, side_steps - 1)
    side_in = lambda blk: pl.BlockSpec((side_rows, COL_BLOCK),
                                       lambda jj, i: (side_step(jj, i), blk))
    side_out = pl.BlockSpec((side_rows, COL_BLOCK), lambda jj, i: (side_step(jj, i), 0))
    n_side = len(conv_blocks)
    hbm = pl.BlockSpec(memory_space=pl.ANY)
    w_buf = pltpu.VMEM((2, D_MODEL, COL_BLOCK), jnp.bfloat16)
    buf = pltpu.VMEM((POOL_TOP + tm, COL_BLOCK), jnp.float32)
    outs = pl.pallas_call(
        functools.partial(_pool_rest_kernel, tiles_per_seq=SEQ // tm),
        grid=(nb, n_tiles),
        in_specs=[
            pl.BlockSpec((tm, D_MODEL), lambda jj, i: (i, 0)),
            pl.BlockSpec((tm, 1), lambda jj, i: (i, 0)),
            w_chunk(0), w_chunk(N_GROUPS),
            pl.BlockSpec((1, GROUP_DIM, GROUP_DIM), lambda jj, i: (1 + staged(jj), 0, 0),
                         pipeline_mode=pl.Buffered(1)),
            pl.BlockSpec((1, COL_BLOCK), lambda jj, i: (0, jj + 1)),
            hbm, hbm,
        ] + [side_in(blk) for blk in conv_blocks],
        out_specs=[pl.BlockSpec((tm, COL_BLOCK), lambda jj, i: (i, jj))] + [side_out] * n_side,
        out_shape=[jax.ShapeDtypeStruct((m, nb * COL_BLOCK), jnp.bfloat16)]
        + [jax.ShapeDtypeStruct((D_MODEL, COL_BLOCK), jnp.bfloat16)] * n_side,
        scratch_shapes=[w_buf, w_buf, buf, buf, buf,
                        pltpu.VMEM((POOL_HALO, COL_BLOCK), jnp.float32),
                        pltpu.SemaphoreType.DMA((2,))],
        compiler_params=_params(2),
        name="pool_rest",
    )(xb, rs, w_in2d, w_in2d, pool_w, pool_scale.reshape(1, HALF),
      *first_blocks, *([w_in2d] * n_side))
    return outs[0], outs[1:]


def _conv_kernel(xb_ref, rs_ref, wu_ref, wb_ref, wc_ref, wz_ref, cw_ref, cb_ref,
                 wu0_ref, wb0_ref, wc0_ref, wz0_ref, n0_ref, n1_ref, n2_ref,
                 y_ref, p0_ref, p1_ref, p2_ref,
                 wu_buf, wb_buf, wc_buf, wz_buf, vbuf, vcarry, sems, *, tiles_per_seq):
    jj = pl.program_id(0)
    i = pl.program_id(1)
    tm = xb_ref.shape[0]
    first_step = (jj == 0) & (i == 0)

    @pl.when(first_step)
    def _():
        vcarry[...] = jnp.zeros_like(vcarry)

    _load_prestaged(first_step, [(wu0_ref, wu_buf), (wb0_ref, wb_buf),
                                 (wc0_ref, wc_buf), (wz0_ref, wz_buf)], sems)

    def stage():
        _stage_chunks((jj + 1) % 2, i, [(wu_ref, wu_buf), (wb_ref, wb_buf),
                                        (wc_ref, wc_buf), (wz_ref, wz_buf)])

    def compute():
        slot = jj % 2
        seq_tile = i % tiles_per_seq
        rs = rs_ref[...]
        rs2 = rs * rs
        u = _dot(xb_ref[...], wu_buf[slot])
        _cast_rows([(n0_ref, p0_ref), (n1_ref, p1_ref)])
        c_gate = _dot(xb_ref[...], wc_buf[slot])
        vbuf[0:CONV_HALO, :] = jnp.where(seq_tile != 0, vcarry[...], 0.0)
        vbuf[CONV_HALO:CONV_HALO + tm, :] = rs2 * (c_gate * u)
        vcarry[...] = vbuf[tm:CONV_HALO + tm, :]
        cw = cw_ref[...]
        y = (cb_ref[...]
             + cw[0:1, :] * vbuf[CONV_HALO - 2:CONV_HALO - 2 + tm, :]
             + cw[1:2, :] * vbuf[CONV_HALO - 1:CONV_HALO - 1 + tm, :]
             + cw[2:3, :] * vbuf[CONV_HALO:CONV_HALO + tm, :])
        gated = (rs * y) * _silu_exp(rs * _dot(xb_ref[...], wz_buf[slot]))
        _cast_rows([(n2_ref, p2_ref)])
        y_ref[...] = (_dot(xb_ref[...], wb_buf[slot]) * gated).astype(y_ref.dtype)

    compute()
    stage()


def _conv_mixer(xb, rs, w_in2d, conv_w, conv_b, first_blocks, w_branch):
    m = xb.shape[0]
    tm = ROW_TILE
    nb = N_HALF_BLOCKS
    n_tiles = m // tm
    chunk_rows = D_MODEL // n_tiles
    chunk, staged = _prestaged_maps(nb, n_tiles)
    w_chunk = lambda first: pl.BlockSpec((chunk_rows, COL_BLOCK),
                                         lambda jj, i: (chunk(jj, i), first + staged(jj)))
    hbm = pl.BlockSpec(memory_space=pl.ANY)
    n_steps = nb * n_tiles
    step = lambda jj, i: jj * n_tiles + i
    gate_rows = D_MODEL // n_steps
    branch_rows = HALF // n_steps
    gate_in = lambda blk: pl.BlockSpec((gate_rows, COL_BLOCK), lambda jj, i: (step(jj, i), blk))
    n_br = w_branch.shape[0]
    branches = pl.BlockSpec((n_br, branch_rows, COL_BLOCK), lambda jj, i: (0, step(jj, i), 0))
    gate_out = pl.BlockSpec((gate_rows, COL_BLOCK), lambda jj, i: (step(jj, i), 0))
    g0_first = 6 * N_HALF_BLOCKS
    g1_first = g0_first + N_MODEL_BLOCKS
    w_buf = pltpu.VMEM((2, D_MODEL, COL_BLOCK), jnp.bfloat16)
    outs = pl.pallas_call(
        functools.partial(_conv_kernel, tiles_per_seq=SEQ // tm),
        grid=(nb, n_tiles),
        in_specs=[
            pl.BlockSpec((tm, D_MODEL), lambda jj, i: (i, 0)),
            pl.BlockSpec((tm, 1), lambda jj, i: (i, 0)),
            w_chunk(2 * nb), w_chunk(3 * nb), w_chunk(4 * nb), w_chunk(5 * nb),
            pl.BlockSpec((CONV_K, COL_BLOCK), lambda jj, i: (0, jj)),
            pl.BlockSpec((1, COL_BLOCK), lambda jj, i: (0, jj)),
            hbm, hbm, hbm, hbm,
            gate_in(g0_first), gate_in(g1_first), branches,
        ],
        out_specs=[pl.BlockSpec((tm, COL_BLOCK), lambda jj, i: (i, jj)),
                   gate_out, gate_out, branches],
        out_shape=[jax.ShapeDtypeStruct((m, HALF), jnp.bfloat16),
                   jax.ShapeDtypeStruct((D_MODEL, COL_BLOCK), jnp.bfloat16),
                   jax.ShapeDtypeStruct((D_MODEL, COL_BLOCK), jnp.bfloat16),
                   jax.ShapeDtypeStruct((n_br, HALF, COL_BLOCK), jnp.bfloat16)],
        scratch_shapes=[w_buf, w_buf, w_buf, w_buf,
                        pltpu.VMEM((CONV_HALO + tm, COL_BLOCK), jnp.float32),
                        pltpu.VMEM((CONV_HALO, COL_BLOCK), jnp.float32),
                        pltpu.SemaphoreType.DMA((4,))],
        compiler_params=_params(2),
        name="conv_mixer",
    )(xb, rs, w_in2d, w_in2d, w_in2d, w_in2d, conv_w,
      conv_b.reshape(1, HALF), *first_blocks, w_in2d, w_in2d, w_branch)
    return outs[0], outs[1:]


def _merge_kernel(xb_ref, rs_ref, yp0_ref, ypr_ref, yc_ref, wg0_ref, wg1_ref, gb_ref,
                  wb_ref, wo_ref, wg00_ref, wg10_ref, wb0_first_ref,
                  o_ref, wo_bf_ref, wg0_buf, wg1_buf, wb0_buf, wb1_buf, sems):
    jj = pl.program_id(0)
    i = pl.program_id(1)

    _load_prestaged((jj == 0) & (i == 0),
                    [(wg00_ref, wg0_buf), (wg10_ref, wg1_buf),
                     (wb0_first_ref.at[0], wb0_buf), (wb0_first_ref.at[1], wb1_buf)], sems)

    def stage():
        _stage_chunks((jj + 1) % 2, i, [(wg0_ref, wg0_buf), (wg1_ref, wg1_buf),
                                        ((wb_ref, 0), wb0_buf), ((wb_ref, 1), wb1_buf)])

    def compute():
        slot = jj % 2
        h = xb_ref[...]
        rs = rs_ref[...]
        g0 = _sigmoid(rs * _dot(h, wg0_buf[slot]) + gb_ref[0:1, :])
        y_pool = jnp.concatenate([yp0_ref[...], ypr_ref[...]], axis=1)
        br0 = _dot(y_pool, wb0_buf[slot])
        g1 = _sigmoid(rs * _dot(h, wg1_buf[slot]) + gb_ref[1:2, :])
        br1 = _dot(yc_ref[...], wb1_buf[slot])
        o_ref[...] = (g0 * br0 + g1 * br1).astype(o_ref.dtype)
        wo_bf_ref[...] = wo_ref[...].astype(wo_bf_ref.dtype)

    compute()
    stage()


def _merge(xb, rs, y_pool0, y_pool_rest, y_conv, w_in2d, gate_b, w_branch, w_out2d,
           first_blocks):
    m = xb.shape[0]
    tm = ROW_TILE
    nb = N_MODEL_BLOCKS
    n_tiles = m // tm
    chunk, staged = _prestaged_maps(nb, n_tiles)
    g0_first = 6 * N_HALF_BLOCKS
    g1_first = g0_first + nb
    gate_chunk = lambda first: pl.BlockSpec(
        (D_MODEL // n_tiles, COL_BLOCK), lambda jj, i: (chunk(jj, i), first + staged(jj)))
    branch_chunks = pl.BlockSpec((w_branch.shape[0], HALF // n_tiles, COL_BLOCK),
                                 lambda jj, i: (0, chunk(jj, i), staged(jj)))
    tile = lambda width: pl.BlockSpec((tm, width), lambda jj, i: (i, 0))
    bias = pl.BlockSpec((2, COL_BLOCK), lambda jj, i: (0, jj))
    wo_rows = D_MODEL // (nb * n_tiles)
    wo_slice = pl.BlockSpec((wo_rows, D_MODEL), lambda jj, i: (jj * n_tiles + i, 0))
    hbm = pl.BlockSpec(memory_space=pl.ANY)
    return pl.pallas_call(
        _merge_kernel,
        grid=(nb, n_tiles),
        in_specs=[tile(D_MODEL), tile(1),
                  tile(COL_BLOCK), tile(HALF - COL_BLOCK), tile(HALF),
                  gate_chunk(g0_first), gate_chunk(g1_first), bias,
                  branch_chunks, wo_slice, hbm, hbm, hbm],
        out_specs=[pl.BlockSpec((tm, COL_BLOCK), lambda jj, i: (i, jj)), wo_slice],
        out_shape=[jax.ShapeDtypeStruct((m, D_MODEL), jnp.bfloat16),
                   jax.ShapeDtypeStruct((D_MODEL, D_MODEL), jnp.bfloat16)],
        scratch_shapes=[pltpu.VMEM((2, D_MODEL, COL_BLOCK), jnp.bfloat16),
                        pltpu.VMEM((2, D_MODEL, COL_BLOCK), jnp.bfloat16),
                        pltpu.VMEM((2, HALF, COL_BLOCK), jnp.bfloat16),
                        pltpu.VMEM((2, HALF, COL_BLOCK), jnp.bfloat16),
                        pltpu.SemaphoreType.DMA((4,))],
        compiler_params=_params(2),
        name="branch_merge",
    )(xb, rs, y_pool0, y_pool_rest, y_conv, w_in2d, w_in2d, gate_b,
      w_branch, w_out2d, *first_blocks)


def _out_kernel(mg_ref, w_ref, x_ref, fw_ref, o_ref, rows, sumsq):
    i = pl.program_id(0)
    n = pl.program_id(1)
    n_tiles = pl.num_programs(0) - 1

    def emit_previous():
        scale = lax.rsqrt(sumsq[(i + 1) % 2] * (1.0 / D_MODEL) + NORM_EPS)
        o_ref[...] = rows[n] * scale * fw_ref[...]

    def project():
        xn = x_ref[...] + _dot(mg_ref[...], w_ref[...])
        rows[n] = xn
        part = jnp.sum(xn * xn, axis=-1, keepdims=True)
        cur = i % 2
        sumsq[cur] = jnp.where(n == 0, part, sumsq[cur] + part)

    @pl.when(i == 0)
    def _():
        project()

    @pl.when((i > 0) & (i < n_tiles))
    def _():
        emit_previous()
        project()

    @pl.when(i == n_tiles)
    def _():
        emit_previous()


def _out_proj(merged, w_out_bf, x2d, final_w):
    m = x2d.shape[0]
    tm = OUT_ROW_TILE
    tn = OUT_COL_BLOCK
    n_tiles = m // tm
    n_blocks = D_MODEL // tn
    last = n_tiles - 1
    in_row = lambda i: jnp.minimum(i, last)
    in_col = lambda i, n: jnp.where(i < n_tiles, n, n_blocks - 1)
    return pl.pallas_call(
        _out_kernel,
        grid=(n_tiles + 1, n_blocks),
        in_specs=[
            pl.BlockSpec((tm, D_MODEL), lambda i, n: (in_row(i), 0)),
            pl.BlockSpec((D_MODEL, tn), lambda i, n: (0, in_col(i, n))),
            pl.BlockSpec((tm, tn), lambda i, n: (in_row(i), in_col(i, n))),
            pl.BlockSpec((1, tn), lambda i, n: (0, n)),
        ],
        out_specs=pl.BlockSpec(
            (tm, tn), lambda i, n: (jnp.maximum(i - 1, 0), jnp.where(i > 0, n, 0))),
        out_shape=jax.ShapeDtypeStruct((m, D_MODEL), jnp.float32),
        scratch_shapes=[pltpu.VMEM((n_blocks, tm, tn), jnp.float32),
                        pltpu.VMEM((2, tm, 1), jnp.float32)],
        compiler_params=_params(2),
        name="out_proj_norm",
    )(merged, w_out_bf, x2d, final_w.reshape(1, D_MODEL))


def kernel(x, norm_w, w_in, pool_w, pool_scale, conv_w, conv_b, gate_b, w_branch, w_out,
           final_norm_w):
    b, s, d = x.shape
    assert (s, d) == (SEQ, D_MODEL) and norm_w.shape[0] == 1
    x2d = x.reshape(b * s, d)
    w_in2d = w_in[0]

    nb = N_HALF_BLOCKS
    conv_first_cols = (2 * nb, 3 * nb, 4 * nb, 5 * nb)
    y_pool0, xb, rs, pool_first = _pool_first(x2d, norm_w, w_in2d, pool_w[0], pool_scale[0])
    y_pool_rest, conv_first = _pool_rest(xb, rs, w_in2d, pool_w[0], pool_scale[0],
                                         pool_first, conv_first_cols)
    y_conv, merge_first = _conv_mixer(xb, rs, w_in2d, conv_w[0], conv_b[0],
                                      conv_first, w_branch[0])
    merged, w_out_bf = _merge(xb, rs, y_pool0, y_pool_rest, y_conv, w_in2d, gate_b[0],
                              w_branch[0], w_out[0], merge_first)
    out = _out_proj(merged, w_out_bf, x2d, final_norm_w)
    return out.reshape(b, s, d)
```

```python
import functools

import jax
import jax.numpy as jnp
from jax import lax
from jax.experimental import pallas as pl
from jax.experimental.pallas import tpu as pltpu

D_MODEL = 4096
SEQ = 4096
HALF = D_MODEL // 2
POOL_WINDOWS = (2, 4, 8, 16)
N_GROUPS = len(POOL_WINDOWS)
GROUP_DIM = HALF // N_GROUPS
CONV_K = 3
NORM_EPS = 1e-6

COL_BLOCK = GROUP_DIM
N_HALF_BLOCKS = HALF // COL_BLOCK
N_MODEL_BLOCKS = D_MODEL // COL_BLOCK
OUT_COL_BLOCK = 512
OUT_ROW_TILE = 1024
ROW_TILE = 512
POOL_ROW_TILE = 512
FIRST_STAGE_STEPS = 8
REST_ROW_TILE = 1024
POOL_HALO = 16
POOL_PAD = 16
POOL_TOP = POOL_PAD + POOL_HALO
CONV_HALO = 8

VMEM_LIMIT_BYTES = 62 * 1024 * 1024


def _dot(a, b):
    return jnp.dot(a, b, preferred_element_type=jnp.float32)


def _sigmoid(z):
    return 0.5 * jnp.tanh(0.5 * z) + 0.5


def _silu(z):
    return z * _sigmoid(z)


def _silu_exp(z):
    return z / (1.0 + jnp.exp(-z))


def _params(n_axes):
    return pltpu.CompilerParams(dimension_semantics=("arbitrary",) * n_axes,
                                vmem_limit_bytes=VMEM_LIMIT_BYTES)


def _stage_chunks(slot, i, pairs, fold=None):
    for k, (chunk_ref, buf) in enumerate(pairs):
        if isinstance(chunk_ref, tuple):
            chunk_ref, which = chunk_ref
            chunk = chunk_ref[which]
        else:
            chunk = chunk_ref[...]
        rows, cols = chunk.shape
        r0 = pl.multiple_of(i * rows, rows)
        chunk = chunk.astype(buf.dtype)
        if k == 0 and fold is not None:
            chunk = _dot(chunk, fold).astype(buf.dtype)
        buf[slot, pl.ds(r0, rows), :] = chunk


def _prestaged_maps(n_blocks, n_tiles):
    def chunk(jj, i):
        return jnp.where(jj + 1 < n_blocks, i, n_tiles - 1)

    def staged(jj):
        return jnp.minimum(jj + 1, n_blocks - 1)

    return chunk, staged


def _load_prestaged(first_step, pairs, sems):
    @pl.when(first_step)
    def _():
        copies = [pltpu.make_async_copy(src, buf.at[0], sems.at[k])
                  for k, (src, buf) in enumerate(pairs)]
        for cp in copies:
            cp.start()
        for cp in copies:
            cp.wait()


def _cast_rows(pairs, fold=None):
    for k, (src_ref, dst_ref) in enumerate(pairs):
        rows = src_ref[...].astype(dst_ref.dtype)
        if k == 0 and fold is not None:
            rows = _dot(rows, fold).astype(dst_ref.dtype)
        dst_ref[...] = rows


def _row_rms_scale(x):
    return lax.rsqrt(jnp.mean(x * x, axis=-1, keepdims=True) + NORM_EPS)


def _pool_tile(load_xb, rs, wu, wz, g, seq_tile, ps_ref, y_ref, ubuf, sbufs, ucarry):
    tm = rs.shape[0]
    end = POOL_TOP + tm
    u = rs * _dot(load_xb(), wu)

    ubuf[0:POOL_PAD, :] = jnp.zeros((POOL_PAD, COL_BLOCK), jnp.float32)
    ubuf[POOL_PAD:POOL_TOP, :] = jnp.where(seq_tile != 0, ucarry[...], 0.0)
    ubuf[POOL_TOP:end, :] = u
    ucarry[...] = ubuf[end - POOL_HALO:end, :]
    z = rs * _dot(load_xb(), wz)

    def shifted_sum(src, lo, shift):
        return src[lo:end, :] + src[lo - shift:end - shift, :]

    s2buf = sbufs[0]
    s2buf[8:end, :] = shifted_sum(ubuf, 8, 1)
    if isinstance(g, int):
        assert g == 0
        win = s2buf[POOL_TOP:end, :]
        window = POOL_WINDOWS[0]
    else:
        s4buf = sbufs[1]
        s4buf[16:end, :] = shifted_sum(s2buf, 16, 2)
        s8buf = s2buf
        s8buf[24:end, :] = shifted_sum(s4buf, 24, 4)
        win = jnp.where(g == 1, s4buf[POOL_TOP:end, :],
                        jnp.where(g == 2, s8buf[POOL_TOP:end, :],
                                  shifted_sum(s8buf, POOL_TOP, 8)))
        window = lax.shift_left(jnp.int32(2), g)
    t1 = seq_tile * tm + 1 + lax.broadcasted_iota(jnp.int32, (tm, 1), 0)
    inv_count = 1.0 / jnp.minimum(t1, window).astype(jnp.float32)
    mixed = win * inv_count - ubuf[POOL_TOP:end, :]
    ubuf[POOL_TOP:end, :] = mixed * ps_ref[...]
    y_ref[...] = (ubuf[POOL_TOP:end, :] * _silu(z)).astype(y_ref.dtype)


def _pool_first_kernel(x_ref, nw_ref, wu_ref, wz_ref, nu_ref, nz_ref, pw_ref, npw_ref, ps_ref,
                       y_ref, xb_ref, rs_ref, pu_ref, pz_ref,
                       wu_buf, wz_buf, ubuf, s2buf, ucarry, *, n_stage, tiles_per_seq):
    s = pl.program_id(0)

    @pl.when(s == 0)
    def _():
        ucarry[...] = jnp.zeros_like(ucarry)

    @pl.when(s < n_stage)
    def _():
        _stage_chunks(0, s, [(wu_ref, wu_buf), (wz_ref, wz_buf)],
                      fold=pw_ref[0].astype(jnp.bfloat16))
        _cast_rows([(nu_ref, pu_ref), (nz_ref, pz_ref)], fold=npw_ref[0].astype(jnp.bfloat16))

    @pl.when(s >= n_stage)
    def _():
        x = x_ref[...]
        xb = (x * nw_ref[...]).astype(jnp.bfloat16)
        rs = _row_rms_scale(x)
        xb_ref[...] = xb
        rs_ref[...] = rs
        _pool_tile(lambda: xb, rs, wu_buf[0], wz_buf[0], 0, (s - n_stage) % tiles_per_seq,
                   ps_ref, y_ref, ubuf, (s2buf,), ucarry)


def _pool_first(x2d, norm_row, w_in2d, pool_w, pool_scale):
    m = x2d.shape[0]
    tm = POOL_ROW_TILE
    n_tiles = m // tm
    n_stage = FIRST_STAGE_STEPS
    chunk_rows = D_MODEL // n_stage
    chunk = lambda s: jnp.minimum(s, n_stage - 1)
    tile = lambda s: jnp.maximum(s - n_stage, 0)
    w_chunk = lambda blk: pl.BlockSpec((chunk_rows, COL_BLOCK), lambda s: (chunk(s), blk))
    group_map = lambda g: pl.BlockSpec((1, GROUP_DIM, GROUP_DIM), lambda s: (g, 0, 0))
    side_out = pl.BlockSpec((chunk_rows, COL_BLOCK), lambda s: (chunk(s), 0))
    w_buf = pltpu.VMEM((1, D_MODEL, COL_BLOCK), jnp.bfloat16)
    buf = pltpu.VMEM((POOL_TOP + tm, COL_BLOCK), jnp.float32)
    outs = pl.pallas_call(
        functools.partial(_pool_first_kernel, n_stage=n_stage, tiles_per_seq=SEQ // tm),
        grid=(n_stage + n_tiles,),
        in_specs=[
            pl.BlockSpec((tm, D_MODEL), lambda s: (tile(s), 0)),
            pl.BlockSpec((1, D_MODEL), lambda s: (0, 0)),
            w_chunk(0), w_chunk(N_GROUPS),
            w_chunk(1), w_chunk(N_GROUPS + 1),
            group_map(0), group_map(1),
            pl.BlockSpec((1, COL_BLOCK), lambda s: (0, 0)),
        ],
        out_specs=[pl.BlockSpec((tm, COL_BLOCK), lambda s: (tile(s), 0)),
                   pl.BlockSpec((tm, D_MODEL), lambda s: (tile(s), 0)),
                   pl.BlockSpec((tm, 1), lambda s: (tile(s), 0)),
                   side_out, side_out],
        out_shape=[jax.ShapeDtypeStruct((m, COL_BLOCK), jnp.bfloat16),
                   jax.ShapeDtypeStruct((m, D_MODEL), jnp.bfloat16),
                   jax.ShapeDtypeStruct((m, 1), jnp.float32)]
        + [jax.ShapeDtypeStruct((D_MODEL, COL_BLOCK), jnp.bfloat16)] * 2,
        scratch_shapes=[w_buf, w_buf, buf, buf,
                        pltpu.VMEM((POOL_HALO, COL_BLOCK), jnp.float32)],
        compiler_params=_params(1),
        name="pool_first",
    )(x2d, norm_row, w_in2d, w_in2d, w_in2d, w_in2d, pool_w, pool_w,
      pool_scale.reshape(1, HALF))
    return outs[0], outs[1], outs[2], outs[3:]


def _pool_rest_kernel(xb_ref, rs_ref, wu_ref, wz_ref, pw_ref, ps_ref,
                      wu0_ref, wz0_ref, n0_ref, n1_ref, n2_ref, n3_ref,
                      y_ref, p0_ref, p1_ref, p2_ref, p3_ref,
                      wu_buf, wz_buf, ubuf, s2buf, s4buf, ucarry, sems, *, tiles_per_seq):
    jj = pl.program_id(0)
    i = pl.program_id(1)
    first_step = (jj == 0) & (i == 0)

    @pl.when(first_step)
    def _():
        ucarry[...] = jnp.zeros_like(ucarry)

    _load_prestaged(first_step, [(wu0_ref, wu_buf), (wz0_ref, wz_buf)], sems)

    _cast_rows([(n0_ref, p0_ref), (n1_ref, p1_ref), (n2_ref, p2_ref), (n3_ref, p3_ref)])
    slot = jj % 2
    _pool_tile(lambda: xb_ref[...], rs_ref[...], wu_buf[slot], wz_buf[slot], jj + 1,
               i % tiles_per_seq, ps_ref, y_ref, ubuf, (s2buf, s4buf), ucarry)
    _stage_chunks((jj + 1) % 2, i, [(wu_ref, wu_buf), (wz_ref, wz_buf)],
                  fold=pw_ref[0].astype(jnp.bfloat16))


def _pool_rest(xb, rs, w_in2d, pool_w, pool_scale, first_blocks, conv_blocks):
    m = xb.shape[0]
    tm = REST_ROW_TILE
    nb = N_GROUPS - 1
    n_tiles = m // tm
    chunk_rows = D_MODEL // n_tiles
    chunk, staged = _prestaged_maps(nb, n_tiles)
    w_chunk = lambda first: pl.BlockSpec((chunk_rows, COL_BLOCK),
                                         lambda jj, i: (chunk(jj, i), first + 1 + staged(jj)))
    side_steps = 2 * n_tiles
    side_rows = D_MODEL // side_steps
    side_step = lambda jj, i: jnp.minimum(jj * n_tiles + i, side_steps - 1)
    side_in = lambda blk: pl.BlockSpec((side_rows, COL_BLOCK),
                                       lambda jj, i: (side_step(jj, i), blk))
    side_out = pl.BlockSpec((side_rows, COL_BLOCK), lambda jj, i: (side_step(jj, i), 0))
    n_side = len(conv_blocks)
    hbm = pl.BlockSpec(memory_space=pl.ANY)
    w_buf = pltpu.VMEM((2, D_MODEL, COL_BLOCK), jnp.bfloat16)
    buf = pltpu.VMEM((POOL_TOP + tm, COL_BLOCK), jnp.float32)
    outs = pl.pallas_call(
        functools.partial(_pool_rest_kernel, tiles_per_seq=SEQ // tm),
        grid=(nb, n_tiles),
        in_specs=[
            pl.BlockSpec((tm, D_MODEL), lambda jj, i: (i, 0)),
            pl.BlockSpec((tm, 1), lambda jj, i: (i, 0)),
            w_chunk(0), w_chunk(N_GROUPS),
            pl.BlockSpec((1, GROUP_DIM, GROUP_DIM), lambda jj, i: (1 + staged(jj), 0, 0),
                         pipeline_mode=pl.Buffered(1)),
            pl.BlockSpec((1, COL_BLOCK), lambda jj, i: (0, jj + 1)),
            hbm, hbm,
        ] + [side_in(blk) for blk in conv_blocks],
        out_specs=[pl.BlockSpec((tm, COL_BLOCK), lambda jj, i: (i, jj))] + [side_out] * n_side,
        out_shape=[jax.ShapeDtypeStruct((m, nb * COL_BLOCK), jnp.bfloat16)]
        + [jax.ShapeDtypeStruct((D_MODEL, COL_BLOCK), jnp.bfloat16)] * n_side,
        scratch_shapes=[w_buf, w_buf, buf, buf, buf,
                        pltpu.VMEM((POOL_HALO, COL_BLOCK), jnp.float32),
                        pltpu.SemaphoreType.DMA((2,))],
        compiler_params=_params(2),
        name="pool_rest",
    )(xb, rs, w_in2d, w_in2d, pool_w, pool_scale.reshape(1, HALF),
      *first_blocks, *([w_in2d] * n_side))
    return outs[0], outs[1:]


def _conv_kernel(xb_ref, rs_ref, wu_ref, wb_ref, wc_ref, wz_ref, cw_ref, cb_ref,
                 wu0_ref, wb0_ref, wc0_ref, wz0_ref, n0_ref, n1_ref, n2_ref,
                 y_ref, p0_ref, p1_ref, p2_ref,
                 wu_buf, wb_buf, wc_buf, wz_buf, vbuf, vcarry, sems, *, tiles_per_seq):
    jj = pl.program_id(0)
    i = pl.program_id(1)
    tm = xb_ref.shape[0]
    first_step = (jj == 0) & (i == 0)

    @pl.when(first_step)
    def _():
        vcarry[...] = jnp.zeros_like(vcarry)

    _load_prestaged(first_step, [(wu0_ref, wu_buf), (wb0_ref, wb_buf),
                                 (wc0_ref, wc_buf), (wz0_ref, wz_buf)], sems)

    def stage():
        _stage_chunks((jj + 1) % 2, i, [(wu_ref, wu_buf), (wb_ref, wb_buf),
                                        (wc_ref, wc_buf), (wz_ref, wz_buf)])

    def compute():
        slot = jj % 2
        seq_tile = i % tiles_per_seq
        rs = rs_ref[...]
        rs2 = rs * rs
        u = _dot(xb_ref[...], wu_buf[slot])
        _cast_rows([(n0_ref, p0_ref), (n1_ref, p1_ref)])
        c_gate = _dot(xb_ref[...], wc_buf[slot])
        vbuf[0:CONV_HALO, :] = jnp.where(seq_tile != 0, vcarry[...], 0.0)
        vbuf[CONV_HALO:CONV_HALO + tm, :] = rs2 * (c_gate * u)
        vcarry[...] = vbuf[tm:CONV_HALO + tm, :]
        cw = cw_ref[...]
        y = (cb_ref[...]
             + cw[0:1, :] * vbuf[CONV_HALO - 2:CONV_HALO - 2 + tm, :]
             + cw[1:2, :] * vbuf[CONV_HALO - 1:CONV_HALO - 1 + tm, :]
             + cw[2:3, :] * vbuf[CONV_HALO:CONV_HALO + tm, :])
        gated = (rs * y) * _silu_exp(rs * _dot(xb_ref[...], wz_buf[slot]))
        _cast_rows([(n2_ref, p2_ref)])
        y_ref[...] = (_dot(xb_ref[...], wb_buf[slot]) * gated).astype(y_ref.dtype)

    compute()
    stage()


def _conv_mixer(xb, rs, w_in2d, conv_w, conv_b, first_blocks, w_branch):
    m = xb.shape[0]
    tm = ROW_TILE
    nb = N_HALF_BLOCKS
    n_tiles = m // tm
    chunk_rows = D_MODEL // n_tiles
    chunk, staged = _prestaged_maps(nb, n_tiles)
    w_chunk = lambda first: pl.BlockSpec((chunk_rows, COL_BLOCK),
                                         lambda jj, i: (chunk(jj, i), first + staged(jj)))
    hbm = pl.BlockSpec(memory_space=pl.ANY)
    n_steps = nb * n_tiles
    step = lambda jj, i: jj * n_tiles + i
    gate_rows = D_MODEL // n_steps
    branch_rows = HALF // n_steps
    gate_in = lambda blk: pl.BlockSpec((gate_rows, COL_BLOCK), lambda jj, i: (step(jj, i), blk))
    n_br = w_branch.shape[0]
    branches = pl.BlockSpec((n_br, branch_rows, COL_BLOCK), lambda jj, i: (0, step(jj, i), 0))
    gate_out = pl.BlockSpec((gate_rows, COL_BLOCK), lambda jj, i: (step(jj, i), 0))
    g0_first = 6 * N_HALF_BLOCKS
    g1_first = g0_first + N_MODEL_BLOCKS
    w_buf = pltpu.VMEM((2, D_MODEL, COL_BLOCK), jnp.bfloat16)
    outs = pl.pallas_call(
        functools.partial(_conv_kernel, tiles_per_seq=SEQ // tm),
        grid=(nb, n_tiles),
        in_specs=[
            pl.BlockSpec((tm, D_MODEL), lambda jj, i: (i, 0)),
            pl.BlockSpec((tm, 1), lambda jj, i: (i, 0)),
            w_chunk(2 * nb), w_chunk(3 * nb), w_chunk(4 * nb), w_chunk(5 * nb),
            pl.BlockSpec((CONV_K, COL_BLOCK), lambda jj, i: (0, jj)),
            pl.BlockSpec((1, COL_BLOCK), lambda jj, i: (0, jj)),
            hbm, hbm, hbm, hbm,
            gate_in(g0_first), gate_in(g1_first), branches,
        ],
        out_specs=[pl.BlockSpec((tm, COL_BLOCK), lambda jj, i: (i, jj)),
                   gate_out, gate_out, branches],
        out_shape=[jax.ShapeDtypeStruct((m, HALF), jnp.bfloat16),
                   jax.ShapeDtypeStruct((D_MODEL, COL_BLOCK), jnp.bfloat16),
                   jax.ShapeDtypeStruct((D_MODEL, COL_BLOCK), jnp.bfloat16),
                   jax.ShapeDtypeStruct((n_br, HALF, COL_BLOCK), jnp.bfloat16)],
        scratch_shapes=[w_buf, w_buf, w_buf, w_buf,
                        pltpu.VMEM((CONV_HALO + tm, COL_BLOCK), jnp.float32),
                        pltpu.VMEM((CONV_HALO, COL_BLOCK), jnp.float32),
                        pltpu.SemaphoreType.DMA((4,))],
        compiler_params=_params(2),
        name="conv_mixer",
    )(xb, rs, w_in2d, w_in2d, w_in2d, w_in2d, conv_w,
      conv_b.reshape(1, HALF), *first_blocks, w_in2d, w_in2d, w_branch)
    return outs[0], outs[1:]


def _merge_kernel(xb_ref, rs_ref, yp0_ref, ypr_ref, yc_ref, wg0_ref, wg1_ref, gb_ref,
                  wb0_ref, wb1_ref, wo_ref, wg00_ref, wg10_ref, wb0_first_ref,
                  o_ref, wo_bf_ref, wg0_buf, wg1_buf, wb0_buf, wb1_buf, sems):
    jj = pl.program_id(0)
    i = pl.program_id(1)

    _load_prestaged((jj == 0) & (i == 0),
                    [(wg00_ref, wg0_buf), (wg10_ref, wg1_buf),
                     (wb0_first_ref.at[0], wb0_buf), (wb0_first_ref.at[1], wb1_buf)], sems)

    def stage():
        _stage_chunks((jj + 1) % 2, i, [(wg0_ref, wg0_buf), (wg1_ref, wg1_buf),
                                        ((wb0_ref, 0), wb0_buf), ((wb1_ref, 0), wb1_buf)])

    def compute():
        slot = jj % 2
        h = xb_ref[...]
        rs = rs_ref[...]
        g0 = _sigmoid(rs * _dot(h, wg0_buf[slot]) + gb_ref[0:1, :])
        y_pool = jnp.concatenate([yp0_ref[...], ypr_ref[...]], axis=1)
        br0 = _dot(y_pool, wb0_buf[slot])
        g1 = _sigmoid(rs * _dot(h, wg1_buf[slot]) + gb_ref[1:2, :])
        br1 = _dot(yc_ref[...], wb1_buf[slot])
        o_ref[...] = (g0 * br0 + g1 * br1).astype(o_ref.dtype)
        wo_bf_ref[...] = wo_ref[...].astype(wo_bf_ref.dtype)

    compute()
    stage()


def _merge(xb, rs, y_pool0, y_pool_rest, y_conv, w_in2d, gate_b, w_branch, w_out2d,
           first_blocks):
    m = xb.shape[0]
    tm = ROW_TILE
    nb = N_MODEL_BLOCKS
    n_tiles = m // tm
    chunk, staged = _prestaged_maps(nb, n_tiles)
    g0_first = 6 * N_HALF_BLOCKS
    g1_first = g0_first + nb
    gate_chunk = lambda first: pl.BlockSpec(
        (D_MODEL // n_tiles, COL_BLOCK), lambda jj, i: (chunk(jj, i), first + staged(jj)))
    branch_chunk = lambda b: pl.BlockSpec(
        (1, HALF // n_tiles, COL_BLOCK), lambda jj, i: (b, chunk(jj, i), staged(jj)))
    tile = lambda width: pl.BlockSpec((tm, width), lambda jj, i: (i, 0))
    bias = pl.BlockSpec((2, COL_BLOCK), lambda jj, i: (0, jj))
    wo_rows = D_MODEL // (nb * n_tiles)
    wo_slice = pl.BlockSpec((wo_rows, D_MODEL), lambda jj, i: (jj * n_tiles + i, 0))
    hbm = pl.BlockSpec(memory_space=pl.ANY)
    return pl.pallas_call(
        _merge_kernel,
        grid=(nb, n_tiles),
        in_specs=[tile(D_MODEL), tile(1),
                  tile(COL_BLOCK), tile(HALF - COL_BLOCK), tile(HALF),
                  gate_chunk(g0_first), gate_chunk(g1_first), bias,
                  branch_chunk(0), branch_chunk(1), wo_slice, hbm, hbm, hbm],
        out_specs=[pl.BlockSpec((tm, COL_BLOCK), lambda jj, i: (i, jj)), wo_slice],
        out_shape=[jax.ShapeDtypeStruct((m, D_MODEL), jnp.bfloat16),
                   jax.ShapeDtypeStruct((D_MODEL, D_MODEL), jnp.bfloat16)],
        scratch_shapes=[pltpu.VMEM((2, D_MODEL, COL_BLOCK), jnp.bfloat16),
                        pltpu.VMEM((2, D_MODEL, COL_BLOCK), jnp.bfloat16),
                        pltpu.VMEM((2, HALF, COL_BLOCK), jnp.bfloat16),
                        pltpu.VMEM((2, HALF, COL_BLOCK), jnp.bfloat16),
                        pltpu.SemaphoreType.DMA((4,))],
        compiler_params=_params(2),
        name="branch_merge",
    )(xb, rs, y_pool0, y_pool_rest, y_conv, w_in2d, w_in2d, gate_b,
      w_branch, w_branch, w_out2d, *first_blocks)


def _out_kernel(mg_ref, w_ref, x_ref, fw_ref, o_ref, rows, sumsq):
    i = pl.program_id(0)
    n = pl.program_id(1)
    n_tiles = pl.num_programs(0) - 1

    def emit_previous():
        scale = lax.rsqrt(sumsq[(i + 1) % 2] * (1.0 / D_MODEL) + NORM_EPS)
        o_ref[...] = rows[n] * scale * fw_ref[...]

    def project():
        xn = x_ref[...] + _dot(mg_ref[...], w_ref[...])
        rows[n] = xn
        part = jnp.sum(xn * xn, axis=-1, keepdims=True)
        cur = i % 2
        sumsq[cur] = jnp.where(n == 0, part, sumsq[cur] + part)

    @pl.when(i == 0)
    def _():
        project()

    @pl.when((i > 0) & (i < n_tiles))
    def _():
        emit_previous()
        project()

    @pl.when(i == n_tiles)
    def _():
        emit_previous()


def _out_proj(merged, w_out_bf, x2d, final_w):
    m = x2d.shape[0]
    tm = OUT_ROW_TILE
    tn = OUT_COL_BLOCK
    n_tiles = m // tm
    n_blocks = D_MODEL // tn
    last = n_tiles - 1
    in_row = lambda i: jnp.minimum(i, last)
    in_col = lambda i, n: jnp.where(i < n_tiles, n, n_blocks - 1)
    return pl.pallas_call(
        _out_kernel,
        grid=(n_tiles + 1, n_blocks),
        in_specs=[
            pl.BlockSpec((tm, D_MODEL), lambda i, n: (in_row(i), 0)),
            pl.BlockSpec((D_MODEL, tn), lambda i, n: (0, in_col(i, n))),
            pl.BlockSpec((tm, tn), lambda i, n: (in_row(i), in_col(i, n))),
            pl.BlockSpec((1, tn), lambda i, n: (0, n)),
        ],
        out_specs=pl.BlockSpec(
            (tm, tn), lambda i, n: (jnp.maximum(i - 1, 0), jnp.where(i > 0, n, 0))),
        out_shape=jax.ShapeDtypeStruct((m, D_MODEL), jnp.float32),
        scratch_shapes=[pltpu.VMEM((n_blocks, tm, tn), jnp.float32),
                        pltpu.VMEM((2, tm, 1), jnp.float32)],
        compiler_params=_params(2),
        name="out_proj_norm",
    )(merged, w_out_bf, x2d, final_w.reshape(1, D_MODEL))


def kernel(x, norm_w, w_in, pool_w, pool_scale, conv_w, conv_b, gate_b, w_branch, w_out,
           final_norm_w):
    b, s, d = x.shape
    assert (s, d) == (SEQ, D_MODEL) and norm_w.shape[0] == 1
    x2d = x.reshape(b * s, d)
    w_in2d = w_in[0]

    nb = N_HALF_BLOCKS
    conv_first_cols = (2 * nb, 3 * nb, 4 * nb, 5 * nb)
    y_pool0, xb, rs, pool_first = _pool_first(x2d, norm_w, w_in2d, pool_w[0], pool_scale[0])
    y_pool_rest, conv_first = _pool_rest(xb, rs, w_in2d, pool_w[0], pool_scale[0],
                                         pool_first, conv_first_cols)
    y_conv, merge_first = _conv_mixer(xb, rs, w_in2d, conv_w[0], conv_b[0],
                                      conv_first, w_branch[0])
    merged, w_out_bf = _merge(xb, rs, y_pool0, y_pool_rest, y_conv, w_in2d, gate_b[0],
                              w_branch[0], w_out[0], merge_first)
    out = _out_proj(merged, w_out_bf, x2d, final_norm_w)
    return out.reshape(b, s, d)
```

```python
import functools

import jax
import jax.numpy as jnp
from jax import lax
from jax.experimental import pallas as pl
from jax.experimental.pallas import tpu as pltpu

D_MODEL = 4096
SEQ = 4096
HALF = D_MODEL // 2
POOL_WINDOWS = (2, 4, 8, 16)
N_GROUPS = len(POOL_WINDOWS)
GROUP_DIM = HALF // N_GROUPS
CONV_K = 3
NORM_EPS = 1e-6

COL_BLOCK = GROUP_DIM
N_HALF_BLOCKS = HALF // COL_BLOCK
N_MODEL_BLOCKS = D_MODEL // COL_BLOCK
OUT_COL_BLOCK = 512
OUT_ROW_TILE = 1024
ROW_TILE = 512
POOL_ROW_TILE = 512
FIRST_STAGE_STEPS = 8
REST_ROW_TILE = 1024
POOL_HALO = 16
POOL_PAD = 16
POOL_TOP = POOL_PAD + POOL_HALO
CONV_HALO = 8

VMEM_LIMIT_BYTES = 62 * 1024 * 1024


def _dot(a, b):
    return jnp.dot(a, b, preferred_element_type=jnp.float32)


def _sigmoid(z):
    return 0.5 * jnp.tanh(0.5 * z) + 0.5


def _silu(z):
    return z * _sigmoid(z)


def _params(n_axes):
    return pltpu.CompilerParams(dimension_semantics=("arbitrary",) * n_axes,
                                vmem_limit_bytes=VMEM_LIMIT_BYTES)


def _stage_chunks(slot, i, pairs, fold=None):
    for k, (chunk_ref, buf) in enumerate(pairs):
        if isinstance(chunk_ref, tuple):
            chunk_ref, which = chunk_ref
            chunk = chunk_ref[which]
        else:
            chunk = chunk_ref[...]
        rows, cols = chunk.shape
        r0 = pl.multiple_of(i * rows, rows)
        chunk = chunk.astype(buf.dtype)
        if k == 0 and fold is not None:
            chunk = _dot(chunk, fold).astype(buf.dtype)
        buf[slot, pl.ds(r0, rows), :] = chunk


def _prestaged_maps(n_blocks, n_tiles):
    def chunk(jj, i):
        return jnp.where(jj + 1 < n_blocks, i, n_tiles - 1)

    def staged(jj):
        return jnp.minimum(jj + 1, n_blocks - 1)

    return chunk, staged


def _load_prestaged(first_step, pairs, sems):
    @pl.when(first_step)
    def _():
        copies = [pltpu.make_async_copy(src, buf.at[0], sems.at[k])
                  for k, (src, buf) in enumerate(pairs)]
        for cp in copies:
            cp.start()
        for cp in copies:
            cp.wait()


def _cast_rows(pairs, fold=None):
    for k, (src_ref, dst_ref) in enumerate(pairs):
        rows = src_ref[...].astype(dst_ref.dtype)
        if k == 0 and fold is not None:
            rows = _dot(rows, fold).astype(dst_ref.dtype)
        dst_ref[...] = rows


def _row_rms_scale(x):
    return lax.rsqrt(jnp.mean(x * x, axis=-1, keepdims=True) + NORM_EPS)


def _pool_tile(load_xb, wu, wz, g, seq_tile, ps_ref, y_ref, ubuf, sbufs, ucarry):
    tm = y_ref.shape[0]
    end = POOL_TOP + tm
    u = _dot(load_xb(), wu)

    ubuf[0:POOL_PAD, :] = jnp.zeros((POOL_PAD, COL_BLOCK), jnp.float32)
    ubuf[POOL_PAD:POOL_TOP, :] = jnp.where(seq_tile != 0, ucarry[...], 0.0)
    ubuf[POOL_TOP:end, :] = u
    ucarry[...] = ubuf[end - POOL_HALO:end, :]
    z = _dot(load_xb(), wz)

    def shifted_sum(src, lo, shift):
        return src[lo:end, :] + src[lo - shift:end - shift, :]

    s2buf = sbufs[0]
    s2buf[8:end, :] = shifted_sum(ubuf, 8, 1)
    if isinstance(g, int):
        assert g == 0
        win = s2buf[POOL_TOP:end, :]
        window = POOL_WINDOWS[0]
    else:
        s4buf = sbufs[1]
        s4buf[16:end, :] = shifted_sum(s2buf, 16, 2)
        s8buf = s2buf
        s8buf[24:end, :] = shifted_sum(s4buf, 24, 4)
        win = jnp.where(g == 1, s4buf[POOL_TOP:end, :],
                        jnp.where(g == 2, s8buf[POOL_TOP:end, :],
                                  shifted_sum(s8buf, POOL_TOP, 8)))
        window = lax.shift_left(jnp.int32(2), g)
    t1 = seq_tile * tm + 1 + lax.broadcasted_iota(jnp.int32, (tm, 1), 0)
    inv_count = 1.0 / jnp.minimum(t1, window).astype(jnp.float32)
    mixed = win * inv_count - ubuf[POOL_TOP:end, :]
    ubuf[POOL_TOP:end, :] = mixed * ps_ref[...]
    y_ref[...] = (ubuf[POOL_TOP:end, :] * _silu(z)).astype(y_ref.dtype)


def _pool_first_kernel(x_ref, nw_ref, wu_ref, wz_ref, nu_ref, nz_ref, pw_ref, npw_ref, ps_ref,
                       y_ref, h_ref, pu_ref, pz_ref,
                       wu_buf, wz_buf, ubuf, s2buf, ucarry, *, n_stage, tiles_per_seq):
    s = pl.program_id(0)

    @pl.when(s == 0)
    def _():
        ucarry[...] = jnp.zeros_like(ucarry)

    @pl.when(s < n_stage)
    def _():
        _stage_chunks(0, s, [(wu_ref, wu_buf), (wz_ref, wz_buf)],
                      fold=pw_ref[0].astype(jnp.bfloat16))
        _cast_rows([(nu_ref, pu_ref), (nz_ref, pz_ref)], fold=npw_ref[0].astype(jnp.bfloat16))

    @pl.when(s >= n_stage)
    def _():
        x = x_ref[...]
        h = ((x * _row_rms_scale(x)) * nw_ref[...]).astype(jnp.bfloat16)
        h_ref[...] = h
        _pool_tile(lambda: h, wu_buf[0], wz_buf[0], 0, (s - n_stage) % tiles_per_seq,
                   ps_ref, y_ref, ubuf, (s2buf,), ucarry)


def _pool_first(x2d, norm_row, w_in2d, pool_w, pool_scale):
    m = x2d.shape[0]
    tm = POOL_ROW_TILE
    n_tiles = m // tm
    n_stage = FIRST_STAGE_STEPS
    chunk_rows = D_MODEL // n_stage
    chunk = lambda s: jnp.minimum(s, n_stage - 1)
    tile = lambda s: jnp.maximum(s - n_stage, 0)
    w_chunk = lambda blk: pl.BlockSpec((chunk_rows, COL_BLOCK), lambda s: (chunk(s), blk))
    group_map = lambda g: pl.BlockSpec((1, GROUP_DIM, GROUP_DIM), lambda s: (g, 0, 0))
    side_out = pl.BlockSpec((chunk_rows, COL_BLOCK), lambda s: (chunk(s), 0))
    w_buf = pltpu.VMEM((1, D_MODEL, COL_BLOCK), jnp.bfloat16)
    buf = pltpu.VMEM((POOL_TOP + tm, COL_BLOCK), jnp.float32)
    outs = pl.pallas_call(
        functools.partial(_pool_first_kernel, n_stage=n_stage, tiles_per_seq=SEQ // tm),
        grid=(n_stage + n_tiles,),
        in_specs=[
            pl.BlockSpec((tm, D_MODEL), lambda s: (tile(s), 0)),
            pl.BlockSpec((1, D_MODEL), lambda s: (0, 0)),
            w_chunk(0), w_chunk(N_GROUPS),
            w_chunk(1), w_chunk(N_GROUPS + 1),
            group_map(0), group_map(1),
            pl.BlockSpec((1, COL_BLOCK), lambda s: (0, 0)),
        ],
        out_specs=[pl.BlockSpec((tm, COL_BLOCK), lambda s: (tile(s), 0)),
                   pl.BlockSpec((tm, D_MODEL), lambda s: (tile(s), 0)),
                   side_out, side_out],
        out_shape=[jax.ShapeDtypeStruct((m, COL_BLOCK), jnp.bfloat16),
                   jax.ShapeDtypeStruct((m, D_MODEL), jnp.bfloat16)]
        + [jax.ShapeDtypeStruct((D_MODEL, COL_BLOCK), jnp.bfloat16)] * 2,
        scratch_shapes=[w_buf, w_buf, buf, buf,
                        pltpu.VMEM((POOL_HALO, COL_BLOCK), jnp.float32)],
        compiler_params=_params(1),
        name="pool_first",
    )(x2d, norm_row, w_in2d, w_in2d, w_in2d, w_in2d, pool_w, pool_w,
      pool_scale.reshape(1, HALF))
    return outs[0], outs[1], outs[2:]


def _pool_rest_kernel(xb_ref, wu_ref, wz_ref, pw_ref, ps_ref,
                      wu0_ref, wz0_ref, n0_ref, n1_ref, n2_ref, n3_ref,
                      y_ref, p0_ref, p1_ref, p2_ref, p3_ref,
                      wu_buf, wz_buf, ubuf, s2buf, s4buf, ucarry, sems, *, tiles_per_seq):
    jj = pl.program_id(0)
    i = pl.program_id(1)
    first_step = (jj == 0) & (i == 0)

    @pl.when(first_step)
    def _():
        ucarry[...] = jnp.zeros_like(ucarry)

    _load_prestaged(first_step, [(wu0_ref, wu_buf), (wz0_ref, wz_buf)], sems)

    _cast_rows([(n0_ref, p0_ref), (n1_ref, p1_ref), (n2_ref, p2_ref), (n3_ref, p3_ref)])
    slot = jj % 2
    _pool_tile(lambda: xb_ref[...], wu_buf[slot], wz_buf[slot], jj + 1,
               i % tiles_per_seq, ps_ref, y_ref, ubuf, (s2buf, s4buf), ucarry)
    _stage_chunks((jj + 1) % 2, i, [(wu_ref, wu_buf), (wz_ref, wz_buf)],
                  fold=pw_ref[0].astype(jnp.bfloat16))


def _pool_rest(xb, w_in2d, pool_w, pool_scale, first_blocks, conv_blocks):
    m = xb.shape[0]
    tm = REST_ROW_TILE
    nb = N_GROUPS - 1
    n_tiles = m // tm
    chunk_rows = D_MODEL // n_tiles
    chunk, staged = _prestaged_maps(nb, n_tiles)
    w_chunk = lambda first: pl.BlockSpec((chunk_rows, COL_BLOCK),
                                         lambda jj, i: (chunk(jj, i), first + 1 + staged(jj)))
    side_steps = 2 * n_tiles
    side_rows = D_MODEL // side_steps
    side_step = lambda jj, i: jnp.minimum(jj * n_tiles + i, side_steps - 1)
    side_in = lambda blk: pl.BlockSpec((side_rows, COL_BLOCK),
                                       lambda jj, i: (side_step(jj, i), blk))
    side_out = pl.BlockSpec((side_rows, COL_BLOCK), lambda jj, i: (side_step(jj, i), 0))
    n_side = len(conv_blocks)
    hbm = pl.BlockSpec(memory_space=pl.ANY)
    w_buf = pltpu.VMEM((2, D_MODEL, COL_BLOCK), jnp.bfloat16)
    buf = pltpu.VMEM((POOL_TOP + tm, COL_BLOCK), jnp.float32)
    outs = pl.pallas_call(
        functools.partial(_pool_rest_kernel, tiles_per_seq=SEQ // tm),
        grid=(nb, n_tiles),
        in_specs=[
            pl.BlockSpec((tm, D_MODEL), lambda jj, i: (i, 0)),
            w_chunk(0), w_chunk(N_GROUPS),
            pl.BlockSpec((1, GROUP_DIM, GROUP_DIM), lambda jj, i: (1 + staged(jj), 0, 0),
                         pipeline_mode=pl.Buffered(1)),
            pl.BlockSpec((1, COL_BLOCK), lambda jj, i: (0, jj + 1)),
            hbm, hbm,
        ] + [side_in(blk) for blk in conv_blocks],
        out_specs=[pl.BlockSpec((tm, COL_BLOCK), lambda jj, i: (i, jj))] + [side_out] * n_side,
        out_shape=[jax.ShapeDtypeStruct((m, nb * COL_BLOCK), jnp.bfloat16)]
        + [jax.ShapeDtypeStruct((D_MODEL, COL_BLOCK), jnp.bfloat16)] * n_side,
        scratch_shapes=[w_buf, w_buf, buf, buf, buf,
                        pltpu.VMEM((POOL_HALO, COL_BLOCK), jnp.float32),
                        pltpu.SemaphoreType.DMA((2,))],
        compiler_params=_params(2),
        name="pool_rest",
    )(xb, w_in2d, w_in2d, pool_w, pool_scale.reshape(1, HALF),
      *first_blocks, *([w_in2d] * n_side))
    return outs[0], outs[1:]


def _conv_kernel(xb_ref, wu_ref, wb_ref, wc_ref, wz_ref, cw_ref, cb_ref,
                 wu0_ref, wb0_ref, wc0_ref, wz0_ref, n0_ref, n1_ref, n2_ref,
                 y_ref, p0_ref, p1_ref, p2_ref,
                 wu_buf, wb_buf, wc_buf, wz_buf, vbuf, vcarry, sems, *, tiles_per_seq):
    jj = pl.program_id(0)
    i = pl.program_id(1)
    tm = xb_ref.shape[0]
    first_step = (jj == 0) & (i == 0)

    @pl.when(first_step)
    def _():
        vcarry[...] = jnp.zeros_like(vcarry)

    _load_prestaged(first_step, [(wu0_ref, wu_buf), (wb0_ref, wb_buf),
                                 (wc0_ref, wc_buf), (wz0_ref, wz_buf)], sems)

    def stage():
        _stage_chunks((jj + 1) % 2, i, [(wu_ref, wu_buf), (wb_ref, wb_buf),
                                        (wc_ref, wc_buf), (wz_ref, wz_buf)])

    def compute():
        slot = jj % 2
        seq_tile = i % tiles_per_seq
        u = _dot(xb_ref[...], wu_buf[slot])
        _cast_rows([(n0_ref, p0_ref), (n1_ref, p1_ref)])
        c_gate = _dot(xb_ref[...], wc_buf[slot])
        vbuf[0:CONV_HALO, :] = jnp.where(seq_tile != 0, vcarry[...], 0.0)
        vbuf[CONV_HALO:CONV_HALO + tm, :] = c_gate * u
        vcarry[...] = vbuf[tm:CONV_HALO + tm, :]
        cw = cw_ref[...]
        y = (cb_ref[...]
             + cw[0:1, :] * vbuf[CONV_HALO - 2:CONV_HALO - 2 + tm, :]
             + cw[1:2, :] * vbuf[CONV_HALO - 1:CONV_HALO - 1 + tm, :]
             + cw[2:3, :] * vbuf[CONV_HALO:CONV_HALO + tm, :])
        gated = y * _silu(_dot(xb_ref[...], wz_buf[slot]))
        _cast_rows([(n2_ref, p2_ref)])
        y_ref[...] = (_dot(xb_ref[...], wb_buf[slot]) * gated).astype(y_ref.dtype)

    compute()
    stage()


def _conv_mixer(xb, w_in2d, conv_w, conv_b, first_blocks, w_branch):
    m = xb.shape[0]
    tm = ROW_TILE
    nb = N_HALF_BLOCKS
    n_tiles = m // tm
    chunk_rows = D_MODEL // n_tiles
    chunk, staged = _prestaged_maps(nb, n_tiles)
    w_chunk = lambda first: pl.BlockSpec((chunk_rows, COL_BLOCK),
                                         lambda jj, i: (chunk(jj, i), first + staged(jj)))
    hbm = pl.BlockSpec(memory_space=pl.ANY)
    n_steps = nb * n_tiles
    step = lambda jj, i: jj * n_tiles + i
    gate_rows = D_MODEL // n_steps
    branch_rows = HALF // n_steps
    gate_in = lambda blk: pl.BlockSpec((gate_rows, COL_BLOCK), lambda jj, i: (step(jj, i), blk))
    n_br = w_branch.shape[0]
    branches = pl.BlockSpec((n_br, branch_rows, COL_BLOCK), lambda jj, i: (0, step(jj, i), 0))
    gate_out = pl.BlockSpec((gate_rows, COL_BLOCK), lambda jj, i: (step(jj, i), 0))
    g0_first = 6 * N_HALF_BLOCKS
    g1_first = g0_first + N_MODEL_BLOCKS
    w_buf = pltpu.VMEM((2, D_MODEL, COL_BLOCK), jnp.bfloat16)
    outs = pl.pallas_call(
        functools.partial(_conv_kernel, tiles_per_seq=SEQ // tm),
        grid=(nb, n_tiles),
        in_specs=[
            pl.BlockSpec((tm, D_MODEL), lambda jj, i: (i, 0)),
            w_chunk(2 * nb), w_chunk(3 * nb), w_chunk(4 * nb), w_chunk(5 * nb),
            pl.BlockSpec((CONV_K, COL_BLOCK), lambda jj, i: (0, jj)),
            pl.BlockSpec((1, COL_BLOCK), lambda jj, i: (0, jj)),
            hbm, hbm, hbm, hbm,
            gate_in(g0_first), gate_in(g1_first), branches,
        ],
        out_specs=[pl.BlockSpec((tm, COL_BLOCK), lambda jj, i: (i, jj)),
                   gate_out, gate_out, branches],
        out_shape=[jax.ShapeDtypeStruct((m, HALF), jnp.bfloat16),
                   jax.ShapeDtypeStruct((D_MODEL, COL_BLOCK), jnp.bfloat16),
                   jax.ShapeDtypeStruct((D_MODEL, COL_BLOCK), jnp.bfloat16),
                   jax.ShapeDtypeStruct((n_br, HALF, COL_BLOCK), jnp.bfloat16)],
        scratch_shapes=[w_buf, w_buf, w_buf, w_buf,
                        pltpu.VMEM((CONV_HALO + tm, COL_BLOCK), jnp.float32),
                        pltpu.VMEM((CONV_HALO, COL_BLOCK), jnp.float32),
                        pltpu.SemaphoreType.DMA((4,))],
        compiler_params=_params(2),
        name="conv_mixer",
    )(xb, w_in2d, w_in2d, w_in2d, w_in2d, conv_w,
      conv_b.reshape(1, HALF), *first_blocks, w_in2d, w_in2d, w_branch)
    return outs[0], outs[1:]


def _merge_kernel(xb_ref, yp0_ref, ypr_ref, yc_ref, wg0_ref, wg1_ref, gb_ref,
                  wb_ref, wo_ref, wg00_ref, wg10_ref, wb0_first_ref,
                  o_ref, wo_bf_ref, wg0_buf, wg1_buf, wb0_buf, wb1_buf, sems):
    jj = pl.program_id(0)
    i = pl.program_id(1)

    _load_prestaged((jj == 0) & (i == 0),
                    [(wg00_ref, wg0_buf), (wg10_ref, wg1_buf),
                     (wb0_first_ref.at[0], wb0_buf), (wb0_first_ref.at[1], wb1_buf)], sems)

    def stage():
        _stage_chunks((jj + 1) % 2, i, [(wg0_ref, wg0_buf), (wg1_ref, wg1_buf),
                                        ((wb_ref, 0), wb0_buf), ((wb_ref, 1), wb1_buf)])

    def compute():
        slot = jj % 2
        h = xb_ref[...]
        g0 = _sigmoid(_dot(h, wg0_buf[slot]) + gb_ref[0:1, :])
        y_pool = jnp.concatenate([yp0_ref[...], ypr_ref[...]], axis=1)
        br0 = _dot(y_pool, wb0_buf[slot])
        g1 = _sigmoid(_dot(h, wg1_buf[slot]) + gb_ref[1:2, :])
        br1 = _dot(yc_ref[...], wb1_buf[slot])
        o_ref[...] = (g0 * br0 + g1 * br1).astype(o_ref.dtype)
        wo_bf_ref[...] = wo_ref[...].astype(wo_bf_ref.dtype)

    compute()
    stage()


def _merge(xb, y_pool0, y_pool_rest, y_conv, w_in2d, gate_b, w_branch, w_out2d,
           first_blocks):
    m = xb.shape[0]
    tm = ROW_TILE
    nb = N_MODEL_BLOCKS
    n_tiles = m // tm
    chunk, staged = _prestaged_maps(nb, n_tiles)
    g0_first = 6 * N_HALF_BLOCKS
    g1_first = g0_first + nb
    gate_chunk = lambda first: pl.BlockSpec(
        (D_MODEL // n_tiles, COL_BLOCK), lambda jj, i: (chunk(jj, i), first + staged(jj)))
    branch_chunks = pl.BlockSpec((w_branch.shape[0], HALF // n_tiles, COL_BLOCK),
                                 lambda jj, i: (0, chunk(jj, i), staged(jj)))
    tile = lambda width: pl.BlockSpec((tm, width), lambda jj, i: (i, 0))
    bias = pl.BlockSpec((2, COL_BLOCK), lambda jj, i: (0, jj))
    wo_rows = D_MODEL // (nb * n_tiles)
    wo_slice = pl.BlockSpec((wo_rows, D_MODEL), lambda jj, i: (jj * n_tiles + i, 0))
    hbm = pl.BlockSpec(memory_space=pl.ANY)
    return pl.pallas_call(
        _merge_kernel,
        grid=(nb, n_tiles),
        in_specs=[tile(D_MODEL),
                  tile(COL_BLOCK), tile(HALF - COL_BLOCK), tile(HALF),
                  gate_chunk(g0_first), gate_chunk(g1_first), bias,
                  branch_chunks, wo_slice, hbm, hbm, hbm],
        out_specs=[pl.BlockSpec((tm, COL_BLOCK), lambda jj, i: (i, jj)), wo_slice],
        out_shape=[jax.ShapeDtypeStruct((m, D_MODEL), jnp.bfloat16),
                   jax.ShapeDtypeStruct((D_MODEL, D_MODEL), jnp.bfloat16)],
        scratch_shapes=[pltpu.VMEM((2, D_MODEL, COL_BLOCK), jnp.bfloat16),
                        pltpu.VMEM((2, D_MODEL, COL_BLOCK), jnp.bfloat16),
                        pltpu.VMEM((2, HALF, COL_BLOCK), jnp.bfloat16),
                        pltpu.VMEM((2, HALF, COL_BLOCK), jnp.bfloat16),
                        pltpu.SemaphoreType.DMA((4,))],
        compiler_params=_params(2),
        name="branch_merge",
    )(xb, y_pool0, y_pool_rest, y_conv, w_in2d, w_in2d, gate_b,
      w_branch, w_out2d, *first_blocks)


def _out_kernel(mg_ref, w_ref, x_ref, fw_ref, o_ref, rows, sumsq):
    i = pl.program_id(0)
    n = pl.program_id(1)
    n_tiles = pl.num_programs(0) - 1

    def emit_previous():
        scale = lax.rsqrt(sumsq[(i + 1) % 2] * (1.0 / D_MODEL) + NORM_EPS)
        o_ref[...] = rows[n] * scale * fw_ref[...]

    def project():
        xn = x_ref[...] + _dot(mg_ref[...], w_ref[...])
        rows[n] = xn
        part = jnp.sum(xn * xn, axis=-1, keepdims=True)
        cur = i % 2
        sumsq[cur] = jnp.where(n == 0, part, sumsq[cur] + part)

    @pl.when(i == 0)
    def _():
        project()

    @pl.when((i > 0) & (i < n_tiles))
    def _():
        emit_previous()
        project()

    @pl.when(i == n_tiles)
    def _():
        emit_previous()


def _out_proj(merged, w_out_bf, x2d, final_w):
    m = x2d.shape[0]
    tm = OUT_ROW_TILE
    tn = OUT_COL_BLOCK
    n_tiles = m // tm
    n_blocks = D_MODEL // tn
    last = n_tiles - 1
    in_row = lambda i: jnp.minimum(i, last)
    in_col = lambda i, n: jnp.where(i < n_tiles, n, n_blocks - 1)
    return pl.pallas_call(
        _out_kernel,
        grid=(n_tiles + 1, n_blocks),
        in_specs=[
            pl.BlockSpec((tm, D_MODEL), lambda i, n: (in_row(i), 0)),
            pl.BlockSpec((D_MODEL, tn), lambda i, n: (0, in_col(i, n))),
            pl.BlockSpec((tm, tn), lambda i, n: (in_row(i), in_col(i, n))),
            pl.BlockSpec((1, tn), lambda i, n: (0, n)),
        ],
        out_specs=pl.BlockSpec(
            (tm, tn), lambda i, n: (jnp.maximum(i - 1, 0), jnp.where(i > 0, n, 0))),
        out_shape=jax.ShapeDtypeStruct((m, D_MODEL), jnp.float32),
        scratch_shapes=[pltpu.VMEM((n_blocks, tm, tn), jnp.float32),
                        pltpu.VMEM((2, tm, 1), jnp.float32)],
        compiler_params=_params(2),
        name="out_proj_norm",
    )(merged, w_out_bf, x2d, final_w.reshape(1, D_MODEL))


def kernel(x, norm_w, w_in, pool_w, pool_scale, conv_w, conv_b, gate_b, w_branch, w_out,
           final_norm_w):
    b, s, d = x.shape
    assert (s, d) == (SEQ, D_MODEL) and norm_w.shape[0] == 1
    x2d = x.reshape(b * s, d)
    w_in2d = w_in[0]

    nb = N_HALF_BLOCKS
    conv_first_cols = (2 * nb, 3 * nb, 4 * nb, 5 * nb)
    y_pool0, h, pool_first = _pool_first(x2d, norm_w, w_in2d, pool_w[0], pool_scale[0])
    y_pool_rest, conv_first = _pool_rest(h, w_in2d, pool_w[0], pool_scale[0],
                                         pool_first, conv_first_cols)
    y_conv, merge_first = _conv_mixer(h, w_in2d, conv_w[0], conv_b[0], conv_first, w_branch[0])
    merged, w_out_bf = _merge(h, y_pool0, y_pool_rest, y_conv, w_in2d, gate_b[0],
                              w_branch[0], w_out[0], merge_first)
    out = _out_proj(merged, w_out_bf, x2d, final_norm_w)
    return out.reshape(b, s, d)
```

```python
import functools

import jax
import jax.numpy as jnp
from jax import lax
from jax.experimental import pallas as pl
from jax.experimental.pallas import tpu as pltpu

D_MODEL = 4096
SEQ = 4096
HALF = D_MODEL // 2
POOL_WINDOWS = (2, 4, 8, 16)
N_GROUPS = len(POOL_WINDOWS)
GROUP_DIM = HALF // N_GROUPS
CONV_K = 3
NORM_EPS = 1e-6

COL_BLOCK = GROUP_DIM
N_HALF_BLOCKS = HALF // COL_BLOCK
N_MODEL_BLOCKS = D_MODEL // COL_BLOCK
OUT_COL_BLOCK = 512
OUT_ROW_TILE = 1024
ROW_TILE = 512
POOL_ROW_TILE = 512
FIRST_STAGE_STEPS = 8
REST_ROW_TILE = 1024
POOL_HALO = 16
POOL_PAD = 16
POOL_TOP = POOL_PAD + POOL_HALO
CONV_HALO = 8
RS_LANES = 128

VMEM_LIMIT_BYTES = 62 * 1024 * 1024


def _dot(a, b):
    return jnp.dot(a, b, preferred_element_type=jnp.float32)


def _sigmoid(z):
    return 0.5 * jnp.tanh(0.5 * z) + 0.5


def _silu(z):
    return z * _sigmoid(z)


def _silu_exp(z):
    return z / (1.0 + jnp.exp(-z))


def _params(n_axes):
    return pltpu.CompilerParams(dimension_semantics=("arbitrary",) * n_axes,
                                vmem_limit_bytes=VMEM_LIMIT_BYTES)


def _stage_chunks(slot, i, pairs, fold=None):
    for k, (chunk_ref, buf) in enumerate(pairs):
        if isinstance(chunk_ref, tuple):
            chunk_ref, which = chunk_ref
            chunk = chunk_ref[which]
        else:
            chunk = chunk_ref[...]
        rows, cols = chunk.shape
        r0 = pl.multiple_of(i * rows, rows)
        chunk = chunk.astype(buf.dtype)
        if k == 0 and fold is not None:
            chunk = _dot(chunk, fold).astype(buf.dtype)
        buf[slot, pl.ds(r0, rows), :] = chunk


def _prestaged_maps(n_blocks, n_tiles):
    def chunk(jj, i):
        return jnp.where(jj + 1 < n_blocks, i, n_tiles - 1)

    def staged(jj):
        return jnp.minimum(jj + 1, n_blocks - 1)

    return chunk, staged


def _load_prestaged(first_step, pairs, sems):
    @pl.when(first_step)
    def _():
        copies = [pltpu.make_async_copy(src, buf.at[0], sems.at[k])
                  for k, (src, buf) in enumerate(pairs)]
        for cp in copies:
            cp.start()
        for cp in copies:
            cp.wait()


def _cast_rows(pairs, fold=None):
    for k, (src_ref, dst_ref) in enumerate(pairs):
        rows = src_ref[...].astype(dst_ref.dtype)
        if k == 0 and fold is not None:
            rows = _dot(rows, fold).astype(dst_ref.dtype)
        dst_ref[...] = rows


def _row_rms_scale(x):
    return lax.rsqrt(jnp.mean(x * x, axis=-1, keepdims=True) + NORM_EPS)


def _pool_tile(load_xb, rs, wu, wz, g, seq_tile, ps_ref, y_ref, ubuf, sbufs, ucarry):
    tm = rs.shape[0]
    end = POOL_TOP + tm
    u = rs * _dot(load_xb(), wu)

    ubuf[0:POOL_PAD, :] = jnp.zeros((POOL_PAD, COL_BLOCK), jnp.float32)
    ubuf[POOL_PAD:POOL_TOP, :] = jnp.where(seq_tile != 0, ucarry[...], 0.0)
    ubuf[POOL_TOP:end, :] = u
    ucarry[...] = ubuf[end - POOL_HALO:end, :]
    z = rs * _dot(load_xb(), wz)

    def shifted_sum(src, lo, shift):
        return src[lo:end, :] + src[lo - shift:end - shift, :]

    s2buf = sbufs[0]
    s2buf[8:end, :] = shifted_sum(ubuf, 8, 1)
    if isinstance(g, int):
        assert g == 0
        win = s2buf[POOL_TOP:end, :]
        window = POOL_WINDOWS[0]
    else:
        s4buf = sbufs[1]
        s4buf[16:end, :] = shifted_sum(s2buf, 16, 2)
        s8buf = s2buf
        s8buf[24:end, :] = shifted_sum(s4buf, 24, 4)
        win = jnp.where(g == 1, s4buf[POOL_TOP:end, :],
                        jnp.where(g == 2, s8buf[POOL_TOP:end, :],
                                  shifted_sum(s8buf, POOL_TOP, 8)))
        window = lax.shift_left(jnp.int32(2), g)
    t1 = seq_tile * tm + 1 + lax.broadcasted_iota(jnp.int32, (tm, 1), 0)
    inv_count = 1.0 / jnp.minimum(t1, window).astype(jnp.float32)
    mixed = win * inv_count - ubuf[POOL_TOP:end, :]
    ubuf[POOL_TOP:end, :] = mixed * ps_ref[...]
    y_ref[...] = (ubuf[POOL_TOP:end, :] * _silu(z)).astype(y_ref.dtype)


def _pool_first_kernel(x_ref, nw_ref, wu_ref, wz_ref, nu_ref, nz_ref, pw_ref, npw_ref, ps_ref,
                       y_ref, xb_ref, rs_ref, pu_ref, pz_ref,
                       wu_buf, wz_buf, ubuf, s2buf, ucarry, *, n_stage, tiles_per_seq):
    s = pl.program_id(0)

    @pl.when(s == 0)
    def _():
        ucarry[...] = jnp.zeros_like(ucarry)

    @pl.when(s < n_stage)
    def _():
        _stage_chunks(0, s, [(wu_ref, wu_buf), (wz_ref, wz_buf)],
                      fold=pw_ref[0].astype(jnp.bfloat16))
        _cast_rows([(nu_ref, pu_ref), (nz_ref, pz_ref)], fold=npw_ref[0].astype(jnp.bfloat16))

    @pl.when(s >= n_stage)
    def _():
        x = x_ref[...]
        xb = (x * nw_ref[...]).astype(jnp.bfloat16)
        rs = _row_rms_scale(x)
        xb_ref[...] = xb
        rs_ref[...] = jnp.broadcast_to(rs, rs_ref.shape)
        _pool_tile(lambda: xb, rs, wu_buf[0], wz_buf[0], 0, (s - n_stage) % tiles_per_seq,
                   ps_ref, y_ref, ubuf, (s2buf,), ucarry)


def _pool_first(x2d, norm_row, w_in2d, pool_w, pool_scale):
    m = x2d.shape[0]
    tm = POOL_ROW_TILE
    n_tiles = m // tm
    n_stage = FIRST_STAGE_STEPS
    chunk_rows = D_MODEL // n_stage
    chunk = lambda s: jnp.minimum(s, n_stage - 1)
    tile = lambda s: jnp.maximum(s - n_stage, 0)
    w_chunk = lambda blk: pl.BlockSpec((chunk_rows, COL_BLOCK), lambda s: (chunk(s), blk))
    group_map = lambda g: pl.BlockSpec((1, GROUP_DIM, GROUP_DIM), lambda s: (g, 0, 0))
    side_out = pl.BlockSpec((chunk_rows, COL_BLOCK), lambda s: (chunk(s), 0))
    w_buf = pltpu.VMEM((1, D_MODEL, COL_BLOCK), jnp.bfloat16)
    buf = pltpu.VMEM((POOL_TOP + tm, COL_BLOCK), jnp.float32)
    outs = pl.pallas_call(
        functools.partial(_pool_first_kernel, n_stage=n_stage, tiles_per_seq=SEQ // tm),
        grid=(n_stage + n_tiles,),
        in_specs=[
            pl.BlockSpec((tm, D_MODEL), lambda s: (tile(s), 0)),
            pl.BlockSpec((1, D_MODEL), lambda s: (0, 0)),
            w_chunk(0), w_chunk(N_GROUPS),
            w_chunk(1), w_chunk(N_GROUPS + 1),
            group_map(0), group_map(1),
            pl.BlockSpec((1, COL_BLOCK), lambda s: (0, 0)),
        ],
        out_specs=[pl.BlockSpec((tm, COL_BLOCK), lambda s: (tile(s), 0)),
                   pl.BlockSpec((tm, D_MODEL), lambda s: (tile(s), 0)),
                   pl.BlockSpec((tm, RS_LANES), lambda s: (tile(s), 0)),
                   side_out, side_out],
        out_shape=[jax.ShapeDtypeStruct((m, COL_BLOCK), jnp.bfloat16),
                   jax.ShapeDtypeStruct((m, D_MODEL), jnp.bfloat16),
                   jax.ShapeDtypeStruct((m, RS_LANES), jnp.float32)]
        + [jax.ShapeDtypeStruct((D_MODEL, COL_BLOCK), jnp.bfloat16)] * 2,
        scratch_shapes=[w_buf, w_buf, buf, buf,
                        pltpu.VMEM((POOL_HALO, COL_BLOCK), jnp.float32)],
        compiler_params=_params(1),
        name="pool_first",
    )(x2d, norm_row, w_in2d, w_in2d, w_in2d, w_in2d, pool_w, pool_w,
      pool_scale.reshape(1, HALF))
    return outs[0], outs[1], outs[2], outs[3:]


def _pool_rest_kernel(xb_ref, rs_ref, wu_ref, wz_ref, pw_ref, ps_ref,
                      wu0_ref, wz0_ref, n0_ref, n1_ref, n2_ref, n3_ref,
                      y_ref, p0_ref, p1_ref, p2_ref, p3_ref,
                      wu_buf, wz_buf, ubuf, s2buf, s4buf, ucarry, sems, *, tiles_per_seq):
    jj = pl.program_id(0)
    i = pl.program_id(1)
    first_step = (jj == 0) & (i == 0)

    @pl.when(first_step)
    def _():
        ucarry[...] = jnp.zeros_like(ucarry)

    _load_prestaged(first_step, [(wu0_ref, wu_buf), (wz0_ref, wz_buf)], sems)

    _cast_rows([(n0_ref, p0_ref), (n1_ref, p1_ref), (n2_ref, p2_ref), (n3_ref, p3_ref)])
    slot = jj % 2
    _pool_tile(lambda: xb_ref[...], rs_ref[:, 0:1], wu_buf[slot], wz_buf[slot], jj + 1,
               i % tiles_per_seq, ps_ref, y_ref, ubuf, (s2buf, s4buf), ucarry)
    _stage_chunks((jj + 1) % 2, i, [(wu_ref, wu_buf), (wz_ref, wz_buf)],
                  fold=pw_ref[0].astype(jnp.bfloat16))


def _pool_rest(xb, rs, w_in2d, pool_w, pool_scale, first_blocks, conv_blocks):
    m = xb.shape[0]
    tm = REST_ROW_TILE
    nb = N_GROUPS - 1
    n_tiles = m // tm
    chunk_rows = D_MODEL // n_tiles
    chunk, staged = _prestaged_maps(nb, n_tiles)
    w_chunk = lambda first: pl.BlockSpec((chunk_rows, COL_BLOCK),
                                         lambda jj, i: (chunk(jj, i), first + 1 + staged(jj)))
    side_steps = 2 * n_tiles
    side_rows = D_MODEL // side_steps
    side_step = lambda jj, i: jnp.minimum(jj * n_tiles + i, side_steps - 1)
    side_in = lambda blk: pl.BlockSpec((side_rows, COL_BLOCK),
                                       lambda jj, i: (side_step(jj, i), blk))
    side_out = pl.BlockSpec((side_rows, COL_BLOCK), lambda jj, i: (side_step(jj, i), 0))
    n_side = len(conv_blocks)
    hbm = pl.BlockSpec(memory_space=pl.ANY)
    w_buf = pltpu.VMEM((2, D_MODEL, COL_BLOCK), jnp.bfloat16)
    buf = pltpu.VMEM((POOL_TOP + tm, COL_BLOCK), jnp.float32)
    outs = pl.pallas_call(
        functools.partial(_pool_rest_kernel, tiles_per_seq=SEQ // tm),
        grid=(nb, n_tiles),
        in_specs=[
            pl.BlockSpec((tm, D_MODEL), lambda jj, i: (i, 0)),
            pl.BlockSpec((tm, RS_LANES), lambda jj, i: (i, 0)),
            w_chunk(0), w_chunk(N_GROUPS),
            pl.BlockSpec((1, GROUP_DIM, GROUP_DIM), lambda jj, i: (1 + staged(jj), 0, 0),
                         pipeline_mode=pl.Buffered(1)),
            pl.BlockSpec((1, COL_BLOCK), lambda jj, i: (0, jj + 1)),
            hbm, hbm,
        ] + [side_in(blk) for blk in conv_blocks],
        out_specs=[pl.BlockSpec((tm, COL_BLOCK), lambda jj, i: (i, jj))] + [side_out] * n_side,
        out_shape=[jax.ShapeDtypeStruct((m, nb * COL_BLOCK), jnp.bfloat16)]
        + [jax.ShapeDtypeStruct((D_MODEL, COL_BLOCK), jnp.bfloat16)] * n_side,
        scratch_shapes=[w_buf, w_buf, buf, buf, buf,
                        pltpu.VMEM((POOL_HALO, COL_BLOCK), jnp.float32),
                        pltpu.SemaphoreType.DMA((2,))],
        compiler_params=_params(2),
        name="pool_rest",
    )(xb, rs, w_in2d, w_in2d, pool_w, pool_scale.reshape(1, HALF),
      *first_blocks, *([w_in2d] * n_side))
    return outs[0], outs[1:]


def _conv_kernel(xb_ref, rs_ref, wu_ref, wb_ref, wc_ref, wz_ref, cw_ref, cb_ref,
                 wu0_ref, wb0_ref, wc0_ref, wz0_ref, n0_ref, n1_ref, n2_ref,
                 y_ref, p0_ref, p1_ref, p2_ref,
                 wu_buf, wb_buf, wc_buf, wz_buf, vbuf, vcarry, sems, *, tiles_per_seq):
    jj = pl.program_id(0)
    i = pl.program_id(1)
    tm = xb_ref.shape[0]
    first_step = (jj == 0) & (i == 0)

    @pl.when(first_step)
    def _():
        vcarry[...] = jnp.zeros_like(vcarry)

    _load_prestaged(first_step, [(wu0_ref, wu_buf), (wb0_ref, wb_buf),
                                 (wc0_ref, wc_buf), (wz0_ref, wz_buf)], sems)

    def stage():
        _stage_chunks((jj + 1) % 2, i, [(wu_ref, wu_buf), (wb_ref, wb_buf),
                                        (wc_ref, wc_buf), (wz_ref, wz_buf)])

    def compute():
        slot = jj % 2
        seq_tile = i % tiles_per_seq
        rs = rs_ref[:, 0:1]
        rs2 = rs * rs
        u = _dot(xb_ref[...], wu_buf[slot])
        _cast_rows([(n0_ref, p0_ref), (n1_ref, p1_ref)])
        c_gate = _dot(xb_ref[...], wc_buf[slot])
        vbuf[0:CONV_HALO, :] = jnp.where(seq_tile != 0, vcarry[...], 0.0)
        vbuf[CONV_HALO:CONV_HALO + tm, :] = rs2 * (c_gate * u)
        vcarry[...] = vbuf[tm:CONV_HALO + tm, :]
        cw = cw_ref[...]
        y = (cb_ref[...]
             + cw[0:1, :] * vbuf[CONV_HALO - 2:CONV_HALO - 2 + tm, :]
             + cw[1:2, :] * vbuf[CONV_HALO - 1:CONV_HALO - 1 + tm, :]
             + cw[2:3, :] * vbuf[CONV_HALO:CONV_HALO + tm, :])
        gated = (rs * y) * _silu_exp(rs * _dot(xb_ref[...], wz_buf[slot]))
        _cast_rows([(n2_ref, p2_ref)])
        y_ref[...] = (_dot(xb_ref[...], wb_buf[slot]) * gated).astype(y_ref.dtype)

    compute()
    stage()


def _conv_mixer(xb, rs, w_in2d, conv_w, conv_b, first_blocks, w_branch):
    m = xb.shape[0]
    tm = ROW_TILE
    nb = N_HALF_BLOCKS
    n_tiles = m // tm
    chunk_rows = D_MODEL // n_tiles
    chunk, staged = _prestaged_maps(nb, n_tiles)
    w_chunk = lambda first: pl.BlockSpec((chunk_rows, COL_BLOCK),
                                         lambda jj, i: (chunk(jj, i), first + staged(jj)))
    hbm = pl.BlockSpec(memory_space=pl.ANY)
    n_steps = nb * n_tiles
    step = lambda jj, i: jj * n_tiles + i
    gate_rows = D_MODEL // n_steps
    branch_rows = HALF // n_steps
    gate_in = lambda blk: pl.BlockSpec((gate_rows, COL_BLOCK), lambda jj, i: (step(jj, i), blk))
    n_br = w_branch.shape[0]
    branches = pl.BlockSpec((n_br, branch_rows, COL_BLOCK), lambda jj, i: (0, step(jj, i), 0))
    gate_out = pl.BlockSpec((gate_rows, COL_BLOCK), lambda jj, i: (step(jj, i), 0))
    g0_first = 6 * N_HALF_BLOCKS
    g1_first = g0_first + N_MODEL_BLOCKS
    w_buf = pltpu.VMEM((2, D_MODEL, COL_BLOCK), jnp.bfloat16)
    outs = pl.pallas_call(
        functools.partial(_conv_kernel, tiles_per_seq=SEQ // tm),
        grid=(nb, n_tiles),
        in_specs=[
            pl.BlockSpec((tm, D_MODEL), lambda jj, i: (i, 0)),
            pl.BlockSpec((tm, RS_LANES), lambda jj, i: (i, 0)),
            w_chunk(2 * nb), w_chunk(3 * nb), w_chunk(4 * nb), w_chunk(5 * nb),
            pl.BlockSpec((CONV_K, COL_BLOCK), lambda jj, i: (0, jj)),
            pl.BlockSpec((1, COL_BLOCK), lambda jj, i: (0, jj)),
            hbm, hbm, hbm, hbm,
            gate_in(g0_first), gate_in(g1_first), branches,
        ],
        out_specs=[pl.BlockSpec((tm, COL_BLOCK), lambda jj, i: (i, jj)),
                   gate_out, gate_out, branches],
        out_shape=[jax.ShapeDtypeStruct((m, HALF), jnp.bfloat16),
                   jax.ShapeDtypeStruct((D_MODEL, COL_BLOCK), jnp.bfloat16),
                   jax.ShapeDtypeStruct((D_MODEL, COL_BLOCK), jnp.bfloat16),
                   jax.ShapeDtypeStruct((n_br, HALF, COL_BLOCK), jnp.bfloat16)],
        scratch_shapes=[w_buf, w_buf, w_buf, w_buf,
                        pltpu.VMEM((CONV_HALO + tm, COL_BLOCK), jnp.float32),
                        pltpu.VMEM((CONV_HALO, COL_BLOCK), jnp.float32),
                        pltpu.SemaphoreType.DMA((4,))],
        compiler_params=_params(2),
        name="conv_mixer",
    )(xb, rs, w_in2d, w_in2d, w_in2d, w_in2d, conv_w,
      conv_b.reshape(1, HALF), *first_blocks, w_in2d, w_in2d, w_branch)
    return outs[0], outs[1:]


def _merge_kernel(xb_ref, rs_ref, yp0_ref, ypr_ref, yc_ref, wg0_ref, wg1_ref, gb_ref,
                  wb_ref, wo_ref, wg00_ref, wg10_ref, wb0_first_ref,
                  o_ref, wo_bf_ref, wg0_buf, wg1_buf, wb0_buf, wb1_buf, sems):
    jj = pl.program_id(0)
    i = pl.program_id(1)

    _load_prestaged((jj == 0) & (i == 0),
                    [(wg00_ref, wg0_buf), (wg10_ref, wg1_buf),
                     (wb0_first_ref.at[0], wb0_buf), (wb0_first_ref.at[1], wb1_buf)], sems)

    def stage():
        _stage_chunks((jj + 1) % 2, i, [(wg0_ref, wg0_buf), (wg1_ref, wg1_buf),
                                        ((wb_ref, 0), wb0_buf), ((wb_ref, 1), wb1_buf)])

    def compute():
        slot = jj % 2
        h = xb_ref[...]
        rs = rs_ref[:, 0:1]
        g0 = _sigmoid(rs * _dot(h, wg0_buf[slot]) + gb_ref[0:1, :])
        y_pool = jnp.concatenate([yp0_ref[...], ypr_ref[...]], axis=1)
        br0 = _dot(y_pool, wb0_buf[slot])
        g1 = _sigmoid(rs * _dot(h, wg1_buf[slot]) + gb_ref[1:2, :])
        br1 = _dot(yc_ref[...], wb1_buf[slot])
        o_ref[...] = (g0 * br0 + g1 * br1).astype(o_ref.dtype)
        wo_bf_ref[...] = wo_ref[...].astype(wo_bf_ref.dtype)

    compute()
    stage()


def _merge(xb, rs, y_pool0, y_pool_rest, y_conv, w_in2d, gate_b, w_branch, w_out2d,
           first_blocks):
    m = xb.shape[0]
    tm = ROW_TILE
    nb = N_MODEL_BLOCKS
    n_tiles = m // tm
    chunk, staged = _prestaged_maps(nb, n_tiles)
    g0_first = 6 * N_HALF_BLOCKS
    g1_first = g0_first + nb
    gate_chunk = lambda first: pl.BlockSpec(
        (D_MODEL // n_tiles, COL_BLOCK), lambda jj, i: (chunk(jj, i), first + staged(jj)))
    branch_chunks = pl.BlockSpec((w_branch.shape[0], HALF // n_tiles, COL_BLOCK),
                                 lambda jj, i: (0, chunk(jj, i), staged(jj)))
    tile = lambda width: pl.BlockSpec((tm, width), lambda jj, i: (i, 0))
    bias = pl.BlockSpec((2, COL_BLOCK), lambda jj, i: (0, jj))
    wo_rows = D_MODEL // (nb * n_tiles)
    wo_slice = pl.BlockSpec((wo_rows, D_MODEL), lambda jj, i: (jj * n_tiles + i, 0))
    hbm = pl.BlockSpec(memory_space=pl.ANY)
    return pl.pallas_call(
        _merge_kernel,
        grid=(nb, n_tiles),
        in_specs=[tile(D_MODEL), tile(RS_LANES),
                  tile(COL_BLOCK), tile(HALF - COL_BLOCK), tile(HALF),
                  gate_chunk(g0_first), gate_chunk(g1_first), bias,
                  branch_chunks, wo_slice, hbm, hbm, hbm],
        out_specs=[pl.BlockSpec((tm, COL_BLOCK), lambda jj, i: (i, jj)), wo_slice],
        out_shape=[jax.ShapeDtypeStruct((m, D_MODEL), jnp.bfloat16),
                   jax.ShapeDtypeStruct((D_MODEL, D_MODEL), jnp.bfloat16)],
        scratch_shapes=[pltpu.VMEM((2, D_MODEL, COL_BLOCK), jnp.bfloat16),
                        pltpu.VMEM((2, D_MODEL, COL_BLOCK), jnp.bfloat16),
                        pltpu.VMEM((2, HALF, COL_BLOCK), jnp.bfloat16),
                        pltpu.VMEM((2, HALF, COL_BLOCK), jnp.bfloat16),
                        pltpu.SemaphoreType.DMA((4,))],
        compiler_params=_params(2),
        name="branch_merge",
    )(xb, rs, y_pool0, y_pool_rest, y_conv, w_in2d, w_in2d, gate_b,
      w_branch, w_out2d, *first_blocks)


def _out_kernel(mg_ref, w_ref, x_ref, fw_ref, o_ref, rows, sumsq):
    i = pl.program_id(0)
    n = pl.program_id(1)
    n_tiles = pl.num_programs(0) - 1

    def emit_previous():
        scale = lax.rsqrt(sumsq[(i + 1) % 2] * (1.0 / D_MODEL) + NORM_EPS)
        o_ref[...] = rows[n] * scale * fw_ref[...]

    def project():
        xn = x_ref[...] + _dot(mg_ref[...], w_ref[...])
        rows[n] = xn
        part = jnp.sum(xn * xn, axis=-1, keepdims=True)
        cur = i % 2
        sumsq[cur] = jnp.where(n == 0, part, sumsq[cur] + part)

    @pl.when(i == 0)
    def _():
        project()

    @pl.when((i > 0) & (i < n_tiles))
    def _():
        emit_previous()
        project()

    @pl.when(i == n_tiles)
    def _():
        emit_previous()


def _out_proj(merged, w_out_bf, x2d, final_w):
    m = x2d.shape[0]
    tm = OUT_ROW_TILE
    tn = OUT_COL_BLOCK
    n_tiles = m // tm
    n_blocks = D_MODEL // tn
    last = n_tiles - 1
    in_row = lambda i: jnp.minimum(i, last)
    in_col = lambda i, n: jnp.where(i < n_tiles, n, n_blocks - 1)
    return pl.pallas_call(
        _out_kernel,
        grid=(n_tiles + 1, n_blocks),
        in_specs=[
            pl.BlockSpec((tm, D_MODEL), lambda i, n: (in_row(i), 0)),
            pl.BlockSpec((D_MODEL, tn), lambda i, n: (0, in_col(i, n))),
            pl.BlockSpec((tm, tn), lambda i, n: (in_row(i), in_col(i, n))),
            pl.BlockSpec((1, tn), lambda i, n: (0, n)),
        ],
        out_specs=pl.BlockSpec(
            (tm, tn), lambda i, n: (jnp.maximum(i - 1, 0), jnp.where(i > 0, n, 0))),
        out_shape=jax.ShapeDtypeStruct((m, D_MODEL), jnp.float32),
        scratch_shapes=[pltpu.VMEM((n_blocks, tm, tn), jnp.float32),
                        pltpu.VMEM((2, tm, 1), jnp.float32)],
        compiler_params=_params(2),
        name="out_proj_norm",
    )(merged, w_out_bf, x2d, final_w.reshape(1, D_MODEL))


def kernel(x, norm_w, w_in, pool_w, pool_scale, conv_w, conv_b, gate_b, w_branch, w_out,
           final_norm_w):
    b, s, d = x.shape
    assert (s, d) == (SEQ, D_MODEL) and norm_w.shape[0] == 1
    x2d = x.reshape(b * s, d)
    w_in2d = w_in[0]

    nb = N_HALF_BLOCKS
    conv_first_cols = (2 * nb, 3 * nb, 4 * nb, 5 * nb)
    y_pool0, xb, rs, pool_first = _pool_first(x2d, norm_w, w_in2d, pool_w[0], pool_scale[0])
    y_pool_rest, conv_first = _pool_rest(xb, rs, w_in2d, pool_w[0], pool_scale[0],
                                         pool_first, conv_first_cols)
    y_conv, merge_first = _conv_mixer(xb, rs, w_in2d, conv_w[0], conv_b[0],
                                      conv_first, w_branch[0])
    merged, w_out_bf = _merge(xb, rs, y_pool0, y_pool_rest, y_conv, w_in2d, gate_b[0],
                              w_branch[0], w_out[0], merge_first)
    out = _out_proj(merged, w_out_bf, x2d, final_norm_w)
    return out.reshape(b, s, d)
```

```python
import functools

import jax
import jax.numpy as jnp
from jax import lax
from jax.experimental import pallas as pl
from jax.experimental.pallas import tpu as pltpu

D_MODEL = 4096
SEQ = 4096
HALF = D_MODEL // 2
POOL_WINDOWS = (2, 4, 8, 16)
N_GROUPS = len(POOL_WINDOWS)
GROUP_DIM = HALF // N_GROUPS
CONV_K = 3
NORM_EPS = 1e-6

COL_BLOCK = GROUP_DIM
N_HALF_BLOCKS = HALF // COL_BLOCK
N_MODEL_BLOCKS = D_MODEL // COL_BLOCK
OUT_COL_BLOCK = 512
OUT_ROW_TILE = 1024
ROW_TILE = 512
POOL_ROW_TILE = 512
FIRST_STAGE_STEPS = 8
REST_ROW_TILE = 1024
POOL_HALO = 16
POOL_PAD = 16
POOL_TOP = POOL_PAD + POOL_HALO
CONV_HALO = 8

VMEM_LIMIT_BYTES = 62 * 1024 * 1024


def _dot(a, b):
    return jnp.dot(a, b, preferred_element_type=jnp.float32)


def _sigmoid(z):
    return 0.5 * jnp.tanh(0.5 * z) + 0.5


def _silu(z):
    return z * _sigmoid(z)


def _silu_exp(z):
    return z / (1.0 + jnp.exp(-z))


def _params(n_axes):
    return pltpu.CompilerParams(dimension_semantics=("arbitrary",) * n_axes,
                                vmem_limit_bytes=VMEM_LIMIT_BYTES)


def _stage_chunks(slot, i, pairs, fold=None):
    for k, (chunk_ref, buf) in enumerate(pairs):
        if isinstance(chunk_ref, tuple):
            chunk_ref, which = chunk_ref
            chunk = chunk_ref[which]
        else:
            chunk = chunk_ref[...]
        rows, cols = chunk.shape
        r0 = pl.multiple_of(i * rows, rows)
        chunk = chunk.astype(buf.dtype)
        if k == 0 and fold is not None:
            chunk = _dot(chunk, fold).astype(buf.dtype)
        buf[slot, pl.ds(r0, rows), :] = chunk


def _prestaged_maps(n_blocks, n_tiles):
    def chunk(jj, i):
        return jnp.where(jj + 1 < n_blocks, i, n_tiles - 1)

    def staged(jj):
        return jnp.minimum(jj + 1, n_blocks - 1)

    return chunk, staged


def _load_prestaged(first_step, pairs, sems):
    @pl.when(first_step)
    def _():
        copies = [pltpu.make_async_copy(src, buf.at[0], sems.at[k])
                  for k, (src, buf) in enumerate(pairs)]
        for cp in copies:
            cp.start()
        for cp in copies:
            cp.wait()


def _cast_rows(pairs, fold=None):
    for k, (src_ref, dst_ref) in enumerate(pairs):
        rows = src_ref[...].astype(dst_ref.dtype)
        if k == 0 and fold is not None:
            rows = _dot(rows, fold).astype(dst_ref.dtype)
        dst_ref[...] = rows


def _row_rms_scale(x):
    return lax.rsqrt(jnp.mean(x * x, axis=-1, keepdims=True) + NORM_EPS)


def _pool_tile(load_xb, rs, wu, wz, g, seq_tile, ps_ref, y_ref, ubuf, sbufs, ucarry):
    tm = rs.shape[0]
    end = POOL_TOP + tm
    u = rs * _dot(load_xb(), wu)

    ubuf[0:POOL_PAD, :] = jnp.zeros((POOL_PAD, COL_BLOCK), jnp.float32)
    ubuf[POOL_PAD:POOL_TOP, :] = jnp.where(seq_tile != 0, ucarry[...], 0.0)
    ubuf[POOL_TOP:end, :] = u
    ucarry[...] = ubuf[end - POOL_HALO:end, :]
    z = rs * _dot(load_xb(), wz)

    def shifted_sum(src, lo, shift):
        return src[lo:end, :] + src[lo - shift:end - shift, :]

    s2buf = sbufs[0]
    s2buf[8:end, :] = shifted_sum(ubuf, 8, 1)
    if isinstance(g, int):
        assert g == 0
        win = s2buf[POOL_TOP:end, :]
        window = POOL_WINDOWS[0]
    else:
        s4buf = sbufs[1]
        s4buf[16:end, :] = shifted_sum(s2buf, 16, 2)
        s8buf = s2buf
        s8buf[24:end, :] = shifted_sum(s4buf, 24, 4)
        win = jnp.where(g == 1, s4buf[POOL_TOP:end, :],
                        jnp.where(g == 2, s8buf[POOL_TOP:end, :],
                                  shifted_sum(s8buf, POOL_TOP, 8)))
        window = lax.shift_left(jnp.int32(2), g)
    t1 = seq_tile * tm + 1 + lax.broadcasted_iota(jnp.int32, (tm, 1), 0)
    inv_count = 1.0 / jnp.minimum(t1, window).astype(jnp.float32)
    mixed = win * inv_count - ubuf[POOL_TOP:end, :]
    ubuf[POOL_TOP:end, :] = mixed * ps_ref[...]
    y_ref[...] = (ubuf[POOL_TOP:end, :] * _silu(z)).astype(y_ref.dtype)


def _pool_first_kernel(x_ref, nw_ref, wu_ref, wz_ref, nu_ref, nz_ref, pw_ref, npw_ref, ps_ref,
                       y_ref, xb_ref, rs_ref, pu_ref, pz_ref,
                       wu_buf, wz_buf, ubuf, s2buf, ucarry, *, n_stage, tiles_per_seq):
    s = pl.program_id(0)

    @pl.when(s == 0)
    def _():
        ucarry[...] = jnp.zeros_like(ucarry)

    @pl.when(s < n_stage)
    def _():
        _stage_chunks(0, s, [(wu_ref, wu_buf), (wz_ref, wz_buf)],
                      fold=pw_ref[0].astype(jnp.bfloat16))
        _cast_rows([(nu_ref, pu_ref), (nz_ref, pz_ref)], fold=npw_ref[0].astype(jnp.bfloat16))

    @pl.when(s >= n_stage)
    def _():
        x = x_ref[...]
        xb = (x * nw_ref[...]).astype(jnp.bfloat16)
        rs = _row_rms_scale(x)
        xb_ref[...] = xb
        rs_ref[...] = rs
        _pool_tile(lambda: xb, rs, wu_buf[0], wz_buf[0], 0, (s - n_stage) % tiles_per_seq,
                   ps_ref, y_ref, ubuf, (s2buf,), ucarry)


def _pool_first(x2d, norm_row, w_in2d, pool_w, pool_scale):
    m = x2d.shape[0]
    tm = POOL_ROW_TILE
    n_tiles = m // tm
    n_stage = FIRST_STAGE_STEPS
    chunk_rows = D_MODEL // n_stage
    chunk = lambda s: jnp.minimum(s, n_stage - 1)
    tile = lambda s: jnp.maximum(s - n_stage, 0)
    w_chunk = lambda blk: pl.BlockSpec((chunk_rows, COL_BLOCK), lambda s: (chunk(s), blk))
    group_map = lambda g: pl.BlockSpec((1, GROUP_DIM, GROUP_DIM), lambda s: (g, 0, 0))
    side_out = pl.BlockSpec((chunk_rows, COL_BLOCK), lambda s: (chunk(s), 0))
    w_buf = pltpu.VMEM((1, D_MODEL, COL_BLOCK), jnp.bfloat16)
    buf = pltpu.VMEM((POOL_TOP + tm, COL_BLOCK), jnp.float32)
    outs = pl.pallas_call(
        functools.partial(_pool_first_kernel, n_stage=n_stage, tiles_per_seq=SEQ // tm),
        grid=(n_stage + n_tiles,),
        in_specs=[
            pl.BlockSpec((tm, D_MODEL), lambda s: (tile(s), 0)),
            pl.BlockSpec((1, D_MODEL), lambda s: (0, 0)),
            w_chunk(0), w_chunk(N_GROUPS),
            w_chunk(1), w_chunk(N_GROUPS + 1),
            group_map(0), group_map(1),
            pl.BlockSpec((1, COL_BLOCK), lambda s: (0, 0)),
        ],
        out_specs=[pl.BlockSpec((tm, COL_BLOCK), lambda s: (tile(s), 0)),
                   pl.BlockSpec((tm, D_MODEL), lambda s: (tile(s), 0)),
                   pl.BlockSpec((tm, 1), lambda s: (tile(s), 0)),
                   side_out, side_out],
        out_shape=[jax.ShapeDtypeStruct((m, COL_BLOCK), jnp.bfloat16),
                   jax.ShapeDtypeStruct((m, D_MODEL), jnp.bfloat16),
                   jax.ShapeDtypeStruct((m, 1), jnp.float32)]
        + [jax.ShapeDtypeStruct((D_MODEL, COL_BLOCK), jnp.bfloat16)] * 2,
        scratch_shapes=[w_buf, w_buf, buf, buf,
                        pltpu.VMEM((POOL_HALO, COL_BLOCK), jnp.float32)],
        compiler_params=_params(1),
        name="pool_first",
    )(x2d, norm_row, w_in2d, w_in2d, w_in2d, w_in2d, pool_w, pool_w,
      pool_scale.reshape(1, HALF))
    return outs[0], outs[1], outs[2], outs[3:]


def _pool_rest_kernel(xb_ref, rs_ref, wu_ref, wz_ref, pw_ref, ps_ref,
                      wu0_ref, wz0_ref, n0_ref, n1_ref, n2_ref, n3_ref,
                      y_ref, p0_ref, p1_ref, p2_ref, p3_ref,
                      wu_buf, wz_buf, ubuf, s2buf, s4buf, ucarry, sems, *, tiles_per_seq):
    jj = pl.program_id(0)
    i = pl.program_id(1)
    first_step = (jj == 0) & (i == 0)

    @pl.when(first_step)
    def _():
        ucarry[...] = jnp.zeros_like(ucarry)

    _load_prestaged(first_step, [(wu0_ref, wu_buf), (wz0_ref, wz_buf)], sems)

    _cast_rows([(n0_ref, p0_ref), (n1_ref, p1_ref), (n2_ref, p2_ref), (n3_ref, p3_ref)])
    slot = jj % 2
    _pool_tile(lambda: xb_ref[...], rs_ref[...], wu_buf[slot], wz_buf[slot], jj + 1,
               i % tiles_per_seq, ps_ref, y_ref, ubuf, (s2buf, s4buf), ucarry)
    _stage_chunks((jj + 1) % 2, i, [(wu_ref, wu_buf), (wz_ref, wz_buf)],
                  fold=pw_ref[0].astype(jnp.bfloat16))


def _pool_rest(xb, rs, w_in2d, pool_w, pool_scale, first_blocks, conv_blocks):
    m = xb.shape[0]
    tm = REST_ROW_TILE
    nb = N_GROUPS - 1
    n_tiles = m // tm
    chunk_rows = D_MODEL // n_tiles
    chunk, staged = _prestaged_maps(nb, n_tiles)
    w_chunk = lambda first: pl.BlockSpec((chunk_rows, COL_BLOCK),
                                         lambda jj, i: (chunk(jj, i), first + 1 + staged(jj)))
    side_steps = 2 * n_tiles
    side_rows = D_MODEL // side_steps
    side_step = lambda jj, i: jnp.minimum(jj * n_tiles + i, side_steps - 1)
    side_in = lambda blk: pl.BlockSpec((side_rows, COL_BLOCK),
                                       lambda jj, i: (side_step(jj, i), blk))
    side_out = pl.BlockSpec((side_rows, COL_BLOCK), lambda jj, i: (side_step(jj, i), 0))
    n_side = len(conv_blocks)
    hbm = pl.BlockSpec(memory_space=pl.ANY)
    w_buf = pltpu.VMEM((2, D_MODEL, COL_BLOCK), jnp.bfloat16)
    buf = pltpu.VMEM((POOL_TOP + tm, COL_BLOCK), jnp.float32)
    outs = pl.pallas_call(
        functools.partial(_pool_rest_kernel, tiles_per_seq=SEQ // tm),
        grid=(nb, n_tiles),
        in_specs=[
            pl.BlockSpec((tm, D_MODEL), lambda jj, i: (i, 0)),
            pl.BlockSpec((tm, 1), lambda jj, i: (i, 0)),
            w_chunk(0), w_chunk(N_GROUPS),
            pl.BlockSpec((1, GROUP_DIM, GROUP_DIM), lambda jj, i: (1 + staged(jj), 0, 0),
                         pipeline_mode=pl.Buffered(1)),
            pl.BlockSpec((1, COL_BLOCK), lambda jj, i: (0, jj + 1)),
            hbm, hbm,
        ] + [side_in(blk) for blk in conv_blocks],
        out_specs=[pl.BlockSpec((tm, COL_BLOCK), lambda jj, i: (i, jj))] + [side_out] * n_side,
        out_shape=[jax.ShapeDtypeStruct((m, nb * COL_BLOCK), jnp.bfloat16)]
        + [jax.ShapeDtypeStruct((D_MODEL, COL_BLOCK), jnp.bfloat16)] * n_side,
        scratch_shapes=[w_buf, w_buf, buf, buf, buf,
                        pltpu.VMEM((POOL_HALO, COL_BLOCK), jnp.float32),
                        pltpu.SemaphoreType.DMA((2,))],
        compiler_params=_params(2),
        name="pool_rest",
    )(xb, rs, w_in2d, w_in2d, pool_w, pool_scale.reshape(1, HALF),
      *first_blocks, *([w_in2d] * n_side))
    return outs[0], outs[1:]


def _conv_kernel(xb_ref, rs_ref, wu_ref, wb_ref, wc_ref, wz_ref, cw_ref, cb_ref,
                 wu0_ref, wb0_ref, wc0_ref, wz0_ref, n0_ref, n1_ref, n2_ref,
                 y_ref, p0_ref, p1_ref, p2_ref,
                 wu_buf, wb_buf, wc_buf, wz_buf, vbuf, vcarry, sems, *, tiles_per_seq):
    jj = pl.program_id(0)
    i = pl.program_id(1)
    tm = xb_ref.shape[0]
    first_step = (jj == 0) & (i == 0)

    @pl.when(first_step)
    def _():
        vcarry[...] = jnp.zeros_like(vcarry)

    _load_prestaged(first_step, [(wu0_ref, wu_buf), (wb0_ref, wb_buf),
                                 (wc0_ref, wc_buf), (wz0_ref, wz_buf)], sems)

    def stage():
        _stage_chunks((jj + 1) % 2, i, [(wu_ref, wu_buf), (wb_ref, wb_buf),
                                        (wc_ref, wc_buf), (wz_ref, wz_buf)])

    def compute():
        slot = jj % 2
        seq_tile = i % tiles_per_seq
        rs = rs_ref[...]
        rs2 = rs * rs
        u = _dot(xb_ref[...], wu_buf[slot])
        _cast_rows([(n0_ref, p0_ref), (n1_ref, p1_ref)])
        c_gate = _dot(xb_ref[...], wc_buf[slot])
        vbuf[0:CONV_HALO, :] = jnp.where(seq_tile != 0, vcarry[...], 0.0)
        vbuf[CONV_HALO:CONV_HALO + tm, :] = rs2 * (c_gate * u)
        vcarry[...] = vbuf[tm:CONV_HALO + tm, :]
        cw = cw_ref[...]
        y = (cb_ref[...]
             + cw[0:1, :] * vbuf[CONV_HALO - 2:CONV_HALO - 2 + tm, :]
             + cw[1:2, :] * vbuf[CONV_HALO - 1:CONV_HALO - 1 + tm, :]
             + cw[2:3, :] * vbuf[CONV_HALO:CONV_HALO + tm, :])
        gated = (rs * y) * _silu_exp(rs * _dot(xb_ref[...], wz_buf[slot]))
        _cast_rows([(n2_ref, p2_ref)])
        y_ref[...] = (_dot(xb_ref[...], wb_buf[slot]) * gated).astype(y_ref.dtype)

    compute()
    stage()


def _conv_mixer(xb, rs, w_in2d, conv_w, conv_b, first_blocks, w_branch):
    m = xb.shape[0]
    tm = ROW_TILE
    nb = N_HALF_BLOCKS
    n_tiles = m // tm
    chunk_rows = D_MODEL // n_tiles
    chunk, staged = _prestaged_maps(nb, n_tiles)
    w_chunk = lambda first: pl.BlockSpec((chunk_rows, COL_BLOCK),
                                         lambda jj, i: (chunk(jj, i), first + staged(jj)))
    hbm = pl.BlockSpec(memory_space=pl.ANY)
    n_steps = nb * n_tiles
    step = lambda jj, i: jj * n_tiles + i
    gate_rows = D_MODEL // n_steps
    branch_rows = HALF // n_steps
    gate_in = lambda blk: pl.BlockSpec((gate_rows, COL_BLOCK), lambda jj, i: (step(jj, i), blk))
    n_br = w_branch.shape[0]
    branches = pl.BlockSpec((n_br, branch_rows, COL_BLOCK), lambda jj, i: (0, step(jj, i), 0))
    gate_out = pl.BlockSpec((gate_rows, COL_BLOCK), lambda jj, i: (step(jj, i), 0))
    g0_first = 6 * N_HALF_BLOCKS
    g1_first = g0_first + N_MODEL_BLOCKS
    w_buf = pltpu.VMEM((2, D_MODEL, COL_BLOCK), jnp.bfloat16)
    outs = pl.pallas_call(
        functools.partial(_conv_kernel, tiles_per_seq=SEQ // tm),
        grid=(nb, n_tiles),
        in_specs=[
            pl.BlockSpec((tm, D_MODEL), lambda jj, i: (i, 0)),
            pl.BlockSpec((tm, 1), lambda jj, i: (i, 0)),
            w_chunk(2 * nb), w_chunk(3 * nb), w_chunk(4 * nb), w_chunk(5 * nb),
            pl.BlockSpec((CONV_K, COL_BLOCK), lambda jj, i: (0, jj)),
            pl.BlockSpec((1, COL_BLOCK), lambda jj, i: (0, jj)),
            hbm, hbm, hbm, hbm,
            gate_in(g0_first), gate_in(g1_first), branches,
        ],
        out_specs=[pl.BlockSpec((tm, COL_BLOCK), lambda jj, i: (i, jj)),
                   gate_out, gate_out, branches],
        out_shape=[jax.ShapeDtypeStruct((m, HALF), jnp.bfloat16),
                   jax.ShapeDtypeStruct((D_MODEL, COL_BLOCK), jnp.bfloat16),
                   jax.ShapeDtypeStruct((D_MODEL, COL_BLOCK), jnp.bfloat16),
                   jax.ShapeDtypeStruct((n_br, HALF, COL_BLOCK), jnp.bfloat16)],
        scratch_shapes=[w_buf, w_buf, w_buf, w_buf,
                        pltpu.VMEM((CONV_HALO + tm, COL_BLOCK), jnp.float32),
                        pltpu.VMEM((CONV_HALO, COL_BLOCK), jnp.float32),
                        pltpu.SemaphoreType.DMA((4,))],
        compiler_params=_params(2),
        name="conv_mixer",
    )(xb, rs, w_in2d, w_in2d, w_in2d, w_in2d, conv_w,
      conv_b.reshape(1, HALF), *first_blocks, w_in2d, w_in2d, w_branch)
    return outs[0], outs[1:]


def _merge_kernel(xb_ref, rs_ref, yp0_ref, ypr_ref, yc_ref, wg0_ref, wg1_ref, gb_ref,
                  wb_ref, wo_ref, wg00_ref, wg10_ref, wb0_first_ref,
                  o_ref, wo_bf_ref, wg0_buf, wg1_buf, wb0_buf, wb1_buf, sems):
    jj = pl.program_id(0)
    i = pl.program_id(1)

    _load_prestaged((jj == 0) & (i == 0),
                    [(wg00_ref, wg0_buf), (wg10_ref, wg1_buf),
                     (wb0_first_ref.at[0], wb0_buf), (wb0_first_ref.at[1], wb1_buf)], sems)

    def stage():
        _stage_chunks((jj + 1) % 2, i, [(wg0_ref, wg0_buf), (wg1_ref, wg1_buf),
                                        ((wb_ref, 0), wb0_buf), ((wb_ref, 1), wb1_buf)])

    def compute():
        slot = jj % 2
        h = xb_ref[...]
        rs = rs_ref[...]
        g0 = _sigmoid(rs * _dot(h, wg0_buf[slot]) + gb_ref[0:1, :])
        y_pool = jnp.concatenate([yp0_ref[...], ypr_ref[...]], axis=1)
        br0 = _dot(y_pool, wb0_buf[slot])
        g1 = _sigmoid(rs * _dot(h, wg1_buf[slot]) + gb_ref[1:2, :])
        br1 = _dot(yc_ref[...], wb1_buf[slot])
        o_ref[...] = (g0 * br0 + g1 * br1).astype(o_ref.dtype)
        wo_bf_ref[...] = wo_ref[...].astype(wo_bf_ref.dtype)

    compute()
    stage()


def _merge(xb, rs, y_pool0, y_pool_rest, y_conv, w_in2d, gate_b, w_branch, w_out2d,
           first_blocks):
    m = xb.shape[0]
    tm = ROW_TILE
    nb = N_MODEL_BLOCKS
    n_tiles = m // tm
    chunk, staged = _prestaged_maps(nb, n_tiles)
    g0_first = 6 * N_HALF_BLOCKS
    g1_first = g0_first + nb
    gate_chunk = lambda first: pl.BlockSpec(
        (D_MODEL // n_tiles, COL_BLOCK), lambda jj, i: (chunk(jj, i), first + staged(jj)))
    branch_chunks = pl.BlockSpec((w_branch.shape[0], HALF // n_tiles, COL_BLOCK),
                                 lambda jj, i: (0, chunk(jj, i), staged(jj)))
    tile = lambda width: pl.BlockSpec((tm, width), lambda jj, i: (i, 0))
    bias = pl.BlockSpec((2, COL_BLOCK), lambda jj, i: (0, jj))
    wo_rows = D_MODEL // (nb * n_tiles)
    wo_slice = pl.BlockSpec((wo_rows, D_MODEL), lambda jj, i: (jj * n_tiles + i, 0))
    hbm = pl.BlockSpec(memory_space=pl.ANY)
    return pl.pallas_call(
        _merge_kernel,
        grid=(nb, n_tiles),
        in_specs=[tile(D_MODEL), tile(1),
                  tile(COL_BLOCK), tile(HALF - COL_BLOCK), tile(HALF),
                  gate_chunk(g0_first), gate_chunk(g1_first), bias,
                  branch_chunks, wo_slice, hbm, hbm, hbm],
        out_specs=[pl.BlockSpec((tm, COL_BLOCK), lambda jj, i: (i, jj)), wo_slice],
        out_shape=[jax.ShapeDtypeStruct((m, D_MODEL), jnp.bfloat16),
                   jax.ShapeDtypeStruct((D_MODEL, D_MODEL), jnp.bfloat16)],
        scratch_shapes=[pltpu.VMEM((2, D_MODEL, COL_BLOCK), jnp.bfloat16),
                        pltpu.VMEM((2, D_MODEL, COL_BLOCK), jnp.bfloat16),
                        pltpu.VMEM((2, HALF, COL_BLOCK), jnp.bfloat16),
                        pltpu.VMEM((2, HALF, COL_BLOCK), jnp.bfloat16),
                        pltpu.SemaphoreType.DMA((4,))],
        compiler_params=_params(2),
        name="branch_merge",
    )(xb, rs, y_pool0, y_pool_rest, y_conv, w_in2d, w_in2d, gate_b,
      w_branch, w_out2d, *first_blocks)


def _out_kernel(mg_ref, w_ref, x_ref, fw_ref, o_ref, *, rows, sumsq, n_tiles):
    i = pl.program_id(0)
    n = pl.program_id(1)

    def emit_previous():
        scale = lax.rsqrt(sumsq[(i + 1) % 2] * (1.0 / D_MODEL) + NORM_EPS)
        o_ref[...] = rows[n] * scale * fw_ref[...]

    def project():
        xn = x_ref[...] + _dot(mg_ref[...], w_ref[...])
        rows[n] = xn
        part = jnp.sum(xn * xn, axis=-1, keepdims=True)
        cur = i % 2
        sumsq[cur] = jnp.where(n == 0, part, sumsq[cur] + part)

    @pl.when(i == 0)
    def _():
        project()

    @pl.when((i > 0) & (i < n_tiles))
    def _():
        emit_previous()
        project()

    @pl.when(i == n_tiles)
    def _():
        emit_previous()


def _out_proj(merged, w_out_bf, x2d, final_w):
    m = x2d.shape[0]
    tm = OUT_ROW_TILE
    tn = OUT_COL_BLOCK
    n_tiles = m // tm
    n_blocks = D_MODEL // tn
    last = n_tiles - 1
    in_row = lambda i: jnp.minimum(i, last)
    in_col = lambda i, n: jnp.where(i < n_tiles, n, n_blocks - 1)
    in_specs = [
        pl.BlockSpec((tm, D_MODEL), lambda i, n: (in_row(i), 0),
                     pipeline_mode=pl.Buffered(2, use_lookahead=True)),
        pl.BlockSpec((D_MODEL, tn), lambda i, n: (0, in_col(i, n))),
        pl.BlockSpec((tm, tn), lambda i, n: (in_row(i), in_col(i, n))),
        pl.BlockSpec((1, tn), lambda i, n: (0, n)),
    ]
    out_spec = pl.BlockSpec(
        (tm, tn), lambda i, n: (jnp.maximum(i - 1, 0), jnp.where(i > 0, n, 0)))

    def pipelined(mg_hbm, w_hbm, x_hbm, fw_hbm, o_hbm, rows, sumsq):
        step = functools.partial(_out_kernel, rows=rows, sumsq=sumsq, n_tiles=n_tiles)
        pltpu.emit_pipeline(step, grid=(n_tiles + 1, n_blocks), in_specs=in_specs,
                            out_specs=[out_spec])(mg_hbm, w_hbm, x_hbm, fw_hbm, o_hbm)

    hbm = pl.BlockSpec(memory_space=pl.ANY)
    return pl.pallas_call(
        pipelined,
        in_specs=[hbm, hbm, hbm, hbm],
        out_specs=hbm,
        out_shape=jax.ShapeDtypeStruct((m, D_MODEL), jnp.float32),
        scratch_shapes=[pltpu.VMEM((n_blocks, tm, tn), jnp.float32),
                        pltpu.VMEM((2, tm, 1), jnp.float32)],
        compiler_params=pltpu.CompilerParams(vmem_limit_bytes=VMEM_LIMIT_BYTES),
        name="out_proj_norm",
    )(merged, w_out_bf, x2d, final_w.reshape(1, D_MODEL))


def kernel(x, norm_w, w_in, pool_w, pool_scale, conv_w, conv_b, gate_b, w_branch, w_out,
           final_norm_w):
    b, s, d = x.shape
    assert (s, d) == (SEQ, D_MODEL) and norm_w.shape[0] == 1
    x2d = x.reshape(b * s, d)
    w_in2d = w_in[0]

    nb = N_HALF_BLOCKS
    conv_first_cols = (2 * nb, 3 * nb, 4 * nb, 5 * nb)
    y_pool0, xb, rs, pool_first = _pool_first(x2d, norm_w, w_in2d, pool_w[0], pool_scale[0])
    y_pool_rest, conv_first = _pool_rest(xb, rs, w_in2d, pool_w[0], pool_scale[0],
                                         pool_first, conv_first_cols)
    y_conv, merge_first = _conv_mixer(xb, rs, w_in2d, conv_w[0], conv_b[0],
                                      conv_first, w_branch[0])
    merged, w_out_bf = _merge(xb, rs, y_pool0, y_pool_rest, y_conv, w_in2d, gate_b[0],
                              w_branch[0], w_out[0], merge_first)
    out = _out_proj(merged, w_out_bf, x2d, final_norm_w)
    return out.reshape(b, s, d)
```

```python
import functools

import jax
import jax.numpy as jnp
from jax import lax
from jax.experimental import pallas as pl
from jax.experimental.pallas import tpu as pltpu

D_MODEL = 4096
SEQ = 4096
HALF = D_MODEL // 2
POOL_WINDOWS = (2, 4, 8, 16)
N_GROUPS = len(POOL_WINDOWS)
GROUP_DIM = HALF // N_GROUPS
CONV_K = 3
NORM_EPS = 1e-6

COL_BLOCK = GROUP_DIM
N_HALF_BLOCKS = HALF // COL_BLOCK
N_MODEL_BLOCKS = D_MODEL // COL_BLOCK
OUT_COL_BLOCK = 512
OUT_ROW_TILE = 1024
ROW_TILE = 512
POOL_ROW_TILE = 512
FIRST_STAGE_STEPS = 8
REST_ROW_TILE = 1024
POOL_HALO = 16
POOL_PAD = 16
POOL_TOP = POOL_PAD + POOL_HALO
CONV_HALO = 8

VMEM_LIMIT_BYTES = 62 * 1024 * 1024


def _dot(a, b):
    return jnp.dot(a, b, preferred_element_type=jnp.float32)


def _sigmoid(z):
    return 0.5 * jnp.tanh(0.5 * z) + 0.5


def _silu(z):
    return z * _sigmoid(z)


def _silu_exp(z):
    return z / (1.0 + jnp.exp(-z))


def _params(n_axes):
    return pltpu.CompilerParams(dimension_semantics=("arbitrary",) * n_axes,
                                vmem_limit_bytes=VMEM_LIMIT_BYTES)


def _stage_chunks(slot, i, pairs, fold=None):
    for k, (chunk_ref, buf) in enumerate(pairs):
        if isinstance(chunk_ref, tuple):
            chunk_ref, which = chunk_ref
            chunk = chunk_ref[which]
        else:
            chunk = chunk_ref[...]
        rows, cols = chunk.shape
        r0 = pl.multiple_of(i * rows, rows)
        chunk = chunk.astype(buf.dtype)
        if k == 0 and fold is not None:
            chunk = _dot(chunk, fold).astype(buf.dtype)
        buf[slot, pl.ds(r0, rows), :] = chunk


def _prestaged_maps(n_blocks, n_tiles):
    def chunk(jj, i):
        return jnp.where(jj + 1 < n_blocks, i, n_tiles - 1)

    def staged(jj):
        return jnp.minimum(jj + 1, n_blocks - 1)

    return chunk, staged


def _load_prestaged(first_step, pairs, sems):
    @pl.when(first_step)
    def _():
        copies = [pltpu.make_async_copy(src, buf.at[0], sems.at[k])
                  for k, (src, buf) in enumerate(pairs)]
        for cp in copies:
            cp.start()
        for cp in copies:
            cp.wait()


def _cast_rows(pairs, fold=None):
    for k, (src_ref, dst_ref) in enumerate(pairs):
        rows = src_ref[...].astype(dst_ref.dtype)
        if k == 0 and fold is not None:
            rows = _dot(rows, fold).astype(dst_ref.dtype)
        dst_ref[...] = rows


def _row_rms_scale(x):
    return lax.rsqrt(jnp.mean(x * x, axis=-1, keepdims=True) + NORM_EPS)


def _pool_tile(load_xb, rs, wu, wz, g, seq_tile, ps_ref, y_ref, ubuf, sbufs, ucarry):
    tm = rs.shape[0]
    end = POOL_TOP + tm
    u = rs * _dot(load_xb(), wu)

    ubuf[0:POOL_PAD, :] = jnp.zeros((POOL_PAD, COL_BLOCK), jnp.float32)
    ubuf[POOL_PAD:POOL_TOP, :] = jnp.where(seq_tile != 0, ucarry[...], 0.0)
    ubuf[POOL_TOP:end, :] = u
    ucarry[...] = ubuf[end - POOL_HALO:end, :]
    z = rs * _dot(load_xb(), wz)

    def shifted_sum(src, lo, shift):
        return src[lo:end, :] + src[lo - shift:end - shift, :]

    s2buf = sbufs[0]
    s2buf[8:end, :] = shifted_sum(ubuf, 8, 1)
    if isinstance(g, int):
        assert g == 0
        win = s2buf[POOL_TOP:end, :]
        window = POOL_WINDOWS[0]
    else:
        s4buf = sbufs[1]
        s4buf[16:end, :] = shifted_sum(s2buf, 16, 2)
        s8buf = s2buf
        s8buf[24:end, :] = shifted_sum(s4buf, 24, 4)
        win = jnp.where(g == 1, s4buf[POOL_TOP:end, :],
                        jnp.where(g == 2, s8buf[POOL_TOP:end, :],
                                  shifted_sum(s8buf, POOL_TOP, 8)))
        window = lax.shift_left(jnp.int32(2), g)
    t1 = seq_tile * tm + 1 + lax.broadcasted_iota(jnp.int32, (tm, 1), 0)
    inv_count = 1.0 / jnp.minimum(t1, window).astype(jnp.float32)
    mixed = win * inv_count - ubuf[POOL_TOP:end, :]
    ubuf[POOL_TOP:end, :] = mixed * ps_ref[...]
    y_ref[...] = (ubuf[POOL_TOP:end, :] * _silu(z)).astype(y_ref.dtype)


def _pool_first_kernel(x_ref, nw_ref, wu_ref, wz_ref, nu_ref, nz_ref, pw_ref, npw_ref, ps_ref,
                       y_ref, xb_ref, rs_ref, pu_ref, pz_ref,
                       wu_buf, wz_buf, ubuf, s2buf, ucarry, *, n_stage, tiles_per_seq):
    s = pl.program_id(0)

    @pl.when(s == 0)
    def _():
        ucarry[...] = jnp.zeros_like(ucarry)

    @pl.when(s < n_stage)
    def _():
        _stage_chunks(0, s, [(wu_ref, wu_buf), (wz_ref, wz_buf)],
                      fold=pw_ref[0].astype(jnp.bfloat16))
        _cast_rows([(nu_ref, pu_ref), (nz_ref, pz_ref)], fold=npw_ref[0].astype(jnp.bfloat16))

    @pl.when(s >= n_stage)
    def _():
        x = x_ref[...]
        xb = (x * nw_ref[...]).astype(jnp.bfloat16)
        rs = _row_rms_scale(x)
        xb_ref[...] = xb
        rs_ref[...] = rs
        _pool_tile(lambda: xb, rs, wu_buf[0], wz_buf[0], 0, (s - n_stage) % tiles_per_seq,
                   ps_ref, y_ref, ubuf, (s2buf,), ucarry)


def _pool_first(x2d, norm_row, w_in2d, pool_w, pool_scale):
    m = x2d.shape[0]
    tm = POOL_ROW_TILE
    n_tiles = m // tm
    n_stage = FIRST_STAGE_STEPS
    chunk_rows = D_MODEL // n_stage
    chunk = lambda s: jnp.minimum(s, n_stage - 1)
    tile = lambda s: jnp.maximum(s - n_stage, 0)
    w_chunk = lambda blk: pl.BlockSpec((chunk_rows, COL_BLOCK), lambda s: (chunk(s), blk))
    group_map = lambda g: pl.BlockSpec((1, GROUP_DIM, GROUP_DIM), lambda s: (g, 0, 0))
    side_out = pl.BlockSpec((chunk_rows, COL_BLOCK), lambda s: (chunk(s), 0))
    w_buf = pltpu.VMEM((1, D_MODEL, COL_BLOCK), jnp.bfloat16)
    buf = pltpu.VMEM((POOL_TOP + tm, COL_BLOCK), jnp.float32)
    outs = pl.pallas_call(
        functools.partial(_pool_first_kernel, n_stage=n_stage, tiles_per_seq=SEQ // tm),
        grid=(n_stage + n_tiles,),
        in_specs=[
            pl.BlockSpec((tm, D_MODEL), lambda s: (tile(s), 0)),
            pl.BlockSpec((1, D_MODEL), lambda s: (0, 0)),
            w_chunk(0), w_chunk(N_GROUPS),
            w_chunk(1), w_chunk(N_GROUPS + 1),
            group_map(0), group_map(1),
            pl.BlockSpec((1, COL_BLOCK), lambda s: (0, 0)),
        ],
        out_specs=[pl.BlockSpec((tm, COL_BLOCK), lambda s: (tile(s), 0)),
                   pl.BlockSpec((tm, D_MODEL), lambda s: (tile(s), 0)),
                   pl.BlockSpec((tm, 1), lambda s: (tile(s), 0)),
                   side_out, side_out],
        out_shape=[jax.ShapeDtypeStruct((m, COL_BLOCK), jnp.bfloat16),
                   jax.ShapeDtypeStruct((m, D_MODEL), jnp.bfloat16),
                   jax.ShapeDtypeStruct((m, 1), jnp.float32)]
        + [jax.ShapeDtypeStruct((D_MODEL, COL_BLOCK), jnp.bfloat16)] * 2,
        scratch_shapes=[w_buf, w_buf, buf, buf,
                        pltpu.VMEM((POOL_HALO, COL_BLOCK), jnp.float32)],
        compiler_params=_params(1),
        name="pool_first",
    )(x2d, norm_row, w_in2d, w_in2d, w_in2d, w_in2d, pool_w, pool_w,
      pool_scale.reshape(1, HALF))
    return outs[0], outs[1], outs[2], outs[3:]


def _pool_rest_kernel(xb_ref, rs_ref, wu_ref, wz_ref, pw_ref, ps_ref,
                      wu0_ref, wz0_ref, n0_ref, n1_ref, n2_ref, n3_ref,
                      y_ref, p0_ref, p1_ref, p2_ref, p3_ref,
                      wu_buf, wz_buf, ubuf, s2buf, s4buf, ucarry, sems, *, tiles_per_seq):
    jj = pl.program_id(0)
    i = pl.program_id(1)
    first_step = (jj == 0) & (i == 0)

    @pl.when(first_step)
    def _():
        ucarry[...] = jnp.zeros_like(ucarry)

    _load_prestaged(first_step, [(wu0_ref, wu_buf), (wz0_ref, wz_buf)], sems)

    _cast_rows([(n0_ref, p0_ref), (n1_ref, p1_ref), (n2_ref, p2_ref), (n3_ref, p3_ref)])
    slot = jj % 2
    _pool_tile(lambda: xb_ref[...], rs_ref[...], wu_buf[slot], wz_buf[slot], jj + 1,
               i % tiles_per_seq, ps_ref, y_ref, ubuf, (s2buf, s4buf), ucarry)
    _stage_chunks((jj + 1) % 2, i, [(wu_ref, wu_buf), (wz_ref, wz_buf)],
                  fold=pw_ref[0].astype(jnp.bfloat16))


def _pool_rest(xb, rs, w_in2d, pool_w, pool_scale, first_blocks, conv_blocks):
    m = xb.shape[0]
    tm = REST_ROW_TILE
    nb = N_GROUPS - 1
    n_tiles = m // tm
    chunk_rows = D_MODEL // n_tiles
    chunk, staged = _prestaged_maps(nb, n_tiles)
    w_chunk = lambda first: pl.BlockSpec((chunk_rows, COL_BLOCK),
                                         lambda jj, i: (chunk(jj, i), first + 1 + staged(jj)))
    side_steps = 2 * n_tiles
    side_rows = D_MODEL // side_steps
    side_step = lambda jj, i: jnp.minimum(jj * n_tiles + i, side_steps - 1)
    side_in = lambda blk: pl.BlockSpec((side_rows, COL_BLOCK),
                                       lambda jj, i: (side_step(jj, i), blk))
    side_out = pl.BlockSpec((side_rows, COL_BLOCK), lambda jj, i: (side_step(jj, i), 0))
    n_side = len(conv_blocks)
    hbm = pl.BlockSpec(memory_space=pl.ANY)
    w_buf = pltpu.VMEM((2, D_MODEL, COL_BLOCK), jnp.bfloat16)
    buf = pltpu.VMEM((POOL_TOP + tm, COL_BLOCK), jnp.float32)
    outs = pl.pallas_call(
        functools.partial(_pool_rest_kernel, tiles_per_seq=SEQ // tm),
        grid=(nb, n_tiles),
        in_specs=[
            pl.BlockSpec((tm, D_MODEL), lambda jj, i: (i, 0)),
            pl.BlockSpec((tm, 1), lambda jj, i: (i, 0)),
            w_chunk(0), w_chunk(N_GROUPS),
            pl.BlockSpec((1, GROUP_DIM, GROUP_DIM), lambda jj, i: (1 + staged(jj), 0, 0),
                         pipeline_mode=pl.Buffered(1)),
            pl.BlockSpec((1, COL_BLOCK), lambda jj, i: (0, jj + 1)),
            hbm, hbm,
        ] + [side_in(blk) for blk in conv_blocks],
        out_specs=[pl.BlockSpec((tm, COL_BLOCK), lambda jj, i: (i, jj))] + [side_out] * n_side,
        out_shape=[jax.ShapeDtypeStruct((m, nb * COL_BLOCK), jnp.bfloat16)]
        + [jax.ShapeDtypeStruct((D_MODEL, COL_BLOCK), jnp.bfloat16)] * n_side,
        scratch_shapes=[w_buf, w_buf, buf, buf, buf,
                        pltpu.VMEM((POOL_HALO, COL_BLOCK), jnp.float32),
                        pltpu.SemaphoreType.DMA((2,))],
        compiler_params=_params(2),
        name="pool_rest",
    )(xb, rs, w_in2d, w_in2d, pool_w, pool_scale.reshape(1, HALF),
      *first_blocks, *([w_in2d] * n_side))
    return outs[0], outs[1:]


def _conv_kernel(xb_ref, rs_ref, wu_ref, wb_ref, wc_ref, wz_ref, cw_ref, cb_ref,
                 wu0_ref, wb0_ref, wc0_ref, wz0_ref, n0_ref, n1_ref, n2_ref,
                 y_ref, p0_ref, p1_ref, p2_ref,
                 wu_buf, wb_buf, wc_buf, wz_buf, vbuf, vcarry, sems, *, tiles_per_seq):
    jj = pl.program_id(0)
    i = pl.program_id(1)
    tm = xb_ref.shape[0]
    first_step = (jj == 0) & (i == 0)

    @pl.when(first_step)
    def _():
        vcarry[...] = jnp.zeros_like(vcarry)

    _load_prestaged(first_step, [(wu0_ref, wu_buf), (wb0_ref, wb_buf),
                                 (wc0_ref, wc_buf), (wz0_ref, wz_buf)], sems)

    def stage():
        _stage_chunks((jj + 1) % 2, i, [(wu_ref, wu_buf), (wb_ref, wb_buf),
                                        (wc_ref, wc_buf), (wz_ref, wz_buf)])

    def compute():
        slot = jj % 2
        seq_tile = i % tiles_per_seq
        rs = rs_ref[...]
        rs2 = rs * rs
        u = _dot(xb_ref[...], wu_buf[slot])
        _cast_rows([(n0_ref, p0_ref), (n1_ref, p1_ref)])
        c_gate = _dot(xb_ref[...], wc_buf[slot])
        vbuf[0:CONV_HALO, :] = jnp.where(seq_tile != 0, vcarry[...], 0.0)
        vbuf[CONV_HALO:CONV_HALO + tm, :] = rs2 * (c_gate * u)
        vcarry[...] = vbuf[tm:CONV_HALO + tm, :]
        cw = cw_ref[...]
        y = (cb_ref[...]
             + cw[0:1, :] * vbuf[CONV_HALO - 2:CONV_HALO - 2 + tm, :]
             + cw[1:2, :] * vbuf[CONV_HALO - 1:CONV_HALO - 1 + tm, :]
             + cw[2:3, :] * vbuf[CONV_HALO:CONV_HALO + tm, :])
        gated = (rs * y) * _silu_exp(rs * _dot(xb_ref[...], wz_buf[slot]))
        _cast_rows([(n2_ref, p2_ref)])
        y_ref[...] = (_dot(xb_ref[...], wb_buf[slot]) * gated).astype(y_ref.dtype)

    compute()
    stage()


def _conv_mixer(xb, rs, w_in2d, conv_w, conv_b, first_blocks, w_branch):
    m = xb.shape[0]
    tm = ROW_TILE
    nb = N_HALF_BLOCKS
    n_tiles = m // tm
    chunk_rows = D_MODEL // n_tiles
    chunk, staged = _prestaged_maps(nb, n_tiles)
    w_chunk = lambda first: pl.BlockSpec((chunk_rows, COL_BLOCK),
                                         lambda jj, i: (chunk(jj, i), first + staged(jj)))
    hbm = pl.BlockSpec(memory_space=pl.ANY)
    n_steps = nb * n_tiles
    step = lambda jj, i: jj * n_tiles + i
    gate_rows = D_MODEL // n_steps
    branch_rows = HALF // n_steps
    gate_in = lambda blk: pl.BlockSpec((gate_rows, COL_BLOCK), lambda jj, i: (step(jj, i), blk))
    n_br = w_branch.shape[0]
    branches = pl.BlockSpec((n_br, branch_rows, COL_BLOCK), lambda jj, i: (0, step(jj, i), 0))
    gate_out = pl.BlockSpec((gate_rows, COL_BLOCK), lambda jj, i: (step(jj, i), 0))
    g0_first = 6 * N_HALF_BLOCKS
    g1_first = g0_first + N_MODEL_BLOCKS
    w_buf = pltpu.VMEM((2, D_MODEL, COL_BLOCK), jnp.bfloat16)
    outs = pl.pallas_call(
        functools.partial(_conv_kernel, tiles_per_seq=SEQ // tm),
        grid=(nb, n_tiles),
        in_specs=[
            pl.BlockSpec((tm, D_MODEL), lambda jj, i: (i, 0)),
            pl.BlockSpec((tm, 1), lambda jj, i: (i, 0)),
            w_chunk(2 * nb), w_chunk(3 * nb), w_chunk(4 * nb), w_chunk(5 * nb),
            pl.BlockSpec((CONV_K, COL_BLOCK), lambda jj, i: (0, jj)),
            pl.BlockSpec((1, COL_BLOCK), lambda jj, i: (0, jj)),
            hbm, hbm, hbm, hbm,
            gate_in(g0_first), gate_in(g1_first), branches,
        ],
        out_specs=[pl.BlockSpec((tm, COL_BLOCK), lambda jj, i: (i, jj)),
                   gate_out, gate_out, branches],
        out_shape=[jax.ShapeDtypeStruct((m, HALF), jnp.bfloat16),
                   jax.ShapeDtypeStruct((D_MODEL, COL_BLOCK), jnp.bfloat16),
                   jax.ShapeDtypeStruct((D_MODEL, COL_BLOCK), jnp.bfloat16),
                   jax.ShapeDtypeStruct((n_br, HALF, COL_BLOCK), jnp.bfloat16)],
        scratch_shapes=[w_buf, w_buf, w_buf, w_buf,
                        pltpu.VMEM((CONV_HALO + tm, COL_BLOCK), jnp.float32),
                        pltpu.VMEM((CONV_HALO, COL_BLOCK), jnp.float32),
                        pltpu.SemaphoreType.DMA((4,))],
        compiler_params=_params(2),
        name="conv_mixer",
    )(xb, rs, w_in2d, w_in2d, w_in2d, w_in2d, conv_w,
      conv_b.reshape(1, HALF), *first_blocks, w_in2d, w_in2d, w_branch)
    return outs[0], outs[1:]


def _merge_kernel(xb_ref, rs_ref, yp0_ref, ypr_ref, yc_ref, wg0_ref, wg1_ref, gb_ref,
                  wb_ref, wo_ref, wg00_ref, wg10_ref, wb0_first_ref,
                  o_ref, wo_bf_ref, wg0_buf, wg1_buf, wb0_buf, wb1_buf, sems):
    jj = pl.program_id(0)
    i = pl.program_id(1)

    _load_prestaged((jj == 0) & (i == 0),
                    [(wg00_ref, wg0_buf), (wg10_ref, wg1_buf),
                     (wb0_first_ref.at[0], wb0_buf), (wb0_first_ref.at[1], wb1_buf)], sems)

    def stage():
        _stage_chunks((jj + 1) % 2, i, [(wg0_ref, wg0_buf), (wg1_ref, wg1_buf),
                                        ((wb_ref, 0), wb0_buf), ((wb_ref, 1), wb1_buf)])

    def compute():
        slot = jj % 2
        h = xb_ref[...]
        rs = rs_ref[...]
        g0 = _sigmoid(rs * _dot(h, wg0_buf[slot]) + gb_ref[0:1, :])
        y_pool = jnp.concatenate([yp0_ref[...], ypr_ref[...]], axis=1)
        br0 = _dot(y_pool, wb0_buf[slot])
        g1 = _sigmoid(rs * _dot(h, wg1_buf[slot]) + gb_ref[1:2, :])
        br1 = _dot(yc_ref[...], wb1_buf[slot])
        o_ref[...] = (g0 * br0 + g1 * br1).astype(o_ref.dtype)
        wo_bf_ref[...] = wo_ref[...].astype(wo_bf_ref.dtype)

    compute()
    stage()


def _merge(xb, rs, y_pool0, y_pool_rest, y_conv, w_in2d, gate_b, w_branch, w_out2d,
           first_blocks):
    m = xb.shape[0]
    tm = ROW_TILE
    nb = N_MODEL_BLOCKS
    n_tiles = m // tm
    chunk, staged = _prestaged_maps(nb, n_tiles)
    g0_first = 6 * N_HALF_BLOCKS
    g1_first = g0_first + nb
    gate_chunk = lambda first: pl.BlockSpec(
        (D_MODEL // n_tiles, COL_BLOCK), lambda jj, i: (chunk(jj, i), first + staged(jj)))
    branch_chunks = pl.BlockSpec((w_branch.shape[0], HALF // n_tiles, COL_BLOCK),
                                 lambda jj, i: (0, chunk(jj, i), staged(jj)))
    tile = lambda width: pl.BlockSpec((tm, width), lambda jj, i: (i, 0))
    bias = pl.BlockSpec((2, COL_BLOCK), lambda jj, i: (0, jj))
    wo_rows = D_MODEL // (nb * n_tiles)
    wo_slice = pl.BlockSpec((wo_rows, D_MODEL), lambda jj, i: (jj * n_tiles + i, 0))
    hbm = pl.BlockSpec(memory_space=pl.ANY)
    return pl.pallas_call(
        _merge_kernel,
        grid=(nb, n_tiles),
        in_specs=[tile(D_MODEL), tile(1),
                  tile(COL_BLOCK), tile(HALF - COL_BLOCK), tile(HALF),
                  gate_chunk(g0_first), gate_chunk(g1_first), bias,
                  branch_chunks, wo_slice, hbm, hbm, hbm],
        out_specs=[pl.BlockSpec((tm, COL_BLOCK), lambda jj, i: (i, jj)), wo_slice],
        out_shape=[jax.ShapeDtypeStruct((m, D_MODEL), jnp.bfloat16),
                   jax.ShapeDtypeStruct((D_MODEL, D_MODEL), jnp.bfloat16)],
        scratch_shapes=[pltpu.VMEM((2, D_MODEL, COL_BLOCK), jnp.bfloat16),
                        pltpu.VMEM((2, D_MODEL, COL_BLOCK), jnp.bfloat16),
                        pltpu.VMEM((2, HALF, COL_BLOCK), jnp.bfloat16),
                        pltpu.VMEM((2, HALF, COL_BLOCK), jnp.bfloat16),
                        pltpu.SemaphoreType.DMA((4,))],
        compiler_params=_params(2),
        name="branch_merge",
    )(xb, rs, y_pool0, y_pool_rest, y_conv, w_in2d, w_in2d, gate_b,
      w_branch, w_out2d, *first_blocks)


def _out_kernel(mg_ref, w_ref, x_ref, fw_ref, o_ref, *, rows, sumsq, n_tiles):
    i = pl.program_id(0)
    n = pl.program_id(1)

    def emit_previous():
        scale = lax.rsqrt(sumsq[(i + 1) % 2] * (1.0 / D_MODEL) + NORM_EPS)
        o_ref[...] = rows[n] * scale * fw_ref[...]

    def project():
        xn = x_ref[...] + _dot(mg_ref[...], w_ref[...])
        rows[n] = xn
        part = jnp.sum(xn * xn, axis=-1, keepdims=True)
        cur = i % 2
        sumsq[cur] = jnp.where(n == 0, part, sumsq[cur] + part)

    @pl.when(i == 0)
    def _():
        project()

    @pl.when((i > 0) & (i < n_tiles))
    def _():
        emit_previous()
        project()

    @pl.when(i == n_tiles)
    def _():
        emit_previous()


def _out_proj(merged, w_out_bf, x2d, final_w):
    m = x2d.shape[0]
    tm = OUT_ROW_TILE
    tn = OUT_COL_BLOCK
    n_tiles = m // tm
    n_blocks = D_MODEL // tn
    last = n_tiles - 1
    in_row = lambda i: jnp.minimum(i, last)
    in_col = lambda i, n: jnp.where(i < n_tiles, n, n_blocks - 1)
    in_specs = [
        pl.BlockSpec((tm, D_MODEL), lambda i, n: (in_row(i), 0),
                     pipeline_mode=pl.Buffered(2, use_lookahead=True)),
        pl.BlockSpec((D_MODEL, tn), lambda i, n: (0, in_col(i, n)),
                     pipeline_mode=pl.Buffered(3)),
        pl.BlockSpec((tm, tn), lambda i, n: (in_row(i), in_col(i, n))),
        pl.BlockSpec((1, tn), lambda i, n: (0, n)),
    ]
    out_spec = pl.BlockSpec(
        (tm, tn), lambda i, n: (jnp.maximum(i - 1, 0), jnp.where(i > 0, n, 0)))

    def pipelined(mg_hbm, w_hbm, x_hbm, fw_hbm, o_hbm, rows, sumsq):
        step = functools.partial(_out_kernel, rows=rows, sumsq=sumsq, n_tiles=n_tiles)
        pltpu.emit_pipeline(step, grid=(n_tiles + 1, n_blocks), in_specs=in_specs,
                            out_specs=[out_spec])(mg_hbm, w_hbm, x_hbm, fw_hbm, o_hbm)

    hbm = pl.BlockSpec(memory_space=pl.ANY)
    return pl.pallas_call(
        pipelined,
        in_specs=[hbm, hbm, hbm, hbm],
        out_specs=hbm,
        out_shape=jax.ShapeDtypeStruct((m, D_MODEL), jnp.float32),
        scratch_shapes=[pltpu.VMEM((n_blocks, tm, tn), jnp.float32),
                        pltpu.VMEM((2, tm, 1), jnp.float32)],
        compiler_params=pltpu.CompilerParams(vmem_limit_bytes=VMEM_LIMIT_BYTES),
        name="out_proj_norm",
    )(merged, w_out_bf, x2d, final_w.reshape(1, D_MODEL))


def kernel(x, norm_w, w_in, pool_w, pool_scale, conv_w, conv_b, gate_b, w_branch, w_out,
           final_norm_w):
    b, s, d = x.shape
    assert (s, d) == (SEQ, D_MODEL) and norm_w.shape[0] == 1
    x2d = x.reshape(b * s, d)
    w_in2d = w_in[0]

    nb = N_HALF_BLOCKS
    conv_first_cols = (2 * nb, 3 * nb, 4 * nb, 5 * nb)
    y_pool0, xb, rs, pool_first = _pool_first(x2d, norm_w, w_in2d, pool_w[0], pool_scale[0])
    y_pool_rest, conv_first = _pool_rest(xb, rs, w_in2d, pool_w[0], pool_scale[0],
                                         pool_first, conv_first_cols)
    y_conv, merge_first = _conv_mixer(xb, rs, w_in2d, conv_w[0], conv_b[0],
                                      conv_first, w_branch[0])
    merged, w_out_bf = _merge(xb, rs, y_pool0, y_pool_rest, y_conv, w_in2d, gate_b[0],
                              w_branch[0], w_out[0], merge_first)
    out = _out_proj(merged, w_out_bf, x2d, final_norm_w)
    return out.reshape(b, s, d)
```

```python
import functools

import jax
import jax.numpy as jnp
from jax import lax
from jax.experimental import pallas as pl
from jax.experimental.pallas import tpu as pltpu

D_MODEL = 4096
SEQ = 4096
HALF = D_MODEL // 2
POOL_WINDOWS = (2, 4, 8, 16)
N_GROUPS = len(POOL_WINDOWS)
GROUP_DIM = HALF // N_GROUPS
CONV_K = 3
NORM_EPS = 1e-6

COL_BLOCK = GROUP_DIM
N_HALF_BLOCKS = HALF // COL_BLOCK
N_MODEL_BLOCKS = D_MODEL // COL_BLOCK
OUT_COL_BLOCK = 512
OUT_ROW_TILE = 1024
ROW_TILE = 512
POOL_ROW_TILE = 512
FIRST_STAGE_STEPS = 8
REST_ROW_TILE = 1024
POOL_HALO = 16
POOL_PAD = 16
POOL_TOP = POOL_PAD + POOL_HALO
CONV_HALO = 8

VMEM_LIMIT_BYTES = 62 * 1024 * 1024


def _dot(a, b):
    return jnp.dot(a, b, preferred_element_type=jnp.float32)


def _sigmoid(z):
    return 0.5 * jnp.tanh(0.5 * z) + 0.5


def _silu(z):
    return z * _sigmoid(z)


def _silu_exp(z):
    return z / (1.0 + jnp.exp(-z))


def _params(n_axes):
    return pltpu.CompilerParams(dimension_semantics=("arbitrary",) * n_axes,
                                vmem_limit_bytes=VMEM_LIMIT_BYTES)


def _stage_chunks(slot, i, pairs, fold=None):
    for k, (chunk_ref, buf) in enumerate(pairs):
        if isinstance(chunk_ref, tuple):
            chunk_ref, which = chunk_ref
            chunk = chunk_ref[which]
        else:
            chunk = chunk_ref[...]
        rows, cols = chunk.shape
        r0 = pl.multiple_of(i * rows, rows)
        chunk = chunk.astype(buf.dtype)
        if k == 0 and fold is not None:
            chunk = _dot(chunk, fold).astype(buf.dtype)
        buf[slot, pl.ds(r0, rows), :] = chunk


def _prestaged_maps(n_blocks, n_tiles):
    def chunk(jj, i):
        return jnp.where(jj + 1 < n_blocks, i, n_tiles - 1)

    def staged(jj):
        return jnp.minimum(jj + 1, n_blocks - 1)

    return chunk, staged


def _load_prestaged(first_step, pairs, sems):
    @pl.when(first_step)
    def _():
        copies = [pltpu.make_async_copy(src, buf.at[0], sems.at[k])
                  for k, (src, buf) in enumerate(pairs)]
        for cp in copies:
            cp.start()
        for cp in copies:
            cp.wait()


def _cast_rows(pairs, fold=None):
    for k, (src_ref, dst_ref) in enumerate(pairs):
        rows = src_ref[...].astype(dst_ref.dtype)
        if k == 0 and fold is not None:
            rows = _dot(rows, fold).astype(dst_ref.dtype)
        dst_ref[...] = rows


def _row_rms_scale(x):
    return lax.rsqrt(jnp.mean(x * x, axis=-1, keepdims=True) + NORM_EPS)


def _pool_tile(load_xb, rs, wu, wz, g, seq_tile, ps_ref, y_ref, ubuf, sbufs, ucarry):
    tm = rs.shape[0]
    end = POOL_TOP + tm
    u = rs * _dot(load_xb(), wu)

    ubuf[0:POOL_PAD, :] = jnp.zeros((POOL_PAD, COL_BLOCK), jnp.float32)
    ubuf[POOL_PAD:POOL_TOP, :] = jnp.where(seq_tile != 0, ucarry[...], 0.0)
    ubuf[POOL_TOP:end, :] = u
    ucarry[...] = ubuf[end - POOL_HALO:end, :]
    z = rs * _dot(load_xb(), wz)

    def shifted_sum(src, lo, shift):
        return src[lo:end, :] + src[lo - shift:end - shift, :]

    s2buf = sbufs[0]
    s2buf[8:end, :] = shifted_sum(ubuf, 8, 1)
    if isinstance(g, int):
        assert g == 0
        win = s2buf[POOL_TOP:end, :]
        window = POOL_WINDOWS[0]
    else:
        s4buf = sbufs[1]
        s4buf[16:end, :] = shifted_sum(s2buf, 16, 2)
        s8buf = s2buf
        s8buf[24:end, :] = shifted_sum(s4buf, 24, 4)
        win = jnp.where(g == 1, s4buf[POOL_TOP:end, :],
                        jnp.where(g == 2, s8buf[POOL_TOP:end, :],
                                  shifted_sum(s8buf, POOL_TOP, 8)))
        window = lax.shift_left(jnp.int32(2), g)
    t1 = seq_tile * tm + 1 + lax.broadcasted_iota(jnp.int32, (tm, 1), 0)
    inv_count = 1.0 / jnp.minimum(t1, window).astype(jnp.float32)
    mixed = win * inv_count - ubuf[POOL_TOP:end, :]
    ubuf[POOL_TOP:end, :] = mixed * ps_ref[...]
    y_ref[...] = (ubuf[POOL_TOP:end, :] * _silu(z)).astype(y_ref.dtype)


def _pool_first_kernel(x_ref, nw_ref, wu_ref, wz_ref, nu_ref, nz_ref, pw_ref, npw_ref, ps_ref,
                       y_ref, xb_ref, rs_ref, pu_ref, pz_ref,
                       wu_buf, wz_buf, ubuf, s2buf, ucarry, *, n_stage, tiles_per_seq):
    s = pl.program_id(0)

    @pl.when(s == 0)
    def _():
        ucarry[...] = jnp.zeros_like(ucarry)

    @pl.when(s < n_stage)
    def _():
        _stage_chunks(0, s, [(wu_ref, wu_buf), (wz_ref, wz_buf)],
                      fold=pw_ref[0].astype(jnp.bfloat16))
        _cast_rows([(nu_ref, pu_ref), (nz_ref, pz_ref)], fold=npw_ref[0].astype(jnp.bfloat16))

    @pl.when(s >= n_stage)
    def _():
        x = x_ref[...]
        xb = (x * nw_ref[...]).astype(jnp.bfloat16)
        rs = _row_rms_scale(x)
        xb_ref[...] = xb
        rs_ref[...] = rs
        _pool_tile(lambda: xb, rs, wu_buf[0], wz_buf[0], 0, (s - n_stage) % tiles_per_seq,
                   ps_ref, y_ref, ubuf, (s2buf,), ucarry)


def _pool_first(x2d, norm_row, w_in2d, pool_w, pool_scale):
    m = x2d.shape[0]
    tm = POOL_ROW_TILE
    n_tiles = m // tm
    n_stage = FIRST_STAGE_STEPS
    chunk_rows = D_MODEL // n_stage
    chunk = lambda s: jnp.minimum(s, n_stage - 1)
    tile = lambda s: jnp.maximum(s - n_stage, 0)
    w_chunk = lambda blk: pl.BlockSpec((chunk_rows, COL_BLOCK), lambda s: (chunk(s), blk))
    group_map = lambda g: pl.BlockSpec((1, GROUP_DIM, GROUP_DIM), lambda s: (g, 0, 0))
    side_out = pl.BlockSpec((chunk_rows, COL_BLOCK), lambda s: (chunk(s), 0))
    w_buf = pltpu.VMEM((1, D_MODEL, COL_BLOCK), jnp.bfloat16)
    buf = pltpu.VMEM((POOL_TOP + tm, COL_BLOCK), jnp.float32)
    in_specs = [
        pl.BlockSpec((tm, D_MODEL), lambda s: (tile(s), 0)),
        pl.BlockSpec((1, D_MODEL), lambda s: (0, 0)),
        w_chunk(0), w_chunk(N_GROUPS),
        w_chunk(1), w_chunk(N_GROUPS + 1),
        group_map(0), group_map(1),
        pl.BlockSpec((1, COL_BLOCK), lambda s: (0, 0)),
    ]
    out_specs = [pl.BlockSpec((tm, COL_BLOCK), lambda s: (tile(s), 0)),
                 pl.BlockSpec((tm, D_MODEL), lambda s: (tile(s), 0)),
                 pl.BlockSpec((tm, 1), lambda s: (tile(s), 0)),
                 side_out, side_out]
    n_in, n_out = len(in_specs), len(out_specs)

    def pipelined(*refs):
        scratch = refs[n_in + n_out:]
        step = lambda *blocks: _pool_first_kernel(*blocks, *scratch, n_stage=n_stage,
                                                  tiles_per_seq=SEQ // tm)
        pltpu.emit_pipeline(step, grid=(n_stage + n_tiles,), in_specs=in_specs,
                            out_specs=out_specs)(*refs[:n_in + n_out])

    hbm = pl.BlockSpec(memory_space=pl.ANY)
    outs = pl.pallas_call(
        pipelined,
        in_specs=[hbm] * n_in,
        out_specs=[hbm] * n_out,
        out_shape=[jax.ShapeDtypeStruct((m, COL_BLOCK), jnp.bfloat16),
                   jax.ShapeDtypeStruct((m, D_MODEL), jnp.bfloat16),
                   jax.ShapeDtypeStruct((m, 1), jnp.float32)]
        + [jax.ShapeDtypeStruct((D_MODEL, COL_BLOCK), jnp.bfloat16)] * 2,
        scratch_shapes=[w_buf, w_buf, buf, buf,
                        pltpu.VMEM((POOL_HALO, COL_BLOCK), jnp.float32)],
        compiler_params=pltpu.CompilerParams(vmem_limit_bytes=VMEM_LIMIT_BYTES),
        name="pool_first",
    )(x2d, norm_row, w_in2d, w_in2d, w_in2d, w_in2d, pool_w, pool_w,
      pool_scale.reshape(1, HALF))
    return outs[0], outs[1], outs[2], outs[3:]


def _pool_rest_kernel(xb_ref, rs_ref, wu_ref, wz_ref, pw_ref, ps_ref,
                      wu0_ref, wz0_ref, n0_ref, n1_ref, n2_ref, n3_ref,
                      y_ref, p0_ref, p1_ref, p2_ref, p3_ref,
                      wu_buf, wz_buf, ubuf, s2buf, s4buf, ucarry, sems, *, tiles_per_seq):
    jj = pl.program_id(0)
    i = pl.program_id(1)
    first_step = (jj == 0) & (i == 0)

    @pl.when(first_step)
    def _():
        ucarry[...] = jnp.zeros_like(ucarry)

    _load_prestaged(first_step, [(wu0_ref, wu_buf), (wz0_ref, wz_buf)], sems)

    _cast_rows([(n0_ref, p0_ref), (n1_ref, p1_ref), (n2_ref, p2_ref), (n3_ref, p3_ref)])
    slot = jj % 2
    _pool_tile(lambda: xb_ref[...], rs_ref[...], wu_buf[slot], wz_buf[slot], jj + 1,
               i % tiles_per_seq, ps_ref, y_ref, ubuf, (s2buf, s4buf), ucarry)
    _stage_chunks((jj + 1) % 2, i, [(wu_ref, wu_buf), (wz_ref, wz_buf)],
                  fold=pw_ref[0].astype(jnp.bfloat16))


def _pool_rest(xb, rs, w_in2d, pool_w, pool_scale, first_blocks, conv_blocks):
    m = xb.shape[0]
    tm = REST_ROW_TILE
    nb = N_GROUPS - 1
    n_tiles = m // tm
    chunk_rows = D_MODEL // n_tiles
    chunk, staged = _prestaged_maps(nb, n_tiles)
    w_chunk = lambda first: pl.BlockSpec((chunk_rows, COL_BLOCK),
                                         lambda jj, i: (chunk(jj, i), first + 1 + staged(jj)))
    side_steps = 2 * n_tiles
    side_rows = D_MODEL // side_steps
    side_step = lambda jj, i: jnp.minimum(jj * n_tiles + i, side_steps - 1)
    side_in = lambda blk: pl.BlockSpec((side_rows, COL_BLOCK),
                                       lambda jj, i: (side_step(jj, i), blk))
    side_out = pl.BlockSpec((side_rows, COL_BLOCK), lambda jj, i: (side_step(jj, i), 0))
    n_side = len(conv_blocks)
    hbm = pl.BlockSpec(memory_space=pl.ANY)
    w_buf = pltpu.VMEM((2, D_MODEL, COL_BLOCK), jnp.bfloat16)
    buf = pltpu.VMEM((POOL_TOP + tm, COL_BLOCK), jnp.float32)
    outs = pl.pallas_call(
        functools.partial(_pool_rest_kernel, tiles_per_seq=SEQ // tm),
        grid=(nb, n_tiles),
        in_specs=[
            pl.BlockSpec((tm, D_MODEL), lambda jj, i: (i, 0)),
            pl.BlockSpec((tm, 1), lambda jj, i: (i, 0)),
            w_chunk(0), w_chunk(N_GROUPS),
            pl.BlockSpec((1, GROUP_DIM, GROUP_DIM), lambda jj, i: (1 + staged(jj), 0, 0),
                         pipeline_mode=pl.Buffered(1)),
            pl.BlockSpec((1, COL_BLOCK), lambda jj, i: (0, jj + 1)),
            hbm, hbm,
        ] + [side_in(blk) for blk in conv_blocks],
        out_specs=[pl.BlockSpec((tm, COL_BLOCK), lambda jj, i: (i, jj))] + [side_out] * n_side,
        out_shape=[jax.ShapeDtypeStruct((m, nb * COL_BLOCK), jnp.bfloat16)]
        + [jax.ShapeDtypeStruct((D_MODEL, COL_BLOCK), jnp.bfloat16)] * n_side,
        scratch_shapes=[w_buf, w_buf, buf, buf, buf,
                        pltpu.VMEM((POOL_HALO, COL_BLOCK), jnp.float32),
                        pltpu.SemaphoreType.DMA((2,))],
        compiler_params=_params(2),
        name="pool_rest",
    )(xb, rs, w_in2d, w_in2d, pool_w, pool_scale.reshape(1, HALF),
      *first_blocks, *([w_in2d] * n_side))
    return outs[0], outs[1:]


def _conv_kernel(xb_ref, rs_ref, wu_ref, wb_ref, wc_ref, wz_ref, cw_ref, cb_ref,
                 wu0_ref, wb0_ref, wc0_ref, wz0_ref, n0_ref, n1_ref, n2_ref,
                 y_ref, p0_ref, p1_ref, p2_ref,
                 wu_buf, wb_buf, wc_buf, wz_buf, vbuf, vcarry, sems, *, tiles_per_seq):
    jj = pl.program_id(0)
    i = pl.program_id(1)
    tm = xb_ref.shape[0]
    first_step = (jj == 0) & (i == 0)

    @pl.when(first_step)
    def _():
        vcarry[...] = jnp.zeros_like(vcarry)

    _load_prestaged(first_step, [(wu0_ref, wu_buf), (wb0_ref, wb_buf),
                                 (wc0_ref, wc_buf), (wz0_ref, wz_buf)], sems)

    def stage():
        _stage_chunks((jj + 1) % 2, i, [(wu_ref, wu_buf), (wb_ref, wb_buf),
                                        (wc_ref, wc_buf), (wz_ref, wz_buf)])

    def compute():
        slot = jj % 2
        seq_tile = i % tiles_per_seq
        rs = rs_ref[...]
        rs2 = rs * rs
        u = _dot(xb_ref[...], wu_buf[slot])
        _cast_rows([(n0_ref, p0_ref), (n1_ref, p1_ref)])
        c_gate = _dot(xb_ref[...], wc_buf[slot])
        vbuf[0:CONV_HALO, :] = jnp.where(seq_tile != 0, vcarry[...], 0.0)
        vbuf[CONV_HALO:CONV_HALO + tm, :] = rs2 * (c_gate * u)
        vcarry[...] = vbuf[tm:CONV_HALO + tm, :]
        cw = cw_ref[...]
        y = (cb_ref[...]
             + cw[0:1, :] * vbuf[CONV_HALO - 2:CONV_HALO - 2 + tm, :]
             + cw[1:2, :] * vbuf[CONV_HALO - 1:CONV_HALO - 1 + tm, :]
             + cw[2:3, :] * vbuf[CONV_HALO:CONV_HALO + tm, :])
        gated = (rs * y) * _silu_exp(rs * _dot(xb_ref[...], wz_buf[slot]))
        _cast_rows([(n2_ref, p2_ref)])
        y_ref[...] = (_dot(xb_ref[...], wb_buf[slot]) * gated).astype(y_ref.dtype)

    compute()
    stage()


def _conv_mixer(xb, rs, w_in2d, conv_w, conv_b, first_blocks, w_branch):
    m = xb.shape[0]
    tm = ROW_TILE
    nb = N_HALF_BLOCKS
    n_tiles = m // tm
    chunk_rows = D_MODEL // n_tiles
    chunk, staged = _prestaged_maps(nb, n_tiles)
    w_chunk = lambda first: pl.BlockSpec((chunk_rows, COL_BLOCK),
                                         lambda jj, i: (chunk(jj, i), first + staged(jj)))
    hbm = pl.BlockSpec(memory_space=pl.ANY)
    n_steps = nb * n_tiles
    step = lambda jj, i: jj * n_tiles + i
    gate_rows = D_MODEL // n_steps
    branch_rows = HALF // n_steps
    gate_in = lambda blk: pl.BlockSpec((gate_rows, COL_BLOCK), lambda jj, i: (step(jj, i), blk))
    n_br = w_branch.shape[0]
    branches = pl.BlockSpec((n_br, branch_rows, COL_BLOCK), lambda jj, i: (0, step(jj, i), 0))
    gate_out = pl.BlockSpec((gate_rows, COL_BLOCK), lambda jj, i: (step(jj, i), 0))
    g0_first = 6 * N_HALF_BLOCKS
    g1_first = g0_first + N_MODEL_BLOCKS
    w_buf = pltpu.VMEM((2, D_MODEL, COL_BLOCK), jnp.bfloat16)
    outs = pl.pallas_call(
        functools.partial(_conv_kernel, tiles_per_seq=SEQ // tm),
        grid=(nb, n_tiles),
        in_specs=[
            pl.BlockSpec((tm, D_MODEL), lambda jj, i: (i, 0)),
            pl.BlockSpec((tm, 1), lambda jj, i: (i, 0)),
            w_chunk(2 * nb), w_chunk(3 * nb), w_chunk(4 * nb), w_chunk(5 * nb),
            pl.BlockSpec((CONV_K, COL_BLOCK), lambda jj, i: (0, jj)),
            pl.BlockSpec((1, COL_BLOCK), lambda jj, i: (0, jj)),
            hbm, hbm, hbm, hbm,
            gate_in(g0_first), gate_in(g1_first), branches,
        ],
        out_specs=[pl.BlockSpec((tm, COL_BLOCK), lambda jj, i: (i, jj)),
                   gate_out, gate_out, branches],
        out_shape=[jax.ShapeDtypeStruct((m, HALF), jnp.bfloat16),
                   jax.ShapeDtypeStruct((D_MODEL, COL_BLOCK), jnp.bfloat16),
                   jax.ShapeDtypeStruct((D_MODEL, COL_BLOCK), jnp.bfloat16),
                   jax.ShapeDtypeStruct((n_br, HALF, COL_BLOCK), jnp.bfloat16)],
        scratch_shapes=[w_buf, w_buf, w_buf, w_buf,
                        pltpu.VMEM((CONV_HALO + tm, COL_BLOCK), jnp.float32),
                        pltpu.VMEM((CONV_HALO, COL_BLOCK), jnp.float32),
                        pltpu.SemaphoreType.DMA((4,))],
        compiler_params=_params(2),
        name="conv_mixer",
    )(xb, rs, w_in2d, w_in2d, w_in2d, w_in2d, conv_w,
      conv_b.reshape(1, HALF), *first_blocks, w_in2d, w_in2d, w_branch)
    return outs[0], outs[1:]


def _merge_kernel(xb_ref, rs_ref, yp0_ref, ypr_ref, yc_ref, wg0_ref, wg1_ref, gb_ref,
                  wb_ref, wo_ref, wg00_ref, wg10_ref, wb0_first_ref,
                  o_ref, wo_bf_ref, wg0_buf, wg1_buf, wb0_buf, wb1_buf, sems):
    jj = pl.program_id(0)
    i = pl.program_id(1)

    _load_prestaged((jj == 0) & (i == 0),
                    [(wg00_ref, wg0_buf), (wg10_ref, wg1_buf),
                     (wb0_first_ref.at[0], wb0_buf), (wb0_first_ref.at[1], wb1_buf)], sems)

    def stage():
        _stage_chunks((jj + 1) % 2, i, [(wg0_ref, wg0_buf), (wg1_ref, wg1_buf),
                                        ((wb_ref, 0), wb0_buf), ((wb_ref, 1), wb1_buf)])

    def compute():
        slot = jj % 2
        h = xb_ref[...]
        rs = rs_ref[...]
        g0 = _sigmoid(rs * _dot(h, wg0_buf[slot]) + gb_ref[0:1, :])
        y_pool = jnp.concatenate([yp0_ref[...], ypr_ref[...]], axis=1)
        br0 = _dot(y_pool, wb0_buf[slot])
        g1 = _sigmoid(rs * _dot(h, wg1_buf[slot]) + gb_ref[1:2, :])
        br1 = _dot(yc_ref[...], wb1_buf[slot])
        o_ref[...] = (g0 * br0 + g1 * br1).astype(o_ref.dtype)
        wo_bf_ref[...] = wo_ref[...].astype(wo_bf_ref.dtype)

    compute()
    stage()


def _merge(xb, rs, y_pool0, y_pool_rest, y_conv, w_in2d, gate_b, w_branch, w_out2d,
           first_blocks):
    m = xb.shape[0]
    tm = ROW_TILE
    nb = N_MODEL_BLOCKS
    n_tiles = m // tm
    chunk, staged = _prestaged_maps(nb, n_tiles)
    g0_first = 6 * N_HALF_BLOCKS
    g1_first = g0_first + nb
    gate_chunk = lambda first: pl.BlockSpec(
        (D_MODEL // n_tiles, COL_BLOCK), lambda jj, i: (chunk(jj, i), first + staged(jj)))
    branch_chunks = pl.BlockSpec((w_branch.shape[0], HALF // n_tiles, COL_BLOCK),
                                 lambda jj, i: (0, chunk(jj, i), staged(jj)))
    tile = lambda width: pl.BlockSpec((tm, width), lambda jj, i: (i, 0))
    bias = pl.BlockSpec((2, COL_BLOCK), lambda jj, i: (0, jj))
    wo_rows = D_MODEL // (nb * n_tiles)
    wo_slice = pl.BlockSpec((wo_rows, D_MODEL), lambda jj, i: (jj * n_tiles + i, 0))
    hbm = pl.BlockSpec(memory_space=pl.ANY)
    return pl.pallas_call(
        _merge_kernel,
        grid=(nb, n_tiles),
        in_specs=[tile(D_MODEL), tile(1),
                  tile(COL_BLOCK), tile(HALF - COL_BLOCK), tile(HALF),
                  gate_chunk(g0_first), gate_chunk(g1_first), bias,
                  branch_chunks, wo_slice, hbm, hbm, hbm],
        out_specs=[pl.BlockSpec((tm, COL_BLOCK), lambda jj, i: (i, jj)), wo_slice],
        out_shape=[jax.ShapeDtypeStruct((m, D_MODEL), jnp.bfloat16),
                   jax.ShapeDtypeStruct((D_MODEL, D_MODEL), jnp.bfloat16)],
        scratch_shapes=[pltpu.VMEM((2, D_MODEL, COL_BLOCK), jnp.bfloat16),
                        pltpu.VMEM((2, D_MODEL, COL_BLOCK), jnp.bfloat16),
                        pltpu.VMEM((2, HALF, COL_BLOCK), jnp.bfloat16),
                        pltpu.VMEM((2, HALF, COL_BLOCK), jnp.bfloat16),
                        pltpu.SemaphoreType.DMA((4,))],
        compiler_params=_params(2),
        name="branch_merge",
    )(xb, rs, y_pool0, y_pool_rest, y_conv, w_in2d, w_in2d, gate_b,
      w_branch, w_out2d, *first_blocks)


def _out_kernel(mg_ref, w_ref, x_ref, fw_ref, o_ref, *, rows, sumsq, n_tiles):
    i = pl.program_id(0)
    n = pl.program_id(1)

    def emit_previous():
        scale = lax.rsqrt(sumsq[(i + 1) % 2] * (1.0 / D_MODEL) + NORM_EPS)
        o_ref[...] = rows[n] * scale * fw_ref[...]

    def project():
        xn = x_ref[...] + _dot(mg_ref[...], w_ref[...])
        rows[n] = xn
        part = jnp.sum(xn * xn, axis=-1, keepdims=True)
        cur = i % 2
        sumsq[cur] = jnp.where(n == 0, part, sumsq[cur] + part)

    @pl.when(i == 0)
    def _():
        project()

    @pl.when((i > 0) & (i < n_tiles))
    def _():
        emit_previous()
        project()

    @pl.when(i == n_tiles)
    def _():
        emit_previous()


def _out_proj(merged, w_out_bf, x2d, final_w):
    m = x2d.shape[0]
    tm = OUT_ROW_TILE
    tn = OUT_COL_BLOCK
    n_tiles = m // tm
    n_blocks = D_MODEL // tn
    last = n_tiles - 1
    in_row = lambda i: jnp.minimum(i, last)
    in_col = lambda i, n: jnp.where(i < n_tiles, n, n_blocks - 1)
    in_specs = [
        pl.BlockSpec((tm, D_MODEL), lambda i, n: (in_row(i), 0),
                     pipeline_mode=pl.Buffered(2, use_lookahead=True)),
        pl.BlockSpec((D_MODEL, tn), lambda i, n: (0, in_col(i, n)),
                     pipeline_mode=pl.Buffered(3)),
        pl.BlockSpec((tm, tn), lambda i, n: (in_row(i), in_col(i, n))),
        pl.BlockSpec((1, tn), lambda i, n: (0, n)),
    ]
    out_spec = pl.BlockSpec(
        (tm, tn), lambda i, n: (jnp.maximum(i - 1, 0), jnp.where(i > 0, n, 0)))

    def pipelined(mg_hbm, w_hbm, x_hbm, fw_hbm, o_hbm, rows, sumsq):
        step = functools.partial(_out_kernel, rows=rows, sumsq=sumsq, n_tiles=n_tiles)
        pltpu.emit_pipeline(step, grid=(n_tiles + 1, n_blocks), in_specs=in_specs,
                            out_specs=[out_spec])(mg_hbm, w_hbm, x_hbm, fw_hbm, o_hbm)

    hbm = pl.BlockSpec(memory_space=pl.ANY)
    return pl.pallas_call(
        pipelined,
        in_specs=[hbm, hbm, hbm, hbm],
        out_specs=hbm,
        out_shape=jax.ShapeDtypeStruct((m, D_MODEL), jnp.float32),
        scratch_shapes=[pltpu.VMEM((n_blocks, tm, tn), jnp.float32),
                        pltpu.VMEM((2, tm, 1), jnp.float32)],
        compiler_params=pltpu.CompilerParams(vmem_limit_bytes=VMEM_LIMIT_BYTES),
        name="out_proj_norm",
    )(merged, w_out_bf, x2d, final_w.reshape(1, D_MODEL))


def kernel(x, norm_w, w_in, pool_w, pool_scale, conv_w, conv_b, gate_b, w_branch, w_out,
           final_norm_w):
    b, s, d = x.shape
    assert (s, d) == (SEQ, D_MODEL) and norm_w.shape[0] == 1
    x2d = x.reshape(b * s, d)
    w_in2d = w_in[0]

    nb = N_HALF_BLOCKS
    conv_first_cols = (2 * nb, 3 * nb, 4 * nb, 5 * nb)
    y_pool0, xb, rs, pool_first = _pool_first(x2d, norm_w, w_in2d, pool_w[0], pool_scale[0])
    y_pool_rest, conv_first = _pool_rest(xb, rs, w_in2d, pool_w[0], pool_scale[0],
                                         pool_first, conv_first_cols)
    y_conv, merge_first = _conv_mixer(xb, rs, w_in2d, conv_w[0], conv_b[0],
                                      conv_first, w_branch[0])
    merged, w_out_bf = _merge(xb, rs, y_pool0, y_pool_rest, y_conv, w_in2d, gate_b[0],
                              w_branch[0], w_out[0], merge_first)
    out = _out_proj(merged, w_out_bf, x2d, final_norm_w)
    return out.reshape(b, s, d)
```

```python
import functools

import jax
import jax.numpy as jnp
from jax import lax
from jax.experimental import pallas as pl
from jax.experimental.pallas import tpu as pltpu

D_MODEL = 4096
SEQ = 4096
HALF = D_MODEL // 2
POOL_WINDOWS = (2, 4, 8, 16)
N_GROUPS = len(POOL_WINDOWS)
GROUP_DIM = HALF // N_GROUPS
CONV_K = 3
NORM_EPS = 1e-6

COL_BLOCK = GROUP_DIM
N_HALF_BLOCKS = HALF // COL_BLOCK
N_MODEL_BLOCKS = D_MODEL // COL_BLOCK
OUT_COL_BLOCK = 512
OUT_ROW_TILE = 1024
ROW_TILE = 512
POOL_ROW_TILE = 512
FIRST_STAGE_STEPS = 8
REST_ROW_TILE = 1024
POOL_HALO = 16
POOL_PAD = 16
POOL_TOP = POOL_PAD + POOL_HALO
CONV_HALO = 8

VMEM_LIMIT_BYTES = 62 * 1024 * 1024


def _dot(a, b):
    return jnp.dot(a, b, preferred_element_type=jnp.float32)


def _sigmoid(z):
    return 0.5 * jnp.tanh(0.5 * z) + 0.5


def _silu(z):
    return z * _sigmoid(z)


def _silu_exp(z):
    return z / (1.0 + jnp.exp(-z))


def _params(n_axes):
    return pltpu.CompilerParams(dimension_semantics=("arbitrary",) * n_axes,
                                vmem_limit_bytes=VMEM_LIMIT_BYTES)


def _stage_chunks(slot, i, pairs, fold=None):
    for k, (chunk_ref, buf) in enumerate(pairs):
        if isinstance(chunk_ref, tuple):
            chunk_ref, which = chunk_ref
            chunk = chunk_ref[which]
        else:
            chunk = chunk_ref[...]
        rows, cols = chunk.shape
        r0 = pl.multiple_of(i * rows, rows)
        chunk = chunk.astype(buf.dtype)
        if k == 0 and fold is not None:
            chunk = _dot(chunk, fold).astype(buf.dtype)
        buf[slot, pl.ds(r0, rows), :] = chunk


def _prestaged_maps(n_blocks, n_tiles):
    def chunk(jj, i):
        return jnp.where(jj + 1 < n_blocks, i, n_tiles - 1)

    def staged(jj):
        return jnp.minimum(jj + 1, n_blocks - 1)

    return chunk, staged


def _load_prestaged(first_step, pairs, sems):
    @pl.when(first_step)
    def _():
        copies = [pltpu.make_async_copy(src, buf.at[0], sems.at[k])
                  for k, (src, buf) in enumerate(pairs)]
        for cp in copies:
            cp.start()
        for cp in copies:
            cp.wait()


def _cast_rows(pairs, fold=None):
    for k, (src_ref, dst_ref) in enumerate(pairs):
        rows = src_ref[...].astype(dst_ref.dtype)
        if k == 0 and fold is not None:
            rows = _dot(rows, fold).astype(dst_ref.dtype)
        dst_ref[...] = rows


def _row_rms_scale(x):
    return lax.rsqrt(jnp.mean(x * x, axis=-1, keepdims=True) + NORM_EPS)


def _pool_tile(load_xb, rs, wu, wz, g, seq_tile, ps_ref, y_ref, ubuf, sbufs, ucarry):
    tm = rs.shape[0]
    end = POOL_TOP + tm
    u = rs * _dot(load_xb(), wu)

    ubuf[0:POOL_PAD, :] = jnp.zeros((POOL_PAD, COL_BLOCK), jnp.float32)
    ubuf[POOL_PAD:POOL_TOP, :] = jnp.where(seq_tile != 0, ucarry[...], 0.0)
    ubuf[POOL_TOP:end, :] = u
    ucarry[...] = ubuf[end - POOL_HALO:end, :]
    z = rs * _dot(load_xb(), wz)

    def shifted_sum(src, lo, shift):
        return src[lo:end, :] + src[lo - shift:end - shift, :]

    s2buf = sbufs[0]
    s2buf[8:end, :] = shifted_sum(ubuf, 8, 1)
    if isinstance(g, int):
        assert g == 0
        win = s2buf[POOL_TOP:end, :]
        window = POOL_WINDOWS[0]
    else:
        s4buf = sbufs[1]
        s4buf[16:end, :] = shifted_sum(s2buf, 16, 2)
        s8buf = s2buf
        s8buf[24:end, :] = shifted_sum(s4buf, 24, 4)
        win = jnp.where(g == 1, s4buf[POOL_TOP:end, :],
                        jnp.where(g == 2, s8buf[POOL_TOP:end, :],
                                  shifted_sum(s8buf, POOL_TOP, 8)))
        window = lax.shift_left(jnp.int32(2), g)
    t1 = seq_tile * tm + 1 + lax.broadcasted_iota(jnp.int32, (tm, 1), 0)
    inv_count = 1.0 / jnp.minimum(t1, window).astype(jnp.float32)
    mixed = win * inv_count - ubuf[POOL_TOP:end, :]
    ubuf[POOL_TOP:end, :] = mixed * ps_ref[...]
    y_ref[...] = (ubuf[POOL_TOP:end, :] * _silu(z)).astype(y_ref.dtype)


def _pool_first_kernel(x_ref, nw_ref, wu_ref, wz_ref, nu_ref, nz_ref, pw_ref, npw_ref, ps_ref,
                       y_ref, xb_ref, rs_ref, pu_ref, pz_ref,
                       wu_buf, wz_buf, ubuf, s2buf, ucarry, *, n_stage, tiles_per_seq):
    s = pl.program_id(0)

    @pl.when(s == 0)
    def _():
        ucarry[...] = jnp.zeros_like(ucarry)

    @pl.when(s < n_stage)
    def _():
        _stage_chunks(0, s, [(wu_ref, wu_buf), (wz_ref, wz_buf)],
                      fold=pw_ref[0].astype(jnp.bfloat16))
        _cast_rows([(nu_ref, pu_ref), (nz_ref, pz_ref)], fold=npw_ref[0].astype(jnp.bfloat16))

    @pl.when(s >= n_stage)
    def _():
        x = x_ref[...]
        xb = (x * nw_ref[...]).astype(jnp.bfloat16)
        rs = _row_rms_scale(x)
        xb_ref[...] = xb
        rs_ref[...] = rs
        _pool_tile(lambda: xb, rs, wu_buf[0], wz_buf[0], 0, (s - n_stage) % tiles_per_seq,
                   ps_ref, y_ref, ubuf, (s2buf,), ucarry)


def _pool_first(x2d, norm_row, w_in2d, pool_w, pool_scale):
    m = x2d.shape[0]
    tm = POOL_ROW_TILE
    n_tiles = m // tm
    n_stage = FIRST_STAGE_STEPS
    chunk_rows = D_MODEL // n_stage
    chunk = lambda s: jnp.minimum(s, n_stage - 1)
    tile = lambda s: jnp.maximum(s - n_stage, 0)
    w_chunk = lambda blk: pl.BlockSpec((chunk_rows, COL_BLOCK), lambda s: (chunk(s), blk))
    group_map = lambda g: pl.BlockSpec((1, GROUP_DIM, GROUP_DIM), lambda s: (g, 0, 0))
    side_out = pl.BlockSpec((chunk_rows, COL_BLOCK), lambda s: (chunk(s), 0))
    w_buf = pltpu.VMEM((1, D_MODEL, COL_BLOCK), jnp.bfloat16)
    buf = pltpu.VMEM((POOL_TOP + tm, COL_BLOCK), jnp.float32)
    in_specs = [
        pl.BlockSpec((tm, D_MODEL), lambda s: (tile(s), 0), pipeline_mode=pl.Buffered(3)),
        pl.BlockSpec((1, D_MODEL), lambda s: (0, 0)),
        w_chunk(0), w_chunk(N_GROUPS),
        w_chunk(1), w_chunk(N_GROUPS + 1),
        group_map(0), group_map(1),
        pl.BlockSpec((1, COL_BLOCK), lambda s: (0, 0)),
    ]
    out_specs = [pl.BlockSpec((tm, COL_BLOCK), lambda s: (tile(s), 0)),
                 pl.BlockSpec((tm, D_MODEL), lambda s: (tile(s), 0)),
                 pl.BlockSpec((tm, 1), lambda s: (tile(s), 0)),
                 side_out, side_out]
    n_in, n_out = len(in_specs), len(out_specs)

    def pipelined(*refs):
        scratch = refs[n_in + n_out:]
        step = lambda *blocks: _pool_first_kernel(*blocks, *scratch, n_stage=n_stage,
                                                  tiles_per_seq=SEQ // tm)
        pltpu.emit_pipeline(step, grid=(n_stage + n_tiles,), in_specs=in_specs,
                            out_specs=out_specs)(*refs[:n_in + n_out])

    hbm = pl.BlockSpec(memory_space=pl.ANY)
    outs = pl.pallas_call(
        pipelined,
        in_specs=[hbm] * n_in,
        out_specs=[hbm] * n_out,
        out_shape=[jax.ShapeDtypeStruct((m, COL_BLOCK), jnp.bfloat16),
                   jax.ShapeDtypeStruct((m, D_MODEL), jnp.bfloat16),
                   jax.ShapeDtypeStruct((m, 1), jnp.float32)]
        + [jax.ShapeDtypeStruct((D_MODEL, COL_BLOCK), jnp.bfloat16)] * 2,
        scratch_shapes=[w_buf, w_buf, buf, buf,
                        pltpu.VMEM((POOL_HALO, COL_BLOCK), jnp.float32)],
        compiler_params=pltpu.CompilerParams(vmem_limit_bytes=VMEM_LIMIT_BYTES),
        name="pool_first",
    )(x2d, norm_row, w_in2d, w_in2d, w_in2d, w_in2d, pool_w, pool_w,
      pool_scale.reshape(1, HALF))
    return outs[0], outs[1], outs[2], outs[3:]


def _pool_rest_kernel(xb_ref, rs_ref, wu_ref, wz_ref, pw_ref, ps_ref,
                      wu0_ref, wz0_ref, n0_ref, n1_ref, n2_ref, n3_ref,
                      y_ref, p0_ref, p1_ref, p2_ref, p3_ref,
                      wu_buf, wz_buf, ubuf, s2buf, s4buf, ucarry, sems, *, tiles_per_seq):
    jj = pl.program_id(0)
    i = pl.program_id(1)
    first_step = (jj == 0) & (i == 0)

    @pl.when(first_step)
    def _():
        ucarry[...] = jnp.zeros_like(ucarry)

    _load_prestaged(first_step, [(wu0_ref, wu_buf), (wz0_ref, wz_buf)], sems)

    _cast_rows([(n0_ref, p0_ref), (n1_ref, p1_ref), (n2_ref, p2_ref), (n3_ref, p3_ref)])
    slot = jj % 2
    _pool_tile(lambda: xb_ref[...], rs_ref[...], wu_buf[slot], wz_buf[slot], jj + 1,
               i % tiles_per_seq, ps_ref, y_ref, ubuf, (s2buf, s4buf), ucarry)
    _stage_chunks((jj + 1) % 2, i, [(wu_ref, wu_buf), (wz_ref, wz_buf)],
                  fold=pw_ref[0].astype(jnp.bfloat16))


def _pool_rest(xb, rs, w_in2d, pool_w, pool_scale, first_blocks, conv_blocks):
    m = xb.shape[0]
    tm = REST_ROW_TILE
    nb = N_GROUPS - 1
    n_tiles = m // tm
    chunk_rows = D_MODEL // n_tiles
    chunk, staged = _prestaged_maps(nb, n_tiles)
    w_chunk = lambda first: pl.BlockSpec((chunk_rows, COL_BLOCK),
                                         lambda jj, i: (chunk(jj, i), first + 1 + staged(jj)))
    side_steps = 2 * n_tiles
    side_rows = D_MODEL // side_steps
    side_step = lambda jj, i: jnp.minimum(jj * n_tiles + i, side_steps - 1)
    side_in = lambda blk: pl.BlockSpec((side_rows, COL_BLOCK),
                                       lambda jj, i: (side_step(jj, i), blk))
    side_out = pl.BlockSpec((side_rows, COL_BLOCK), lambda jj, i: (side_step(jj, i), 0))
    n_side = len(conv_blocks)
    hbm = pl.BlockSpec(memory_space=pl.ANY)
    w_buf = pltpu.VMEM((2, D_MODEL, COL_BLOCK), jnp.bfloat16)
    buf = pltpu.VMEM((POOL_TOP + tm, COL_BLOCK), jnp.float32)
    outs = pl.pallas_call(
        functools.partial(_pool_rest_kernel, tiles_per_seq=SEQ // tm),
        grid=(nb, n_tiles),
        in_specs=[
            pl.BlockSpec((tm, D_MODEL), lambda jj, i: (i, 0)),
            pl.BlockSpec((tm, 1), lambda jj, i: (i, 0)),
            w_chunk(0), w_chunk(N_GROUPS),
            pl.BlockSpec((1, GROUP_DIM, GROUP_DIM), lambda jj, i: (1 + staged(jj), 0, 0),
                         pipeline_mode=pl.Buffered(1)),
            pl.BlockSpec((1, COL_BLOCK), lambda jj, i: (0, jj + 1)),
            hbm, hbm,
        ] + [side_in(blk) for blk in conv_blocks],
        out_specs=[pl.BlockSpec((tm, COL_BLOCK), lambda jj, i: (i, jj))] + [side_out] * n_side,
        out_shape=[jax.ShapeDtypeStruct((m, nb * COL_BLOCK), jnp.bfloat16)]
        + [jax.ShapeDtypeStruct((D_MODEL, COL_BLOCK), jnp.bfloat16)] * n_side,
        scratch_shapes=[w_buf, w_buf, buf, buf, buf,
                        pltpu.VMEM((POOL_HALO, COL_BLOCK), jnp.float32),
                        pltpu.SemaphoreType.DMA((2,))],
        compiler_params=_params(2),
        name="pool_rest",
    )(xb, rs, w_in2d, w_in2d, pool_w, pool_scale.reshape(1, HALF),
      *first_blocks, *([w_in2d] * n_side))
    return outs[0], outs[1:]


def _conv_kernel(xb_ref, rs_ref, wu_ref, wb_ref, wc_ref, wz_ref, cw_ref, cb_ref,
                 wu0_ref, wb0_ref, wc0_ref, wz0_ref, n0_ref, n1_ref, n2_ref,
                 y_ref, p0_ref, p1_ref, p2_ref,
                 wu_buf, wb_buf, wc_buf, wz_buf, vbuf, vcarry, sems, *, tiles_per_seq):
    jj = pl.program_id(0)
    i = pl.program_id(1)
    tm = xb_ref.shape[0]
    first_step = (jj == 0) & (i == 0)

    @pl.when(first_step)
    def _():
        vcarry[...] = jnp.zeros_like(vcarry)

    _load_prestaged(first_step, [(wu0_ref, wu_buf), (wb0_ref, wb_buf),
                                 (wc0_ref, wc_buf), (wz0_ref, wz_buf)], sems)

    def stage():
        _stage_chunks((jj + 1) % 2, i, [(wu_ref, wu_buf), (wb_ref, wb_buf),
                                        (wc_ref, wc_buf), (wz_ref, wz_buf)])

    def compute():
        slot = jj % 2
        seq_tile = i % tiles_per_seq
        rs = rs_ref[...]
        rs2 = rs * rs
        u = _dot(xb_ref[...], wu_buf[slot])
        _cast_rows([(n0_ref, p0_ref), (n1_ref, p1_ref)])
        c_gate = _dot(xb_ref[...], wc_buf[slot])
        vbuf[0:CONV_HALO, :] = jnp.where(seq_tile != 0, vcarry[...], 0.0)
        vbuf[CONV_HALO:CONV_HALO + tm, :] = rs2 * (c_gate * u)
        vcarry[...] = vbuf[tm:CONV_HALO + tm, :]
        cw = cw_ref[...]
        y = (cb_ref[...]
             + cw[0:1, :] * vbuf[CONV_HALO - 2:CONV_HALO - 2 + tm, :]
             + cw[1:2, :] * vbuf[CONV_HALO - 1:CONV_HALO - 1 + tm, :]
             + cw[2:3, :] * vbuf[CONV_HALO:CONV_HALO + tm, :])
        gated = (rs * y) * _silu_exp(rs * _dot(xb_ref[...], wz_buf[slot]))
        _cast_rows([(n2_ref, p2_ref)])
        y_ref[...] = (_dot(xb_ref[...], wb_buf[slot]) * gated).astype(y_ref.dtype)

    compute()
    stage()


def _conv_mixer(xb, rs, w_in2d, conv_w, conv_b, first_blocks, w_branch):
    m = xb.shape[0]
    tm = ROW_TILE
    nb = N_HALF_BLOCKS
    n_tiles = m // tm
    chunk_rows = D_MODEL // n_tiles
    chunk, staged = _prestaged_maps(nb, n_tiles)
    w_chunk = lambda first: pl.BlockSpec((chunk_rows, COL_BLOCK),
                                         lambda jj, i: (chunk(jj, i), first + staged(jj)))
    hbm = pl.BlockSpec(memory_space=pl.ANY)
    n_steps = nb * n_tiles
    step = lambda jj, i: jj * n_tiles + i
    gate_rows = D_MODEL // n_steps
    branch_rows = HALF // n_steps
    gate_in = lambda blk: pl.BlockSpec((gate_rows, COL_BLOCK), lambda jj, i: (step(jj, i), blk))
    n_br = w_branch.shape[0]
    branches = pl.BlockSpec((n_br, branch_rows, COL_BLOCK), lambda jj, i: (0, step(jj, i), 0))
    gate_out = pl.BlockSpec((gate_rows, COL_BLOCK), lambda jj, i: (step(jj, i), 0))
    g0_first = 6 * N_HALF_BLOCKS
    g1_first = g0_first + N_MODEL_BLOCKS
    w_buf = pltpu.VMEM((2, D_MODEL, COL_BLOCK), jnp.bfloat16)
    outs = pl.pallas_call(
        functools.partial(_conv_kernel, tiles_per_seq=SEQ // tm),
        grid=(nb, n_tiles),
        in_specs=[
            pl.BlockSpec((tm, D_MODEL), lambda jj, i: (i, 0)),
            pl.BlockSpec((tm, 1), lambda jj, i: (i, 0)),
            w_chunk(2 * nb), w_chunk(3 * nb), w_chunk(4 * nb), w_chunk(5 * nb),
            pl.BlockSpec((CONV_K, COL_BLOCK), lambda jj, i: (0, jj)),
            pl.BlockSpec((1, COL_BLOCK), lambda jj, i: (0, jj)),
            hbm, hbm, hbm, hbm,
            gate_in(g0_first), gate_in(g1_first), branches,
        ],
        out_specs=[pl.BlockSpec((tm, COL_BLOCK), lambda jj, i: (i, jj)),
                   gate_out, gate_out, branches],
        out_shape=[jax.ShapeDtypeStruct((m, HALF), jnp.bfloat16),
                   jax.ShapeDtypeStruct((D_MODEL, COL_BLOCK), jnp.bfloat16),
                   jax.ShapeDtypeStruct((D_MODEL, COL_BLOCK), jnp.bfloat16),
                   jax.ShapeDtypeStruct((n_br, HALF, COL_BLOCK), jnp.bfloat16)],
        scratch_shapes=[w_buf, w_buf, w_buf, w_buf,
                        pltpu.VMEM((CONV_HALO + tm, COL_BLOCK), jnp.float32),
                        pltpu.VMEM((CONV_HALO, COL_BLOCK), jnp.float32),
                        pltpu.SemaphoreType.DMA((4,))],
        compiler_params=_params(2),
        name="conv_mixer",
    )(xb, rs, w_in2d, w_in2d, w_in2d, w_in2d, conv_w,
      conv_b.reshape(1, HALF), *first_blocks, w_in2d, w_in2d, w_branch)
    return outs[0], outs[1:]


def _merge_kernel(xb_ref, rs_ref, yp0_ref, ypr_ref, yc_ref, wg0_ref, wg1_ref, gb_ref,
                  wb_ref, wo_ref, wg00_ref, wg10_ref, wb0_first_ref,
                  o_ref, wo_bf_ref, wg0_buf, wg1_buf, wb0_buf, wb1_buf, sems):
    jj = pl.program_id(0)
    i = pl.program_id(1)

    _load_prestaged((jj == 0) & (i == 0),
                    [(wg00_ref, wg0_buf), (wg10_ref, wg1_buf),
                     (wb0_first_ref.at[0], wb0_buf), (wb0_first_ref.at[1], wb1_buf)], sems)

    def stage():
        _stage_chunks((jj + 1) % 2, i, [(wg0_ref, wg0_buf), (wg1_ref, wg1_buf),
                                        ((wb_ref, 0), wb0_buf), ((wb_ref, 1), wb1_buf)])

    def compute():
        slot = jj % 2
        h = xb_ref[...]
        rs = rs_ref[...]
        g0 = _sigmoid(rs * _dot(h, wg0_buf[slot]) + gb_ref[0:1, :])
        y_pool = jnp.concatenate([yp0_ref[...], ypr_ref[...]], axis=1)
        br0 = _dot(y_pool, wb0_buf[slot])
        g1 = _sigmoid(rs * _dot(h, wg1_buf[slot]) + gb_ref[1:2, :])
        br1 = _dot(yc_ref[...], wb1_buf[slot])
        o_ref[...] = (g0 * br0 + g1 * br1).astype(o_ref.dtype)
        wo_bf_ref[...] = wo_ref[...].astype(wo_bf_ref.dtype)

    compute()
    stage()


def _merge(xb, rs, y_pool0, y_pool_rest, y_conv, w_in2d, gate_b, w_branch, w_out2d,
           first_blocks):
    m = xb.shape[0]
    tm = ROW_TILE
    nb = N_MODEL_BLOCKS
    n_tiles = m // tm
    chunk, staged = _prestaged_maps(nb, n_tiles)
    g0_first = 6 * N_HALF_BLOCKS
    g1_first = g0_first + nb
    gate_chunk = lambda first: pl.BlockSpec(
        (D_MODEL // n_tiles, COL_BLOCK), lambda jj, i: (chunk(jj, i), first + staged(jj)))
    branch_chunks = pl.BlockSpec((w_branch.shape[0], HALF // n_tiles, COL_BLOCK),
                                 lambda jj, i: (0, chunk(jj, i), staged(jj)))
    tile = lambda width: pl.BlockSpec((tm, width), lambda jj, i: (i, 0))
    bias = pl.BlockSpec((2, COL_BLOCK), lambda jj, i: (0, jj))
    wo_rows = D_MODEL // (nb * n_tiles)
    wo_slice = pl.BlockSpec((wo_rows, D_MODEL), lambda jj, i: (jj * n_tiles + i, 0))
    hbm = pl.BlockSpec(memory_space=pl.ANY)
    return pl.pallas_call(
        _merge_kernel,
        grid=(nb, n_tiles),
        in_specs=[tile(D_MODEL), tile(1),
                  tile(COL_BLOCK), tile(HALF - COL_BLOCK), tile(HALF),
                  gate_chunk(g0_first), gate_chunk(g1_first), bias,
                  branch_chunks, wo_slice, hbm, hbm, hbm],
        out_specs=[pl.BlockSpec((tm, COL_BLOCK), lambda jj, i: (i, jj)), wo_slice],
        out_shape=[jax.ShapeDtypeStruct((m, D_MODEL), jnp.bfloat16),
                   jax.ShapeDtypeStruct((D_MODEL, D_MODEL), jnp.bfloat16)],
        scratch_shapes=[pltpu.VMEM((2, D_MODEL, COL_BLOCK), jnp.bfloat16),
                        pltpu.VMEM((2, D_MODEL, COL_BLOCK), jnp.bfloat16),
                        pltpu.VMEM((2, HALF, COL_BLOCK), jnp.bfloat16),
                        pltpu.VMEM((2, HALF, COL_BLOCK), jnp.bfloat16),
                        pltpu.SemaphoreType.DMA((4,))],
        compiler_params=_params(2),
        name="branch_merge",
    )(xb, rs, y_pool0, y_pool_rest, y_conv, w_in2d, w_in2d, gate_b,
      w_branch, w_out2d, *first_blocks)


def _out_kernel(mg_ref, w_ref, x_ref, fw_ref, o_ref, *, rows, sumsq, n_tiles):
    i = pl.program_id(0)
    n = pl.program_id(1)

    def emit_previous():
        scale = lax.rsqrt(sumsq[(i + 1) % 2] * (1.0 / D_MODEL) + NORM_EPS)
        o_ref[...] = rows[n] * scale * fw_ref[...]

    def project():
        xn = x_ref[...] + _dot(mg_ref[...], w_ref[...])
        rows[n] = xn
        part = jnp.sum(xn * xn, axis=-1, keepdims=True)
        cur = i % 2
        sumsq[cur] = jnp.where(n == 0, part, sumsq[cur] + part)

    @pl.when(i == 0)
    def _():
        project()

    @pl.when((i > 0) & (i < n_tiles))
    def _():
        emit_previous()
        project()

    @pl.when(i == n_tiles)
    def _():
        emit_previous()


def _out_proj(merged, w_out_bf, x2d, final_w):
    m = x2d.shape[0]
    tm = OUT_ROW_TILE
    tn = OUT_COL_BLOCK
    n_tiles = m // tm
    n_blocks = D_MODEL // tn
    last = n_tiles - 1
    in_row = lambda i: jnp.minimum(i, last)
    in_col = lambda i, n: jnp.where(i < n_tiles, n, n_blocks - 1)
    in_specs = [
        pl.BlockSpec((tm, D_MODEL), lambda i, n: (in_row(i), 0),
                     pipeline_mode=pl.Buffered(2, use_lookahead=True)),
        pl.BlockSpec((D_MODEL, tn), lambda i, n: (0, in_col(i, n)),
                     pipeline_mode=pl.Buffered(3)),
        pl.BlockSpec((tm, tn), lambda i, n: (in_row(i), in_col(i, n))),
        pl.BlockSpec((1, tn), lambda i, n: (0, n)),
    ]
    out_spec = pl.BlockSpec(
        (tm, tn), lambda i, n: (jnp.maximum(i - 1, 0), jnp.where(i > 0, n, 0)))

    def pipelined(mg_hbm, w_hbm, x_hbm, fw_hbm, o_hbm, rows, sumsq):
        step = functools.partial(_out_kernel, rows=rows, sumsq=sumsq, n_tiles=n_tiles)
        pltpu.emit_pipeline(step, grid=(n_tiles + 1, n_blocks), in_specs=in_specs,
                            out_specs=[out_spec])(mg_hbm, w_hbm, x_hbm, fw_hbm, o_hbm)

    hbm = pl.BlockSpec(memory_space=pl.ANY)
    return pl.pallas_call(
        pipelined,
        in_specs=[hbm, hbm, hbm, hbm],
        out_specs=hbm,
        out_shape=jax.ShapeDtypeStruct((m, D_MODEL), jnp.float32),
        scratch_shapes=[pltpu.VMEM((n_blocks, tm, tn), jnp.float32),
                        pltpu.VMEM((2, tm, 1), jnp.float32)],
        compiler_params=pltpu.CompilerParams(vmem_limit_bytes=VMEM_LIMIT_BYTES),
        name="out_proj_norm",
    )(merged, w_out_bf, x2d, final_w.reshape(1, D_MODEL))


def kernel(x, norm_w, w_in, pool_w, pool_scale, conv_w, conv_b, gate_b, w_branch, w_out,
           final_norm_w):
    b, s, d = x.shape
    assert (s, d) == (SEQ, D_MODEL) and norm_w.shape[0] == 1
    x2d = x.reshape(b * s, d)
    w_in2d = w_in[0]

    nb = N_HALF_BLOCKS
    conv_first_cols = (2 * nb, 3 * nb, 4 * nb, 5 * nb)
    y_pool0, xb, rs, pool_first = _pool_first(x2d, norm_w, w_in2d, pool_w[0], pool_scale[0])
    y_pool_rest, conv_first = _pool_rest(xb, rs, w_in2d, pool_w[0], pool_scale[0],
                                         pool_first, conv_first_cols)
    y_conv, merge_first = _conv_mixer(xb, rs, w_in2d, conv_w[0], conv_b[0],
                                      conv_first, w_branch[0])
    merged, w_out_bf = _merge(xb, rs, y_pool0, y_pool_rest, y_conv, w_in2d, gate_b[0],
                              w_branch[0], w_out[0], merge_first)
    out = _out_proj(merged, w_out_bf, x2d, final_norm_w)
    return out.reshape(b, s, d)
```
